```python
import math
import jax, jax.numpy as jnp
from jax import lax
import numpy as np

D_MODEL = 1024
BATCH = 8
SEQ = 2048
DEPTH = 1

PLE_DIM = 256
DIFF_HEADS = 4
DIFF_QK_DIM = 64
DIFF_V_DIM = 2 * DIFF_QK_DIM
DIFF_QK_WIDTH = DIFF_HEADS * 2 * DIFF_QK_DIM
DIFF_WIDTH = DIFF_HEADS * DIFF_V_DIM
SSM_WIDTH = D_MODEL - DIFF_WIDTH
SSM_GROUP = 16
SSM_GROUPS = SSM_WIDTH // SSM_GROUP
SSM_STATE = 64
DT_MIN = 1e-3
DT_MAX = 1e-1
MIX_WIDTH = DIFF_WIDTH + SSM_WIDTH
IN_COLS = 2 * DIFF_QK_WIDTH + DIFF_WIDTH + SSM_WIDTH
D_FF = 2816
CONV_WIDTH = 3
Q_BLOCK = 128
LN_EPS = 1e-5
DEEPNORM_ALPHA = (2 * DEPTH) ** 0.25
DEEPNORM_BETA = (8 * DEPTH) ** -0.25

kernel_name = "hybrid_diffattn_s5_convffn_deepnorm"


def layer_norm(x, g, b):
    xf = x.astype(jnp.float32)
    mu = jnp.mean(xf, axis=-1, keepdims=True)
    var = jnp.mean(jnp.square(xf - mu), axis=-1, keepdims=True)
    return ((xf - mu) * lax.rsqrt(var + LN_EPS) * g + b).astype(x.dtype)


def rms_norm(x, g):
    xf = x.astype(jnp.float32)
    ms = jnp.mean(jnp.square(xf), axis=-1, keepdims=True)
    return (xf * lax.rsqrt(ms + LN_EPS) * g).astype(x.dtype)


def diff_attention(q, k, v, lam, subln_g, lam_init):
    B, L, H, _, dq = q.shape
    E = v.shape[-1]
    nb = L // Q_BLOCK
    qb = q.reshape(B, nb, Q_BLOCK, H, 2, dq).transpose(1, 0, 2, 3, 4, 5)
    k_pos = jnp.arange(L)
    scale = DIFF_QK_DIM ** -0.5

    def block(args):
        q_blk, start = args
        s = jnp.einsum('bqhcd,bkhcd->bhcqk', q_blk, k).astype(jnp.float32) * scale
        q_pos = start + jnp.arange(Q_BLOCK)
        mask = k_pos[None, :] <= q_pos[:, None]
        s = jnp.where(mask, s, -jnp.inf)
        pr = jax.nn.softmax(s, axis=-1)
        a = pr[:, :, 0] - lam * pr[:, :, 1]
        return jnp.einsum('bhqk,bkhe->bqhe', a.astype(v.dtype), v)

    starts = jnp.arange(nb) * Q_BLOCK
    o = lax.map(block, (qb, starts))
    o = o.transpose(1, 0, 2, 3, 4).reshape(B, L, H, E)
    o = rms_norm(o, subln_g) * (1.0 - lam_init)
    return o.reshape(B, L, H * E)


def s5_mixer(u, lam_re, lam_im, log_dt, b_re, b_im, c_re, c_im, d_skip, w_glu, b_glu):
    Bsz, L, W = u.shape
    f32 = jnp.float32
    uf = u.astype(f32).reshape(Bsz, L, SSM_GROUPS, SSM_GROUP)
    dt = jnp.exp(log_dt.astype(f32))[:, None]
    lr = lam_re.astype(f32)
    li = lam_im.astype(f32)
    mag = jnp.exp(lr * dt)
    ab_re = mag * jnp.cos(li * dt)
    ab_im = mag * jnp.sin(li * dt)
    den = lr * lr + li * li
    zr = ab_re - 1.0
    zi = ab_im
    fr = (zr * lr + zi * li) / den
    fi = (zi * lr - zr * li) / den
    br = b_re.astype(f32)
    bi = b_im.astype(f32)
    bb_re = fr[..., None] * br - fi[..., None] * bi
    bb_im = fr[..., None] * bi + fi[..., None] * br
    bu_re = jnp.einsum('blgh,gph->blgp', uf, bb_re)
    bu_im = jnp.einsum('blgh,gph->blgp', uf, bb_im)
    a_re = jnp.broadcast_to(ab_re, bu_re.shape)
    a_im = jnp.broadcast_to(ab_im, bu_im.shape)

    def combine(e1, e2):
        a1r, a1i, b1r, b1i = e1
        a2r, a2i, b2r, b2i = e2
        return (a1r * a2r - a1i * a2i,
                a1r * a2i + a1i * a2r,
                a2r * b1r - a2i * b1i + b2r,
                a2r * b1i + a2i * b1r + b2i)

    _, _, s_re, s_im = lax.associative_scan(combine, (a_re, a_im, bu_re, bu_im), axis=1)
    y = (jnp.einsum('blgp,ghp->blgh', s_re, c_re.astype(f32))
         - jnp.einsum('blgp,ghp->blgh', s_im, c_im.astype(f32)))
    y = (y + d_skip.astype(f32) * uf).reshape(Bsz, L, W)
    y = jax.nn.gelu(y)
    y = y * jax.nn.sigmoid(y @ w_glu.astype(f32) + b_glu.astype(f32))
    return y.astype(u.dtype)


def conv_ffn(x, w_up, conv_w, conv_b, w_down):
    L = x.shape[1]
    h = x @ w_up
    hp = jnp.pad(h, ((0, 0), (CONV_WIDTH - 1, 0), (0, 0)))
    hc = conv_b + sum(hp[:, j:j + L] * conv_w[j] for j in range(CONV_WIDTH))
    g, val = jnp.split(hc, 2, axis=-1)
    return (jax.nn.silu(g) * val) @ w_down


def _normal(key, shape, std):
    return jax.random.normal(key, shape, jnp.float32) * std


def setup_inputs(seed: int = 0) -> dict:
    key = jax.random.key(seed)
    ks = jax.random.split(key, 32)
    N = DEPTH
    G, P, H = SSM_GROUPS, SSM_STATE, SSM_GROUP
    x = _normal(ks[0], (BATCH, SEQ, D_MODEL), 1.0)
    p = _normal(ks[1], (DEPTH, BATCH, SEQ, PLE_DIM), 1.0)
    w_in = _normal(ks[2], (N, D_MODEL, IN_COLS), D_MODEL ** -0.5)
    diff_lambda_q1 = _normal(ks[3], (N, DIFF_QK_DIM), 0.1)
    diff_lambda_k1 = _normal(ks[4], (N, DIFF_QK_DIM), 0.1)
    diff_lambda_q2 = _normal(ks[5], (N, DIFF_QK_DIM), 0.1)
    diff_lambda_k2 = _normal(ks[6], (N, DIFF_QK_DIM), 0.1)
    diff_subln_g = 1.0 + _normal(ks[7], (N, DIFF_V_DIM), 0.01)
    ssm_lambda_re = -0.5 + _normal(ks[8], (N, G, P), 0.01)
    ssm_lambda_im = jnp.pi * jnp.arange(P, dtype=jnp.float32) + _normal(ks[9], (N, G, P), 0.01)
    ssm_log_dt = jax.random.uniform(ks[10], (N, G), jnp.float32, math.log(DT_MIN), math.log(DT_MAX))
    ssm_b_re = _normal(ks[11], (N, G, P, H), (2.0 * H) ** -0.5)
    ssm_b_im = _normal(ks[12], (N, G, P, H), (2.0 * H) ** -0.5)
    ssm_c_re = _normal(ks[13], (N, G, H, P), 0.5)
    ssm_c_im = _normal(ks[14], (N, G, H, P), 0.5)
    ssm_d = _normal(ks[15], (N, G, H), 1.0)
    ssm_w_glu = _normal(ks[16], (N, SSM_WIDTH, SSM_WIDTH), SSM_WIDTH ** -0.5)
    ssm_b_glu = _normal(ks[17], (N, SSM_WIDTH), 0.01)
    w_o = _normal(ks[18], (N, MIX_WIDTH, D_MODEL), MIX_WIDTH ** -0.5 * DEEPNORM_BETA)
    ln1_g = 1.0 + _normal(ks[19], (N, D_MODEL), 0.01)
    ln1_b = _normal(ks[20], (N, D_MODEL), 0.01)
    ffn_w_up = _normal(ks[21], (N, D_MODEL, 2 * D_FF), D_MODEL ** -0.5)
    ffn_conv_w = _normal(ks[22], (N, CONV_WIDTH, 2 * D_FF), CONV_WIDTH ** -0.5)
    ffn_conv_b = _normal(ks[23], (N, 2 * D_FF), 0.01)
    ffn_w_down = _normal(ks[24], (N, D_FF, D_MODEL), D_FF ** -0.5 * DEEPNORM_BETA)
    w_ple = _normal(ks[25], (N, PLE_DIM, D_MODEL), PLE_DIM ** -0.5)
    w_ple_gate = _normal(ks[26], (N, D_MODEL, D_MODEL), D_MODEL ** -0.5)
    ln2_g = 1.0 + _normal(ks[27], (N, D_MODEL), 0.01)
    ln2_b = _normal(ks[28], (N, D_MODEL), 0.01)
    return {"x": x, "p": p, "w_in": w_in,
            "diff_lambda_q1": diff_lambda_q1, "diff_lambda_k1": diff_lambda_k1,
            "diff_lambda_q2": diff_lambda_q2, "diff_lambda_k2": diff_lambda_k2,
            "diff_subln_g": diff_subln_g,
            "ssm_lambda_re": ssm_lambda_re, "ssm_lambda_im": ssm_lambda_im, "ssm_log_dt": ssm_log_dt,
            "ssm_b_re": ssm_b_re, "ssm_b_im": ssm_b_im, "ssm_c_re": ssm_c_re, "ssm_c_im": ssm_c_im,
            "ssm_d": ssm_d, "ssm_w_glu": ssm_w_glu, "ssm_b_glu": ssm_b_glu,
            "w_o": w_o, "ln1_g": ln1_g, "ln1_b": ln1_b,
            "ffn_w_up": ffn_w_up, "ffn_conv_w": ffn_conv_w, "ffn_conv_b": ffn_conv_b, "ffn_w_down": ffn_w_down,
            "w_ple": w_ple, "w_ple_gate": w_ple_gate, "ln2_g": ln2_g, "ln2_b": ln2_b}


def reference(x, p, w_in, diff_lambda_q1, diff_lambda_k1, diff_lambda_q2, diff_lambda_k2, diff_subln_g,
              ssm_lambda_re, ssm_lambda_im, ssm_log_dt, ssm_b_re, ssm_b_im, ssm_c_re, ssm_c_im,
              ssm_d, ssm_w_glu, ssm_b_glu, w_o, ln1_g, ln1_b,
              ffn_w_up, ffn_conv_w, ffn_conv_b, ffn_w_down, w_ple, w_ple_gate, ln2_g, ln2_b):
    B, L, _ = x.shape
    for i in range(DEPTH):
        lam_init = 0.8 - 0.6 * math.exp(-0.3 * i)
        h = x @ w_in[i]
        q = h[..., :DIFF_QK_WIDTH].reshape(B, L, DIFF_HEADS, 2, DIFF_QK_DIM)
        k = h[..., DIFF_QK_WIDTH:2 * DIFF_QK_WIDTH].reshape(B, L, DIFF_HEADS, 2, DIFF_QK_DIM)
        v = h[..., 2 * DIFF_QK_WIDTH:2 * DIFF_QK_WIDTH + DIFF_WIDTH].reshape(B, L, DIFF_HEADS, DIFF_V_DIM)
        u = h[..., 2 * DIFF_QK_WIDTH + DIFF_WIDTH:]
        lam = (jnp.exp(jnp.sum(diff_lambda_q1[i].astype(jnp.float32) * diff_lambda_k1[i].astype(jnp.float32)))
               - jnp.exp(jnp.sum(diff_lambda_q2[i].astype(jnp.float32) * diff_lambda_k2[i].astype(jnp.float32)))
               + lam_init)
        attn = diff_attention(q, k, v, lam, diff_subln_g[i], lam_init)
        ssm = s5_mixer(u, ssm_lambda_re[i], ssm_lambda_im[i], ssm_log_dt[i], ssm_b_re[i], ssm_b_im[i],
                       ssm_c_re[i], ssm_c_im[i], ssm_d[i], ssm_w_glu[i], ssm_b_glu[i])
        mix = jnp.concatenate([attn, ssm.astype(attn.dtype)], axis=-1) @ w_o[i]
        x = layer_norm(DEEPNORM_ALPHA * x + mix, ln1_g[i], ln1_b[i])
        f = conv_ffn(x, ffn_w_up[i], ffn_conv_w[i], ffn_conv_b[i], ffn_w_down[i])
        ple = (p[i] @ w_ple[i]) * jax.nn.sigmoid(x @ w_ple_gate[i])
        x = layer_norm(DEEPNORM_ALPHA * x + f + ple, ln2_g[i], ln2_b[i])
    return x
```

```python
import functools
import math

import jax
import jax.numpy as jnp
from jax import lax
from jax.experimental import pallas as pl
from jax.experimental.pallas import tpu as pltpu

F32 = jnp.float32
BF16 = jnp.bfloat16

SUBLANES = 8
LANES = 128

DEPTH = 1
D_MODEL = 1024
PLE_DIM = 256
HEADS = 4
QK_DIM = 64
V_DIM = 2 * QK_DIM
QK_WIDTH = HEADS * 2 * QK_DIM
ATT_WIDTH = HEADS * V_DIM
SSM_WIDTH = D_MODEL - ATT_WIDTH
SSM_GROUP = 16
SSM_GROUPS = SSM_WIDTH // SSM_GROUP
SSM_STATE = 64
D_FF = 2816
CONV_WIDTH = 3
LN_EPS = 1e-5
ALPHA = (2 * DEPTH) ** 0.25
QK_SCALE = QK_DIM ** -0.5

CHUNK = 16
CHUNK_COLS = CHUNK * SSM_GROUP
GROUPS_PER_TILE = LANES // SSM_GROUP
LANE_TILES = SSM_WIDTH // LANES
PAIRS_PER_TILE = GROUPS_PER_TILE // 2
STATE_COLS = GROUPS_PER_TILE * SSM_STATE

VMEM_LIMIT = 56 * 1024 * 1024


def _cparams(sem):
    return pltpu.CompilerParams(dimension_semantics=sem, vmem_limit_bytes=VMEM_LIMIT)


def _in_proj_kernel(x_ref, w_ref, qkv_ref, u_ref, *, tl):
    b = pl.program_id(1)
    xb = x_ref[0].astype(BF16)
    for n in range(3):
        h = jnp.dot(xb, w_ref[:, n * 512:(n + 1) * 512], preferred_element_type=F32)
        if n == 0:
            h = h * QK_SCALE
        qkv_ref[0, :, n * 512:(n + 1) * 512] = h.astype(BF16)
    hu = jnp.dot(xb, w_ref[:, 3 * 512:], preferred_element_type=F32)
    for q in range(LANE_TILES):
        u_ref[q, pl.ds(b, tl, stride=SUBLANES), :] = hu[:, q * LANES:(q + 1) * LANES]


def _in_proj(x, w_in_bf, tl=512):
    B, L, D = x.shape
    n_l = L // tl
    return pl.pallas_call(
        functools.partial(_in_proj_kernel, tl=tl),
        grid=(n_l, B),
        in_specs=[
            pl.BlockSpec((1, tl, D), lambda l, b: (b, l, 0)),
            pl.BlockSpec((D, 2048), lambda l, b: (0, 0)),
        ],
        out_specs=[
            pl.BlockSpec((1, tl, 3 * 512), lambda l, b: (b, l, 0)),
            pl.BlockSpec((LANE_TILES, tl * B, LANES), lambda l, b: (0, l, 0)),
        ],
        out_shape=[
            jax.ShapeDtypeStruct((B, L, 3 * 512), BF16),
            jax.ShapeDtypeStruct((LANE_TILES, L * B, LANES), F32),
        ],
        compiler_params=_cparams(("arbitrary", "arbitrary")),
        name="in_proj",
    )(x, w_in_bf)


def _attn_kernel(lq1_ref, lk1_ref, lq2_ref, lk2_ref, g_ref, q_ref, k_ref, v_ref, o_ref,
                 m_sc, l_sc, acc_sc, *, tq, lam_init):
    qi = pl.program_id(2)
    lam = (jnp.exp(jnp.sum(lq1_ref[...] * lk1_ref[...], axis=-1, keepdims=True))
           - jnp.exp(jnp.sum(lq2_ref[...] * lk2_ref[...], axis=-1, keepdims=True)) + lam_init)

    q = q_ref[0]
    lane = lax.broadcasted_iota(jnp.int32, q.shape, 1)
    zero = jnp.zeros_like(q)
    qs = jnp.concatenate([jnp.where(lane < QK_DIM, q, zero), jnp.where(lane >= QK_DIM, q, zero)], axis=0)

    m_sc[...] = jnp.full(m_sc.shape, -jnp.inf, F32)
    l_sc[...] = jnp.zeros(l_sc.shape, F32)
    acc_sc[...] = jnp.zeros(acc_sc.shape, F32)

    def block(j, masked):
        start = pl.multiple_of(j * tq, tq)
        kb = k_ref[0, pl.ds(start, tq), :]
        vb = v_ref[0, pl.ds(start, tq), :]
        s = lax.dot_general(qs, kb, (((1,), (1,)), ((), ())), preferred_element_type=F32)
        if masked:
            row = lax.broadcasted_iota(jnp.int32, s.shape, 0)
            row = jnp.where(row >= tq, row - tq, row)
            col = lax.broadcasted_iota(jnp.int32, s.shape, 1)
            s = jnp.where(col <= row, s, -jnp.inf)
        m_prev = m_sc[...]
        m_new = jnp.maximum(m_prev, jnp.max(s, axis=-1, keepdims=True))
        alpha = jnp.exp(m_prev - m_new)
        p = jnp.exp(s - m_new)
        l_sc[...] = alpha * l_sc[...] + jnp.sum(p, axis=-1, keepdims=True)
        acc_sc[...] = alpha * acc_sc[...] + jnp.dot(p.astype(BF16), vb, preferred_element_type=F32)
        m_sc[...] = m_new

    def body(j, c):
        block(j, False)
        return c

    lax.fori_loop(0, qi, body, 0)
    block(qi, True)

    o = acc_sc[...] / l_sc[...]
    o = o[:tq] - lam * o[tq:]
    ms = jnp.mean(o * o, axis=-1, keepdims=True)
    o_ref[0] = (o * lax.rsqrt(ms + LN_EPS) * g_ref[...] * (1.0 - lam_init)).astype(o_ref.dtype)


def _diff_attn(qkv, lq1, lk1, lq2, lk2, subln_g, lam_init, tq=256):
    B, L, _ = qkv.shape
    n_q = L // tq
    vec = pl.BlockSpec((1, QK_DIM), lambda b, h, i: (0, 0))
    return pl.pallas_call(
        functools.partial(_attn_kernel, tq=tq, lam_init=lam_init),
        grid=(B, HEADS, n_q),
        in_specs=[
            vec, vec, vec, vec,
            pl.BlockSpec((1, V_DIM), lambda b, h, i: (0, 0)),
            pl.BlockSpec((1, tq, LANES), lambda b, h, i: (b, i, h)),
            pl.BlockSpec((1, L, LANES), lambda b, h, i: (b, 0, HEADS + h)),
            pl.BlockSpec((1, L, LANES), lambda b, h, i: (b, 0, 2 * HEADS + h)),
        ],
        out_specs=pl.BlockSpec((1, tq, LANES), lambda b, h, i: (b, i, h)),
        out_shape=jax.ShapeDtypeStruct((B, L, ATT_WIDTH), BF16),
        scratch_shapes=[
            pltpu.VMEM((2 * tq, 1), F32),
            pltpu.VMEM((2 * tq, 1), F32),
            pltpu.VMEM((2 * tq, V_DIM), F32),
        ],
        compiler_params=_cparams(("parallel", "parallel", "arbitrary")),
        name="diff_attn",
    )(lq1, lk1, lq2, lk2, subln_g, qkv, qkv, qkv)


def _block_transpose8(vs):
    lane_blk = lax.broadcasted_iota(jnp.int32, vs[0].shape, 1) // SSM_GROUP
    for d in (4, 2, 1):
        keep = (lane_blk & d) == 0
        new = list(vs)
        for a in range(8):
            if a & d == 0:
                lo, hi = vs[a], vs[a + d]
                new[a] = jnp.where(keep, lo, pltpu.roll(hi, SSM_GROUP * d, 1))
                new[a + d] = jnp.where(keep, pltpu.roll(lo, LANES - SSM_GROUP * d, 1), hi)
        vs = new
    return vs


def _ssm_kernel(u_ref, toep_ref, win_ref, wout_ref, a_ref, d_ref, y_ref, s_sc, x_sc, sp_sc, *, jt):
    rows = jt * SUBLANES

    @pl.when(pl.program_id(1) == 0)
    def _():
        s_sc[...] = jnp.zeros(s_sc.shape, F32)

    xs = [u_ref[0, :, r].reshape(rows, LANES) for r in range(CHUNK)]
    z_lo = _block_transpose8(xs[:8])
    z_hi = _block_transpose8(xs[8:])
    ug = [jnp.concatenate([z_lo[g], z_hi[g]], axis=1).astype(BF16) for g in range(GROUPS_PER_TILE)]

    y = [jnp.dot(ug[g], toep_ref[0, g], preferred_element_type=F32) for g in range(GROUPS_PER_TILE)]

    for m in range(PAIRS_PER_TILE):
        up = jnp.concatenate([ug[2 * m], ug[2 * m + 1]], axis=1)
        xp = jnp.dot(up, win_ref[0, m], preferred_element_type=F32)
        x_sc[0, :, m * LANES:(m + 1) * LANES] = xp[:, :LANES]
        x_sc[1, :, m * LANES:(m + 1) * LANES] = xp[:, LANES:]

    ar = a_ref[0, 0:1, :]
    ai = a_ref[0, 1:2, :]

    def step(j, carry):
        sr, si = carry
        r8 = pl.ds(pl.multiple_of(j * SUBLANES, SUBLANES), SUBLANES)
        sp_sc[0, r8, :] = sr
        sp_sc[1, r8, :] = si
        return (ar * sr - ai * si + x_sc[0, r8, :], ar * si + ai * sr + x_sc[1, r8, :])

    sr, si = lax.fori_loop(0, jt, step, (s_sc[0], s_sc[1]))
    s_sc[0] = sr
    s_sc[1] = si

    for m in range(PAIRS_PER_TILE):
        sp = jnp.concatenate([sp_sc[0, :, m * LANES:(m + 1) * LANES],
                              sp_sc[1, :, m * LANES:(m + 1) * LANES]], axis=1).astype(BF16)
        yc = jnp.dot(sp, wout_ref[0, m], preferred_element_type=F32)
        y[2 * m] = y[2 * m] + yc[:, :CHUNK_COLS]
        y[2 * m + 1] = y[2 * m + 1] + yc[:, CHUNK_COLS:]

    d = d_ref[0]
    for half in range(2):
        ws = _block_transpose8([y[g][:, half * LANES:(half + 1) * LANES] for g in range(GROUPS_PER_TILE)])
        for t8 in range(8):
            t = half * 8 + t8
            u_t = u_ref[0, :, t].reshape(rows, LANES)
            y_ref[0, :, t] = (ws[t8] + d * u_t).reshape(jt, SUBLANES, LANES)


def _ssm(u_tm, tables, d_skip, B, L, lt=512):
    toep, win, wout, a16 = tables
    jt = lt // CHUNK
    n_l = L // lt
    u5 = u_tm.reshape(LANE_TILES, L // CHUNK, CHUNK, B, LANES)
    rows = jt * B
    y5 = pl.pallas_call(
        functools.partial(_ssm_kernel, jt=jt),
        grid=(LANE_TILES, n_l),
        in_specs=[
            pl.BlockSpec((1, jt, CHUNK, B, LANES), lambda q, l: (q, l, 0, 0, 0)),
            pl.BlockSpec((1, GROUPS_PER_TILE, CHUNK_COLS, CHUNK_COLS), lambda q, l: (q, 0, 0, 0)),
            pl.BlockSpec((1, PAIRS_PER_TILE, 2 * CHUNK_COLS, 2 * LANES), lambda q, l: (q, 0, 0, 0)),
            pl.BlockSpec((1, PAIRS_PER_TILE, 2 * LANES, 2 * CHUNK_COLS), lambda q, l: (q, 0, 0, 0)),
            pl.BlockSpec((1, 2, STATE_COLS), lambda q, l: (q, 0, 0)),
            pl.BlockSpec((1, 1, LANES), lambda q, l: (q, 0, 0)),
        ],
        out_specs=pl.BlockSpec((1, jt, CHUNK, B, LANES), lambda q, l: (q, l, 0, 0, 0)),
        out_shape=jax.ShapeDtypeStruct(u5.shape, F32),
        scratch_shapes=[
            pltpu.VMEM((2, B, STATE_COLS), F32),
            pltpu.VMEM((2, rows, STATE_COLS), F32),
            pltpu.VMEM((2, rows, STATE_COLS), F32),
        ],
        compiler_params=_cparams(("parallel", "arbitrary")),
        name="ssm",
    )(u5, toep, win, wout, a16, d_skip.reshape(LANE_TILES, 1, LANES))
    return y5.reshape(LANE_TILES, L * B, LANES)


def _ssm_tables(lam_re, lam_im, log_dt, b_re, b_im, c_re, c_im):
    G, P, H = b_re.shape
    hi = lax.Precision.HIGHEST
    dt = jnp.exp(log_dt)[:, None]
    ab_re = jnp.exp(lam_re * dt) * jnp.cos(lam_im * dt)
    ab_im = jnp.exp(lam_re * dt) * jnp.sin(lam_im * dt)
    den = lam_re * lam_re + lam_im * lam_im
    zr, zi = ab_re - 1.0, ab_im
    fr = (zr * lam_re + zi * lam_im) / den
    fi = (zi * lam_re - zr * lam_im) / den
    bb_re = fr[..., None] * b_re - fi[..., None] * b_im
    bb_im = fr[..., None] * b_im + fi[..., None] * b_re
    tau = jnp.arange(CHUNK + 1, dtype=F32)[None, :, None]
    mag = jnp.exp(lam_re[:, None, :] * dt[:, :, None] * tau)
    ang = lam_im[:, None, :] * dt[:, :, None] * tau
    pw_re, pw_im = mag * jnp.cos(ang), mag * jnp.sin(ang)

    e_re = pw_re[:, :CHUNK, :, None] * bb_re[:, None] - pw_im[:, :CHUNK, :, None] * bb_im[:, None]
    e_im = pw_re[:, :CHUNK, :, None] * bb_im[:, None] + pw_im[:, :CHUNK, :, None] * bb_re[:, None]
    k = (jnp.einsum('gop,gtpi->gtoi', c_re, e_re, precision=hi)
         - jnp.einsum('gop,gtpi->gtoi', c_im, e_im, precision=hi))
    r_idx = jnp.arange(CHUNK)[:, None]
    t_idx = jnp.arange(CHUNK)[None, :]
    diff = t_idx - r_idx
    kt = jnp.take(k, jnp.clip(diff, 0, CHUNK - 1), axis=1)
    kt = jnp.where((diff >= 0)[None, :, :, None, None], kt, 0.0)
    toep = kt.transpose(0, 1, 4, 2, 3).reshape(G, CHUNK_COLS, CHUNK_COLS)

    pr = pw_re[:, CHUNK - 1::-1][:, :CHUNK]
    pi_ = pw_im[:, CHUNK - 1::-1][:, :CHUNK]
    win_re = (pr[:, :, None, :] * bb_re.transpose(0, 2, 1)[:, None] - pi_[:, :, None, :] * bb_im.transpose(0, 2, 1)[:, None])
    win_im = (pr[:, :, None, :] * bb_im.transpose(0, 2, 1)[:, None] + pi_[:, :, None, :] * bb_re.transpose(0, 2, 1)[:, None])
    win_re = win_re.reshape(G, CHUNK_COLS, P)
    win_im = win_im.reshape(G, CHUNK_COLS, P)

    qr, qi = pw_re[:, 1:], pw_im[:, 1:]
    cl_re = c_re[:, None] * qr[:, :, None, :] - c_im[:, None] * qi[:, :, None, :]
    cl_im = c_re[:, None] * qi[:, :, None, :] + c_im[:, None] * qr[:, :, None, :]
    wout_re = cl_re.reshape(G, CHUNK_COLS, P).transpose(0, 2, 1)
    wout_im = -cl_im.reshape(G, CHUNK_COLS, P).transpose(0, 2, 1)

    zw = jnp.zeros((G // 2, CHUNK_COLS, P), F32)
    wr0, wr1 = win_re[0::2], win_re[1::2]
    wi0, wi1 = win_im[0::2], win_im[1::2]
    win_pair = jnp.concatenate([
        jnp.concatenate([wr0, zw, wi0, zw], axis=2),
        jnp.concatenate([zw, wr1, zw, wi1], axis=2)], axis=1)
    zo = jnp.zeros((G // 2, P, CHUNK_COLS), F32)
    wout_pair = jnp.concatenate([
        jnp.concatenate([wout_re[0::2], zo], axis=2),
        jnp.concatenate([zo, wout_re[1::2]], axis=2),
        jnp.concatenate([wout_im[0::2], zo], axis=2),
        jnp.concatenate([zo, wout_im[1::2]], axis=2)], axis=1)

    a16 = jnp.stack([pw_re[:, CHUNK].reshape(LANE_TILES, STATE_COLS),
                     pw_im[:, CHUNK].reshape(LANE_TILES, STATE_COLS)], axis=1)
    toep = toep.reshape(LANE_TILES, GROUPS_PER_TILE, CHUNK_COLS, CHUNK_COLS).astype(BF16)
    win_pair = win_pair.reshape(LANE_TILES, PAIRS_PER_TILE, 2 * CHUNK_COLS, 2 * LANES).astype(BF16)
    wout_pair = wout_pair.reshape(LANE_TILES, PAIRS_PER_TILE, 2 * LANES, 2 * CHUNK_COLS).astype(BF16)
    return toep, win_pair, wout_pair, a16


def _layer_norm(r, g, b):
    mu = jnp.mean(r, axis=-1, keepdims=True)
    c = r - mu
    var = jnp.mean(c * c, axis=-1, keepdims=True)
    return c * lax.rsqrt(var + LN_EPS) * g + b


def _gelu_tanh(y):
    return 0.5 * y * (1.0 + jnp.tanh(math.sqrt(2.0 / math.pi) * (y + 0.044715 * (y * y * y))))


def _mix_ln1_kernel(x_ref, attn_ref, y_ref, wglu_ref, bglu_ref, wo_ref, g_ref, b_ref, o_ref, *, tl):
    b = pl.program_id(1)
    y = jnp.concatenate([y_ref[q, pl.ds(b, tl, stride=SUBLANES), :] for q in range(LANE_TILES)], axis=1)
    gl = _gelu_tanh(y)
    z = gl * jax.nn.sigmoid(jnp.dot(gl.astype(BF16), wglu_ref[...], preferred_element_type=F32) + bglu_ref[...])
    mix = (jnp.dot(attn_ref[0], wo_ref[:ATT_WIDTH], preferred_element_type=F32)
           + jnp.dot(z.astype(BF16), wo_ref[ATT_WIDTH:], preferred_element_type=F32))
    o_ref[0] = _layer_norm(ALPHA * x_ref[0] + mix, g_ref[...], b_ref[...])


def _mix_ln1(x, attn, y_tm, wglu_bf, b_glu, wo_bf, ln_g, ln_b, tl=512):
    B, L, D = x.shape
    n_l = L // tl
    const = lambda l, b: (0, 0)
    return pl.pallas_call(
        functools.partial(_mix_ln1_kernel, tl=tl),
        grid=(n_l, B),
        in_specs=[
            pl.BlockSpec((1, tl, D), lambda l, b: (b, l, 0)),
            pl.BlockSpec((1, tl, ATT_WIDTH), lambda l, b: (b, l, 0)),
            pl.BlockSpec((LANE_TILES, tl * B, LANES), lambda l, b: (0, l, 0)),
            pl.BlockSpec((SSM_WIDTH, SSM_WIDTH), const),
            pl.BlockSpec((1, SSM_WIDTH), const),
            pl.BlockSpec((D, D), const),
            pl.BlockSpec((1, D), const),
            pl.BlockSpec((1, D), const),
        ],
        out_specs=pl.BlockSpec((1, tl, D), lambda l, b: (b, l, 0)),
        out_shape=jax.ShapeDtypeStruct((B, L, D), F32),
        compiler_params=_cparams(("parallel", "arbitrary")),
        name="mix_ln1",
    )(x, attn, y_tm, wglu_bf, b_glu, wo_bf, ln_g, ln_b)


HALO = 16


def _ffn_kernel(x_ref, xh_ref, p_ref, wple_ref, wgate_ref, wg_ref, wv_ref, cwg_ref, cwv_ref, cbg_ref, cbv_ref,
                wd_ref, g_ref, b_ref, o_ref, xb_sc, acc_sc, hg_sc, hv_sc, *, tm, tiles_per_seq):
    i = pl.program_id(0)
    f = pl.program_id(1)

    @pl.when(f == 0)
    def _():
        keep = jnp.where(i % tiles_per_seq == 0, 0.0, 1.0)
        xb_sc[0:HALO] = (xh_ref[...] * keep).astype(BF16)
        xb_sc[HALO:] = x_ref[...].astype(BF16)
        gate = jax.nn.sigmoid(jnp.dot(xb_sc[HALO:], wgate_ref[...], preferred_element_type=F32))
        ple = jnp.dot(p_ref[...].astype(BF16), wple_ref[...], preferred_element_type=F32) * gate
        acc_sc[...] = ALPHA * x_ref[...] + ple

    xb = xb_sc[...]
    hg_sc[...] = jnp.dot(xb, wg_ref[...], preferred_element_type=F32)
    hv_sc[...] = jnp.dot(xb, wv_ref[...], preferred_element_type=F32)

    def conv(h_sc, w_ref, b_ref):
        out = b_ref[...] + w_ref[CONV_WIDTH - 1:CONV_WIDTH] * h_sc[HALO:]
        for j in range(CONV_WIDTH - 1):
            back = CONV_WIDTH - 1 - j
            out = out + w_ref[j:j + 1] * h_sc[HALO - back:HALO - back + tm]
        return out

    gc = conv(hg_sc, cwg_ref, cbg_ref)
    vc = conv(hv_sc, cwv_ref, cbv_ref)
    act = (gc * jax.nn.sigmoid(gc) * vc).astype(BF16)
    acc_sc[...] += jnp.dot(act, wd_ref[...], preferred_element_type=F32)

    @pl.when(f == pl.num_programs(1) - 1)
    def _():
        o_ref[...] = _layer_norm(acc_sc[...], g_ref[...], b_ref[...])


def _ffn_ln2(x1, p, wple_bf, wgate_bf, wup_bf, conv_w, conv_b, wdown_bf, ln_g, ln_b, L, tm=512, tf=1408):
    T, D = x1.shape
    n_t = T // tm
    n_f = D_FF // tf
    const = lambda i, f: (0, 0)
    return pl.pallas_call(
        functools.partial(_ffn_kernel, tm=tm, tiles_per_seq=L // tm),
        grid=(n_t, n_f),
        in_specs=[
            pl.BlockSpec((tm, D), lambda i, f: (i, 0)),
            pl.BlockSpec((HALO, D), lambda i, f: (jnp.maximum(i * (tm // HALO) - 1, 0), 0)),
            pl.BlockSpec((tm, PLE_DIM), lambda i, f: (i, 0)),
            pl.BlockSpec((PLE_DIM, D), const),
            pl.BlockSpec((D, D), const),
            pl.BlockSpec((D, tf), lambda i, f: (0, f)),
            pl.BlockSpec((D, tf), lambda i, f: (0, f + n_f)),
            pl.BlockSpec((CONV_WIDTH, tf), lambda i, f: (0, f)),
            pl.BlockSpec((CONV_WIDTH, tf), lambda i, f: (0, f + n_f)),
            pl.BlockSpec((1, tf), lambda i, f: (0, f)),
            pl.BlockSpec((1, tf), lambda i, f: (0, f + n_f)),
            pl.BlockSpec((tf, D), lambda i, f: (f, 0)),
            pl.BlockSpec((1, D), const),
            pl.BlockSpec((1, D), const),
        ],
        out_specs=pl.BlockSpec((tm, D), lambda i, f: (i, 0)),
        out_shape=jax.ShapeDtypeStruct((T, D), F32),
        scratch_shapes=[
            pltpu.VMEM((tm + HALO, D), BF16),
            pltpu.VMEM((tm, D), F32),
            pltpu.VMEM((tm + HALO, tf), F32),
            pltpu.VMEM((tm + HALO, tf), F32),
        ],
        compiler_params=_cparams(("parallel", "arbitrary")),
        name="ffn_ln2",
    )(x1, x1, p, wple_bf, wgate_bf, wup_bf, wup_bf, conv_w, conv_w, conv_b, conv_b, wdown_bf, ln_g, ln_b)


def kernel(x, p, w_in, diff_lambda_q1, diff_lambda_k1, diff_lambda_q2, diff_lambda_k2, diff_subln_g, ssm_lambda_re, ssm_lambda_im, ssm_log_dt, ssm_b_re, ssm_b_im, ssm_c_re, ssm_c_im, ssm_d, ssm_w_glu, ssm_b_glu, w_o, ln1_g, ln1_b, ffn_w_up, ffn_conv_w, ffn_conv_b, ffn_w_down, w_ple, w_ple_gate, ln2_g, ln2_b):
    B, L, D = x.shape
    assert B == SUBLANES and D == D_MODEL and L % 512 == 0
    for i in range(DEPTH):
        lam_init = 0.8 - 0.6 * math.exp(-0.3 * i)
        row = lambda a: a[i].reshape(1, -1)
        qkv, u_tm = _in_proj(x, w_in[i].astype(BF16))
        attn = _diff_attn(qkv, row(diff_lambda_q1), row(diff_lambda_k1), row(diff_lambda_q2), row(diff_lambda_k2),
                          row(diff_subln_g), lam_init)
        tables = _ssm_tables(ssm_lambda_re[i], ssm_lambda_im[i], ssm_log_dt[i], ssm_b_re[i], ssm_b_im[i],
                             ssm_c_re[i], ssm_c_im[i])
        y_tm = _ssm(u_tm, tables, ssm_d[i], B, L)
        x1 = _mix_ln1(x, attn, y_tm, ssm_w_glu[i].astype(BF16), row(ssm_b_glu), w_o[i].astype(BF16),
                      row(ln1_g), row(ln1_b))
        x2 = _ffn_ln2(x1.reshape(B * L, D), p[i].reshape(B * L, PLE_DIM), w_ple[i].astype(BF16),
                      w_ple_gate[i].astype(BF16), ffn_w_up[i].astype(BF16), ffn_conv_w[i], row(ffn_conv_b),
                      ffn_w_down[i].astype(BF16), row(ln2_g), row(ln2_b), L)
        x = x2.reshape(B, L, D)
    return x
```

```python
import functools
import math

import jax
import jax.numpy as jnp
from jax import lax
from jax.experimental import pallas as pl
from jax.experimental.pallas import tpu as pltpu

F32 = jnp.float32
BF16 = jnp.bfloat16

SUBLANES = 8
LANES = 128

DEPTH = 1
D_MODEL = 1024
PLE_DIM = 256
HEADS = 4
QK_DIM = 64
V_DIM = 2 * QK_DIM
QK_WIDTH = HEADS * 2 * QK_DIM
ATT_WIDTH = HEADS * V_DIM
SSM_WIDTH = D_MODEL - ATT_WIDTH
SSM_GROUP = 16
SSM_GROUPS = SSM_WIDTH // SSM_GROUP
SSM_STATE = 64
D_FF = 2816
CONV_WIDTH = 3
LN_EPS = 1e-5
ALPHA = (2 * DEPTH) ** 0.25
QK_SCALE = QK_DIM ** -0.5

CHUNK = 16
CHUNK_COLS = CHUNK * SSM_GROUP
GROUPS_PER_TILE = LANES // SSM_GROUP
LANE_TILES = SSM_WIDTH // LANES
PAIRS_PER_TILE = GROUPS_PER_TILE // 2
STATE_COLS = GROUPS_PER_TILE * SSM_STATE

VMEM_LIMIT = 56 * 1024 * 1024


def _cparams(sem):
    return pltpu.CompilerParams(dimension_semantics=sem, vmem_limit_bytes=VMEM_LIMIT)


TB = 256
HEADS_PER_STEP = 2


def _in_proj_kernel(x_ref, wqt_ref, wk_ref, wvt_ref, wu_ref, qt_ref, k_ref, vt_ref, u_ref, *, tl):
    b = pl.program_id(1)
    xb = x_ref[0].astype(BF16)
    nt = (((1,), (1,)), ((), ()))
    qt = lax.dot_general(wqt_ref[...], xb, nt, preferred_element_type=F32) * QK_SCALE
    vt = lax.dot_general(wvt_ref[...], xb, nt, preferred_element_type=F32)
    for h in range(HEADS):
        for c in range(tl // TB):
            qt_ref[0, h, c] = qt[h * LANES:(h + 1) * LANES, c * TB:(c + 1) * TB].astype(BF16)
            vt_ref[0, h, c] = vt[h * LANES:(h + 1) * LANES, c * TB:(c + 1) * TB].astype(BF16)
    k_ref[0] = jnp.dot(xb, wk_ref[...], preferred_element_type=F32).astype(BF16)
    hu = jnp.dot(xb, wu_ref[...], preferred_element_type=F32)
    for q in range(LANE_TILES):
        u_ref[q, pl.ds(b, tl, stride=SUBLANES), :] = hu[:, q * LANES:(q + 1) * LANES]


def _in_proj(x, w_in, tl=512):
    B, L, D = x.shape
    n_l = L // tl
    w = w_in.astype(BF16)
    wqt = w[:, :QK_WIDTH].T
    wk = w[:, QK_WIDTH:2 * QK_WIDTH]
    wvt = w[:, 2 * QK_WIDTH:2 * QK_WIDTH + ATT_WIDTH].T
    wu = w[:, 2 * QK_WIDTH + ATT_WIDTH:]
    const = lambda l, b: (0, 0)
    slab = pl.BlockSpec((1, HEADS, tl // TB, LANES, TB), lambda l, b: (b, 0, l, 0, 0))
    slab_shape = jax.ShapeDtypeStruct((B, HEADS, L // TB, LANES, TB), BF16)
    return pl.pallas_call(
        functools.partial(_in_proj_kernel, tl=tl),
        grid=(n_l, B),
        in_specs=[
            pl.BlockSpec((1, tl, D), lambda l, b: (b, l, 0)),
            pl.BlockSpec((QK_WIDTH, D), const),
            pl.BlockSpec((D, QK_WIDTH), const),
            pl.BlockSpec((ATT_WIDTH, D), const),
            pl.BlockSpec((D, SSM_WIDTH), const),
        ],
        out_specs=[
            slab,
            pl.BlockSpec((1, tl, QK_WIDTH), lambda l, b: (b, l, 0)),
            slab,
            pl.BlockSpec((LANE_TILES, tl * B, LANES), lambda l, b: (0, l, 0)),
        ],
        out_shape=[
            slab_shape,
            jax.ShapeDtypeStruct((B, L, QK_WIDTH), BF16),
            slab_shape,
            jax.ShapeDtypeStruct((LANE_TILES, L * B, LANES), F32),
        ],
        compiler_params=_cparams(("arbitrary", "arbitrary")),
        name="in_proj",
    )(x, wqt, wk, wvt, wu)


def _attn_kernel(lq1_ref, lk1_ref, lq2_ref, lk2_ref, g_ref, qt_ref, k_ref, vt_ref, o_ref,
                 acc_sc, qst_sc, s_sc, p_sc, bm_sc, *, lam_init):
    qi = pl.program_id(2)
    lam = (jnp.exp(jnp.sum(lq1_ref[...] * lk1_ref[...], axis=-1, keepdims=True))
           - jnp.exp(jnp.sum(lq2_ref[...] * lk2_ref[...], axis=-1, keepdims=True)) + lam_init)

    for hh in range(HEADS_PER_STEP):
        qt = qt_ref[0, hh, 0]
        row = lax.broadcasted_iota(jnp.int32, qt.shape, 0)
        zero = jnp.zeros_like(qt)
        qst_sc[hh, :, :TB] = jnp.where(row < QK_DIM, qt, zero)
        qst_sc[hh, :, TB:] = jnp.where(row >= QK_DIM, qt, zero)

    acc_sc[...] = jnp.zeros(acc_sc.shape, F32)
    p_sc[1] = jnp.zeros(p_sc.shape[1:], BF16)
    n_strips = 2 * TB // LANES
    n_blocks = vt_ref.shape[2]
    heads = range(HEADS_PER_STEP)

    def scores(j, slot):
        j = jnp.minimum(j, n_blocks - 1)
        rows = pl.ds(pl.multiple_of(j * TB, TB), TB)
        for hh in heads:
            kb = k_ref[0, rows, hh * LANES:(hh + 1) * LANES]
            s = jnp.dot(kb, qst_sc[hh], preferred_element_type=F32)
            s_sc[slot, hh] = s
            bm_sc[slot, hh] = jnp.max(s, axis=0, keepdims=True)

    def values(j, slot, carry):
        j = jnp.clip(j, 0, n_blocks - 1)
        for hh in heads:
            pv = jnp.dot(vt_ref[0, hh, j], p_sc[slot, hh], preferred_element_type=F32)
            acc_sc[hh] = carry[hh][2] * acc_sc[hh] + pv

    def softmax(j, slot, carry, masked):
        out = []
        for hh in heads:
            m, l, _ = carry[hh]
            m_new, l_new, alpha = [], [], []
            for c in range(n_strips):
                lanes = slice(c * LANES, (c + 1) * LANES)
                s = s_sc[slot, hh, :, lanes]
                if masked:
                    key = lax.broadcasted_iota(jnp.int32, s.shape, 0) + (j - qi) * TB
                    qry = lax.broadcasted_iota(jnp.int32, s.shape, 1) + (c * LANES) % TB
                    s = jnp.where(key <= qry, s, -jnp.inf)
                    bm = jnp.max(s, axis=0, keepdims=True)
                else:
                    bm = bm_sc[slot, hh, :, lanes]
                mc = jnp.maximum(m[:, lanes], bm)
                ac = jnp.exp(m[:, lanes] - mc)
                p = jnp.exp(s - mc)
                l_new.append(ac * l[:, lanes] + jnp.sum(p, axis=0, keepdims=True))
                p_sc[slot, hh, :, lanes] = p.astype(BF16)
                m_new.append(mc)
                alpha.append(ac)
            out.append(tuple(jnp.concatenate(v, axis=1) for v in (m_new, l_new, alpha)))
        return tuple(out)

    def pair(t, carry, masked):
        a = 2 * t
        scores(a + 1, 1)
        values(a - 1, 1, carry)
        carry = softmax(a, 0, carry, masked)
        if not masked:
            scores(a + 2, 0)
        values(a, 0, carry)
        return softmax(a + 1, 1, carry, masked)

    row0 = lambda v: jnp.full((1, 2 * TB), v, F32)
    init = tuple((row0(-jnp.inf), row0(0.0), row0(1.0)) for _ in heads)
    scores(0, 0)
    n_fast = qi // 2
    carry = lax.fori_loop(0, n_fast, lambda t, c: pair(t, c, False), init)
    carry = pair(n_fast, carry, True)
    values(2 * n_fast + 1, 1, carry)

    for hh in range(HEADS_PER_STEP):
        o = acc_sc[hh] * (1.0 / carry[hh][1])
        o = o[:, :TB] - lam * o[:, TB:]
        ms = jnp.mean(o * o, axis=0, keepdims=True)
        o = o * lax.rsqrt(ms + LN_EPS) * (g_ref[...] * (1.0 - lam_init))
        o_ref[0, :, hh * LANES:(hh + 1) * LANES] = o.T.astype(o_ref.dtype)


def _diff_attn(qt, k, vt, lq1, lk1, lq2, lk2, subln_g, lam_init):
    B, L, _ = k.shape
    n_q = L // TB
    hp = HEADS_PER_STEP
    vec = pl.BlockSpec((1, QK_DIM), lambda b, h, i: (0, 0))
    return pl.pallas_call(
        functools.partial(_attn_kernel, lam_init=lam_init),
        grid=(B, HEADS // hp, n_q),
        in_specs=[
            vec, vec, vec, vec,
            pl.BlockSpec((V_DIM, 1), lambda b, h, i: (0, 0)),
            pl.BlockSpec((1, hp, 1, LANES, TB), lambda b, h, i: (b, h, i, 0, 0)),
            pl.BlockSpec((1, L, hp * LANES), lambda b, h, i: (b, 0, h)),
            pl.BlockSpec((1, hp, n_q, LANES, TB), lambda b, h, i: (b, h, 0, 0, 0)),
        ],
        out_specs=pl.BlockSpec((1, TB, hp * LANES), lambda b, h, i: (b, i, h)),
        out_shape=jax.ShapeDtypeStruct((B, L, ATT_WIDTH), BF16),
        scratch_shapes=[
            pltpu.VMEM((hp, V_DIM, 2 * TB), F32),
            pltpu.VMEM((hp, LANES, 2 * TB), BF16),
            pltpu.VMEM((2, hp, TB, 2 * TB), F32),
            pltpu.VMEM((2, hp, TB, 2 * TB), BF16),
            pltpu.VMEM((2, hp, 1, 2 * TB), F32),
        ],
        compiler_params=_cparams(("parallel", "parallel", "arbitrary")),
        name="diff_attn",
    )(lq1, lk1, lq2, lk2, subln_g.reshape(V_DIM, 1), qt, k, vt)


def _block_transpose8(vs):
    lane_blk = lax.broadcasted_iota(jnp.int32, vs[0].shape, 1) // SSM_GROUP
    for d in (4, 2, 1):
        keep = (lane_blk & d) == 0
        new = list(vs)
        for a in range(8):
            if a & d == 0:
                lo, hi = vs[a], vs[a + d]
                new[a] = jnp.where(keep, lo, pltpu.roll(hi, SSM_GROUP * d, 1))
                new[a + d] = jnp.where(keep, pltpu.roll(lo, LANES - SSM_GROUP * d, 1), hi)
        vs = new
    return vs


def _ssm_kernel(u_ref, toep_ref, win_ref, wout_ref, a_ref, d_ref, y_ref, s_sc, x_sc, sp_sc, *, jt):
    rows = jt * SUBLANES

    @pl.when(pl.program_id(1) == 0)
    def _():
        s_sc[...] = jnp.zeros(s_sc.shape, F32)

    xs = [u_ref[0, :, r].reshape(rows, LANES) for r in range(CHUNK)]
    z_lo = _block_transpose8(xs[:8])
    z_hi = _block_transpose8(xs[8:])
    ug = [jnp.concatenate([z_lo[g], z_hi[g]], axis=1).astype(BF16) for g in range(GROUPS_PER_TILE)]

    y = [jnp.dot(ug[g], toep_ref[0, g], preferred_element_type=F32) for g in range(GROUPS_PER_TILE)]

    for m in range(PAIRS_PER_TILE):
        up = jnp.concatenate([ug[2 * m], ug[2 * m + 1]], axis=1)
        xp = jnp.dot(up, win_ref[0, m], preferred_element_type=F32)
        x_sc[0, :, m * LANES:(m + 1) * LANES] = xp[:, :LANES]
        x_sc[1, :, m * LANES:(m + 1) * LANES] = xp[:, LANES:]

    ar = a_ref[0, 0:1, :]
    ai = a_ref[0, 1:2, :]

    def step(j, carry):
        sr, si = carry
        r8 = pl.ds(pl.multiple_of(j * SUBLANES, SUBLANES), SUBLANES)
        sp_sc[0, r8, :] = sr
        sp_sc[1, r8, :] = si
        return (ar * sr - ai * si + x_sc[0, r8, :], ar * si + ai * sr + x_sc[1, r8, :])

    sr, si = lax.fori_loop(0, jt, step, (s_sc[0], s_sc[1]))
    s_sc[0] = sr
    s_sc[1] = si

    for m in range(PAIRS_PER_TILE):
        sp = jnp.concatenate([sp_sc[0, :, m * LANES:(m + 1) * LANES],
                              sp_sc[1, :, m * LANES:(m + 1) * LANES]], axis=1).astype(BF16)
        yc = jnp.dot(sp, wout_ref[0, m], preferred_element_type=F32)
        y[2 * m] = y[2 * m] + yc[:, :CHUNK_COLS]
        y[2 * m + 1] = y[2 * m + 1] + yc[:, CHUNK_COLS:]

    d = d_ref[0]
    for half in range(2):
        ws = _block_transpose8([y[g][:, half * LANES:(half + 1) * LANES] for g in range(GROUPS_PER_TILE)])
        for t8 in range(8):
            t = half * 8 + t8
            u_t = u_ref[0, :, t].reshape(rows, LANES)
            y_ref[0, :, t] = (ws[t8] + d * u_t).reshape(jt, SUBLANES, LANES)


def _ssm(u_tm, tables, d_skip, B, L, lt=512):
    toep, win, wout, a16 = tables
    jt = lt // CHUNK
    n_l = L // lt
    u5 = u_tm.reshape(LANE_TILES, L // CHUNK, CHUNK, B, LANES)
    rows = jt * B
    y5 = pl.pallas_call(
        functools.partial(_ssm_kernel, jt=jt),
        grid=(LANE_TILES, n_l),
        in_specs=[
            pl.BlockSpec((1, jt, CHUNK, B, LANES), lambda q, l: (q, l, 0, 0, 0)),
            pl.BlockSpec((1, GROUPS_PER_TILE, CHUNK_COLS, CHUNK_COLS), lambda q, l: (q, 0, 0, 0)),
            pl.BlockSpec((1, PAIRS_PER_TILE, 2 * CHUNK_COLS, 2 * LANES), lambda q, l: (q, 0, 0, 0)),
            pl.BlockSpec((1, PAIRS_PER_TILE, 2 * LANES, 2 * CHUNK_COLS), lambda q, l: (q, 0, 0, 0)),
            pl.BlockSpec((1, 2, STATE_COLS), lambda q, l: (q, 0, 0)),
            pl.BlockSpec((1, 1, LANES), lambda q, l: (q, 0, 0)),
        ],
        out_specs=pl.BlockSpec((1, jt, CHUNK, B, LANES), lambda q, l: (q, l, 0, 0, 0)),
        out_shape=jax.ShapeDtypeStruct(u5.shape, F32),
        scratch_shapes=[
            pltpu.VMEM((2, B, STATE_COLS), F32),
            pltpu.VMEM((2, rows, STATE_COLS), F32),
            pltpu.VMEM((2, rows, STATE_COLS), F32),
        ],
        compiler_params=_cparams(("parallel", "arbitrary")),
        name="ssm",
    )(u5, toep, win, wout, a16, d_skip.reshape(LANE_TILES, 1, LANES))
    return y5.reshape(LANE_TILES, L * B, LANES)


def _ssm_tables(lam_re, lam_im, log_dt, b_re, b_im, c_re, c_im):
    G, P, H = b_re.shape
    hi = lax.Precision.HIGHEST
    dt = jnp.exp(log_dt)[:, None]
    ab_re = jnp.exp(lam_re * dt) * jnp.cos(lam_im * dt)
    ab_im = jnp.exp(lam_re * dt) * jnp.sin(lam_im * dt)
    den = lam_re * lam_re + lam_im * lam_im
    zr, zi = ab_re - 1.0, ab_im
    fr = (zr * lam_re + zi * lam_im) / den
    fi = (zi * lam_re - zr * lam_im) / den
    bb_re = fr[..., None] * b_re - fi[..., None] * b_im
    bb_im = fr[..., None] * b_im + fi[..., None] * b_re
    tau = jnp.arange(CHUNK + 1, dtype=F32)[None, :, None]
    mag = jnp.exp(lam_re[:, None, :] * dt[:, :, None] * tau)
    ang = lam_im[:, None, :] * dt[:, :, None] * tau
    pw_re, pw_im = mag * jnp.cos(ang), mag * jnp.sin(ang)

    e_re = pw_re[:, :CHUNK, :, None] * bb_re[:, None] - pw_im[:, :CHUNK, :, None] * bb_im[:, None]
    e_im = pw_re[:, :CHUNK, :, None] * bb_im[:, None] + pw_im[:, :CHUNK, :, None] * bb_re[:, None]
    k = (jnp.einsum('gop,gtpi->gtoi', c_re, e_re, precision=hi)
         - jnp.einsum('gop,gtpi->gtoi', c_im, e_im, precision=hi))
    r_idx = jnp.arange(CHUNK)[:, None]
    t_idx = jnp.arange(CHUNK)[None, :]
    diff = t_idx - r_idx
    kt = jnp.take(k, jnp.clip(diff, 0, CHUNK - 1), axis=1)
    kt = jnp.where((diff >= 0)[None, :, :, None, None], kt, 0.0)
    toep = kt.transpose(0, 1, 4, 2, 3).reshape(G, CHUNK_COLS, CHUNK_COLS)

    pr = pw_re[:, CHUNK - 1::-1][:, :CHUNK]
    pi_ = pw_im[:, CHUNK - 1::-1][:, :CHUNK]
    win_re = (pr[:, :, None, :] * bb_re.transpose(0, 2, 1)[:, None] - pi_[:, :, None, :] * bb_im.transpose(0, 2, 1)[:, None])
    win_im = (pr[:, :, None, :] * bb_im.transpose(0, 2, 1)[:, None] + pi_[:, :, None, :] * bb_re.transpose(0, 2, 1)[:, None])
    win_re = win_re.reshape(G, CHUNK_COLS, P)
    win_im = win_im.reshape(G, CHUNK_COLS, P)

    qr, qi = pw_re[:, 1:], pw_im[:, 1:]
    cl_re = c_re[:, None] * qr[:, :, None, :] - c_im[:, None] * qi[:, :, None, :]
    cl_im = c_re[:, None] * qi[:, :, None, :] + c_im[:, None] * qr[:, :, None, :]
    wout_re = cl_re.reshape(G, CHUNK_COLS, P).transpose(0, 2, 1)
    wout_im = -cl_im.reshape(G, CHUNK_COLS, P).transpose(0, 2, 1)

    zw = jnp.zeros((G // 2, CHUNK_COLS, P), F32)
    wr0, wr1 = win_re[0::2], win_re[1::2]
    wi0, wi1 = win_im[0::2], win_im[1::2]
    win_pair = jnp.concatenate([
        jnp.concatenate([wr0, zw, wi0, zw], axis=2),
        jnp.concatenate([zw, wr1, zw, wi1], axis=2)], axis=1)
    zo = jnp.zeros((G // 2, P, CHUNK_COLS), F32)
    wout_pair = jnp.concatenate([
        jnp.concatenate([wout_re[0::2], zo], axis=2),
        jnp.concatenate([zo, wout_re[1::2]], axis=2),
        jnp.concatenate([wout_im[0::2], zo], axis=2),
        jnp.concatenate([zo, wout_im[1::2]], axis=2)], axis=1)

    a16 = jnp.stack([pw_re[:, CHUNK].reshape(LANE_TILES, STATE_COLS),
                     pw_im[:, CHUNK].reshape(LANE_TILES, STATE_COLS)], axis=1)
    toep = toep.reshape(LANE_TILES, GROUPS_PER_TILE, CHUNK_COLS, CHUNK_COLS).astype(BF16)
    win_pair = win_pair.reshape(LANE_TILES, PAIRS_PER_TILE, 2 * CHUNK_COLS, 2 * LANES).astype(BF16)
    wout_pair = wout_pair.reshape(LANE_TILES, PAIRS_PER_TILE, 2 * LANES, 2 * CHUNK_COLS).astype(BF16)
    return toep, win_pair, wout_pair, a16


def _layer_norm(r, g, b):
    mu = jnp.mean(r, axis=-1, keepdims=True)
    c = r - mu
    var = jnp.mean(c * c, axis=-1, keepdims=True)
    return c * lax.rsqrt(var + LN_EPS) * g + b


def _gelu_tanh(y):
    return 0.5 * y * (1.0 + jnp.tanh(math.sqrt(2.0 / math.pi) * (y + 0.044715 * (y * y * y))))


def _mix_ln1_kernel(x_ref, attn_ref, y_ref, wglu_ref, bglu_ref, wo_ref, g_ref, b_ref, o_ref, *, tl):
    b = pl.program_id(1)
    y = jnp.concatenate([y_ref[q, pl.ds(b, tl, stride=SUBLANES), :] for q in range(LANE_TILES)], axis=1)
    gl = _gelu_tanh(y)
    z = gl * jax.nn.sigmoid(jnp.dot(gl.astype(BF16), wglu_ref[...], preferred_element_type=F32) + bglu_ref[...])
    mix = (jnp.dot(attn_ref[0], wo_ref[:ATT_WIDTH], preferred_element_type=F32)
           + jnp.dot(z.astype(BF16), wo_ref[ATT_WIDTH:], preferred_element_type=F32))
    o_ref[0] = _layer_norm(ALPHA * x_ref[0] + mix, g_ref[...], b_ref[...])


def _mix_ln1(x, attn, y_tm, wglu_bf, b_glu, wo_bf, ln_g, ln_b, tl=512):
    B, L, D = x.shape
    n_l = L // tl
    const = lambda l, b: (0, 0)
    return pl.pallas_call(
        functools.partial(_mix_ln1_kernel, tl=tl),
        grid=(n_l, B),
        in_specs=[
            pl.BlockSpec((1, tl, D), lambda l, b: (b, l, 0)),
            pl.BlockSpec((1, tl, ATT_WIDTH), lambda l, b: (b, l, 0)),
            pl.BlockSpec((LANE_TILES, tl * B, LANES), lambda l, b: (0, l, 0)),
            pl.BlockSpec((SSM_WIDTH, SSM_WIDTH), const),
            pl.BlockSpec((1, SSM_WIDTH), const),
            pl.BlockSpec((D, D), const),
            pl.BlockSpec((1, D), const),
            pl.BlockSpec((1, D), const),
        ],
        out_specs=pl.BlockSpec((1, tl, D), lambda l, b: (b, l, 0)),
        out_shape=jax.ShapeDtypeStruct((B, L, D), F32),
        compiler_params=_cparams(("parallel", "arbitrary")),
        name="mix_ln1",
    )(x, attn, y_tm, wglu_bf, b_glu, wo_bf, ln_g, ln_b)


HALO = 16


def _ffn_kernel(x_ref, xh_ref, p_ref, wple_ref, wgate_ref, wg_ref, wv_ref, cwg_ref, cwv_ref, cbg_ref, cbv_ref,
                wd_ref, g_ref, b_ref, o_ref, xb_sc, acc_sc, hg_sc, hv_sc, *, tm, tiles_per_seq):
    i = pl.program_id(0)
    f = pl.program_id(1)

    @pl.when(f == 0)
    def _():
        keep = jnp.where(i % tiles_per_seq == 0, 0.0, 1.0)
        xb_sc[0:HALO] = (xh_ref[...] * keep).astype(BF16)
        xb_sc[HALO:] = x_ref[...].astype(BF16)
        gate = jax.nn.sigmoid(jnp.dot(xb_sc[HALO:], wgate_ref[...], preferred_element_type=F32))
        ple = jnp.dot(p_ref[...].astype(BF16), wple_ref[...], preferred_element_type=F32) * gate
        acc_sc[...] = ALPHA * x_ref[...] + ple

    xb = xb_sc[...]
    hg_sc[...] = jnp.dot(xb, wg_ref[...], preferred_element_type=F32)
    hv_sc[...] = jnp.dot(xb, wv_ref[...], preferred_element_type=F32)

    def conv(h_sc, w_ref, b_ref):
        out = b_ref[...] + w_ref[CONV_WIDTH - 1:CONV_WIDTH] * h_sc[HALO:]
        for j in range(CONV_WIDTH - 1):
            back = CONV_WIDTH - 1 - j
            out = out + w_ref[j:j + 1] * h_sc[HALO - back:HALO - back + tm]
        return out

    gc = conv(hg_sc, cwg_ref, cbg_ref)
    vc = conv(hv_sc, cwv_ref, cbv_ref)
    act = (gc * jax.nn.sigmoid(gc) * vc).astype(BF16)
    acc_sc[...] += jnp.dot(act, wd_ref[...], preferred_element_type=F32)

    @pl.when(f == pl.num_programs(1) - 1)
    def _():
        o_ref[...] = _layer_norm(acc_sc[...], g_ref[...], b_ref[...])


def _ffn_ln2(x1, p, wple_bf, wgate_bf, wup_bf, conv_w, conv_b, wdown_bf, ln_g, ln_b, L, tm=512, tf=1408):
    T, D = x1.shape
    n_t = T // tm
    n_f = D_FF // tf
    const = lambda i, f: (0, 0)
    return pl.pallas_call(
        functools.partial(_ffn_kernel, tm=tm, tiles_per_seq=L // tm),
        grid=(n_t, n_f),
        in_specs=[
            pl.BlockSpec((tm, D), lambda i, f: (i, 0)),
            pl.BlockSpec((HALO, D), lambda i, f: (jnp.maximum(i * (tm // HALO) - 1, 0), 0)),
            pl.BlockSpec((tm, PLE_DIM), lambda i, f: (i, 0)),
            pl.BlockSpec((PLE_DIM, D), const),
            pl.BlockSpec((D, D), const),
            pl.BlockSpec((D, tf), lambda i, f: (0, f)),
            pl.BlockSpec((D, tf), lambda i, f: (0, f + n_f)),
            pl.BlockSpec((CONV_WIDTH, tf), lambda i, f: (0, f)),
            pl.BlockSpec((CONV_WIDTH, tf), lambda i, f: (0, f + n_f)),
            pl.BlockSpec((1, tf), lambda i, f: (0, f)),
            pl.BlockSpec((1, tf), lambda i, f: (0, f + n_f)),
            pl.BlockSpec((tf, D), lambda i, f: (f, 0)),
            pl.BlockSpec((1, D), const),
            pl.BlockSpec((1, D), const),
        ],
        out_specs=pl.BlockSpec((tm, D), lambda i, f: (i, 0)),
        out_shape=jax.ShapeDtypeStruct((T, D), F32),
        scratch_shapes=[
            pltpu.VMEM((tm + HALO, D), BF16),
            pltpu.VMEM((tm, D), F32),
            pltpu.VMEM((tm + HALO, tf), F32),
            pltpu.VMEM((tm + HALO, tf), F32),
        ],
        compiler_params=_cparams(("parallel", "arbitrary")),
        name="ffn_ln2",
    )(x1, x1, p, wple_bf, wgate_bf, wup_bf, wup_bf, conv_w, conv_w, conv_b, conv_b, wdown_bf, ln_g, ln_b)


def kernel(x, p, w_in, diff_lambda_q1, diff_lambda_k1, diff_lambda_q2, diff_lambda_k2, diff_subln_g, ssm_lambda_re, ssm_lambda_im, ssm_log_dt, ssm_b_re, ssm_b_im, ssm_c_re, ssm_c_im, ssm_d, ssm_w_glu, ssm_b_glu, w_o, ln1_g, ln1_b, ffn_w_up, ffn_conv_w, ffn_conv_b, ffn_w_down, w_ple, w_ple_gate, ln2_g, ln2_b):
    B, L, D = x.shape
    assert B == SUBLANES and D == D_MODEL and L % 512 == 0
    for i in range(DEPTH):
        lam_init = 0.8 - 0.6 * math.exp(-0.3 * i)
        row = lambda a: a[i].reshape(1, -1)
        qt, k, vt, u_tm = _in_proj(x, w_in[i])
        attn = _diff_attn(qt, k, vt, row(diff_lambda_q1), row(diff_lambda_k1), row(diff_lambda_q2),
                          row(diff_lambda_k2), diff_subln_g[i], lam_init)
        tables = _ssm_tables(ssm_lambda_re[i], ssm_lambda_im[i], ssm_log_dt[i], ssm_b_re[i], ssm_b_im[i],
                             ssm_c_re[i], ssm_c_im[i])
        y_tm = _ssm(u_tm, tables, ssm_d[i], B, L)
        x1 = _mix_ln1(x, attn, y_tm, ssm_w_glu[i].astype(BF16), row(ssm_b_glu), w_o[i].astype(BF16),
                      row(ln1_g), row(ln1_b))
        x2 = _ffn_ln2(x1.reshape(B * L, D), p[i].reshape(B * L, PLE_DIM), w_ple[i].astype(BF16),
                      w_ple_gate[i].astype(BF16), ffn_w_up[i].astype(BF16), ffn_conv_w[i], row(ffn_conv_b),
                      ffn_w_down[i].astype(BF16), row(ln2_g), row(ln2_b), L)
        x = x2.reshape(B, L, D)
    return x
```

```python
import functools
import math

import jax
import jax.numpy as jnp
from jax import lax
from jax.experimental import pallas as pl
from jax.experimental.pallas import tpu as pltpu

F32 = jnp.float32
BF16 = jnp.bfloat16

SUBLANES = 8
LANES = 128

DEPTH = 1
D_MODEL = 1024
PLE_DIM = 256
HEADS = 4
QK_DIM = 64
V_DIM = 2 * QK_DIM
QK_WIDTH = HEADS * 2 * QK_DIM
ATT_WIDTH = HEADS * V_DIM
SSM_WIDTH = D_MODEL - ATT_WIDTH
SSM_GROUP = 16
SSM_GROUPS = SSM_WIDTH // SSM_GROUP
SSM_STATE = 64
D_FF = 2816
CONV_WIDTH = 3
LN_EPS = 1e-5
ALPHA = (2 * DEPTH) ** 0.25
QK_SCALE = QK_DIM ** -0.5

CHUNK = 16
CHUNK_COLS = CHUNK * SSM_GROUP
GROUPS_PER_TILE = LANES // SSM_GROUP
LANE_TILES = SSM_WIDTH // LANES
PAIRS_PER_TILE = GROUPS_PER_TILE // 2
STATE_COLS = GROUPS_PER_TILE * SSM_STATE

VMEM_LIMIT = 56 * 1024 * 1024


def _cparams(sem):
    return pltpu.CompilerParams(dimension_semantics=sem, vmem_limit_bytes=VMEM_LIMIT)


TB = 256
HEADS_PER_STEP = 2


def _in_proj_kernel(x_ref, wqt_ref, wk_ref, wvt_ref, wu_ref, qt_ref, k_ref, vt_ref, u_ref, *, tl):
    b = pl.program_id(1)
    xb = x_ref[0].astype(BF16)
    nt = (((1,), (1,)), ((), ()))
    qt = lax.dot_general(wqt_ref[...], xb, nt, preferred_element_type=F32) * QK_SCALE
    vt = lax.dot_general(wvt_ref[...], xb, nt, preferred_element_type=F32)
    for h in range(HEADS):
        for c in range(tl // TB):
            qt_ref[0, h, c] = qt[h * LANES:(h + 1) * LANES, c * TB:(c + 1) * TB].astype(BF16)
            vt_ref[0, h, c] = vt[h * LANES:(h + 1) * LANES, c * TB:(c + 1) * TB].astype(BF16)
    k_ref[0] = jnp.dot(xb, wk_ref[...], preferred_element_type=F32).astype(BF16)
    hu = jnp.dot(xb, wu_ref[...], preferred_element_type=F32)
    for q in range(LANE_TILES):
        u_ref[q, pl.ds(b, tl, stride=SUBLANES), :] = hu[:, q * LANES:(q + 1) * LANES]


def _in_proj(x, w_in, tl=512):
    B, L, D = x.shape
    n_l = L // tl
    w = w_in.astype(BF16)
    wqt = w[:, :QK_WIDTH].T
    wk = w[:, QK_WIDTH:2 * QK_WIDTH]
    wvt = w[:, 2 * QK_WIDTH:2 * QK_WIDTH + ATT_WIDTH].T
    wu = w[:, 2 * QK_WIDTH + ATT_WIDTH:]
    const = lambda l, b: (0, 0)
    slab = pl.BlockSpec((1, HEADS, tl // TB, LANES, TB), lambda l, b: (b, 0, l, 0, 0))
    slab_shape = jax.ShapeDtypeStruct((B, HEADS, L // TB, LANES, TB), BF16)
    return pl.pallas_call(
        functools.partial(_in_proj_kernel, tl=tl),
        grid=(n_l, B),
        in_specs=[
            pl.BlockSpec((1, tl, D), lambda l, b: (b, l, 0)),
            pl.BlockSpec((QK_WIDTH, D), const),
            pl.BlockSpec((D, QK_WIDTH), const),
            pl.BlockSpec((ATT_WIDTH, D), const),
            pl.BlockSpec((D, SSM_WIDTH), const),
        ],
        out_specs=[
            slab,
            pl.BlockSpec((1, tl, QK_WIDTH), lambda l, b: (b, l, 0)),
            slab,
            pl.BlockSpec((LANE_TILES, tl * B, LANES), lambda l, b: (0, l, 0)),
        ],
        out_shape=[
            slab_shape,
            jax.ShapeDtypeStruct((B, L, QK_WIDTH), BF16),
            slab_shape,
            jax.ShapeDtypeStruct((LANE_TILES, L * B, LANES), F32),
        ],
        compiler_params=_cparams(("arbitrary", "arbitrary")),
        name="in_proj",
    )(x, wqt, wk, wvt, wu)


def _attn_kernel(lq1_ref, lk1_ref, lq2_ref, lk2_ref, g_ref, qt_ref, k_ref, vt_ref, o_ref,
                 acc_sc, qst_sc, s_sc, p_sc, bm_sc, l_sc, *, lam_init):
    qi = pl.program_id(2)
    lam = (jnp.exp(jnp.sum(lq1_ref[...] * lk1_ref[...], axis=-1, keepdims=True))
           - jnp.exp(jnp.sum(lq2_ref[...] * lk2_ref[...], axis=-1, keepdims=True)) + lam_init)

    for hh in range(HEADS_PER_STEP):
        qt = qt_ref[0, hh, 0]
        row = lax.broadcasted_iota(jnp.int32, qt.shape, 0)
        zero = jnp.zeros_like(qt)
        qst_sc[hh, :, :TB] = jnp.where(row < QK_DIM, qt, zero)
        qst_sc[hh, :, TB:] = jnp.where(row >= QK_DIM, qt, zero)

    acc_sc[...] = jnp.zeros(acc_sc.shape, F32)
    p_sc[1] = jnp.zeros(p_sc.shape[1:], BF16)
    n_strips = 2 * TB // LANES
    heads = range(HEADS_PER_STEP)
    assert TB == 2 * LANES

    def scores(j, slot):
        rows = pl.ds(pl.multiple_of(j * TB, TB), TB)
        for hh in heads:
            kb = k_ref[0, rows, hh * LANES:(hh + 1) * LANES]
            s = jnp.dot(kb, qst_sc[hh], preferred_element_type=F32)
            s_sc[slot, hh] = s
            bm_sc[slot, hh] = jnp.max(s, axis=0, keepdims=True)

    def values(j, slot, carry):
        j = jnp.maximum(j, 0)
        for hh in heads:
            pv = jnp.dot(vt_ref[0, hh, j], p_sc[slot, hh], preferred_element_type=F32)
            acc_sc[hh] = carry[hh][2] * acc_sc[hh] + pv

    def softmax(slot, carry, diagonal):
        out = []
        for hh in heads:
            m, l, _ = carry[hh]
            m_new, l_new, alpha = [], [], []
            for c in range(n_strips):
                lanes = slice(c * LANES, (c + 1) * LANES)
                if not diagonal:
                    mc = jnp.maximum(m[:, lanes], bm_sc[slot, hh, :, lanes])
                    p = jnp.exp(s_sc[slot, hh, :, lanes] - mc)
                    p_sc[slot, hh, :, lanes] = p.astype(BF16)
                    psum = jnp.sum(p, axis=0, keepdims=True)
                else:
                    q0 = (c * LANES) % TB
                    s = s_sc[slot, hh, :, lanes]
                    key = lax.broadcasted_iota(jnp.int32, s.shape, 0)
                    qry = lax.broadcasted_iota(jnp.int32, s.shape, 1) + q0
                    s = jnp.where(key <= qry, s, -jnp.inf)
                    mc = jnp.maximum(m[:, lanes], jnp.max(s, axis=0, keepdims=True))
                    p = jnp.exp(s - mc)
                    p_sc[slot, hh, :, lanes] = p.astype(BF16)
                    psum = jnp.sum(p, axis=0, keepdims=True)
                ac = jnp.exp(m[:, lanes] - mc)
                l_new.append(ac * l[:, lanes] + psum)
                m_new.append(mc)
                alpha.append(ac)
            out.append(tuple(jnp.concatenate(v, axis=1) for v in (m_new, l_new, alpha)))
        return tuple(out)

    def pair(t, carry):
        a = 2 * t
        scores(a + 1, 1)
        values(a - 1, 1, carry)
        carry = softmax(0, carry, False)
        scores(a + 2, 0)
        values(a, 0, carry)
        return softmax(1, carry, False)

    row0 = lambda v: jnp.full((1, 2 * TB), v, F32)
    init = tuple((row0(-jnp.inf), row0(0.0), row0(1.0)) for _ in heads)
    scores(0, 0)
    carry = lax.fori_loop(0, qi // 2, pair, init)

    @pl.when(qi % 2 == 1)
    def _():
        scores(qi, 1)
        values(qi - 2, 1, carry)
        c = softmax(0, carry, False)
        values(qi - 1, 0, c)
        c = softmax(1, c, True)
        values(qi, 1, c)
        for hh in heads:
            l_sc[hh] = c[hh][1]

    @pl.when(qi % 2 == 0)
    def _():
        values(qi - 1, 1, carry)
        c = softmax(0, carry, True)
        values(qi, 0, c)
        for hh in heads:
            l_sc[hh] = c[hh][1]

    l_fin = [l_sc[hh] for hh in heads]

    for hh in range(HEADS_PER_STEP):
        o = acc_sc[hh] * (1.0 / l_fin[hh])
        o = o[:, :TB] - lam * o[:, TB:]
        ms = jnp.mean(o * o, axis=0, keepdims=True)
        o = o * lax.rsqrt(ms + LN_EPS) * (g_ref[...] * (1.0 - lam_init))
        o_ref[0, :, hh * LANES:(hh + 1) * LANES] = o.T.astype(o_ref.dtype)


def _diff_attn(qt, k, vt, lq1, lk1, lq2, lk2, subln_g, lam_init):
    B, L, _ = k.shape
    n_q = L // TB
    hp = HEADS_PER_STEP
    vec = pl.BlockSpec((1, QK_DIM), lambda b, h, i: (0, 0))
    return pl.pallas_call(
        functools.partial(_attn_kernel, lam_init=lam_init),
        grid=(B, HEADS // hp, n_q),
        in_specs=[
            vec, vec, vec, vec,
            pl.BlockSpec((V_DIM, 1), lambda b, h, i: (0, 0)),
            pl.BlockSpec((1, hp, 1, LANES, TB), lambda b, h, i: (b, h, i, 0, 0)),
            pl.BlockSpec((1, L, hp * LANES), lambda b, h, i: (b, 0, h)),
            pl.BlockSpec((1, hp, n_q, LANES, TB), lambda b, h, i: (b, h, 0, 0, 0)),
        ],
        out_specs=pl.BlockSpec((1, TB, hp * LANES), lambda b, h, i: (b, i, h)),
        out_shape=jax.ShapeDtypeStruct((B, L, ATT_WIDTH), BF16),
        scratch_shapes=[
            pltpu.VMEM((hp, V_DIM, 2 * TB), F32),
            pltpu.VMEM((hp, LANES, 2 * TB), BF16),
            pltpu.VMEM((2, hp, TB, 2 * TB), F32),
            pltpu.VMEM((2, hp, TB, 2 * TB), BF16),
            pltpu.VMEM((2, hp, 1, 2 * TB), F32),
            pltpu.VMEM((hp, 1, 2 * TB), F32),
        ],
        compiler_params=_cparams(("parallel", "parallel", "arbitrary")),
        name="diff_attn",
    )(lq1, lk1, lq2, lk2, subln_g.reshape(V_DIM, 1), qt, k, vt)


def _block_transpose8(vs):
    lane_blk = lax.broadcasted_iota(jnp.int32, vs[0].shape, 1) // SSM_GROUP
    for d in (4, 2, 1):
        keep = (lane_blk & d) == 0
        new = list(vs)
        for a in range(8):
            if a & d == 0:
                lo, hi = vs[a], vs[a + d]
                new[a] = jnp.where(keep, lo, pltpu.roll(hi, SSM_GROUP * d, 1))
                new[a + d] = jnp.where(keep, pltpu.roll(lo, LANES - SSM_GROUP * d, 1), hi)
        vs = new
    return vs


def _ssm_kernel(u_ref, toep_ref, win_ref, wout_ref, a_ref, d_ref, y_ref, s_sc, x_sc, sp_sc, *, jt):
    rows = jt * SUBLANES

    @pl.when(pl.program_id(1) == 0)
    def _():
        s_sc[...] = jnp.zeros(s_sc.shape, F32)

    xs = [u_ref[0, :, r].reshape(rows, LANES) for r in range(CHUNK)]
    z_lo = _block_transpose8(xs[:8])
    z_hi = _block_transpose8(xs[8:])
    ug = [jnp.concatenate([z_lo[g], z_hi[g]], axis=1).astype(BF16) for g in range(GROUPS_PER_TILE)]

    y = [jnp.dot(ug[g], toep_ref[0, g], preferred_element_type=F32) for g in range(GROUPS_PER_TILE)]

    for m in range(PAIRS_PER_TILE):
        up = jnp.concatenate([ug[2 * m], ug[2 * m + 1]], axis=1)
        xp = jnp.dot(up, win_ref[0, m], preferred_element_type=F32)
        x_sc[0, :, m * LANES:(m + 1) * LANES] = xp[:, :LANES]
        x_sc[1, :, m * LANES:(m + 1) * LANES] = xp[:, LANES:]

    ar = a_ref[0, 0:1, :]
    ai = a_ref[0, 1:2, :]

    def step(j, carry):
        sr, si = carry
        r8 = pl.ds(pl.multiple_of(j * SUBLANES, SUBLANES), SUBLANES)
        sp_sc[0, r8, :] = sr
        sp_sc[1, r8, :] = si
        return (ar * sr - ai * si + x_sc[0, r8, :], ar * si + ai * sr + x_sc[1, r8, :])

    sr, si = lax.fori_loop(0, jt, step, (s_sc[0], s_sc[1]))
    s_sc[0] = sr
    s_sc[1] = si

    for m in range(PAIRS_PER_TILE):
        sp = jnp.concatenate([sp_sc[0, :, m * LANES:(m + 1) * LANES],
                              sp_sc[1, :, m * LANES:(m + 1) * LANES]], axis=1).astype(BF16)
        yc = jnp.dot(sp, wout_ref[0, m], preferred_element_type=F32)
        y[2 * m] = y[2 * m] + yc[:, :CHUNK_COLS]
        y[2 * m + 1] = y[2 * m + 1] + yc[:, CHUNK_COLS:]

    d = d_ref[0]
    for half in range(2):
        ws = _block_transpose8([y[g][:, half * LANES:(half + 1) * LANES] for g in range(GROUPS_PER_TILE)])
        for t8 in range(8):
            t = half * 8 + t8
            u_t = u_ref[0, :, t].reshape(rows, LANES)
            y_ref[0, :, t] = (ws[t8] + d * u_t).reshape(jt, SUBLANES, LANES)


def _ssm(u_tm, tables, d_skip, B, L, lt=512):
    toep, win, wout, a16 = tables
    jt = lt // CHUNK
    n_l = L // lt
    u5 = u_tm.reshape(LANE_TILES, L // CHUNK, CHUNK, B, LANES)
    rows = jt * B
    y5 = pl.pallas_call(
        functools.partial(_ssm_kernel, jt=jt),
        grid=(LANE_TILES, n_l),
        in_specs=[
            pl.BlockSpec((1, jt, CHUNK, B, LANES), lambda q, l: (q, l, 0, 0, 0)),
            pl.BlockSpec((1, GROUPS_PER_TILE, CHUNK_COLS, CHUNK_COLS), lambda q, l: (q, 0, 0, 0)),
            pl.BlockSpec((1, PAIRS_PER_TILE, 2 * CHUNK_COLS, 2 * LANES), lambda q, l: (q, 0, 0, 0)),
            pl.BlockSpec((1, PAIRS_PER_TILE, 2 * LANES, 2 * CHUNK_COLS), lambda q, l: (q, 0, 0, 0)),
            pl.BlockSpec((1, 2, STATE_COLS), lambda q, l: (q, 0, 0)),
            pl.BlockSpec((1, 1, LANES), lambda q, l: (q, 0, 0)),
        ],
        out_specs=pl.BlockSpec((1, jt, CHUNK, B, LANES), lambda q, l: (q, l, 0, 0, 0)),
        out_shape=jax.ShapeDtypeStruct(u5.shape, F32),
        scratch_shapes=[
            pltpu.VMEM((2, B, STATE_COLS), F32),
            pltpu.VMEM((2, rows, STATE_COLS), F32),
            pltpu.VMEM((2, rows, STATE_COLS), F32),
        ],
        compiler_params=_cparams(("parallel", "arbitrary")),
        name="ssm",
    )(u5, toep, win, wout, a16, d_skip.reshape(LANE_TILES, 1, LANES))
    return y5.reshape(LANE_TILES, L * B, LANES)


def _ssm_tables(lam_re, lam_im, log_dt, b_re, b_im, c_re, c_im):
    G, P, H = b_re.shape
    hi = lax.Precision.HIGHEST
    dt = jnp.exp(log_dt)[:, None]
    ab_re = jnp.exp(lam_re * dt) * jnp.cos(lam_im * dt)
    ab_im = jnp.exp(lam_re * dt) * jnp.sin(lam_im * dt)
    den = lam_re * lam_re + lam_im * lam_im
    zr, zi = ab_re - 1.0, ab_im
    fr = (zr * lam_re + zi * lam_im) / den
    fi = (zi * lam_re - zr * lam_im) / den
    bb_re = fr[..., None] * b_re - fi[..., None] * b_im
    bb_im = fr[..., None] * b_im + fi[..., None] * b_re
    tau = jnp.arange(CHUNK + 1, dtype=F32)[None, :, None]
    mag = jnp.exp(lam_re[:, None, :] * dt[:, :, None] * tau)
    ang = lam_im[:, None, :] * dt[:, :, None] * tau
    pw_re, pw_im = mag * jnp.cos(ang), mag * jnp.sin(ang)

    e_re = pw_re[:, :CHUNK, :, None] * bb_re[:, None] - pw_im[:, :CHUNK, :, None] * bb_im[:, None]
    e_im = pw_re[:, :CHUNK, :, None] * bb_im[:, None] + pw_im[:, :CHUNK, :, None] * bb_re[:, None]
    k = (jnp.einsum('gop,gtpi->gtoi', c_re, e_re, precision=hi)
         - jnp.einsum('gop,gtpi->gtoi', c_im, e_im, precision=hi))
    r_idx = jnp.arange(CHUNK)[:, None]
    t_idx = jnp.arange(CHUNK)[None, :]
    diff = t_idx - r_idx
    kt = jnp.take(k, jnp.clip(diff, 0, CHUNK - 1), axis=1)
    kt = jnp.where((diff >= 0)[None, :, :, None, None], kt, 0.0)
    toep = kt.transpose(0, 1, 4, 2, 3).reshape(G, CHUNK_COLS, CHUNK_COLS)

    pr = pw_re[:, CHUNK - 1::-1][:, :CHUNK]
    pi_ = pw_im[:, CHUNK - 1::-1][:, :CHUNK]
    win_re = (pr[:, :, None, :] * bb_re.transpose(0, 2, 1)[:, None] - pi_[:, :, None, :] * bb_im.transpose(0, 2, 1)[:, None])
    win_im = (pr[:, :, None, :] * bb_im.transpose(0, 2, 1)[:, None] + pi_[:, :, None, :] * bb_re.transpose(0, 2, 1)[:, None])
    win_re = win_re.reshape(G, CHUNK_COLS, P)
    win_im = win_im.reshape(G, CHUNK_COLS, P)

    qr, qi = pw_re[:, 1:], pw_im[:, 1:]
    cl_re = c_re[:, None] * qr[:, :, None, :] - c_im[:, None] * qi[:, :, None, :]
    cl_im = c_re[:, None] * qi[:, :, None, :] + c_im[:, None] * qr[:, :, None, :]
    wout_re = cl_re.reshape(G, CHUNK_COLS, P).transpose(0, 2, 1)
    wout_im = -cl_im.reshape(G, CHUNK_COLS, P).transpose(0, 2, 1)

    zw = jnp.zeros((G // 2, CHUNK_COLS, P), F32)
    wr0, wr1 = win_re[0::2], win_re[1::2]
    wi0, wi1 = win_im[0::2], win_im[1::2]
    win_pair = jnp.concatenate([
        jnp.concatenate([wr0, zw, wi0, zw], axis=2),
        jnp.concatenate([zw, wr1, zw, wi1], axis=2)], axis=1)
    zo = jnp.zeros((G // 2, P, CHUNK_COLS), F32)
    wout_pair = jnp.concatenate([
        jnp.concatenate([wout_re[0::2], zo], axis=2),
        jnp.concatenate([zo, wout_re[1::2]], axis=2),
        jnp.concatenate([wout_im[0::2], zo], axis=2),
        jnp.concatenate([zo, wout_im[1::2]], axis=2)], axis=1)

    a16 = jnp.stack([pw_re[:, CHUNK].reshape(LANE_TILES, STATE_COLS),
                     pw_im[:, CHUNK].reshape(LANE_TILES, STATE_COLS)], axis=1)
    toep = toep.reshape(LANE_TILES, GROUPS_PER_TILE, CHUNK_COLS, CHUNK_COLS).astype(BF16)
    win_pair = win_pair.reshape(LANE_TILES, PAIRS_PER_TILE, 2 * CHUNK_COLS, 2 * LANES).astype(BF16)
    wout_pair = wout_pair.reshape(LANE_TILES, PAIRS_PER_TILE, 2 * LANES, 2 * CHUNK_COLS).astype(BF16)
    return toep, win_pair, wout_pair, a16


def _layer_norm(r, g, b):
    mu = jnp.mean(r, axis=-1, keepdims=True)
    c = r - mu
    var = jnp.mean(c * c, axis=-1, keepdims=True)
    return c * lax.rsqrt(var + LN_EPS) * g + b


def _gelu_tanh(y):
    return 0.5 * y * (1.0 + jnp.tanh(math.sqrt(2.0 / math.pi) * (y + 0.044715 * (y * y * y))))


def _mix_ln1_kernel(x_ref, attn_ref, y_ref, wglu_ref, bglu_ref, wo_ref, g_ref, b_ref, o_ref, *, tl):
    b = pl.program_id(1)
    y = jnp.concatenate([y_ref[q, pl.ds(b, tl, stride=SUBLANES), :] for q in range(LANE_TILES)], axis=1)
    gl = _gelu_tanh(y)
    z = gl * jax.nn.sigmoid(jnp.dot(gl.astype(BF16), wglu_ref[...], preferred_element_type=F32) + bglu_ref[...])
    mix = (jnp.dot(attn_ref[0], wo_ref[:ATT_WIDTH], preferred_element_type=F32)
           + jnp.dot(z.astype(BF16), wo_ref[ATT_WIDTH:], preferred_element_type=F32))
    o_ref[0] = _layer_norm(ALPHA * x_ref[0] + mix, g_ref[...], b_ref[...])


def _mix_ln1(x, attn, y_tm, wglu_bf, b_glu, wo_bf, ln_g, ln_b, tl=512):
    B, L, D = x.shape
    n_l = L // tl
    const = lambda l, b: (0, 0)
    return pl.pallas_call(
        functools.partial(_mix_ln1_kernel, tl=tl),
        grid=(n_l, B),
        in_specs=[
            pl.BlockSpec((1, tl, D), lambda l, b: (b, l, 0)),
            pl.BlockSpec((1, tl, ATT_WIDTH), lambda l, b: (b, l, 0)),
            pl.BlockSpec((LANE_TILES, tl * B, LANES), lambda l, b: (0, l, 0)),
            pl.BlockSpec((SSM_WIDTH, SSM_WIDTH), const),
            pl.BlockSpec((1, SSM_WIDTH), const),
            pl.BlockSpec((D, D), const),
            pl.BlockSpec((1, D), const),
            pl.BlockSpec((1, D), const),
        ],
        out_specs=pl.BlockSpec((1, tl, D), lambda l, b: (b, l, 0)),
        out_shape=jax.ShapeDtypeStruct((B, L, D), F32),
        compiler_params=_cparams(("parallel", "arbitrary")),
        name="mix_ln1",
    )(x, attn, y_tm, wglu_bf, b_glu, wo_bf, ln_g, ln_b)


HALO = 16


def _ffn_kernel(x_ref, xh_ref, p_ref, wple_ref, wgate_ref, wg_ref, wv_ref, cwg_ref, cwv_ref, cbg_ref, cbv_ref,
                wd_ref, g_ref, b_ref, o_ref, xb_sc, acc_sc, hg_sc, hv_sc, *, tm, tiles_per_seq):
    i = pl.program_id(0)
    f = pl.program_id(1)

    @pl.when(f == 0)
    def _():
        keep = jnp.where(i % tiles_per_seq == 0, 0.0, 1.0)
        xb_sc[0:HALO] = (xh_ref[...] * keep).astype(BF16)
        xb_sc[HALO:] = x_ref[...].astype(BF16)
        gate = jax.nn.sigmoid(jnp.dot(xb_sc[HALO:], wgate_ref[...], preferred_element_type=F32))
        ple = jnp.dot(p_ref[...].astype(BF16), wple_ref[...], preferred_element_type=F32) * gate
        acc_sc[...] = ALPHA * x_ref[...] + ple

    xb = xb_sc[...]
    hg_sc[...] = jnp.dot(xb, wg_ref[...], preferred_element_type=F32)
    hv_sc[...] = jnp.dot(xb, wv_ref[...], preferred_element_type=F32)

    def conv(h_sc, w_ref, b_ref):
        out = b_ref[...] + w_ref[CONV_WIDTH - 1:CONV_WIDTH] * h_sc[HALO:]
        for j in range(CONV_WIDTH - 1):
            back = CONV_WIDTH - 1 - j
            out = out + w_ref[j:j + 1] * h_sc[HALO - back:HALO - back + tm]
        return out

    gc = conv(hg_sc, cwg_ref, cbg_ref)
    vc = conv(hv_sc, cwv_ref, cbv_ref)
    act = (gc * jax.nn.sigmoid(gc) * vc).astype(BF16)
    acc_sc[...] += jnp.dot(act, wd_ref[...], preferred_element_type=F32)

    @pl.when(f == pl.num_programs(1) - 1)
    def _():
        o_ref[...] = _layer_norm(acc_sc[...], g_ref[...], b_ref[...])


def _ffn_ln2(x1, p, wple_bf, wgate_bf, wup_bf, conv_w, conv_b, wdown_bf, ln_g, ln_b, L, tm=512, tf=1408):
    T, D = x1.shape
    n_t = T // tm
    n_f = D_FF // tf
    const = lambda i, f: (0, 0)
    return pl.pallas_call(
        functools.partial(_ffn_kernel, tm=tm, tiles_per_seq=L // tm),
        grid=(n_t, n_f),
        in_specs=[
            pl.BlockSpec((tm, D), lambda i, f: (i, 0)),
            pl.BlockSpec((HALO, D), lambda i, f: (jnp.maximum(i * (tm // HALO) - 1, 0), 0)),
            pl.BlockSpec((tm, PLE_DIM), lambda i, f: (i, 0)),
            pl.BlockSpec((PLE_DIM, D), const),
            pl.BlockSpec((D, D), const),
            pl.BlockSpec((D, tf), lambda i, f: (0, f)),
            pl.BlockSpec((D, tf), lambda i, f: (0, f + n_f)),
            pl.BlockSpec((CONV_WIDTH, tf), lambda i, f: (0, f)),
            pl.BlockSpec((CONV_WIDTH, tf), lambda i, f: (0, f + n_f)),
            pl.BlockSpec((1, tf), lambda i, f: (0, f)),
            pl.BlockSpec((1, tf), lambda i, f: (0, f + n_f)),
            pl.BlockSpec((tf, D), lambda i, f: (f, 0)),
            pl.BlockSpec((1, D), const),
            pl.BlockSpec((1, D), const),
        ],
        out_specs=pl.BlockSpec((tm, D), lambda i, f: (i, 0)),
        out_shape=jax.ShapeDtypeStruct((T, D), F32),
        scratch_shapes=[
            pltpu.VMEM((tm + HALO, D), BF16),
            pltpu.VMEM((tm, D), F32),
            pltpu.VMEM((tm + HALO, tf), F32),
            pltpu.VMEM((tm + HALO, tf), F32),
        ],
        compiler_params=_cparams(("parallel", "arbitrary")),
        name="ffn_ln2",
    )(x1, x1, p, wple_bf, wgate_bf, wup_bf, wup_bf, conv_w, conv_w, conv_b, conv_b, wdown_bf, ln_g, ln_b)


def kernel(x, p, w_in, diff_lambda_q1, diff_lambda_k1, diff_lambda_q2, diff_lambda_k2, diff_subln_g, ssm_lambda_re, ssm_lambda_im, ssm_log_dt, ssm_b_re, ssm_b_im, ssm_c_re, ssm_c_im, ssm_d, ssm_w_glu, ssm_b_glu, w_o, ln1_g, ln1_b, ffn_w_up, ffn_conv_w, ffn_conv_b, ffn_w_down, w_ple, w_ple_gate, ln2_g, ln2_b):
    B, L, D = x.shape
    assert B == SUBLANES and D == D_MODEL and L % 512 == 0
    for i in range(DEPTH):
        lam_init = 0.8 - 0.6 * math.exp(-0.3 * i)
        row = lambda a: a[i].reshape(1, -1)
        qt, k, vt, u_tm = _in_proj(x, w_in[i])
        attn = _diff_attn(qt, k, vt, row(diff_lambda_q1), row(diff_lambda_k1), row(diff_lambda_q2),
                          row(diff_lambda_k2), diff_subln_g[i], lam_init)
        tables = _ssm_tables(ssm_lambda_re[i], ssm_lambda_im[i], ssm_log_dt[i], ssm_b_re[i], ssm_b_im[i],
                             ssm_c_re[i], ssm_c_im[i])
        y_tm = _ssm(u_tm, tables, ssm_d[i], B, L)
        x1 = _mix_ln1(x, attn, y_tm, ssm_w_glu[i].astype(BF16), row(ssm_b_glu), w_o[i].astype(BF16),
                      row(ln1_g), row(ln1_b))
        x2 = _ffn_ln2(x1.reshape(B * L, D), p[i].reshape(B * L, PLE_DIM), w_ple[i].astype(BF16),
                      w_ple_gate[i].astype(BF16), ffn_w_up[i].astype(BF16), ffn_conv_w[i], row(ffn_conv_b),
                      ffn_w_down[i].astype(BF16), row(ln2_g), row(ln2_b), L)
        x = x2.reshape(B, L, D)
    return x
```

```python
import functools
import math

import jax
import jax.numpy as jnp
from jax import lax
from jax.experimental import pallas as pl
from jax.experimental.pallas import tpu as pltpu

F32 = jnp.float32
BF16 = jnp.bfloat16

SUBLANES = 8
LANES = 128

DEPTH = 1
D_MODEL = 1024
PLE_DIM = 256
HEADS = 4
QK_DIM = 64
V_DIM = 2 * QK_DIM
QK_WIDTH = HEADS * 2 * QK_DIM
ATT_WIDTH = HEADS * V_DIM
SSM_WIDTH = D_MODEL - ATT_WIDTH
SSM_GROUP = 16
SSM_GROUPS = SSM_WIDTH // SSM_GROUP
SSM_STATE = 64
D_FF = 2816
CONV_WIDTH = 3
LN_EPS = 1e-5
ALPHA = (2 * DEPTH) ** 0.25
QK_SCALE = QK_DIM ** -0.5

CHUNK = 16
CHUNK_COLS = CHUNK * SSM_GROUP
GROUPS_PER_TILE = LANES // SSM_GROUP
LANE_TILES = SSM_WIDTH // LANES
PAIRS_PER_TILE = GROUPS_PER_TILE // 2
STATE_COLS = GROUPS_PER_TILE * SSM_STATE

VMEM_LIMIT = 56 * 1024 * 1024


def _cparams(sem):
    return pltpu.CompilerParams(dimension_semantics=sem, vmem_limit_bytes=VMEM_LIMIT)


TB = 256
HEADS_PER_STEP = 2


def _in_proj_kernel(x_ref, wqt_ref, wk_ref, wvt_ref, wu_ref, qt_ref, k_ref, vt_ref, u_ref, *, tl):
    b = pl.program_id(1)
    xb = x_ref[0].astype(BF16)
    nt = (((1,), (1,)), ((), ()))
    qt = lax.dot_general(wqt_ref[...], xb, nt, preferred_element_type=F32) * QK_SCALE
    vt = lax.dot_general(wvt_ref[...], xb, nt, preferred_element_type=F32)
    for h in range(HEADS):
        for c in range(tl // TB):
            qt_ref[0, h, c] = qt[h * LANES:(h + 1) * LANES, c * TB:(c + 1) * TB].astype(BF16)
            vt_ref[0, h, c] = vt[h * LANES:(h + 1) * LANES, c * TB:(c + 1) * TB].astype(BF16)
    k_ref[0] = jnp.dot(xb, wk_ref[...], preferred_element_type=F32).astype(BF16)
    hu = jnp.dot(xb, wu_ref[...], preferred_element_type=F32)
    for q in range(LANE_TILES):
        u_ref[q, pl.ds(b, tl, stride=SUBLANES), :] = hu[:, q * LANES:(q + 1) * LANES]


def _in_proj(x, w_in, tl=512):
    B, L, D = x.shape
    n_l = L // tl
    w = w_in.astype(BF16)
    wqt = w[:, :QK_WIDTH].T
    wk = w[:, QK_WIDTH:2 * QK_WIDTH]
    wvt = w[:, 2 * QK_WIDTH:2 * QK_WIDTH + ATT_WIDTH].T
    wu = w[:, 2 * QK_WIDTH + ATT_WIDTH:]
    const = lambda l, b: (0, 0)
    slab = pl.BlockSpec((1, HEADS, tl // TB, LANES, TB), lambda l, b: (b, 0, l, 0, 0))
    slab_shape = jax.ShapeDtypeStruct((B, HEADS, L // TB, LANES, TB), BF16)
    return pl.pallas_call(
        functools.partial(_in_proj_kernel, tl=tl),
        grid=(n_l, B),
        in_specs=[
            pl.BlockSpec((1, tl, D), lambda l, b: (b, l, 0)),
            pl.BlockSpec((QK_WIDTH, D), const),
            pl.BlockSpec((D, QK_WIDTH), const),
            pl.BlockSpec((ATT_WIDTH, D), const),
            pl.BlockSpec((D, SSM_WIDTH), const),
        ],
        out_specs=[
            slab,
            pl.BlockSpec((1, tl, QK_WIDTH), lambda l, b: (b, l, 0)),
            slab,
            pl.BlockSpec((LANE_TILES, tl * B, LANES), lambda l, b: (0, l, 0)),
        ],
        out_shape=[
            slab_shape,
            jax.ShapeDtypeStruct((B, L, QK_WIDTH), BF16),
            slab_shape,
            jax.ShapeDtypeStruct((LANE_TILES, L * B, LANES), F32),
        ],
        compiler_params=_cparams(("arbitrary", "arbitrary")),
        name="in_proj",
    )(x, wqt, wk, wvt, wu)


def _attn_kernel(lq1_ref, lk1_ref, lq2_ref, lk2_ref, g_ref, qt_ref, k_ref, vt_ref, o_ref,
                 acc_sc, qst_sc, s_sc, p_sc, bm_sc, l_sc, *, lam_init):
    qi = pl.program_id(2)
    lam = (jnp.exp(jnp.sum(lq1_ref[...] * lk1_ref[...], axis=-1, keepdims=True))
           - jnp.exp(jnp.sum(lq2_ref[...] * lk2_ref[...], axis=-1, keepdims=True)) + lam_init)

    for hh in range(HEADS_PER_STEP):
        qt = qt_ref[0, hh, 0]
        row = lax.broadcasted_iota(jnp.int32, qt.shape, 0)
        zero = jnp.zeros_like(qt)
        qst_sc[hh, :, :TB] = jnp.where(row < QK_DIM, qt, zero)
        qst_sc[hh, :, TB:] = jnp.where(row >= QK_DIM, qt, zero)

    acc_sc[...] = jnp.zeros(acc_sc.shape, F32)
    p_sc[1] = jnp.zeros(p_sc.shape[1:], BF16)
    n_strips = 2 * TB // LANES
    heads = range(HEADS_PER_STEP)
    assert TB == 2 * LANES

    def scores(j, slot):
        rows = pl.ds(pl.multiple_of(j * TB, TB), TB)
        for hh in heads:
            kb = k_ref[0, rows, hh * LANES:(hh + 1) * LANES]
            s = jnp.dot(kb, qst_sc[hh], preferred_element_type=F32)
            s_sc[slot, hh] = s
            bm_sc[slot, hh] = jnp.max(s, axis=0, keepdims=True)

    def values(j, slot, carry):
        j = jnp.maximum(j, 0)
        for hh in heads:
            pv = jnp.dot(vt_ref[0, hh, j], p_sc[slot, hh], preferred_element_type=F32)
            acc_sc[hh] = carry[hh][2] * acc_sc[hh] + pv

    def softmax(slot, carry, diagonal):
        out = []
        for hh in heads:
            m, l, _ = carry[hh]
            m_new, l_new, alpha = [], [], []
            for c in range(n_strips):
                lanes = slice(c * LANES, (c + 1) * LANES)
                if not diagonal:
                    mc = jnp.maximum(m[:, lanes], bm_sc[slot, hh, :, lanes])
                    p = jnp.exp(s_sc[slot, hh, :, lanes] - mc)
                    p_sc[slot, hh, :, lanes] = p.astype(BF16)
                    psum = jnp.sum(p, axis=0, keepdims=True)
                else:
                    q0 = (c * LANES) % TB
                    s = s_sc[slot, hh, :, lanes]
                    key = lax.broadcasted_iota(jnp.int32, s.shape, 0)
                    qry = lax.broadcasted_iota(jnp.int32, s.shape, 1) + q0
                    s = jnp.where(key <= qry, s, -jnp.inf)
                    mc = jnp.maximum(m[:, lanes], jnp.max(s, axis=0, keepdims=True))
                    p = jnp.exp(s - mc)
                    p_sc[slot, hh, :, lanes] = p.astype(BF16)
                    psum = jnp.sum(p, axis=0, keepdims=True)
                ac = jnp.exp(m[:, lanes] - mc)
                l_new.append(ac * l[:, lanes] + psum)
                m_new.append(mc)
                alpha.append(ac)
            out.append(tuple(jnp.concatenate(v, axis=1) for v in (m_new, l_new, alpha)))
        return tuple(out)

    def pair(t, carry):
        a = 2 * t
        scores(a + 1, 1)
        values(a - 1, 1, carry)
        carry = softmax(0, carry, False)
        scores(a + 2, 0)
        values(a, 0, carry)
        return softmax(1, carry, False)

    row0 = lambda v: jnp.full((1, 2 * TB), v, F32)
    init = tuple((row0(-jnp.inf), row0(0.0), row0(1.0)) for _ in heads)
    scores(0, 0)
    carry = lax.fori_loop(0, qi // 2, pair, init)

    @pl.when(qi % 2 == 1)
    def _():
        scores(qi, 1)
        values(qi - 2, 1, carry)
        c = softmax(0, carry, False)
        values(qi - 1, 0, c)
        c = softmax(1, c, True)
        values(qi, 1, c)
        for hh in heads:
            l_sc[hh] = c[hh][1]

    @pl.when(qi % 2 == 0)
    def _():
        values(qi - 1, 1, carry)
        c = softmax(0, carry, True)
        values(qi, 0, c)
        for hh in heads:
            l_sc[hh] = c[hh][1]

    l_fin = [l_sc[hh] for hh in heads]

    for hh in range(HEADS_PER_STEP):
        o = acc_sc[hh] * (1.0 / l_fin[hh])
        o = o[:, :TB] - lam * o[:, TB:]
        ms = jnp.mean(o * o, axis=0, keepdims=True)
        o = o * lax.rsqrt(ms + LN_EPS) * (g_ref[...] * (1.0 - lam_init))
        o_ref[0, :, hh * LANES:(hh + 1) * LANES] = o.T.astype(o_ref.dtype)


def _diff_attn(qt, k, vt, lq1, lk1, lq2, lk2, subln_g, lam_init):
    B, L, _ = k.shape
    n_q = L // TB
    hp = HEADS_PER_STEP
    vec = pl.BlockSpec((1, QK_DIM), lambda b, h, i: (0, 0))
    return pl.pallas_call(
        functools.partial(_attn_kernel, lam_init=lam_init),
        grid=(B, HEADS // hp, n_q),
        in_specs=[
            vec, vec, vec, vec,
            pl.BlockSpec((V_DIM, 1), lambda b, h, i: (0, 0)),
            pl.BlockSpec((1, hp, 1, LANES, TB), lambda b, h, i: (b, h, i, 0, 0)),
            pl.BlockSpec((1, L, hp * LANES), lambda b, h, i: (b, 0, h)),
            pl.BlockSpec((1, hp, n_q, LANES, TB), lambda b, h, i: (b, h, 0, 0, 0)),
        ],
        out_specs=pl.BlockSpec((1, TB, hp * LANES), lambda b, h, i: (b, i, h)),
        out_shape=jax.ShapeDtypeStruct((B, L, ATT_WIDTH), BF16),
        scratch_shapes=[
            pltpu.VMEM((hp, V_DIM, 2 * TB), F32),
            pltpu.VMEM((hp, LANES, 2 * TB), BF16),
            pltpu.VMEM((2, hp, TB, 2 * TB), F32),
            pltpu.VMEM((2, hp, TB, 2 * TB), BF16),
            pltpu.VMEM((2, hp, 1, 2 * TB), F32),
            pltpu.VMEM((hp, 1, 2 * TB), F32),
        ],
        compiler_params=_cparams(("parallel", "parallel", "arbitrary")),
        name="diff_attn",
    )(lq1, lk1, lq2, lk2, subln_g.reshape(V_DIM, 1), qt, k, vt)


def _block_transpose8(vs):
    lane_blk = lax.broadcasted_iota(jnp.int32, vs[0].shape, 1) // SSM_GROUP
    for d in (4, 2, 1):
        keep = (lane_blk & d) == 0
        new = list(vs)
        for a in range(8):
            if a & d == 0:
                lo, hi = vs[a], vs[a + d]
                new[a] = jnp.where(keep, lo, pltpu.roll(hi, SSM_GROUP * d, 1))
                new[a + d] = jnp.where(keep, pltpu.roll(lo, LANES - SSM_GROUP * d, 1), hi)
        vs = new
    return vs


def _ssm_kernel(u_ref, toep_ref, win_ref, wout_ref, a_ref, d_ref, y_ref, s_sc, x_sc, sp_sc, *, jt):
    rows = jt * SUBLANES

    @pl.when(pl.program_id(1) == 0)
    def _():
        s_sc[...] = jnp.zeros(s_sc.shape, F32)

    xs = [u_ref[0, :, r].reshape(rows, LANES) for r in range(CHUNK)]
    z_lo = _block_transpose8(xs[:8])
    z_hi = _block_transpose8(xs[8:])
    ug = [jnp.concatenate([z_lo[g], z_hi[g]], axis=1).astype(BF16) for g in range(GROUPS_PER_TILE)]

    y = [jnp.dot(ug[g], toep_ref[0, g], preferred_element_type=F32) for g in range(GROUPS_PER_TILE)]

    for m in range(PAIRS_PER_TILE):
        up = jnp.concatenate([ug[2 * m], ug[2 * m + 1]], axis=1)
        xp = jnp.dot(up, win_ref[0, m], preferred_element_type=F32)
        x_sc[0, :, m * LANES:(m + 1) * LANES] = xp[:, :LANES]
        x_sc[1, :, m * LANES:(m + 1) * LANES] = xp[:, LANES:]

    ar = a_ref[0, 0:1, :]
    ai = a_ref[0, 1:2, :]

    def step(j, carry):
        sr, si = carry
        r8 = pl.ds(pl.multiple_of(j * SUBLANES, SUBLANES), SUBLANES)
        sp_sc[0, r8, :] = sr
        sp_sc[1, r8, :] = si
        return (ar * sr - ai * si + x_sc[0, r8, :], ar * si + ai * sr + x_sc[1, r8, :])

    sr, si = lax.fori_loop(0, jt, step, (s_sc[0], s_sc[1]))
    s_sc[0] = sr
    s_sc[1] = si

    for m in range(PAIRS_PER_TILE):
        sp = jnp.concatenate([sp_sc[0, :, m * LANES:(m + 1) * LANES],
                              sp_sc[1, :, m * LANES:(m + 1) * LANES]], axis=1).astype(BF16)
        yc = jnp.dot(sp, wout_ref[0, m], preferred_element_type=F32)
        y[2 * m] = y[2 * m] + yc[:, :CHUNK_COLS]
        y[2 * m + 1] = y[2 * m + 1] + yc[:, CHUNK_COLS:]

    d = d_ref[0]
    for half in range(2):
        ws = _block_transpose8([y[g][:, half * LANES:(half + 1) * LANES] for g in range(GROUPS_PER_TILE)])
        for t8 in range(8):
            t = half * 8 + t8
            u_t = u_ref[0, :, t].reshape(rows, LANES)
            y_ref[0, :, t] = (ws[t8] + d * u_t).reshape(jt, SUBLANES, LANES)


def _ssm(u_tm, tables, d_skip, B, L, lt=512):
    toep, win, wout, a16 = tables
    jt = lt // CHUNK
    n_l = L // lt
    u5 = u_tm.reshape(LANE_TILES, L // CHUNK, CHUNK, B, LANES)
    rows = jt * B
    y5 = pl.pallas_call(
        functools.partial(_ssm_kernel, jt=jt),
        grid=(LANE_TILES, n_l),
        in_specs=[
            pl.BlockSpec((1, jt, CHUNK, B, LANES), lambda q, l: (q, l, 0, 0, 0)),
            pl.BlockSpec((1, GROUPS_PER_TILE, CHUNK_COLS, CHUNK_COLS), lambda q, l: (q, 0, 0, 0)),
            pl.BlockSpec((1, PAIRS_PER_TILE, 2 * CHUNK_COLS, 2 * LANES), lambda q, l: (q, 0, 0, 0)),
            pl.BlockSpec((1, PAIRS_PER_TILE, 2 * LANES, 2 * CHUNK_COLS), lambda q, l: (q, 0, 0, 0)),
            pl.BlockSpec((1, 2, STATE_COLS), lambda q, l: (q, 0, 0)),
            pl.BlockSpec((1, 1, LANES), lambda q, l: (q, 0, 0)),
        ],
        out_specs=pl.BlockSpec((1, jt, CHUNK, B, LANES), lambda q, l: (q, l, 0, 0, 0)),
        out_shape=jax.ShapeDtypeStruct(u5.shape, F32),
        scratch_shapes=[
            pltpu.VMEM((2, B, STATE_COLS), F32),
            pltpu.VMEM((2, rows, STATE_COLS), F32),
            pltpu.VMEM((2, rows, STATE_COLS), F32),
        ],
        compiler_params=_cparams(("parallel", "arbitrary")),
        name="ssm",
    )(u5, toep, win, wout, a16, d_skip.reshape(LANE_TILES, 1, LANES))
    return y5.reshape(LANE_TILES, L * B, LANES)


def _cmul(a, b):
    return a[0] * b[0] - a[1] * b[1], a[0] * b[1] + a[1] * b[0]


def _zoh(lr, li, ldt):
    dt = jnp.exp(ldt)
    mag = jnp.exp(lr * dt)
    ab = (mag * jnp.cos(li * dt), mag * jnp.sin(li * dt))
    den = lr * lr + li * li
    zr, zi = ab[0] - 1.0, ab[1]
    return ab, ((zr * lr + zi * li) / den, (zi * lr - zr * li) / den)


def _cpow(base, exponent, bits):
    out = None
    for k in range(bits):
        on = (exponent & (1 << k)) != 0
        sel = (jnp.where(on, base[0], 1.0), jnp.where(on, base[1], 0.0))
        out = sel if out is None else _cmul(out, sel)
        if k + 1 < bits:
            base = _cmul(base, base)
    return out


def _ssm_prep_kernel(col_ref, row_ref, b_ref, ct_ref, bt_ref, toep_ref, win_ref, wout_ref, a_ref):
    bits = CHUNK.bit_length() - 1
    lane = lax.broadcasted_iota(jnp.int32, (1, CHUNK_COLS), 1)
    tau = lane // SSM_GROUP
    rowi = lax.broadcasted_iota(jnp.int32, (CHUNK_COLS, 1), 0)
    back = (CHUNK - 1) - rowi // SSM_GROUP
    lane2 = lax.broadcasted_iota(jnp.int32, (1, LANES), 1)
    a_rows = []
    for m in range(PAIRS_PER_TILE):
        abp, fp = _zoh(row_ref[0, m], row_ref[1, m], row_ref[2, m])
        pwr = _cpow(abp, back, bits)
        a16 = abp
        for _ in range(bits):
            a16 = _cmul(a16, a16)
        a_rows.append(a16)
        wout_rows = [[None, None], [None, None]]
        for j in range(2):
            g = 2 * m + j
            ab, f = _zoh(col_ref[0, g], col_ref[1, g], col_ref[2, g])
            bb = _cmul(f, (b_ref[0, g], b_ref[1, g]))
            pw = _cpow(ab, tau, bits)
            g0 = _cmul(pw, (ct_ref[0, g], ct_ref[1, g]))
            g1 = _cmul(g0, ab)
            kt = jnp.concatenate(
                [jnp.sum(bb[0][:, h:h + 1] * g0[0] - bb[1][:, h:h + 1] * g0[1], axis=0, keepdims=True)
                 for h in range(SSM_GROUP)], axis=0)
            toep_ref[0, g] = jnp.concatenate(
                [jnp.where(lane >= SSM_GROUP * r, pltpu.roll(kt, SSM_GROUP * r, 1), 0.0) if r else kt
                 for r in range(CHUNK)], axis=0).astype(BF16)
            bb2 = _cmul(fp, (bt_ref[0, g], bt_ref[1, g]))
            w = _cmul(pwr, bb2)
            mine = (lane2 < SSM_STATE) if j == 0 else (lane2 >= SSM_STATE)
            win_ref[0, m, j * CHUNK_COLS:(j + 1) * CHUNK_COLS, :] = jnp.concatenate(
                [jnp.where(mine, w[0], 0.0), jnp.where(mine, w[1], 0.0)], axis=1).astype(BF16)
            wout_rows[0][j] = g1[0]
            wout_rows[1][j] = -g1[1]
        zero = jnp.zeros((SSM_STATE, CHUNK_COLS), F32)
        wout_ref[0, m] = jnp.concatenate([
            jnp.concatenate([wout_rows[0][0], zero], axis=1),
            jnp.concatenate([zero, wout_rows[0][1]], axis=1),
            jnp.concatenate([wout_rows[1][0], zero], axis=1),
            jnp.concatenate([zero, wout_rows[1][1]], axis=1)], axis=0).astype(BF16)
    a_ref[0] = jnp.concatenate([jnp.concatenate([a[0] for a in a_rows], axis=1),
                                jnp.concatenate([a[1] for a in a_rows], axis=1)], axis=0)


def _ssm_prep(lam_re, lam_im, log_dt, b_re, b_im, c_re, c_im):
    G, P, H = b_re.shape
    ldt = jnp.broadcast_to(log_dt[:, None], (G, P))
    cols = jnp.stack([lam_re, lam_im, ldt])[..., None]
    rows = jnp.stack([lam_re, lam_im, ldt]).reshape(3, G // 2, 1, 2 * P)
    b = jnp.stack([b_re, b_im])
    ct = jnp.tile(jnp.stack([c_re, c_im]).transpose(0, 1, 3, 2), (1, 1, 1, CHUNK))
    bt = jnp.tile(b.transpose(0, 1, 3, 2), (1, 1, CHUNK, 2))
    gt = GROUPS_PER_TILE
    return pl.pallas_call(
        _ssm_prep_kernel,
        grid=(LANE_TILES,),
        in_specs=[
            pl.BlockSpec((3, gt, P, 1), lambda q: (0, q, 0, 0)),
            pl.BlockSpec((3, PAIRS_PER_TILE, 1, 2 * P), lambda q: (0, q, 0, 0)),
            pl.BlockSpec((2, gt, P, H), lambda q: (0, q, 0, 0)),
            pl.BlockSpec((2, gt, P, CHUNK_COLS), lambda q: (0, q, 0, 0)),
            pl.BlockSpec((2, gt, CHUNK_COLS, 2 * P), lambda q: (0, q, 0, 0)),
        ],
        out_specs=[
            pl.BlockSpec((1, gt, CHUNK_COLS, CHUNK_COLS), lambda q: (q, 0, 0, 0)),
            pl.BlockSpec((1, PAIRS_PER_TILE, 2 * CHUNK_COLS, 2 * LANES), lambda q: (q, 0, 0, 0)),
            pl.BlockSpec((1, PAIRS_PER_TILE, 2 * LANES, 2 * CHUNK_COLS), lambda q: (q, 0, 0, 0)),
            pl.BlockSpec((1, 2, STATE_COLS), lambda q: (q, 0, 0)),
        ],
        out_shape=[
            jax.ShapeDtypeStruct((LANE_TILES, gt, CHUNK_COLS, CHUNK_COLS), BF16),
            jax.ShapeDtypeStruct((LANE_TILES, PAIRS_PER_TILE, 2 * CHUNK_COLS, 2 * LANES), BF16),
            jax.ShapeDtypeStruct((LANE_TILES, PAIRS_PER_TILE, 2 * LANES, 2 * CHUNK_COLS), BF16),
            jax.ShapeDtypeStruct((LANE_TILES, 2, STATE_COLS), F32),
        ],
        compiler_params=_cparams(("parallel",)),
        name="ssm_prep",
    )(cols, rows, b, ct, bt)


def _layer_norm(r, g, b):
    mu = jnp.mean(r, axis=-1, keepdims=True)
    c = r - mu
    var = jnp.mean(c * c, axis=-1, keepdims=True)
    return c * lax.rsqrt(var + LN_EPS) * g + b


def _gelu_tanh(y):
    return 0.5 * y * (1.0 + jnp.tanh(math.sqrt(2.0 / math.pi) * (y + 0.044715 * (y * y * y))))


def _mix_ln1_kernel(x_ref, attn_ref, y_ref, wglu_ref, bglu_ref, wo_ref, g_ref, b_ref, o_ref, *, tl):
    b = pl.program_id(1)
    y = jnp.concatenate([y_ref[q, pl.ds(b, tl, stride=SUBLANES), :] for q in range(LANE_TILES)], axis=1)
    gl = _gelu_tanh(y)
    z = gl * jax.nn.sigmoid(jnp.dot(gl.astype(BF16), wglu_ref[...], preferred_element_type=F32) + bglu_ref[...])
    mix = (jnp.dot(attn_ref[0], wo_ref[:ATT_WIDTH], preferred_element_type=F32)
           + jnp.dot(z.astype(BF16), wo_ref[ATT_WIDTH:], preferred_element_type=F32))
    o_ref[0] = _layer_norm(ALPHA * x_ref[0] + mix, g_ref[...], b_ref[...])


def _mix_ln1(x, attn, y_tm, wglu_bf, b_glu, wo_bf, ln_g, ln_b, tl=512):
    B, L, D = x.shape
    n_l = L // tl
    const = lambda l, b: (0, 0)
    return pl.pallas_call(
        functools.partial(_mix_ln1_kernel, tl=tl),
        grid=(n_l, B),
        in_specs=[
            pl.BlockSpec((1, tl, D), lambda l, b: (b, l, 0)),
            pl.BlockSpec((1, tl, ATT_WIDTH), lambda l, b: (b, l, 0)),
            pl.BlockSpec((LANE_TILES, tl * B, LANES), lambda l, b: (0, l, 0)),
            pl.BlockSpec((SSM_WIDTH, SSM_WIDTH), const),
            pl.BlockSpec((1, SSM_WIDTH), const),
            pl.BlockSpec((D, D), const),
            pl.BlockSpec((1, D), const),
            pl.BlockSpec((1, D), const),
        ],
        out_specs=pl.BlockSpec((1, tl, D), lambda l, b: (b, l, 0)),
        out_shape=jax.ShapeDtypeStruct((B, L, D), F32),
        compiler_params=_cparams(("parallel", "arbitrary")),
        name="mix_ln1",
    )(x, attn, y_tm, wglu_bf, b_glu, wo_bf, ln_g, ln_b)


HALO = 16


def _ffn_kernel(x_ref, xh_ref, p_ref, wple_ref, wgate_ref, wg_ref, wv_ref, cwg_ref, cwv_ref, cbg_ref, cbv_ref,
                wd_ref, g_ref, b_ref, o_ref, xb_sc, acc_sc, hg_sc, hv_sc, *, tm, tiles_per_seq):
    i = pl.program_id(0)
    f = pl.program_id(1)

    @pl.when(f == 0)
    def _():
        keep = jnp.where(i % tiles_per_seq == 0, 0.0, 1.0)
        xb_sc[0:HALO] = (xh_ref[...] * keep).astype(BF16)
        xb_sc[HALO:] = x_ref[...].astype(BF16)
        gate = jax.nn.sigmoid(jnp.dot(xb_sc[HALO:], wgate_ref[...], preferred_element_type=F32))
        ple = jnp.dot(p_ref[...].astype(BF16), wple_ref[...], preferred_element_type=F32) * gate
        acc_sc[...] = ALPHA * x_ref[...] + ple

    xb = xb_sc[...]
    hg_sc[...] = jnp.dot(xb, wg_ref[...], preferred_element_type=F32)
    hv_sc[...] = jnp.dot(xb, wv_ref[...], preferred_element_type=F32)

    def conv(h_sc, w_ref, b_ref):
        out = b_ref[...] + w_ref[CONV_WIDTH - 1:CONV_WIDTH] * h_sc[HALO:]
        for j in range(CONV_WIDTH - 1):
            back = CONV_WIDTH - 1 - j
            out = out + w_ref[j:j + 1] * h_sc[HALO - back:HALO - back + tm]
        return out

    gc = conv(hg_sc, cwg_ref, cbg_ref)
    vc = conv(hv_sc, cwv_ref, cbv_ref)
    act = (gc * jax.nn.sigmoid(gc) * vc).astype(BF16)
    acc_sc[...] += jnp.dot(act, wd_ref[...], preferred_element_type=F32)

    @pl.when(f == pl.num_programs(1) - 1)
    def _():
        o_ref[...] = _layer_norm(acc_sc[...], g_ref[...], b_ref[...])


def _ffn_ln2(x1, p, wple_bf, wgate_bf, wup_bf, conv_w, conv_b, wdown_bf, ln_g, ln_b, L, tm=512, tf=1408):
    T, D = x1.shape
    n_t = T // tm
    n_f = D_FF // tf
    const = lambda i, f: (0, 0)
    return pl.pallas_call(
        functools.partial(_ffn_kernel, tm=tm, tiles_per_seq=L // tm),
        grid=(n_t, n_f),
        in_specs=[
            pl.BlockSpec((tm, D), lambda i, f: (i, 0)),
            pl.BlockSpec((HALO, D), lambda i, f: (jnp.maximum(i * (tm // HALO) - 1, 0), 0)),
            pl.BlockSpec((tm, PLE_DIM), lambda i, f: (i, 0)),
            pl.BlockSpec((PLE_DIM, D), const),
            pl.BlockSpec((D, D), const),
            pl.BlockSpec((D, tf), lambda i, f: (0, f)),
            pl.BlockSpec((D, tf), lambda i, f: (0, f + n_f)),
            pl.BlockSpec((CONV_WIDTH, tf), lambda i, f: (0, f)),
            pl.BlockSpec((CONV_WIDTH, tf), lambda i, f: (0, f + n_f)),
            pl.BlockSpec((1, tf), lambda i, f: (0, f)),
            pl.BlockSpec((1, tf), lambda i, f: (0, f + n_f)),
            pl.BlockSpec((tf, D), lambda i, f: (f, 0)),
            pl.BlockSpec((1, D), const),
            pl.BlockSpec((1, D), const),
        ],
        out_specs=pl.BlockSpec((tm, D), lambda i, f: (i, 0)),
        out_shape=jax.ShapeDtypeStruct((T, D), F32),
        scratch_shapes=[
            pltpu.VMEM((tm + HALO, D), BF16),
            pltpu.VMEM((tm, D), F32),
            pltpu.VMEM((tm + HALO, tf), F32),
            pltpu.VMEM((tm + HALO, tf), F32),
        ],
        compiler_params=_cparams(("parallel", "arbitrary")),
        name="ffn_ln2",
    )(x1, x1, p, wple_bf, wgate_bf, wup_bf, wup_bf, conv_w, conv_w, conv_b, conv_b, wdown_bf, ln_g, ln_b)


def kernel(x, p, w_in, diff_lambda_q1, diff_lambda_k1, diff_lambda_q2, diff_lambda_k2, diff_subln_g, ssm_lambda_re, ssm_lambda_im, ssm_log_dt, ssm_b_re, ssm_b_im, ssm_c_re, ssm_c_im, ssm_d, ssm_w_glu, ssm_b_glu, w_o, ln1_g, ln1_b, ffn_w_up, ffn_conv_w, ffn_conv_b, ffn_w_down, w_ple, w_ple_gate, ln2_g, ln2_b):
    B, L, D = x.shape
    assert B == SUBLANES and D == D_MODEL and L % 512 == 0
    for i in range(DEPTH):
        lam_init = 0.8 - 0.6 * math.exp(-0.3 * i)
        row = lambda a: a[i].reshape(1, -1)
        qt, k, vt, u_tm = _in_proj(x, w_in[i])
        attn = _diff_attn(qt, k, vt, row(diff_lambda_q1), row(diff_lambda_k1), row(diff_lambda_q2),
                          row(diff_lambda_k2), diff_subln_g[i], lam_init)
        tables = _ssm_prep(ssm_lambda_re[i], ssm_lambda_im[i], ssm_log_dt[i], ssm_b_re[i], ssm_b_im[i],
                           ssm_c_re[i], ssm_c_im[i])
        y_tm = _ssm(u_tm, tables, ssm_d[i], B, L)
        x1 = _mix_ln1(x, attn, y_tm, ssm_w_glu[i].astype(BF16), row(ssm_b_glu), w_o[i].astype(BF16),
                      row(ln1_g), row(ln1_b))
        x2 = _ffn_ln2(x1.reshape(B * L, D), p[i].reshape(B * L, PLE_DIM), w_ple[i].astype(BF16),
                      w_ple_gate[i].astype(BF16), ffn_w_up[i].astype(BF16), ffn_conv_w[i], row(ffn_conv_b),
                      ffn_w_down[i].astype(BF16), row(ln2_g), row(ln2_b), L)
        x = x2.reshape(B, L, D)
    return x
```

```python
import functools
import math

import jax
import jax.numpy as jnp
from jax import lax
from jax.experimental import pallas as pl
from jax.experimental.pallas import tpu as pltpu

F32 = jnp.float32
BF16 = jnp.bfloat16

SUBLANES = 8
LANES = 128

DEPTH = 1
D_MODEL = 1024
PLE_DIM = 256
HEADS = 4
QK_DIM = 64
V_DIM = 2 * QK_DIM
QK_WIDTH = HEADS * 2 * QK_DIM
ATT_WIDTH = HEADS * V_DIM
SSM_WIDTH = D_MODEL - ATT_WIDTH
SSM_GROUP = 16
SSM_GROUPS = SSM_WIDTH // SSM_GROUP
SSM_STATE = 64
D_FF = 2816
CONV_WIDTH = 3
LN_EPS = 1e-5
ALPHA = (2 * DEPTH) ** 0.25
QK_SCALE = QK_DIM ** -0.5 * math.log2(math.e)

CHUNK = 16
CHUNK_COLS = CHUNK * SSM_GROUP
GROUPS_PER_TILE = LANES // SSM_GROUP
LANE_TILES = SSM_WIDTH // LANES
PAIRS_PER_TILE = GROUPS_PER_TILE // 2
STATE_COLS = GROUPS_PER_TILE * SSM_STATE

VMEM_LIMIT = 56 * 1024 * 1024


def _cparams(sem):
    return pltpu.CompilerParams(dimension_semantics=sem, vmem_limit_bytes=VMEM_LIMIT)


TB = 256
HEADS_PER_STEP = 4


def _in_proj_kernel(x_ref, wqt_ref, wk_ref, wvt_ref, wu_ref, qt_ref, k_ref, vt_ref, u_ref, *, tl):
    b = pl.program_id(1)
    xb = x_ref[0].astype(BF16)
    nt = (((1,), (1,)), ((), ()))
    qt = lax.dot_general(wqt_ref[...], xb, nt, preferred_element_type=F32) * QK_SCALE
    vt = lax.dot_general(wvt_ref[...], xb, nt, preferred_element_type=F32)
    for h in range(HEADS):
        for c in range(tl // TB):
            qt_ref[0, h, c] = qt[h * LANES:(h + 1) * LANES, c * TB:(c + 1) * TB].astype(BF16)
            vt_ref[0, h, c] = vt[h * LANES:(h + 1) * LANES, c * TB:(c + 1) * TB].astype(BF16)
    k_ref[0] = jnp.dot(xb, wk_ref[...], preferred_element_type=F32).astype(BF16)
    hu = jnp.dot(xb, wu_ref[...], preferred_element_type=F32)
    for q in range(LANE_TILES):
        u_ref[q, pl.ds(b, tl, stride=SUBLANES), :] = hu[:, q * LANES:(q + 1) * LANES]


def _in_proj(x, w_in, tl=512):
    B, L, D = x.shape
    n_l = L // tl
    w = w_in.astype(BF16)
    wqt = w[:, :QK_WIDTH].T
    wk = w[:, QK_WIDTH:2 * QK_WIDTH]
    wvt = w[:, 2 * QK_WIDTH:2 * QK_WIDTH + ATT_WIDTH].T
    wu = w[:, 2 * QK_WIDTH + ATT_WIDTH:]
    const = lambda l, b: (0, 0)
    slab = pl.BlockSpec((1, HEADS, tl // TB, LANES, TB), lambda l, b: (b, 0, l, 0, 0))
    slab_shape = jax.ShapeDtypeStruct((B, HEADS, L // TB, LANES, TB), BF16)
    return pl.pallas_call(
        functools.partial(_in_proj_kernel, tl=tl),
        grid=(n_l, B),
        in_specs=[
            pl.BlockSpec((1, tl, D), lambda l, b: (b, l, 0)),
            pl.BlockSpec((QK_WIDTH, D), const),
            pl.BlockSpec((D, QK_WIDTH), const),
            pl.BlockSpec((ATT_WIDTH, D), const),
            pl.BlockSpec((D, SSM_WIDTH), const),
        ],
        out_specs=[
            slab,
            pl.BlockSpec((1, tl, QK_WIDTH), lambda l, b: (b, l, 0)),
            slab,
            pl.BlockSpec((LANE_TILES, tl * B, LANES), lambda l, b: (0, l, 0)),
        ],
        out_shape=[
            slab_shape,
            jax.ShapeDtypeStruct((B, L, QK_WIDTH), BF16),
            slab_shape,
            jax.ShapeDtypeStruct((LANE_TILES, L * B, LANES), F32),
        ],
        compiler_params=_cparams(("arbitrary", "arbitrary")),
        name="in_proj",
    )(x, wqt, wk, wvt, wu)


def _attn_kernel(lq1_ref, lk1_ref, lq2_ref, lk2_ref, g_ref, qt_ref, k_ref, vt_ref, o_ref,
                 acc_sc, qst_sc, s_sc, p_sc, bm_sc, st_sc, *, lam_init):
    qi = pl.program_id(2)
    lam = (jnp.exp(jnp.sum(lq1_ref[...] * lk1_ref[...], axis=-1, keepdims=True))
           - jnp.exp(jnp.sum(lq2_ref[...] * lk2_ref[...], axis=-1, keepdims=True)) + lam_init)

    for hh in range(HEADS_PER_STEP):
        qt = qt_ref[0, hh, 0]
        row = lax.broadcasted_iota(jnp.int32, qt.shape, 0)
        zero = jnp.zeros_like(qt)
        qst_sc[hh, :, :TB] = jnp.where(row < QK_DIM, qt, zero)
        qst_sc[hh, :, TB:] = jnp.where(row >= QK_DIM, qt, zero)

    acc_sc[...] = jnp.zeros(acc_sc.shape, F32)
    p_sc[1] = jnp.zeros(p_sc.shape[1:], BF16)
    n_strips = 2 * TB // LANES
    heads = range(HEADS_PER_STEP)
    assert TB == 2 * LANES

    def scores(j, slot):
        rows = pl.ds(pl.multiple_of(j * TB, TB), TB)
        for hh in heads:
            kb = k_ref[0, rows, hh * LANES:(hh + 1) * LANES]
            s = jnp.dot(kb, qst_sc[hh], preferred_element_type=F32)
            s_sc[slot, hh] = s
            bm_sc[slot, hh] = jnp.max(s, axis=0, keepdims=True)

    def values(j, slot):
        j = jnp.maximum(j, 0)
        for hh in heads:
            pv = jnp.dot(vt_ref[0, hh, j], p_sc[slot, hh], preferred_element_type=F32)
            acc_sc[hh] = st_sc[hh, 2] * acc_sc[hh] + pv

    def softmax(slot, diagonal):
        for hh in heads:
            for c in range(n_strips):
                lanes = slice(c * LANES, (c + 1) * LANES)
                m = st_sc[hh, 0, :, lanes]
                if not diagonal:
                    mc = jnp.maximum(m, bm_sc[slot, hh, :, lanes])
                    p = jnp.exp2(s_sc[slot, hh, :, lanes] - mc)
                    p_sc[slot, hh, :, lanes] = p.astype(BF16)
                    psum = jnp.sum(p, axis=0, keepdims=True)
                else:
                    q0 = (c * LANES) % TB
                    s = s_sc[slot, hh, :, lanes]
                    key = lax.broadcasted_iota(jnp.int32, s.shape, 0)
                    qry = lax.broadcasted_iota(jnp.int32, s.shape, 1) + q0
                    s = jnp.where(key <= qry, s, -jnp.inf)
                    mc = jnp.maximum(m, jnp.max(s, axis=0, keepdims=True))
                    p = jnp.exp2(s - mc)
                    p_sc[slot, hh, :, lanes] = p.astype(BF16)
                    psum = jnp.sum(p, axis=0, keepdims=True)
                ac = jnp.exp2(m - mc)
                st_sc[hh, 0, :, lanes] = mc
                st_sc[hh, 1, :, lanes] = ac * st_sc[hh, 1, :, lanes] + psum
                st_sc[hh, 2, :, lanes] = ac

    def pair(t, c):
        a = 2 * t
        scores(a + 1, 1)
        values(a - 1, 1)
        softmax(0, False)
        scores(a + 2, 0)
        values(a, 0)
        softmax(1, False)
        return c

    for hh in heads:
        st_sc[hh, 0] = jnp.full((1, 2 * TB), -jnp.inf, F32)
        st_sc[hh, 1] = jnp.zeros((1, 2 * TB), F32)
        st_sc[hh, 2] = jnp.ones((1, 2 * TB), F32)
    scores(0, 0)
    lax.fori_loop(0, qi // 2, pair, 0)

    @pl.when(qi % 2 == 1)
    def _():
        values(qi - 2, 1)
        scores(qi, 1)
        softmax(0, False)
        values(qi - 1, 0)
        softmax(1, True)
        values(qi, 1)

    @pl.when(qi % 2 == 0)
    def _():
        values(qi - 1, 1)
        softmax(0, True)
        values(qi, 0)

    l_fin = [st_sc[hh, 1] for hh in heads]

    for hh in range(HEADS_PER_STEP):
        o = acc_sc[hh] * (1.0 / l_fin[hh])
        o = o[:, :TB] - lam * o[:, TB:]
        ms = jnp.mean(o * o, axis=0, keepdims=True)
        o = o * lax.rsqrt(ms + LN_EPS) * (g_ref[...] * (1.0 - lam_init))
        o_ref[0, :, hh * LANES:(hh + 1) * LANES] = o.T.astype(o_ref.dtype)


def _diff_attn(qt, k, vt, lq1, lk1, lq2, lk2, subln_g, lam_init):
    B, L, _ = k.shape
    n_q = L // TB
    hp = HEADS_PER_STEP
    vec = pl.BlockSpec((1, QK_DIM), lambda b, h, i: (0, 0))
    return pl.pallas_call(
        functools.partial(_attn_kernel, lam_init=lam_init),
        grid=(B, HEADS // hp, n_q),
        in_specs=[
            vec, vec, vec, vec,
            pl.BlockSpec((V_DIM, 1), lambda b, h, i: (0, 0)),
            pl.BlockSpec((1, hp, 1, LANES, TB), lambda b, h, i: (b, h, i, 0, 0)),
            pl.BlockSpec((1, L, hp * LANES), lambda b, h, i: (b, 0, h)),
            pl.BlockSpec((1, hp, n_q, LANES, TB), lambda b, h, i: (b, h, 0, 0, 0)),
        ],
        out_specs=pl.BlockSpec((1, TB, hp * LANES), lambda b, h, i: (b, i, h)),
        out_shape=jax.ShapeDtypeStruct((B, L, ATT_WIDTH), BF16),
        scratch_shapes=[
            pltpu.VMEM((hp, V_DIM, 2 * TB), F32),
            pltpu.VMEM((hp, LANES, 2 * TB), BF16),
            pltpu.VMEM((2, hp, TB, 2 * TB), F32),
            pltpu.VMEM((2, hp, TB, 2 * TB), BF16),
            pltpu.VMEM((2, hp, 1, 2 * TB), F32),
            pltpu.VMEM((hp, 3, 1, 2 * TB), F32),
        ],
        compiler_params=_cparams(("parallel", "parallel", "arbitrary")),
        name="diff_attn",
    )(lq1, lk1, lq2, lk2, subln_g.reshape(V_DIM, 1), qt, k, vt)


def _block_transpose8(vs):
    lane_blk = lax.broadcasted_iota(jnp.int32, vs[0].shape, 1) // SSM_GROUP
    for d in (4, 2, 1):
        keep = (lane_blk & d) == 0
        new = list(vs)
        for a in range(8):
            if a & d == 0:
                lo, hi = vs[a], vs[a + d]
                new[a] = jnp.where(keep, lo, pltpu.roll(hi, SSM_GROUP * d, 1))
                new[a + d] = jnp.where(keep, pltpu.roll(lo, LANES - SSM_GROUP * d, 1), hi)
        vs = new
    return vs


def _ssm_kernel(u_ref, toep_ref, win_ref, wout_ref, a_ref, d_ref, y_ref, s_sc, x_sc, sp_sc, *, jt):
    rows = jt * SUBLANES

    @pl.when(pl.program_id(1) == 0)
    def _():
        s_sc[...] = jnp.zeros(s_sc.shape, F32)

    xs = [u_ref[0, :, r].reshape(rows, LANES) for r in range(CHUNK)]
    z_lo = _block_transpose8(xs[:8])
    z_hi = _block_transpose8(xs[8:])
    ug = [jnp.concatenate([z_lo[g], z_hi[g]], axis=1).astype(BF16) for g in range(GROUPS_PER_TILE)]

    y = [jnp.dot(ug[g], toep_ref[0, g], preferred_element_type=F32) for g in range(GROUPS_PER_TILE)]

    for m in range(PAIRS_PER_TILE):
        up = jnp.concatenate([ug[2 * m], ug[2 * m + 1]], axis=1)
        xp = jnp.dot(up, win_ref[0, m], preferred_element_type=F32)
        x_sc[0, :, m * LANES:(m + 1) * LANES] = xp[:, :LANES]
        x_sc[1, :, m * LANES:(m + 1) * LANES] = xp[:, LANES:]

    ar = a_ref[0, 0:1, :]
    ai = a_ref[0, 1:2, :]

    def step(j, carry):
        sr, si = carry
        r8 = pl.ds(pl.multiple_of(j * SUBLANES, SUBLANES), SUBLANES)
        sp_sc[0, r8, :] = sr
        sp_sc[1, r8, :] = si
        return (ar * sr - ai * si + x_sc[0, r8, :], ar * si + ai * sr + x_sc[1, r8, :])

    sr, si = lax.fori_loop(0, jt, step, (s_sc[0], s_sc[1]))
    s_sc[0] = sr
    s_sc[1] = si

    for m in range(PAIRS_PER_TILE):
        sp = jnp.concatenate([sp_sc[0, :, m * LANES:(m + 1) * LANES],
                              sp_sc[1, :, m * LANES:(m + 1) * LANES]], axis=1).astype(BF16)
        yc = jnp.dot(sp, wout_ref[0, m], preferred_element_type=F32)
        y[2 * m] = y[2 * m] + yc[:, :CHUNK_COLS]
        y[2 * m + 1] = y[2 * m + 1] + yc[:, CHUNK_COLS:]

    d = d_ref[0]
    for half in range(2):
        ws = _block_transpose8([y[g][:, half * LANES:(half + 1) * LANES] for g in range(GROUPS_PER_TILE)])
        for t8 in range(8):
            t = half * 8 + t8
            u_t = u_ref[0, :, t].reshape(rows, LANES)
            y_ref[0, :, t] = (ws[t8] + d * u_t).reshape(jt, SUBLANES, LANES)


def _ssm(u_tm, tables, d_skip, B, L, lt=512):
    toep, win, wout, a16 = tables
    jt = lt // CHUNK
    n_l = L // lt
    u5 = u_tm.reshape(LANE_TILES, L // CHUNK, CHUNK, B, LANES)
    rows = jt * B
    y5 = pl.pallas_call(
        functools.partial(_ssm_kernel, jt=jt),
        grid=(LANE_TILES, n_l),
        in_specs=[
            pl.BlockSpec((1, jt, CHUNK, B, LANES), lambda q, l: (q, l, 0, 0, 0)),
            pl.BlockSpec((1, GROUPS_PER_TILE, CHUNK_COLS, CHUNK_COLS), lambda q, l: (q, 0, 0, 0)),
            pl.BlockSpec((1, PAIRS_PER_TILE, 2 * CHUNK_COLS, 2 * LANES), lambda q, l: (q, 0, 0, 0)),
            pl.BlockSpec((1, PAIRS_PER_TILE, 2 * LANES, 2 * CHUNK_COLS), lambda q, l: (q, 0, 0, 0)),
            pl.BlockSpec((1, 2, STATE_COLS), lambda q, l: (q, 0, 0)),
            pl.BlockSpec((1, 1, LANES), lambda q, l: (q, 0, 0)),
        ],
        out_specs=pl.BlockSpec((1, jt, CHUNK, B, LANES), lambda q, l: (q, l, 0, 0, 0)),
        out_shape=jax.ShapeDtypeStruct(u5.shape, F32),
        scratch_shapes=[
            pltpu.VMEM((2, B, STATE_COLS), F32),
            pltpu.VMEM((2, rows, STATE_COLS), F32),
            pltpu.VMEM((2, rows, STATE_COLS), F32),
        ],
        compiler_params=_cparams(("parallel", "arbitrary")),
        name="ssm",
    )(u5, toep, win, wout, a16, d_skip.reshape(LANE_TILES, 1, LANES))
    return y5.reshape(LANE_TILES, L * B, LANES)


def _cmul(a, b):
    return a[0] * b[0] - a[1] * b[1], a[0] * b[1] + a[1] * b[0]


def _zoh(lr, li, ldt):
    dt = jnp.exp(ldt)
    mag = jnp.exp(lr * dt)
    ab = (mag * jnp.cos(li * dt), mag * jnp.sin(li * dt))
    den = lr * lr + li * li
    zr, zi = ab[0] - 1.0, ab[1]
    return ab, ((zr * lr + zi * li) / den, (zi * lr - zr * li) / den)


def _cpow(base, exponent, bits):
    out = None
    for k in range(bits):
        on = (exponent & (1 << k)) != 0
        sel = (jnp.where(on, base[0], 1.0), jnp.where(on, base[1], 0.0))
        out = sel if out is None else _cmul(out, sel)
        if k + 1 < bits:
            base = _cmul(base, base)
    return out


def _ssm_prep_kernel(col_ref, row_ref, b_ref, ct_ref, bt_ref, toep_ref, win_ref, wout_ref, a_ref):
    bits = CHUNK.bit_length() - 1
    lane = lax.broadcasted_iota(jnp.int32, (1, CHUNK_COLS), 1)
    tau = lane // SSM_GROUP
    rowi = lax.broadcasted_iota(jnp.int32, (CHUNK_COLS, 1), 0)
    back = (CHUNK - 1) - rowi // SSM_GROUP
    lane2 = lax.broadcasted_iota(jnp.int32, (1, LANES), 1)
    a_rows = []
    for m in range(PAIRS_PER_TILE):
        abp, fp = _zoh(row_ref[0, m], row_ref[1, m], row_ref[2, m])
        pwr = _cpow(abp, back, bits)
        a16 = abp
        for _ in range(bits):
            a16 = _cmul(a16, a16)
        a_rows.append(a16)
        wout_rows = [[None, None], [None, None]]
        for j in range(2):
            g = 2 * m + j
            ab, f = _zoh(col_ref[0, g], col_ref[1, g], col_ref[2, g])
            bb = _cmul(f, (b_ref[0, g], b_ref[1, g]))
            pw = _cpow(ab, tau, bits)
            g0 = _cmul(pw, (ct_ref[0, g], ct_ref[1, g]))
            g1 = _cmul(g0, ab)
            kt = jnp.concatenate(
                [jnp.sum(bb[0][:, h:h + 1] * g0[0] - bb[1][:, h:h + 1] * g0[1], axis=0, keepdims=True)
                 for h in range(SSM_GROUP)], axis=0)
            toep_ref[0, g] = jnp.concatenate(
                [jnp.where(lane >= SSM_GROUP * r, pltpu.roll(kt, SSM_GROUP * r, 1), 0.0) if r else kt
                 for r in range(CHUNK)], axis=0).astype(BF16)
            bb2 = _cmul(fp, (bt_ref[0, g], bt_ref[1, g]))
            w = _cmul(pwr, bb2)
            mine = (lane2 < SSM_STATE) if j == 0 else (lane2 >= SSM_STATE)
            win_ref[0, m, j * CHUNK_COLS:(j + 1) * CHUNK_COLS, :] = jnp.concatenate(
                [jnp.where(mine, w[0], 0.0), jnp.where(mine, w[1], 0.0)], axis=1).astype(BF16)
            wout_rows[0][j] = g1[0]
            wout_rows[1][j] = -g1[1]
        zero = jnp.zeros((SSM_STATE, CHUNK_COLS), F32)
        wout_ref[0, m] = jnp.concatenate([
            jnp.concatenate([wout_rows[0][0], zero], axis=1),
            jnp.concatenate([zero, wout_rows[0][1]], axis=1),
            jnp.concatenate([wout_rows[1][0], zero], axis=1),
            jnp.concatenate([zero, wout_rows[1][1]], axis=1)], axis=0).astype(BF16)
    a_ref[0] = jnp.concatenate([jnp.concatenate([a[0] for a in a_rows], axis=1),
                                jnp.concatenate([a[1] for a in a_rows], axis=1)], axis=0)


def _ssm_prep(lam_re, lam_im, log_dt, b_re, b_im, c_re, c_im):
    G, P, H = b_re.shape
    ldt = jnp.broadcast_to(log_dt[:, None], (G, P))
    cols = jnp.stack([lam_re, lam_im, ldt])[..., None]
    rows = jnp.stack([lam_re, lam_im, ldt]).reshape(3, G // 2, 1, 2 * P)
    b = jnp.stack([b_re, b_im])
    ct = jnp.tile(jnp.stack([c_re, c_im]).transpose(0, 1, 3, 2), (1, 1, 1, CHUNK))
    bt = jnp.tile(b.transpose(0, 1, 3, 2), (1, 1, CHUNK, 2))
    gt = GROUPS_PER_TILE
    return pl.pallas_call(
        _ssm_prep_kernel,
        grid=(LANE_TILES,),
        in_specs=[
            pl.BlockSpec((3, gt, P, 1), lambda q: (0, q, 0, 0)),
            pl.BlockSpec((3, PAIRS_PER_TILE, 1, 2 * P), lambda q: (0, q, 0, 0)),
            pl.BlockSpec((2, gt, P, H), lambda q: (0, q, 0, 0)),
            pl.BlockSpec((2, gt, P, CHUNK_COLS), lambda q: (0, q, 0, 0)),
            pl.BlockSpec((2, gt, CHUNK_COLS, 2 * P), lambda q: (0, q, 0, 0)),
        ],
        out_specs=[
            pl.BlockSpec((1, gt, CHUNK_COLS, CHUNK_COLS), lambda q: (q, 0, 0, 0)),
            pl.BlockSpec((1, PAIRS_PER_TILE, 2 * CHUNK_COLS, 2 * LANES), lambda q: (q, 0, 0, 0)),
            pl.BlockSpec((1, PAIRS_PER_TILE, 2 * LANES, 2 * CHUNK_COLS), lambda q: (q, 0, 0, 0)),
            pl.BlockSpec((1, 2, STATE_COLS), lambda q: (q, 0, 0)),
        ],
        out_shape=[
            jax.ShapeDtypeStruct((LANE_TILES, gt, CHUNK_COLS, CHUNK_COLS), BF16),
            jax.ShapeDtypeStruct((LANE_TILES, PAIRS_PER_TILE, 2 * CHUNK_COLS, 2 * LANES), BF16),
            jax.ShapeDtypeStruct((LANE_TILES, PAIRS_PER_TILE, 2 * LANES, 2 * CHUNK_COLS), BF16),
            jax.ShapeDtypeStruct((LANE_TILES, 2, STATE_COLS), F32),
        ],
        compiler_params=_cparams(("parallel",)),
        name="ssm_prep",
    )(cols, rows, b, ct, bt)


def _layer_norm(r, g, b):
    mu = jnp.mean(r, axis=-1, keepdims=True)
    c = r - mu
    var = jnp.mean(c * c, axis=-1, keepdims=True)
    return c * lax.rsqrt(var + LN_EPS) * g + b


def _gelu_tanh(y):
    return 0.5 * y * (1.0 + jnp.tanh(math.sqrt(2.0 / math.pi) * (y + 0.044715 * (y * y * y))))


def _mix_ln1_kernel(x_ref, attn_ref, y_ref, wglu_ref, bglu_ref, wo_ref, g_ref, b_ref, o_ref, *, tl):
    b = pl.program_id(1)
    y = jnp.concatenate([y_ref[q, pl.ds(b, tl, stride=SUBLANES), :] for q in range(LANE_TILES)], axis=1)
    gl = _gelu_tanh(y)
    z = gl * jax.nn.sigmoid(jnp.dot(gl.astype(BF16), wglu_ref[...], preferred_element_type=F32) + bglu_ref[...])
    mix = (jnp.dot(attn_ref[0], wo_ref[:ATT_WIDTH], preferred_element_type=F32)
           + jnp.dot(z.astype(BF16), wo_ref[ATT_WIDTH:], preferred_element_type=F32))
    o_ref[0] = _layer_norm(ALPHA * x_ref[0] + mix, g_ref[...], b_ref[...])


def _mix_ln1(x, attn, y_tm, wglu_bf, b_glu, wo_bf, ln_g, ln_b, tl=512):
    B, L, D = x.shape
    n_l = L // tl
    const = lambda l, b: (0, 0)
    return pl.pallas_call(
        functools.partial(_mix_ln1_kernel, tl=tl),
        grid=(n_l, B),
        in_specs=[
            pl.BlockSpec((1, tl, D), lambda l, b: (b, l, 0)),
            pl.BlockSpec((1, tl, ATT_WIDTH), lambda l, b: (b, l, 0)),
            pl.BlockSpec((LANE_TILES, tl * B, LANES), lambda l, b: (0, l, 0)),
            pl.BlockSpec((SSM_WIDTH, SSM_WIDTH), const),
            pl.BlockSpec((1, SSM_WIDTH), const),
            pl.BlockSpec((D, D), const),
            pl.BlockSpec((1, D), const),
            pl.BlockSpec((1, D), const),
        ],
        out_specs=pl.BlockSpec((1, tl, D), lambda l, b: (b, l, 0)),
        out_shape=jax.ShapeDtypeStruct((B, L, D), F32),
        compiler_params=_cparams(("parallel", "arbitrary")),
        name="mix_ln1",
    )(x, attn, y_tm, wglu_bf, b_glu, wo_bf, ln_g, ln_b)


HALO = 16


def _ffn_kernel(x_ref, xh_ref, p_ref, wple_ref, wgate_ref, wg_ref, wv_ref, cwg_ref, cwv_ref, cbg_ref, cbv_ref,
                wd_ref, g_ref, b_ref, o_ref, xb_sc, acc_sc, hg_sc, hv_sc, *, tm, tiles_per_seq):
    i = pl.program_id(0)
    f = pl.program_id(1)

    @pl.when(f == 0)
    def _():
        keep = jnp.where(i % tiles_per_seq == 0, 0.0, 1.0)
        xb_sc[0:HALO] = (xh_ref[...] * keep).astype(BF16)
        xb_sc[HALO:] = x_ref[...].astype(BF16)
        gate = jax.nn.sigmoid(jnp.dot(xb_sc[HALO:], wgate_ref[...], preferred_element_type=F32))
        ple = jnp.dot(p_ref[...].astype(BF16), wple_ref[...], preferred_element_type=F32) * gate
        acc_sc[...] = ALPHA * x_ref[...] + ple

    xb = xb_sc[...]
    hg_sc[...] = jnp.dot(xb, wg_ref[...], preferred_element_type=F32)
    hv_sc[...] = jnp.dot(xb, wv_ref[...], preferred_element_type=F32)

    def conv(h_sc, w_ref, b_ref):
        out = b_ref[...] + w_ref[CONV_WIDTH - 1:CONV_WIDTH] * h_sc[HALO:]
        for j in range(CONV_WIDTH - 1):
            back = CONV_WIDTH - 1 - j
            out = out + w_ref[j:j + 1] * h_sc[HALO - back:HALO - back + tm]
        return out

    gc = conv(hg_sc, cwg_ref, cbg_ref)
    vc = conv(hv_sc, cwv_ref, cbv_ref)
    act = (gc * jax.nn.sigmoid(gc) * vc).astype(BF16)
    acc_sc[...] += jnp.dot(act, wd_ref[...], preferred_element_type=F32)

    @pl.when(f == pl.num_programs(1) - 1)
    def _():
        o_ref[...] = _layer_norm(acc_sc[...], g_ref[...], b_ref[...])


def _ffn_ln2(x1, p, wple_bf, wgate_bf, wup_bf, conv_w, conv_b, wdown_bf, ln_g, ln_b, L, tm=512, tf=1408):
    T, D = x1.shape
    n_t = T // tm
    n_f = D_FF // tf
    const = lambda i, f: (0, 0)
    return pl.pallas_call(
        functools.partial(_ffn_kernel, tm=tm, tiles_per_seq=L // tm),
        grid=(n_t, n_f),
        in_specs=[
            pl.BlockSpec((tm, D), lambda i, f: (i, 0)),
            pl.BlockSpec((HALO, D), lambda i, f: (jnp.maximum(i * (tm // HALO) - 1, 0), 0)),
            pl.BlockSpec((tm, PLE_DIM), lambda i, f: (i, 0)),
            pl.BlockSpec((PLE_DIM, D), const),
            pl.BlockSpec((D, D), const),
            pl.BlockSpec((D, tf), lambda i, f: (0, f)),
            pl.BlockSpec((D, tf), lambda i, f: (0, f + n_f)),
            pl.BlockSpec((CONV_WIDTH, tf), lambda i, f: (0, f)),
            pl.BlockSpec((CONV_WIDTH, tf), lambda i, f: (0, f + n_f)),
            pl.BlockSpec((1, tf), lambda i, f: (0, f)),
            pl.BlockSpec((1, tf), lambda i, f: (0, f + n_f)),
            pl.BlockSpec((tf, D), lambda i, f: (f, 0)),
            pl.BlockSpec((1, D), const),
            pl.BlockSpec((1, D), const),
        ],
        out_specs=pl.BlockSpec((tm, D), lambda i, f: (i, 0)),
        out_shape=jax.ShapeDtypeStruct((T, D), F32),
        scratch_shapes=[
            pltpu.VMEM((tm + HALO, D), BF16),
            pltpu.VMEM((tm, D), F32),
            pltpu.VMEM((tm + HALO, tf), F32),
            pltpu.VMEM((tm + HALO, tf), F32),
        ],
        compiler_params=_cparams(("parallel", "arbitrary")),
        name="ffn_ln2",
    )(x1, x1, p, wple_bf, wgate_bf, wup_bf, wup_bf, conv_w, conv_w, conv_b, conv_b, wdown_bf, ln_g, ln_b)


def kernel(x, p, w_in, diff_lambda_q1, diff_lambda_k1, diff_lambda_q2, diff_lambda_k2, diff_subln_g, ssm_lambda_re, ssm_lambda_im, ssm_log_dt, ssm_b_re, ssm_b_im, ssm_c_re, ssm_c_im, ssm_d, ssm_w_glu, ssm_b_glu, w_o, ln1_g, ln1_b, ffn_w_up, ffn_conv_w, ffn_conv_b, ffn_w_down, w_ple, w_ple_gate, ln2_g, ln2_b):
    B, L, D = x.shape
    assert B == SUBLANES and D == D_MODEL and L % 512 == 0
    for i in range(DEPTH):
        lam_init = 0.8 - 0.6 * math.exp(-0.3 * i)
        row = lambda a: a[i].reshape(1, -1)
        qt, k, vt, u_tm = _in_proj(x, w_in[i])
        attn = _diff_attn(qt, k, vt, row(diff_lambda_q1), row(diff_lambda_k1), row(diff_lambda_q2),
                          row(diff_lambda_k2), diff_subln_g[i], lam_init)
        tables = _ssm_prep(ssm_lambda_re[i], ssm_lambda_im[i], ssm_log_dt[i], ssm_b_re[i], ssm_b_im[i],
                           ssm_c_re[i], ssm_c_im[i])
        y_tm = _ssm(u_tm, tables, ssm_d[i], B, L)
        x1 = _mix_ln1(x, attn, y_tm, ssm_w_glu[i].astype(BF16), row(ssm_b_glu), w_o[i].astype(BF16),
                      row(ln1_g), row(ln1_b))
        x2 = _ffn_ln2(x1.reshape(B * L, D), p[i].reshape(B * L, PLE_DIM), w_ple[i].astype(BF16),
                      w_ple_gate[i].astype(BF16), ffn_w_up[i].astype(BF16), ffn_conv_w[i], row(ffn_conv_b),
                      ffn_w_down[i].astype(BF16), row(ln2_g), row(ln2_b), L)
        x = x2.reshape(B, L, D)
    return x
```

```python
import functools
import math

import jax
import jax.numpy as jnp
from jax import lax
from jax.experimental import pallas as pl
from jax.experimental.pallas import tpu as pltpu

F32 = jnp.float32
BF16 = jnp.bfloat16

SUBLANES = 8
LANES = 128

DEPTH = 1
D_MODEL = 1024
PLE_DIM = 256
HEADS = 4
QK_DIM = 64
V_DIM = 2 * QK_DIM
QK_WIDTH = HEADS * 2 * QK_DIM
ATT_WIDTH = HEADS * V_DIM
SSM_WIDTH = D_MODEL - ATT_WIDTH
SSM_GROUP = 16
SSM_GROUPS = SSM_WIDTH // SSM_GROUP
SSM_STATE = 64
D_FF = 2816
CONV_WIDTH = 3
LN_EPS = 1e-5
ALPHA = (2 * DEPTH) ** 0.25
QK_SCALE = QK_DIM ** -0.5 * math.log2(math.e)

CHUNK = 16
CHUNK_COLS = CHUNK * SSM_GROUP
GROUPS_PER_TILE = LANES // SSM_GROUP
LANE_TILES = SSM_WIDTH // LANES
PAIRS_PER_TILE = GROUPS_PER_TILE // 2
STATE_COLS = GROUPS_PER_TILE * SSM_STATE

VMEM_LIMIT = 56 * 1024 * 1024


def _cparams(sem):
    return pltpu.CompilerParams(dimension_semantics=sem, vmem_limit_bytes=VMEM_LIMIT)


TB = 256
HEADS_PER_STEP = 4


def _in_proj_kernel(x_ref, wqt_ref, wk_ref, wvt_ref, wu_ref, qt_ref, k_ref, vt_ref, u_ref, *, tl):
    b = pl.program_id(1)
    xb = x_ref[0].astype(BF16)
    nt = (((1,), (1,)), ((), ()))
    qt = lax.dot_general(wqt_ref[...], xb, nt, preferred_element_type=F32) * QK_SCALE
    vt = lax.dot_general(wvt_ref[...], xb, nt, preferred_element_type=F32)
    for h in range(HEADS):
        for c in range(tl // TB):
            qt_ref[0, h, c] = qt[h * LANES:(h + 1) * LANES, c * TB:(c + 1) * TB].astype(BF16)
            vt_ref[0, h, c] = vt[h * LANES:(h + 1) * LANES, c * TB:(c + 1) * TB].astype(BF16)
    k_ref[0] = jnp.dot(xb, wk_ref[...], preferred_element_type=F32).astype(BF16)
    hu = jnp.dot(xb, wu_ref[...], preferred_element_type=F32)
    for q in range(LANE_TILES):
        u_ref[q, pl.ds(b, tl, stride=SUBLANES), :] = hu[:, q * LANES:(q + 1) * LANES]


def _in_proj(x, w_in, tl=512):
    B, L, D = x.shape
    n_l = L // tl
    w = w_in.astype(BF16)
    wqt = w[:, :QK_WIDTH].T
    wk = w[:, QK_WIDTH:2 * QK_WIDTH]
    wvt = w[:, 2 * QK_WIDTH:2 * QK_WIDTH + ATT_WIDTH].T
    wu = w[:, 2 * QK_WIDTH + ATT_WIDTH:]
    const = lambda l, b: (0, 0)
    slab = pl.BlockSpec((1, HEADS, tl // TB, LANES, TB), lambda l, b: (b, 0, l, 0, 0))
    slab_shape = jax.ShapeDtypeStruct((B, HEADS, L // TB, LANES, TB), BF16)
    return pl.pallas_call(
        functools.partial(_in_proj_kernel, tl=tl),
        grid=(n_l, B),
        in_specs=[
            pl.BlockSpec((1, tl, D), lambda l, b: (b, l, 0)),
            pl.BlockSpec((QK_WIDTH, D), const),
            pl.BlockSpec((D, QK_WIDTH), const),
            pl.BlockSpec((ATT_WIDTH, D), const),
            pl.BlockSpec((D, SSM_WIDTH), const),
        ],
        out_specs=[
            slab,
            pl.BlockSpec((1, tl, QK_WIDTH), lambda l, b: (b, l, 0)),
            slab,
            pl.BlockSpec((LANE_TILES, tl * B, LANES), lambda l, b: (0, l, 0)),
        ],
        out_shape=[
            slab_shape,
            jax.ShapeDtypeStruct((B, L, QK_WIDTH), BF16),
            slab_shape,
            jax.ShapeDtypeStruct((LANE_TILES, L * B, LANES), F32),
        ],
        compiler_params=_cparams(("arbitrary", "arbitrary")),
        name="in_proj",
    )(x, wqt, wk, wvt, wu)


def _attn_kernel(lq1_ref, lk1_ref, lq2_ref, lk2_ref, g_ref, qt_ref, k_ref, vt_ref, o_ref,
                 acc_sc, qst_sc, s_sc, p_sc, bm_sc, st_sc, *, lam_init):
    qi = pl.program_id(2)
    lam = (jnp.exp(jnp.sum(lq1_ref[...] * lk1_ref[...], axis=-1, keepdims=True))
           - jnp.exp(jnp.sum(lq2_ref[...] * lk2_ref[...], axis=-1, keepdims=True)) + lam_init)

    for hh in range(HEADS_PER_STEP):
        qt = qt_ref[0, hh, 0]
        row = lax.broadcasted_iota(jnp.int32, qt.shape, 0)
        zero = jnp.zeros_like(qt)
        qst_sc[hh, :, :TB] = jnp.where(row < QK_DIM, qt, zero)
        qst_sc[hh, :, TB:] = jnp.where(row >= QK_DIM, qt, zero)

    acc_sc[...] = jnp.zeros(acc_sc.shape, F32)
    p_sc[1] = jnp.zeros(p_sc.shape[1:], BF16)
    n_strips = 2 * TB // LANES
    heads = range(HEADS_PER_STEP)
    assert TB == 2 * LANES

    def scores(j, slot):
        rows = pl.ds(pl.multiple_of(j * TB, TB), TB)
        for hh in heads:
            kb = k_ref[0, rows, hh * LANES:(hh + 1) * LANES]
            s = jnp.dot(kb, qst_sc[hh], preferred_element_type=F32)
            s_sc[slot, hh] = s
            bm_sc[slot, hh] = jnp.max(s, axis=0, keepdims=True)

    def values(j, slot):
        j = jnp.maximum(j, 0)
        for hh in heads:
            pv = jnp.dot(vt_ref[0, hh, j], p_sc[slot, hh], preferred_element_type=F32)
            acc_sc[hh] = st_sc[hh, 2] * acc_sc[hh] + pv

    def softmax(slot, diagonal):
        for hh in heads:
            for c in range(n_strips):
                lanes = slice(c * LANES, (c + 1) * LANES)
                m = st_sc[hh, 0, :, lanes]
                if not diagonal:
                    mc = jnp.maximum(m, bm_sc[slot, hh, :, lanes])
                    p = jnp.exp2(s_sc[slot, hh, :, lanes] - mc)
                    p_sc[slot, hh, :, lanes] = p.astype(BF16)
                    psum = jnp.sum(p, axis=0, keepdims=True)
                else:
                    q0 = (c * LANES) % TB
                    s = s_sc[slot, hh, :, lanes]
                    key = lax.broadcasted_iota(jnp.int32, s.shape, 0)
                    qry = lax.broadcasted_iota(jnp.int32, s.shape, 1) + q0
                    s = jnp.where(key <= qry, s, -jnp.inf)
                    mc = jnp.maximum(m, jnp.max(s, axis=0, keepdims=True))
                    p = jnp.exp2(s - mc)
                    p_sc[slot, hh, :, lanes] = p.astype(BF16)
                    psum = jnp.sum(p, axis=0, keepdims=True)
                ac = jnp.exp2(m - mc)
                st_sc[hh, 0, :, lanes] = mc
                st_sc[hh, 1, :, lanes] = ac * st_sc[hh, 1, :, lanes] + psum
                st_sc[hh, 2, :, lanes] = ac

    def pair(t, c):
        a = 2 * t
        scores(a + 1, 1)
        values(a - 1, 1)
        softmax(0, False)
        scores(a + 2, 0)
        values(a, 0)
        softmax(1, False)
        return c

    for hh in heads:
        st_sc[hh, 0] = jnp.full((1, 2 * TB), -jnp.inf, F32)
        st_sc[hh, 1] = jnp.zeros((1, 2 * TB), F32)
        st_sc[hh, 2] = jnp.ones((1, 2 * TB), F32)
    scores(0, 0)
    lax.fori_loop(0, qi // 2, pair, 0)

    @pl.when(qi % 2 == 1)
    def _():
        values(qi - 2, 1)
        scores(qi, 1)
        softmax(0, False)
        values(qi - 1, 0)
        softmax(1, True)
        values(qi, 1)

    @pl.when(qi % 2 == 0)
    def _():
        values(qi - 1, 1)
        softmax(0, True)
        values(qi, 0)

    l_fin = [st_sc[hh, 1] for hh in heads]

    for hh in range(HEADS_PER_STEP):
        o = acc_sc[hh] * (1.0 / l_fin[hh])
        o = o[:, :TB] - lam * o[:, TB:]
        ms = jnp.mean(o * o, axis=0, keepdims=True)
        o = o * lax.rsqrt(ms + LN_EPS) * (g_ref[...] * (1.0 - lam_init))
        o_ref[0, :, hh * LANES:(hh + 1) * LANES] = o.T.astype(o_ref.dtype)


def _diff_attn(qt, k, vt, lq1, lk1, lq2, lk2, subln_g, lam_init):
    B, L, _ = k.shape
    n_q = L // TB
    hp = HEADS_PER_STEP
    vec = pl.BlockSpec((1, QK_DIM), lambda b, h, i: (0, 0))
    return pl.pallas_call(
        functools.partial(_attn_kernel, lam_init=lam_init),
        grid=(B, HEADS // hp, n_q),
        in_specs=[
            vec, vec, vec, vec,
            pl.BlockSpec((V_DIM, 1), lambda b, h, i: (0, 0)),
            pl.BlockSpec((1, hp, 1, LANES, TB), lambda b, h, i: (b, h, i, 0, 0)),
            pl.BlockSpec((1, L, hp * LANES), lambda b, h, i: (b, 0, h)),
            pl.BlockSpec((1, hp, n_q, LANES, TB), lambda b, h, i: (b, h, 0, 0, 0)),
        ],
        out_specs=pl.BlockSpec((1, TB, hp * LANES), lambda b, h, i: (b, i, h)),
        out_shape=jax.ShapeDtypeStruct((B, L, ATT_WIDTH), BF16),
        scratch_shapes=[
            pltpu.VMEM((hp, V_DIM, 2 * TB), F32),
            pltpu.VMEM((hp, LANES, 2 * TB), BF16),
            pltpu.VMEM((2, hp, TB, 2 * TB), F32),
            pltpu.VMEM((2, hp, TB, 2 * TB), BF16),
            pltpu.VMEM((2, hp, 1, 2 * TB), F32),
            pltpu.VMEM((hp, 3, 1, 2 * TB), F32),
        ],
        compiler_params=_cparams(("parallel", "parallel", "arbitrary")),
        name="diff_attn",
    )(lq1, lk1, lq2, lk2, subln_g.reshape(V_DIM, 1), qt, k, vt)


def _block_transpose8(vs):
    lane_blk = lax.broadcasted_iota(jnp.int32, vs[0].shape, 1) // SSM_GROUP
    for d in (4, 2, 1):
        keep = (lane_blk & d) == 0
        new = list(vs)
        for a in range(8):
            if a & d == 0:
                lo, hi = vs[a], vs[a + d]
                new[a] = jnp.where(keep, lo, pltpu.roll(hi, SSM_GROUP * d, 1))
                new[a + d] = jnp.where(keep, pltpu.roll(lo, LANES - SSM_GROUP * d, 1), hi)
        vs = new
    return vs


def _ssm_kernel(u_ref, toep_ref, win_ref, wout_ref, a_ref, d_ref, y_ref, s_sc, x_sc, sp_sc, *, jt):
    rows = jt * SUBLANES

    @pl.when(pl.program_id(1) == 0)
    def _():
        s_sc[...] = jnp.zeros(s_sc.shape, F32)

    xs = [u_ref[0, :, r].reshape(rows, LANES) for r in range(CHUNK)]
    z_lo = _block_transpose8(xs[:8])
    z_hi = _block_transpose8(xs[8:])
    ug = [jnp.concatenate([z_lo[g], z_hi[g]], axis=1).astype(BF16) for g in range(GROUPS_PER_TILE)]

    y = [jnp.dot(ug[g], toep_ref[0, g], preferred_element_type=F32) for g in range(GROUPS_PER_TILE)]

    for m in range(PAIRS_PER_TILE):
        up = jnp.concatenate([ug[2 * m], ug[2 * m + 1]], axis=1)
        xp = jnp.dot(up, win_ref[0, m], preferred_element_type=F32)
        x_sc[0, :, m * LANES:(m + 1) * LANES] = xp[:, :LANES]
        x_sc[1, :, m * LANES:(m + 1) * LANES] = xp[:, LANES:]

    ar = a_ref[0, 0:1, :]
    ai = a_ref[0, 1:2, :]

    def step(j, carry):
        sr, si = carry
        r8 = pl.ds(pl.multiple_of(j * SUBLANES, SUBLANES), SUBLANES)
        sp_sc[0, r8, :] = sr
        sp_sc[1, r8, :] = si
        return (ar * sr - ai * si + x_sc[0, r8, :], ar * si + ai * sr + x_sc[1, r8, :])

    sr, si = lax.fori_loop(0, jt, step, (s_sc[0], s_sc[1]))
    s_sc[0] = sr
    s_sc[1] = si

    for m in range(PAIRS_PER_TILE):
        sp = jnp.concatenate([sp_sc[0, :, m * LANES:(m + 1) * LANES],
                              sp_sc[1, :, m * LANES:(m + 1) * LANES]], axis=1).astype(BF16)
        yc = jnp.dot(sp, wout_ref[0, m], preferred_element_type=F32)
        y[2 * m] = y[2 * m] + yc[:, :CHUNK_COLS]
        y[2 * m + 1] = y[2 * m + 1] + yc[:, CHUNK_COLS:]

    d = d_ref[0]
    for half in range(2):
        ws = _block_transpose8([y[g][:, half * LANES:(half + 1) * LANES] for g in range(GROUPS_PER_TILE)])
        for t8 in range(8):
            t = half * 8 + t8
            u_t = u_ref[0, :, t].reshape(rows, LANES)
            y_ref[0, :, t] = (ws[t8] + d * u_t).reshape(jt, SUBLANES, LANES)


def _ssm(u_tm, tables, d_skip, B, L, lt=512):
    toep, win, wout, a16 = tables
    jt = lt // CHUNK
    n_l = L // lt
    u5 = u_tm.reshape(LANE_TILES, L // CHUNK, CHUNK, B, LANES)
    rows = jt * B
    y5 = pl.pallas_call(
        functools.partial(_ssm_kernel, jt=jt),
        grid=(LANE_TILES, n_l),
        in_specs=[
            pl.BlockSpec((1, jt, CHUNK, B, LANES), lambda q, l: (q, l, 0, 0, 0)),
            pl.BlockSpec((1, GROUPS_PER_TILE, CHUNK_COLS, CHUNK_COLS), lambda q, l: (q, 0, 0, 0)),
            pl.BlockSpec((1, PAIRS_PER_TILE, 2 * CHUNK_COLS, 2 * LANES), lambda q, l: (q, 0, 0, 0)),
            pl.BlockSpec((1, PAIRS_PER_TILE, 2 * LANES, 2 * CHUNK_COLS), lambda q, l: (q, 0, 0, 0)),
            pl.BlockSpec((1, 2, STATE_COLS), lambda q, l: (q, 0, 0)),
            pl.BlockSpec((1, 1, LANES), lambda q, l: (q, 0, 0)),
        ],
        out_specs=pl.BlockSpec((1, jt, CHUNK, B, LANES), lambda q, l: (q, l, 0, 0, 0)),
        out_shape=jax.ShapeDtypeStruct(u5.shape, F32),
        scratch_shapes=[
            pltpu.VMEM((2, B, STATE_COLS), F32),
            pltpu.VMEM((2, rows, STATE_COLS), F32),
            pltpu.VMEM((2, rows, STATE_COLS), F32),
        ],
        compiler_params=_cparams(("parallel", "arbitrary")),
        name="ssm",
    )(u5, toep, win, wout, a16, d_skip.reshape(LANE_TILES, 1, LANES))
    return y5.reshape(LANE_TILES, L * B, LANES)


def _cmul(a, b):
    return a[0] * b[0] - a[1] * b[1], a[0] * b[1] + a[1] * b[0]


def _zoh(lr, li, ldt):
    dt = jnp.exp(ldt)
    mag = jnp.exp(lr * dt)
    ab = (mag * jnp.cos(li * dt), mag * jnp.sin(li * dt))
    den = lr * lr + li * li
    zr, zi = ab[0] - 1.0, ab[1]
    return ab, ((zr * lr + zi * li) / den, (zi * lr - zr * li) / den)


def _cpow(base, exponent, bits):
    out = None
    for k in range(bits):
        on = (exponent & (1 << k)) != 0
        sel = (jnp.where(on, base[0], 1.0), jnp.where(on, base[1], 0.0))
        out = sel if out is None else _cmul(out, sel)
        if k + 1 < bits:
            base = _cmul(base, base)
    return out


def _ssm_prep_kernel(col_ref, row_ref, b_ref, ct_ref, bt_ref, toep_ref, win_ref, wout_ref, a_ref):
    bits = CHUNK.bit_length() - 1
    lane = lax.broadcasted_iota(jnp.int32, (1, CHUNK_COLS), 1)
    tau = lane // SSM_GROUP
    rowi = lax.broadcasted_iota(jnp.int32, (CHUNK_COLS, 1), 0)
    back = (CHUNK - 1) - rowi // SSM_GROUP
    lane2 = lax.broadcasted_iota(jnp.int32, (1, LANES), 1)
    a_rows = []
    for m in range(PAIRS_PER_TILE):
        abp, fp = _zoh(row_ref[0, m], row_ref[1, m], row_ref[2, m])
        pwr = _cpow(abp, back, bits)
        a16 = abp
        for _ in range(bits):
            a16 = _cmul(a16, a16)
        a_rows.append(a16)
        wout_rows = [[None, None], [None, None]]
        for j in range(2):
            g = 2 * m + j
            ab, f = _zoh(col_ref[0, g], col_ref[1, g], col_ref[2, g])
            bb = _cmul(f, (b_ref[0, g], b_ref[1, g]))
            pw = _cpow(ab, tau, bits)
            g0 = _cmul(pw, (ct_ref[0, g], ct_ref[1, g]))
            g1 = _cmul(g0, ab)
            kt = jnp.concatenate(
                [jnp.sum(bb[0][:, h:h + 1] * g0[0] - bb[1][:, h:h + 1] * g0[1], axis=0, keepdims=True)
                 for h in range(SSM_GROUP)], axis=0)
            toep_ref[0, g] = jnp.concatenate(
                [jnp.where(lane >= SSM_GROUP * r, pltpu.roll(kt, SSM_GROUP * r, 1), 0.0) if r else kt
                 for r in range(CHUNK)], axis=0).astype(BF16)
            bb2 = _cmul(fp, (bt_ref[0, g], bt_ref[1, g]))
            w = _cmul(pwr, bb2)
            mine = (lane2 < SSM_STATE) if j == 0 else (lane2 >= SSM_STATE)
            win_ref[0, m, j * CHUNK_COLS:(j + 1) * CHUNK_COLS, :] = jnp.concatenate(
                [jnp.where(mine, w[0], 0.0), jnp.where(mine, w[1], 0.0)], axis=1).astype(BF16)
            wout_rows[0][j] = g1[0]
            wout_rows[1][j] = -g1[1]
        zero = jnp.zeros((SSM_STATE, CHUNK_COLS), F32)
        wout_ref[0, m] = jnp.concatenate([
            jnp.concatenate([wout_rows[0][0], zero], axis=1),
            jnp.concatenate([zero, wout_rows[0][1]], axis=1),
            jnp.concatenate([wout_rows[1][0], zero], axis=1),
            jnp.concatenate([zero, wout_rows[1][1]], axis=1)], axis=0).astype(BF16)
    a_ref[0] = jnp.concatenate([jnp.concatenate([a[0] for a in a_rows], axis=1),
                                jnp.concatenate([a[1] for a in a_rows], axis=1)], axis=0)


def _ssm_prep(lam_re, lam_im, log_dt, b_re, b_im, c_re, c_im):
    G, P, H = b_re.shape
    ldt = jnp.broadcast_to(log_dt[:, None], (G, P))
    cols = jnp.stack([lam_re, lam_im, ldt])[..., None]
    rows = jnp.stack([lam_re, lam_im, ldt]).reshape(3, G // 2, 1, 2 * P)
    b = jnp.stack([b_re, b_im])
    ct = jnp.tile(jnp.stack([c_re, c_im]).transpose(0, 1, 3, 2), (1, 1, 1, CHUNK))
    bt = jnp.tile(b.transpose(0, 1, 3, 2), (1, 1, CHUNK, 2))
    gt = GROUPS_PER_TILE
    return pl.pallas_call(
        _ssm_prep_kernel,
        grid=(LANE_TILES,),
        in_specs=[
            pl.BlockSpec((3, gt, P, 1), lambda q: (0, q, 0, 0)),
            pl.BlockSpec((3, PAIRS_PER_TILE, 1, 2 * P), lambda q: (0, q, 0, 0)),
            pl.BlockSpec((2, gt, P, H), lambda q: (0, q, 0, 0)),
            pl.BlockSpec((2, gt, P, CHUNK_COLS), lambda q: (0, q, 0, 0)),
            pl.BlockSpec((2, gt, CHUNK_COLS, 2 * P), lambda q: (0, q, 0, 0)),
        ],
        out_specs=[
            pl.BlockSpec((1, gt, CHUNK_COLS, CHUNK_COLS), lambda q: (q, 0, 0, 0)),
            pl.BlockSpec((1, PAIRS_PER_TILE, 2 * CHUNK_COLS, 2 * LANES), lambda q: (q, 0, 0, 0)),
            pl.BlockSpec((1, PAIRS_PER_TILE, 2 * LANES, 2 * CHUNK_COLS), lambda q: (q, 0, 0, 0)),
            pl.BlockSpec((1, 2, STATE_COLS), lambda q: (q, 0, 0)),
        ],
        out_shape=[
            jax.ShapeDtypeStruct((LANE_TILES, gt, CHUNK_COLS, CHUNK_COLS), BF16),
            jax.ShapeDtypeStruct((LANE_TILES, PAIRS_PER_TILE, 2 * CHUNK_COLS, 2 * LANES), BF16),
            jax.ShapeDtypeStruct((LANE_TILES, PAIRS_PER_TILE, 2 * LANES, 2 * CHUNK_COLS), BF16),
            jax.ShapeDtypeStruct((LANE_TILES, 2, STATE_COLS), F32),
        ],
        compiler_params=_cparams(("parallel",)),
        name="ssm_prep",
    )(cols, rows, b, ct, bt)


def _layer_norm(r, g, b):
    mu = jnp.mean(r, axis=-1, keepdims=True)
    c = r - mu
    var = jnp.mean(c * c, axis=-1, keepdims=True)
    return c * lax.rsqrt(var + LN_EPS) * g + b


def _gelu_tanh(y):
    return 0.5 * y * (1.0 + jnp.tanh(math.sqrt(2.0 / math.pi) * (y + 0.044715 * (y * y * y))))


def _mix_ln1_kernel(x_ref, attn_ref, y_ref, wglu_ref, bglu_ref, wo_ref, g_ref, b_ref, o_ref, *, tl):
    b = pl.program_id(1)
    y = jnp.concatenate([y_ref[q, pl.ds(b, tl, stride=SUBLANES), :] for q in range(LANE_TILES)], axis=1)
    gl = _gelu_tanh(y)
    z = gl * jax.nn.sigmoid(jnp.dot(gl.astype(BF16), wglu_ref[...], preferred_element_type=F32) + bglu_ref[...])
    mix = (jnp.dot(attn_ref[0], wo_ref[:ATT_WIDTH], preferred_element_type=F32)
           + jnp.dot(z.astype(BF16), wo_ref[ATT_WIDTH:], preferred_element_type=F32))
    o_ref[0] = _layer_norm(ALPHA * x_ref[0] + mix, g_ref[...], b_ref[...])


def _mix_ln1(x, attn, y_tm, wglu_bf, b_glu, wo_bf, ln_g, ln_b, tl=512):
    B, L, D = x.shape
    n_l = L // tl
    const = lambda l, b: (0, 0)
    return pl.pallas_call(
        functools.partial(_mix_ln1_kernel, tl=tl),
        grid=(n_l, B),
        in_specs=[
            pl.BlockSpec((1, tl, D), lambda l, b: (b, l, 0)),
            pl.BlockSpec((1, tl, ATT_WIDTH), lambda l, b: (b, l, 0)),
            pl.BlockSpec((LANE_TILES, tl * B, LANES), lambda l, b: (0, l, 0)),
            pl.BlockSpec((SSM_WIDTH, SSM_WIDTH), const),
            pl.BlockSpec((1, SSM_WIDTH), const),
            pl.BlockSpec((D, D), const),
            pl.BlockSpec((1, D), const),
            pl.BlockSpec((1, D), const),
        ],
        out_specs=pl.BlockSpec((1, tl, D), lambda l, b: (b, l, 0)),
        out_shape=jax.ShapeDtypeStruct((B, L, D), F32),
        compiler_params=_cparams(("parallel", "arbitrary")),
        name="mix_ln1",
    )(x, attn, y_tm, wglu_bf, b_glu, wo_bf, ln_g, ln_b)


HALO = 16
FFN_SUB = 256
FFN_ROWS = 128


def _ffn_kernel(x_ref, xh_ref, p_ref, wple_ref, wgate_ref, wup_ref, cw_ref, cb_ref, wd_ref, g_ref, b_ref, o_ref,
                xb_sc, acc_sc, hg_sc, hv_sc, act_sc, *, tm, tiles_per_seq):
    keep = jnp.where(pl.program_id(0) % tiles_per_seq == 0, 0.0, 1.0)
    xb_sc[0:HALO] = (xh_ref[...] * keep).astype(BF16)
    xb_sc[HALO:] = x_ref[...].astype(BF16)
    gate = jax.nn.sigmoid(jnp.dot(xb_sc[HALO:], wgate_ref[...], preferred_element_type=F32))
    ple = jnp.dot(p_ref[...].astype(BF16), wple_ref[...], preferred_element_type=F32) * gate
    acc_sc[...] = ALPHA * x_ref[...] + ple

    def conv(h_sc, slot, cols, r0, rows):
        out = cb_ref[:, cols] + cw_ref[CONV_WIDTH - 1:CONV_WIDTH, cols] * h_sc[slot, HALO + r0:HALO + r0 + rows]
        for j in range(CONV_WIDTH - 1):
            back = CONV_WIDTH - 1 - j
            out = out + cw_ref[j:j + 1, cols] * h_sc[slot, HALO + r0 - back:HALO + r0 - back + rows]
        return out

    n_sub = D_FF // FFN_SUB
    g_cols = lambda n: slice(n * FFN_SUB, (n + 1) * FFN_SUB)
    v_cols = lambda n: slice(D_FF + n * FFN_SUB, D_FF + (n + 1) * FFN_SUB)

    def up(n):
        xb = xb_sc[...]
        hg_sc[n % 2] = jnp.dot(xb, wup_ref[:, g_cols(n)], preferred_element_type=F32)
        hv_sc[n % 2] = jnp.dot(xb, wup_ref[:, v_cols(n)], preferred_element_type=F32)

    def gate_block(n):
        for r0 in range(0, tm, FFN_ROWS):
            gc = conv(hg_sc, n % 2, g_cols(n), r0, FFN_ROWS)
            vc = conv(hv_sc, n % 2, v_cols(n), r0, FFN_ROWS)
            act_sc[r0:r0 + FFN_ROWS, g_cols(n)] = (gc * jax.nn.sigmoid(gc) * vc).astype(BF16)

    def down(lo, hi):
        cols = slice(lo * FFN_SUB, hi * FFN_SUB)
        return jnp.dot(act_sc[:, cols], wd_ref[cols, :], preferred_element_type=F32)

    half = (n_sub + 1) // 2
    up(0)
    for n in range(n_sub):
        if n + 1 < n_sub:
            up(n + 1)
        gate_block(n)
        if n + 1 == half:
            acc_sc[...] += down(0, half)

    o_ref[...] = _layer_norm(acc_sc[...] + down(half, n_sub), g_ref[...], b_ref[...])


def _ffn_ln2(x1, p, wple_bf, wgate_bf, wup_bf, conv_w, conv_b, wdown_bf, ln_g, ln_b, L, tm=512):
    T, D = x1.shape
    assert D_FF % FFN_SUB == 0
    const = lambda i: (0, 0)
    resident = lambda shape: pl.BlockSpec(shape, const, pipeline_mode=pl.Buffered(1))
    return pl.pallas_call(
        functools.partial(_ffn_kernel, tm=tm, tiles_per_seq=L // tm),
        grid=(T // tm,),
        in_specs=[
            pl.BlockSpec((tm, D), lambda i: (i, 0)),
            pl.BlockSpec((HALO, D), lambda i: (jnp.maximum(i * (tm // HALO) - 1, 0), 0)),
            pl.BlockSpec((tm, PLE_DIM), lambda i: (i, 0)),
            resident((PLE_DIM, D)),
            resident((D, D)),
            resident((D, 2 * D_FF)),
            resident((CONV_WIDTH, 2 * D_FF)),
            resident((1, 2 * D_FF)),
            resident((D_FF, D)),
            resident((1, D)),
            resident((1, D)),
        ],
        out_specs=pl.BlockSpec((tm, D), lambda i: (i, 0)),
        out_shape=jax.ShapeDtypeStruct((T, D), F32),
        scratch_shapes=[
            pltpu.VMEM((tm + HALO, D), BF16),
            pltpu.VMEM((tm, D), F32),
            pltpu.VMEM((2, tm + HALO, FFN_SUB), F32),
            pltpu.VMEM((2, tm + HALO, FFN_SUB), F32),
            pltpu.VMEM((tm, D_FF), BF16),
        ],
        compiler_params=_cparams(("parallel",)),
        name="ffn_ln2",
    )(x1, x1, p, wple_bf, wgate_bf, wup_bf, conv_w, conv_b, wdown_bf, ln_g, ln_b)


def kernel(x, p, w_in, diff_lambda_q1, diff_lambda_k1, diff_lambda_q2, diff_lambda_k2, diff_subln_g, ssm_lambda_re, ssm_lambda_im, ssm_log_dt, ssm_b_re, ssm_b_im, ssm_c_re, ssm_c_im, ssm_d, ssm_w_glu, ssm_b_glu, w_o, ln1_g, ln1_b, ffn_w_up, ffn_conv_w, ffn_conv_b, ffn_w_down, w_ple, w_ple_gate, ln2_g, ln2_b):
    B, L, D = x.shape
    assert B == SUBLANES and D == D_MODEL and L % 512 == 0
    for i in range(DEPTH):
        lam_init = 0.8 - 0.6 * math.exp(-0.3 * i)
        row = lambda a: a[i].reshape(1, -1)
        qt, k, vt, u_tm = _in_proj(x, w_in[i])
        attn = _diff_attn(qt, k, vt, row(diff_lambda_q1), row(diff_lambda_k1), row(diff_lambda_q2),
                          row(diff_lambda_k2), diff_subln_g[i], lam_init)
        tables = _ssm_prep(ssm_lambda_re[i], ssm_lambda_im[i], ssm_log_dt[i], ssm_b_re[i], ssm_b_im[i],
                           ssm_c_re[i], ssm_c_im[i])
        y_tm = _ssm(u_tm, tables, ssm_d[i], B, L)
        x1 = _mix_ln1(x, attn, y_tm, ssm_w_glu[i].astype(BF16), row(ssm_b_glu), w_o[i].astype(BF16),
                      row(ln1_g), row(ln1_b))
        x2 = _ffn_ln2(x1.reshape(B * L, D), p[i].reshape(B * L, PLE_DIM), w_ple[i].astype(BF16),
                      w_ple_gate[i].astype(BF16), ffn_w_up[i].astype(BF16), ffn_conv_w[i], row(ffn_conv_b),
                      ffn_w_down[i].astype(BF16), row(ln2_g), row(ln2_b), L)
        x = x2.reshape(B, L, D)
    return x
```

```python
import functools
import math

import jax
import jax.numpy as jnp
from jax import lax
from jax.experimental import pallas as pl
from jax.experimental.pallas import tpu as pltpu

F32 = jnp.float32
BF16 = jnp.bfloat16

SUBLANES = 8
LANES = 128

DEPTH = 1
D_MODEL = 1024
PLE_DIM = 256
HEADS = 4
QK_DIM = 64
V_DIM = 2 * QK_DIM
QK_WIDTH = HEADS * 2 * QK_DIM
ATT_WIDTH = HEADS * V_DIM
SSM_WIDTH = D_MODEL - ATT_WIDTH
SSM_GROUP = 16
SSM_GROUPS = SSM_WIDTH // SSM_GROUP
SSM_STATE = 64
D_FF = 2816
CONV_WIDTH = 3
LN_EPS = 1e-5
ALPHA = (2 * DEPTH) ** 0.25
QK_SCALE = QK_DIM ** -0.5 * math.log2(math.e)

CHUNK = 16
CHUNK_COLS = CHUNK * SSM_GROUP
GROUPS_PER_TILE = LANES // SSM_GROUP
LANE_TILES = SSM_WIDTH // LANES
PAIRS_PER_TILE = GROUPS_PER_TILE // 2
STATE_COLS = GROUPS_PER_TILE * SSM_STATE

VMEM_LIMIT = 56 * 1024 * 1024


def _cparams(sem):
    return pltpu.CompilerParams(dimension_semantics=sem, vmem_limit_bytes=VMEM_LIMIT)


TB = 256
HEADS_PER_STEP = 4


def _in_proj_kernel(x_ref, wqt_ref, wk_ref, wvt_ref, wu_ref, qt_ref, k_ref, vt_ref, u_ref, *, tl):
    b = pl.program_id(1)
    xb = x_ref[0].astype(BF16)
    nt = (((1,), (1,)), ((), ()))
    qt = lax.dot_general(wqt_ref[...], xb, nt, preferred_element_type=F32) * QK_SCALE
    vt = lax.dot_general(wvt_ref[...], xb, nt, preferred_element_type=F32)
    for h in range(HEADS):
        for c in range(tl // TB):
            qt_ref[0, h, c] = qt[h * LANES:(h + 1) * LANES, c * TB:(c + 1) * TB].astype(BF16)
            vt_ref[0, h, c] = vt[h * LANES:(h + 1) * LANES, c * TB:(c + 1) * TB].astype(BF16)
    k_ref[0] = jnp.dot(xb, wk_ref[...], preferred_element_type=F32).astype(BF16)
    hu = jnp.dot(xb, wu_ref[...], preferred_element_type=F32)
    for q in range(LANE_TILES):
        u_ref[q, pl.ds(b, tl, stride=SUBLANES), :] = hu[:, q * LANES:(q + 1) * LANES]


def _in_proj(x, w_in, tl=512):
    B, L, D = x.shape
    n_l = L // tl
    w = w_in.astype(BF16)
    wqt = w[:, :QK_WIDTH].T
    wk = w[:, QK_WIDTH:2 * QK_WIDTH]
    wvt = w[:, 2 * QK_WIDTH:2 * QK_WIDTH + ATT_WIDTH].T
    wu = w[:, 2 * QK_WIDTH + ATT_WIDTH:]
    const = lambda l, b: (0, 0)
    slab = pl.BlockSpec((1, HEADS, tl // TB, LANES, TB), lambda l, b: (b, 0, l, 0, 0))
    slab_shape = jax.ShapeDtypeStruct((B, HEADS, L // TB, LANES, TB), BF16)
    return pl.pallas_call(
        functools.partial(_in_proj_kernel, tl=tl),
        grid=(n_l, B),
        in_specs=[
            pl.BlockSpec((1, tl, D), lambda l, b: (b, l, 0)),
            pl.BlockSpec((QK_WIDTH, D), const),
            pl.BlockSpec((D, QK_WIDTH), const),
            pl.BlockSpec((ATT_WIDTH, D), const),
            pl.BlockSpec((D, SSM_WIDTH), const),
        ],
        out_specs=[
            slab,
            pl.BlockSpec((1, tl, QK_WIDTH), lambda l, b: (b, l, 0)),
            slab,
            pl.BlockSpec((LANE_TILES, tl * B, LANES), lambda l, b: (0, l, 0)),
        ],
        out_shape=[
            slab_shape,
            jax.ShapeDtypeStruct((B, L, QK_WIDTH), BF16),
            slab_shape,
            jax.ShapeDtypeStruct((LANE_TILES, L * B, LANES), F32),
        ],
        compiler_params=_cparams(("arbitrary", "arbitrary")),
        name="in_proj",
    )(x, wqt, wk, wvt, wu)


def _attn_kernel(lq1_ref, lk1_ref, lq2_ref, lk2_ref, g_ref, qt_ref, k_ref, vt_ref, o_ref,
                 acc_sc, qst_sc, s_sc, p_sc, bm_sc, st_sc, *, lam_init):
    qi = pl.program_id(2)
    lam = (jnp.exp(jnp.sum(lq1_ref[...] * lk1_ref[...], axis=-1, keepdims=True))
           - jnp.exp(jnp.sum(lq2_ref[...] * lk2_ref[...], axis=-1, keepdims=True)) + lam_init)

    for hh in range(HEADS_PER_STEP):
        qt = qt_ref[0, hh, 0]
        row = lax.broadcasted_iota(jnp.int32, qt.shape, 0)
        zero = jnp.zeros_like(qt)
        qst_sc[hh, :, :TB] = jnp.where(row < QK_DIM, qt, zero)
        qst_sc[hh, :, TB:] = jnp.where(row >= QK_DIM, qt, zero)

    acc_sc[...] = jnp.zeros(acc_sc.shape, F32)
    p_sc[1] = jnp.zeros(p_sc.shape[1:], BF16)
    n_strips = 2 * TB // LANES
    heads = range(HEADS_PER_STEP)
    assert TB == 2 * LANES

    def scores(j, slot):
        rows = pl.ds(pl.multiple_of(j * TB, TB), TB)
        for hh in heads:
            kb = k_ref[0, rows, hh * LANES:(hh + 1) * LANES]
            s = jnp.dot(kb, qst_sc[hh], preferred_element_type=F32)
            s_sc[slot, hh] = s
            bm_sc[slot, hh] = jnp.max(s, axis=0, keepdims=True)

    def values(j, slot):
        j = jnp.maximum(j, 0)
        for hh in heads:
            pv = jnp.dot(vt_ref[0, hh, j], p_sc[slot, hh], preferred_element_type=F32)
            acc_sc[hh] = st_sc[hh, 2] * acc_sc[hh] + pv

    def softmax(slot, diagonal):
        for hh in heads:
            for c in range(n_strips):
                lanes = slice(c * LANES, (c + 1) * LANES)
                m = st_sc[hh, 0, :, lanes]
                if not diagonal:
                    mc = jnp.maximum(m, bm_sc[slot, hh, :, lanes])
                    p = jnp.exp2(s_sc[slot, hh, :, lanes] - mc)
                    p_sc[slot, hh, :, lanes] = p.astype(BF16)
                    psum = jnp.sum(p, axis=0, keepdims=True)
                else:
                    q0 = (c * LANES) % TB
                    s = s_sc[slot, hh, :, lanes]
                    key = lax.broadcasted_iota(jnp.int32, s.shape, 0)
                    qry = lax.broadcasted_iota(jnp.int32, s.shape, 1) + q0
                    s = jnp.where(key <= qry, s, -jnp.inf)
                    mc = jnp.maximum(m, jnp.max(s, axis=0, keepdims=True))
                    p = jnp.exp2(s - mc)
                    p_sc[slot, hh, :, lanes] = p.astype(BF16)
                    psum = jnp.sum(p, axis=0, keepdims=True)
                ac = jnp.exp2(m - mc)
                st_sc[hh, 0, :, lanes] = mc
                st_sc[hh, 1, :, lanes] = ac * st_sc[hh, 1, :, lanes] + psum
                st_sc[hh, 2, :, lanes] = ac

    def pair(t, c):
        a = 2 * t
        scores(a + 1, 1)
        values(a - 1, 1)
        softmax(0, False)
        scores(a + 2, 0)
        values(a, 0)
        softmax(1, False)
        return c

    for hh in heads:
        st_sc[hh, 0] = jnp.full((1, 2 * TB), -jnp.inf, F32)
        st_sc[hh, 1] = jnp.zeros((1, 2 * TB), F32)
        st_sc[hh, 2] = jnp.ones((1, 2 * TB), F32)
    scores(0, 0)
    lax.fori_loop(0, qi // 2, pair, 0)

    @pl.when(qi % 2 == 1)
    def _():
        values(qi - 2, 1)
        scores(qi, 1)
        softmax(0, False)
        values(qi - 1, 0)
        softmax(1, True)
        values(qi, 1)

    @pl.when(qi % 2 == 0)
    def _():
        values(qi - 1, 1)
        softmax(0, True)
        values(qi, 0)

    l_fin = [st_sc[hh, 1] for hh in heads]

    for hh in range(HEADS_PER_STEP):
        o = acc_sc[hh] * (1.0 / l_fin[hh])
        o = o[:, :TB] - lam * o[:, TB:]
        ms = jnp.mean(o * o, axis=0, keepdims=True)
        o = o * lax.rsqrt(ms + LN_EPS) * (g_ref[...] * (1.0 - lam_init))
        o_ref[0, :, hh * LANES:(hh + 1) * LANES] = o.T.astype(o_ref.dtype)


def _diff_attn(qt, k, vt, lq1, lk1, lq2, lk2, subln_g, lam_init):
    B, L, _ = k.shape
    n_q = L // TB
    hp = HEADS_PER_STEP
    vec = pl.BlockSpec((1, QK_DIM), lambda b, h, i: (0, 0))
    return pl.pallas_call(
        functools.partial(_attn_kernel, lam_init=lam_init),
        grid=(B, HEADS // hp, n_q),
        in_specs=[
            vec, vec, vec, vec,
            pl.BlockSpec((V_DIM, 1), lambda b, h, i: (0, 0)),
            pl.BlockSpec((1, hp, 1, LANES, TB), lambda b, h, i: (b, h, i, 0, 0)),
            pl.BlockSpec((1, L, hp * LANES), lambda b, h, i: (b, 0, h)),
            pl.BlockSpec((1, hp, n_q, LANES, TB), lambda b, h, i: (b, h, 0, 0, 0)),
        ],
        out_specs=pl.BlockSpec((1, TB, hp * LANES), lambda b, h, i: (b, i, h)),
        out_shape=jax.ShapeDtypeStruct((B, L, ATT_WIDTH), BF16),
        scratch_shapes=[
            pltpu.VMEM((hp, V_DIM, 2 * TB), F32),
            pltpu.VMEM((hp, LANES, 2 * TB), BF16),
            pltpu.VMEM((2, hp, TB, 2 * TB), F32),
            pltpu.VMEM((2, hp, TB, 2 * TB), BF16),
            pltpu.VMEM((2, hp, 1, 2 * TB), F32),
            pltpu.VMEM((hp, 3, 1, 2 * TB), F32),
        ],
        compiler_params=_cparams(("parallel", "parallel", "arbitrary")),
        name="diff_attn",
    )(lq1, lk1, lq2, lk2, subln_g.reshape(V_DIM, 1), qt, k, vt)


def _block_transpose8(vs):
    lane_blk = lax.broadcasted_iota(jnp.int32, vs[0].shape, 1) // SSM_GROUP
    for d in (4, 2, 1):
        keep = (lane_blk & d) == 0
        new = list(vs)
        for a in range(8):
            if a & d == 0:
                lo, hi = vs[a], vs[a + d]
                new[a] = jnp.where(keep, lo, pltpu.roll(hi, SSM_GROUP * d, 1))
                new[a + d] = jnp.where(keep, pltpu.roll(lo, LANES - SSM_GROUP * d, 1), hi)
        vs = new
    return vs


def _ssm_kernel(u_ref, toep_ref, win_ref, wout_ref, a_ref, d_ref, y_ref, s_sc, x_sc, sp_sc, *, jt):
    rows = jt * SUBLANES

    @pl.when(pl.program_id(1) == 0)
    def _():
        s_sc[...] = jnp.zeros(s_sc.shape, F32)

    xs = [u_ref[0, :, r].reshape(rows, LANES) for r in range(CHUNK)]
    z_lo = _block_transpose8(xs[:8])
    z_hi = _block_transpose8(xs[8:])
    ug = [jnp.concatenate([z_lo[g], z_hi[g]], axis=1).astype(BF16) for g in range(GROUPS_PER_TILE)]

    y = [jnp.dot(ug[g], toep_ref[0, g], preferred_element_type=F32) for g in range(GROUPS_PER_TILE)]

    for m in range(PAIRS_PER_TILE):
        up = jnp.concatenate([ug[2 * m], ug[2 * m + 1]], axis=1)
        xp = jnp.dot(up, win_ref[0, m], preferred_element_type=F32)
        x_sc[0, :, m * LANES:(m + 1) * LANES] = xp[:, :LANES]
        x_sc[1, :, m * LANES:(m + 1) * LANES] = xp[:, LANES:]

    ar = a_ref[0, 0:1, :]
    ai = a_ref[0, 1:2, :]

    def step(j, carry):
        sr, si = carry
        r8 = pl.ds(pl.multiple_of(j * SUBLANES, SUBLANES), SUBLANES)
        sp_sc[0, r8, :] = sr
        sp_sc[1, r8, :] = si
        return (ar * sr - ai * si + x_sc[0, r8, :], ar * si + ai * sr + x_sc[1, r8, :])

    sr, si = lax.fori_loop(0, jt, step, (s_sc[0], s_sc[1]))
    s_sc[0] = sr
    s_sc[1] = si

    for m in range(PAIRS_PER_TILE):
        sp = jnp.concatenate([sp_sc[0, :, m * LANES:(m + 1) * LANES],
                              sp_sc[1, :, m * LANES:(m + 1) * LANES]], axis=1).astype(BF16)
        yc = jnp.dot(sp, wout_ref[0, m], preferred_element_type=F32)
        y[2 * m] = y[2 * m] + yc[:, :CHUNK_COLS]
        y[2 * m + 1] = y[2 * m + 1] + yc[:, CHUNK_COLS:]

    d = d_ref[0]
    for half in range(2):
        ws = _block_transpose8([y[g][:, half * LANES:(half + 1) * LANES] for g in range(GROUPS_PER_TILE)])
        for t8 in range(8):
            t = half * 8 + t8
            u_t = u_ref[0, :, t].reshape(rows, LANES)
            y_ref[0, :, t] = (ws[t8] + d * u_t).reshape(jt, SUBLANES, LANES)


def _ssm(u_tm, tables, d_skip, B, L, lt=512):
    toep, win, wout, a16 = tables
    jt = lt // CHUNK
    n_l = L // lt
    u5 = u_tm.reshape(LANE_TILES, L // CHUNK, CHUNK, B, LANES)
    rows = jt * B
    y5 = pl.pallas_call(
        functools.partial(_ssm_kernel, jt=jt),
        grid=(LANE_TILES, n_l),
        in_specs=[
            pl.BlockSpec((1, jt, CHUNK, B, LANES), lambda q, l: (q, l, 0, 0, 0)),
            pl.BlockSpec((1, GROUPS_PER_TILE, CHUNK_COLS, CHUNK_COLS), lambda q, l: (q, 0, 0, 0)),
            pl.BlockSpec((1, PAIRS_PER_TILE, 2 * CHUNK_COLS, 2 * LANES), lambda q, l: (q, 0, 0, 0)),
            pl.BlockSpec((1, PAIRS_PER_TILE, 2 * LANES, 2 * CHUNK_COLS), lambda q, l: (q, 0, 0, 0)),
            pl.BlockSpec((1, 2, STATE_COLS), lambda q, l: (q, 0, 0)),
            pl.BlockSpec((1, 1, LANES), lambda q, l: (q, 0, 0)),
        ],
        out_specs=pl.BlockSpec((1, jt, CHUNK, B, LANES), lambda q, l: (q, l, 0, 0, 0)),
        out_shape=jax.ShapeDtypeStruct(u5.shape, F32),
        scratch_shapes=[
            pltpu.VMEM((2, B, STATE_COLS), F32),
            pltpu.VMEM((2, rows, STATE_COLS), F32),
            pltpu.VMEM((2, rows, STATE_COLS), F32),
        ],
        compiler_params=_cparams(("parallel", "arbitrary")),
        name="ssm",
    )(u5, toep, win, wout, a16, d_skip.reshape(LANE_TILES, 1, LANES))
    return y5.reshape(LANE_TILES, L * B, LANES)


def _cmul(a, b):
    return a[0] * b[0] - a[1] * b[1], a[0] * b[1] + a[1] * b[0]


def _zoh(lr, li, ldt):
    dt = jnp.exp(ldt)
    mag = jnp.exp(lr * dt)
    ab = (mag * jnp.cos(li * dt), mag * jnp.sin(li * dt))
    den = lr * lr + li * li
    zr, zi = ab[0] - 1.0, ab[1]
    return ab, ((zr * lr + zi * li) / den, (zi * lr - zr * li) / den)


def _cpow(base, exponent, bits):
    out = None
    for k in range(bits):
        on = (exponent & (1 << k)) != 0
        sel = (jnp.where(on, base[0], 1.0), jnp.where(on, base[1], 0.0))
        out = sel if out is None else _cmul(out, sel)
        if k + 1 < bits:
            base = _cmul(base, base)
    return out


def _ssm_prep_kernel(col_ref, row_ref, b_ref, ct_ref, bt_ref, toep_ref, win_ref, wout_ref, a_ref):
    bits = CHUNK.bit_length() - 1
    lane = lax.broadcasted_iota(jnp.int32, (1, CHUNK_COLS), 1)
    tau = lane // SSM_GROUP
    rowi = lax.broadcasted_iota(jnp.int32, (CHUNK_COLS, 1), 0)
    back = (CHUNK - 1) - rowi // SSM_GROUP
    lane2 = lax.broadcasted_iota(jnp.int32, (1, LANES), 1)
    a_rows = []
    for m in range(PAIRS_PER_TILE):
        abp, fp = _zoh(row_ref[0, m], row_ref[1, m], row_ref[2, m])
        pwr = _cpow(abp, back, bits)
        a16 = abp
        for _ in range(bits):
            a16 = _cmul(a16, a16)
        a_rows.append(a16)
        wout_rows = [[None, None], [None, None]]
        for j in range(2):
            g = 2 * m + j
            ab, f = _zoh(col_ref[0, g], col_ref[1, g], col_ref[2, g])
            bb = _cmul(f, (b_ref[0, g], b_ref[1, g]))
            pw = _cpow(ab, tau, bits)
            g0 = _cmul(pw, (ct_ref[0, g], ct_ref[1, g]))
            g1 = _cmul(g0, ab)
            kt = jnp.concatenate(
                [jnp.sum(bb[0][:, h:h + 1] * g0[0] - bb[1][:, h:h + 1] * g0[1], axis=0, keepdims=True)
                 for h in range(SSM_GROUP)], axis=0)
            toep_ref[0, g] = jnp.concatenate(
                [jnp.where(lane >= SSM_GROUP * r, pltpu.roll(kt, SSM_GROUP * r, 1), 0.0) if r else kt
                 for r in range(CHUNK)], axis=0).astype(BF16)
            bb2 = _cmul(fp, (bt_ref[0, g], bt_ref[1, g]))
            w = _cmul(pwr, tuple(jnp.concatenate([v] * CHUNK, axis=0) for v in bb2))
            mine = (lane2 < SSM_STATE) if j == 0 else (lane2 >= SSM_STATE)
            win_ref[0, m, j * CHUNK_COLS:(j + 1) * CHUNK_COLS, :] = jnp.concatenate(
                [jnp.where(mine, w[0], 0.0), jnp.where(mine, w[1], 0.0)], axis=1).astype(BF16)
            wout_rows[0][j] = g1[0]
            wout_rows[1][j] = -g1[1]
        zero = jnp.zeros((SSM_STATE, CHUNK_COLS), F32)
        wout_ref[0, m] = jnp.concatenate([
            jnp.concatenate([wout_rows[0][0], zero], axis=1),
            jnp.concatenate([zero, wout_rows[0][1]], axis=1),
            jnp.concatenate([wout_rows[1][0], zero], axis=1),
            jnp.concatenate([zero, wout_rows[1][1]], axis=1)], axis=0).astype(BF16)
    a_ref[0] = jnp.concatenate([jnp.concatenate([a[0] for a in a_rows], axis=1),
                                jnp.concatenate([a[1] for a in a_rows], axis=1)], axis=0)


def _ssm_prep(lam_re, lam_im, log_dt, b_re, b_im, c_re, c_im):
    G, P, H = b_re.shape
    ldt = jnp.broadcast_to(log_dt[:, None], (G, P))
    cols = jnp.stack([lam_re, lam_im, ldt])[..., None]
    rows = jnp.stack([lam_re, lam_im, ldt]).reshape(3, G // 2, 1, 2 * P)
    b = jnp.stack([b_re, b_im])
    ct = jnp.tile(jnp.stack([c_re, c_im]).transpose(0, 1, 3, 2), (1, 1, 1, CHUNK))
    bt = jnp.tile(b.transpose(0, 1, 3, 2), (1, 1, 1, 2))
    gt = GROUPS_PER_TILE
    return pl.pallas_call(
        _ssm_prep_kernel,
        grid=(LANE_TILES,),
        in_specs=[
            pl.BlockSpec((3, gt, P, 1), lambda q: (0, q, 0, 0)),
            pl.BlockSpec((3, PAIRS_PER_TILE, 1, 2 * P), lambda q: (0, q, 0, 0)),
            pl.BlockSpec((2, gt, P, H), lambda q: (0, q, 0, 0)),
            pl.BlockSpec((2, gt, P, CHUNK_COLS), lambda q: (0, q, 0, 0)),
            pl.BlockSpec((2, gt, H, 2 * P), lambda q: (0, q, 0, 0)),
        ],
        out_specs=[
            pl.BlockSpec((1, gt, CHUNK_COLS, CHUNK_COLS), lambda q: (q, 0, 0, 0)),
            pl.BlockSpec((1, PAIRS_PER_TILE, 2 * CHUNK_COLS, 2 * LANES), lambda q: (q, 0, 0, 0)),
            pl.BlockSpec((1, PAIRS_PER_TILE, 2 * LANES, 2 * CHUNK_COLS), lambda q: (q, 0, 0, 0)),
            pl.BlockSpec((1, 2, STATE_COLS), lambda q: (q, 0, 0)),
        ],
        out_shape=[
            jax.ShapeDtypeStruct((LANE_TILES, gt, CHUNK_COLS, CHUNK_COLS), BF16),
            jax.ShapeDtypeStruct((LANE_TILES, PAIRS_PER_TILE, 2 * CHUNK_COLS, 2 * LANES), BF16),
            jax.ShapeDtypeStruct((LANE_TILES, PAIRS_PER_TILE, 2 * LANES, 2 * CHUNK_COLS), BF16),
            jax.ShapeDtypeStruct((LANE_TILES, 2, STATE_COLS), F32),
        ],
        compiler_params=_cparams(("parallel",)),
        name="ssm_prep",
    )(cols, rows, b, ct, bt)


def _layer_norm(r, g, b):
    mu = jnp.mean(r, axis=-1, keepdims=True)
    c = r - mu
    var = jnp.mean(c * c, axis=-1, keepdims=True)
    return c * lax.rsqrt(var + LN_EPS) * g + b


def _gelu_tanh(y):
    return 0.5 * y * (1.0 + jnp.tanh(math.sqrt(2.0 / math.pi) * (y + 0.044715 * (y * y * y))))


def _mix_ln1_kernel(x_ref, attn_ref, y_ref, wglu_ref, bglu_ref, wo_ref, g_ref, b_ref, o_ref, *, tl):
    b = pl.program_id(1)
    y = jnp.concatenate([y_ref[q, pl.ds(b, tl, stride=SUBLANES), :] for q in range(LANE_TILES)], axis=1)
    gl = _gelu_tanh(y)
    z = gl * jax.nn.sigmoid(jnp.dot(gl.astype(BF16), wglu_ref[...], preferred_element_type=F32) + bglu_ref[...])
    mix = (jnp.dot(attn_ref[0], wo_ref[:ATT_WIDTH], preferred_element_type=F32)
           + jnp.dot(z.astype(BF16), wo_ref[ATT_WIDTH:], preferred_element_type=F32))
    o_ref[0] = _layer_norm(ALPHA * x_ref[0] + mix, g_ref[...], b_ref[...])


def _mix_ln1(x, attn, y_tm, wglu_bf, b_glu, wo_bf, ln_g, ln_b, tl=512):
    B, L, D = x.shape
    n_l = L // tl
    const = lambda l, b: (0, 0)
    return pl.pallas_call(
        functools.partial(_mix_ln1_kernel, tl=tl),
        grid=(n_l, B),
        in_specs=[
            pl.BlockSpec((1, tl, D), lambda l, b: (b, l, 0)),
            pl.BlockSpec((1, tl, ATT_WIDTH), lambda l, b: (b, l, 0)),
            pl.BlockSpec((LANE_TILES, tl * B, LANES), lambda l, b: (0, l, 0)),
            pl.BlockSpec((SSM_WIDTH, SSM_WIDTH), const),
            pl.BlockSpec((1, SSM_WIDTH), const),
            pl.BlockSpec((D, D), const),
            pl.BlockSpec((1, D), const),
            pl.BlockSpec((1, D), const),
        ],
        out_specs=pl.BlockSpec((1, tl, D), lambda l, b: (b, l, 0)),
        out_shape=jax.ShapeDtypeStruct((B, L, D), F32),
        compiler_params=_cparams(("parallel", "arbitrary")),
        name="mix_ln1",
    )(x, attn, y_tm, wglu_bf, b_glu, wo_bf, ln_g, ln_b)


HALO = 16
FFN_SUB = 256
FFN_ROWS = 128


def _ffn_kernel(x_ref, xh_ref, p_ref, wple_ref, wgate_ref, wup_ref, cw_ref, cb_ref, wd_ref, g_ref, b_ref, o_ref,
                x32_sc, xh32_sc, p32_sc, xb_sc, acc_sc, hg_sc, hv_sc, act_sc, *, tl):
    nb = x_ref.shape[0]
    tm = tl * nb
    d_slabs = x_ref.shape[2] // LANES
    hp = (CONV_WIDTH - 1)
    for bb in range(nb):
        for s in range(d_slabs):
            lanes = slice(s * LANES, (s + 1) * LANES)
            x32_sc[s, pl.ds(bb, tl, stride=nb), :] = x_ref[bb, :, lanes]
            xh32_sc[s, pl.ds(bb, hp, stride=nb), :] = xh_ref[bb, SUBLANES - hp:, lanes]
        for s in range(p_ref.shape[2] // LANES):
            p32_sc[s, pl.ds(bb, tl, stride=nb), :] = p_ref[bb, :, s * LANES:(s + 1) * LANES]
    x_tm = jnp.concatenate([x32_sc[s] for s in range(d_slabs)], axis=1)
    p_tm = jnp.concatenate([p32_sc[s] for s in range(p32_sc.shape[0])], axis=1)
    keep = jnp.where(pl.program_id(0) == 0, 0.0, 1.0)
    xb_sc[0:HALO] = (jnp.concatenate([xh32_sc[s] for s in range(d_slabs)], axis=1) * keep).astype(BF16)
    xb_sc[HALO:] = x_tm.astype(BF16)
    gate = jax.nn.sigmoid(jnp.dot(xb_sc[HALO:], wgate_ref[...], preferred_element_type=F32))
    ple = jnp.dot(p_tm.astype(BF16), wple_ref[...], preferred_element_type=F32) * gate
    acc_sc[...] = ALPHA * x_tm + ple

    def conv(h_sc, slot, cols, r0, rows):
        out = cb_ref[:, cols] + cw_ref[CONV_WIDTH - 1:CONV_WIDTH, cols] * h_sc[slot, HALO + r0:HALO + r0 + rows]
        for j in range(CONV_WIDTH - 1):
            back = (CONV_WIDTH - 1 - j) * nb
            out = out + cw_ref[j:j + 1, cols] * h_sc[slot, HALO + r0 - back:HALO + r0 - back + rows]
        return out

    n_sub = D_FF // FFN_SUB
    g_cols = lambda n: slice(n * FFN_SUB, (n + 1) * FFN_SUB)
    v_cols = lambda n: slice(D_FF + n * FFN_SUB, D_FF + (n + 1) * FFN_SUB)

    def up(n):
        xb = xb_sc[...]
        hg_sc[n % 2] = jnp.dot(xb, wup_ref[:, g_cols(n)], preferred_element_type=F32)
        hv_sc[n % 2] = jnp.dot(xb, wup_ref[:, v_cols(n)], preferred_element_type=F32)

    def gate_block(n):
        for r0 in range(0, tm, FFN_ROWS):
            gc = conv(hg_sc, n % 2, g_cols(n), r0, FFN_ROWS)
            vc = conv(hv_sc, n % 2, v_cols(n), r0, FFN_ROWS)
            act_sc[r0:r0 + FFN_ROWS, g_cols(n)] = (gc * jax.nn.sigmoid(gc) * vc).astype(BF16)

    def down(lo, hi):
        cols = slice(lo * FFN_SUB, hi * FFN_SUB)
        return jnp.dot(act_sc[:, cols], wd_ref[cols, :], preferred_element_type=F32)

    half = (n_sub + 1) // 2
    up(0)
    for n in range(n_sub):
        if n + 1 < n_sub:
            up(n + 1)
        gate_block(n)
        if n + 1 == half:
            acc_sc[...] += down(0, half)

    res = _layer_norm(acc_sc[...] + down(half, n_sub), g_ref[...], b_ref[...])
    for s in range(d_slabs):
        x32_sc[s] = res[:, s * LANES:(s + 1) * LANES]
    for bb in range(nb):
        for s in range(d_slabs):
            o_ref[bb, :, s * LANES:(s + 1) * LANES] = x32_sc[s, pl.ds(bb, tl, stride=nb), :]


def _ffn_ln2(x1, p, wple_bf, wgate_bf, wup_bf, conv_w, conv_b, wdown_bf, ln_g, ln_b, tl=64):
    B, L, D = x1.shape
    tm = tl * B
    assert D_FF % FFN_SUB == 0 and HALO == (CONV_WIDTH - 1) * B and tm % FFN_ROWS == 0
    const = lambda i: (0, 0)
    resident = lambda shape: pl.BlockSpec(shape, const, pipeline_mode=pl.Buffered(1))
    return pl.pallas_call(
        functools.partial(_ffn_kernel, tl=tl),
        grid=(L // tl,),
        in_specs=[
            pl.BlockSpec((B, tl, D), lambda i: (0, i, 0)),
            pl.BlockSpec((B, SUBLANES, D), lambda i: (0, jnp.maximum(i * (tl // SUBLANES) - 1, 0), 0)),
            pl.BlockSpec((B, tl, PLE_DIM), lambda i: (0, i, 0)),
            resident((PLE_DIM, D)),
            resident((D, D)),
            resident((D, 2 * D_FF)),
            resident((CONV_WIDTH, 2 * D_FF)),
            resident((1, 2 * D_FF)),
            resident((D_FF, D)),
            resident((1, D)),
            resident((1, D)),
        ],
        out_specs=pl.BlockSpec((B, tl, D), lambda i: (0, i, 0)),
        out_shape=jax.ShapeDtypeStruct((B, L, D), F32),
        scratch_shapes=[
            pltpu.VMEM((D // LANES, tm, LANES), F32),
            pltpu.VMEM((D // LANES, HALO, LANES), F32),
            pltpu.VMEM((PLE_DIM // LANES, tm, LANES), F32),
            pltpu.VMEM((tm + HALO, D), BF16),
            pltpu.VMEM((tm, D), F32),
            pltpu.VMEM((2, tm + HALO, FFN_SUB), F32),
            pltpu.VMEM((2, tm + HALO, FFN_SUB), F32),
            pltpu.VMEM((tm, D_FF), BF16),
        ],
        compiler_params=_cparams(("parallel",)),
        name="ffn_ln2",
    )(x1, x1, p, wple_bf, wgate_bf, wup_bf, conv_w, conv_b, wdown_bf, ln_g, ln_b)


def kernel(x, p, w_in, diff_lambda_q1, diff_lambda_k1, diff_lambda_q2, diff_lambda_k2, diff_subln_g, ssm_lambda_re, ssm_lambda_im, ssm_log_dt, ssm_b_re, ssm_b_im, ssm_c_re, ssm_c_im, ssm_d, ssm_w_glu, ssm_b_glu, w_o, ln1_g, ln1_b, ffn_w_up, ffn_conv_w, ffn_conv_b, ffn_w_down, w_ple, w_ple_gate, ln2_g, ln2_b):
    B, L, D = x.shape
    assert B == SUBLANES and D == D_MODEL and L % 512 == 0
    for i in range(DEPTH):
        lam_init = 0.8 - 0.6 * math.exp(-0.3 * i)
        row = lambda a: a[i].reshape(1, -1)
        qt, k, vt, u_tm = _in_proj(x, w_in[i])
        attn = _diff_attn(qt, k, vt, row(diff_lambda_q1), row(diff_lambda_k1), row(diff_lambda_q2),
                          row(diff_lambda_k2), diff_subln_g[i], lam_init)
        tables = _ssm_prep(ssm_lambda_re[i], ssm_lambda_im[i], ssm_log_dt[i], ssm_b_re[i], ssm_b_im[i],
                           ssm_c_re[i], ssm_c_im[i])
        y_tm = _ssm(u_tm, tables, ssm_d[i], B, L)
        x1 = _mix_ln1(x, attn, y_tm, ssm_w_glu[i].astype(BF16), row(ssm_b_glu), w_o[i].astype(BF16),
                      row(ln1_g), row(ln1_b))
        x = _ffn_ln2(x1, p[i], w_ple[i].astype(BF16), w_ple_gate[i].astype(BF16), ffn_w_up[i].astype(BF16),
                     ffn_conv_w[i], row(ffn_conv_b), ffn_w_down[i].astype(BF16), row(ln2_g), row(ln2_b))
    return x
```

```python
import functools
import math

import jax
import jax.numpy as jnp
from jax import lax
from jax.experimental import pallas as pl
from jax.experimental.pallas import tpu as pltpu

F32 = jnp.float32
BF16 = jnp.bfloat16

SUBLANES = 8
LANES = 128

DEPTH = 1
D_MODEL = 1024
PLE_DIM = 256
HEADS = 4
QK_DIM = 64
V_DIM = 2 * QK_DIM
QK_WIDTH = HEADS * 2 * QK_DIM
ATT_WIDTH = HEADS * V_DIM
SSM_WIDTH = D_MODEL - ATT_WIDTH
SSM_GROUP = 16
SSM_GROUPS = SSM_WIDTH // SSM_GROUP
SSM_STATE = 64
D_FF = 2816
CONV_WIDTH = 3
LN_EPS = 1e-5
ALPHA = (2 * DEPTH) ** 0.25
QK_SCALE = QK_DIM ** -0.5 * math.log2(math.e)

CHUNK = 16
CHUNK_COLS = CHUNK * SSM_GROUP
GROUPS_PER_TILE = LANES // SSM_GROUP
LANE_TILES = SSM_WIDTH // LANES
PAIRS_PER_TILE = GROUPS_PER_TILE // 2
STATE_COLS = GROUPS_PER_TILE * SSM_STATE

VMEM_LIMIT = 56 * 1024 * 1024


def _cparams(sem):
    return pltpu.CompilerParams(dimension_semantics=sem, vmem_limit_bytes=VMEM_LIMIT)


TB = 256
HEADS_PER_STEP = 4


def _in_proj_kernel(x_ref, wqt_ref, wk_ref, wvt_ref, wu_ref, qt_ref, k_ref, vt_ref, u_ref):
    nb, tl, d = x_ref.shape
    xb = x_ref[...].reshape(nb * tl, d).astype(BF16)
    nt = (((1,), (1,)), ((), ()))
    qt = lax.dot_general(wqt_ref[...], xb, nt, preferred_element_type=F32) * QK_SCALE
    vt = lax.dot_general(wvt_ref[...], xb, nt, preferred_element_type=F32)
    for b in range(nb):
        for h in range(HEADS):
            for c in range(tl // TB):
                cols = slice(b * tl + c * TB, b * tl + (c + 1) * TB)
                qt_ref[b, h, c] = qt[h * LANES:(h + 1) * LANES, cols].astype(BF16)
                vt_ref[b, h, c] = vt[h * LANES:(h + 1) * LANES, cols].astype(BF16)
    k_ref[...] = jnp.dot(xb, wk_ref[...], preferred_element_type=F32).astype(BF16).reshape(k_ref.shape)
    hu = jnp.dot(xb, wu_ref[...], preferred_element_type=F32)
    for b in range(nb):
        for q in range(LANE_TILES):
            u_ref[q, pl.ds(b, tl, stride=nb), :] = hu[b * tl:(b + 1) * tl, q * LANES:(q + 1) * LANES]


def _in_proj(x, w_in, tl=256):
    B, L, D = x.shape
    w = w_in.astype(BF16)
    wqt = w[:, :QK_WIDTH].T
    wk = w[:, QK_WIDTH:2 * QK_WIDTH]
    wvt = w[:, 2 * QK_WIDTH:2 * QK_WIDTH + ATT_WIDTH].T
    wu = w[:, 2 * QK_WIDTH + ATT_WIDTH:]
    resident = lambda shape: pl.BlockSpec(shape, lambda l: (0, 0), pipeline_mode=pl.Buffered(1))
    slab = pl.BlockSpec((B, HEADS, tl // TB, LANES, TB), lambda l: (0, 0, l, 0, 0))
    slab_shape = jax.ShapeDtypeStruct((B, HEADS, L // TB, LANES, TB), BF16)
    return pl.pallas_call(
        _in_proj_kernel,
        grid=(L // tl,),
        in_specs=[
            pl.BlockSpec((B, tl, D), lambda l: (0, l, 0)),
            resident((QK_WIDTH, D)),
            resident((D, QK_WIDTH)),
            resident((ATT_WIDTH, D)),
            resident((D, SSM_WIDTH)),
        ],
        out_specs=[
            slab,
            pl.BlockSpec((B, tl, QK_WIDTH), lambda l: (0, l, 0)),
            slab,
            pl.BlockSpec((LANE_TILES, tl * B, LANES), lambda l: (0, l, 0)),
        ],
        out_shape=[
            slab_shape,
            jax.ShapeDtypeStruct((B, L, QK_WIDTH), BF16),
            slab_shape,
            jax.ShapeDtypeStruct((LANE_TILES, L * B, LANES), F32),
        ],
        compiler_params=_cparams(("parallel",)),
        name="in_proj",
    )(x, wqt, wk, wvt, wu)


def _attn_kernel(lq1_ref, lk1_ref, lq2_ref, lk2_ref, g_ref, qt_ref, k_ref, vt_ref, o_ref,
                 acc_sc, qst_sc, s_sc, p_sc, bm_sc, st_sc, *, lam_init):
    qi = pl.program_id(2)
    lam = (jnp.exp(jnp.sum(lq1_ref[...] * lk1_ref[...], axis=-1, keepdims=True))
           - jnp.exp(jnp.sum(lq2_ref[...] * lk2_ref[...], axis=-1, keepdims=True)) + lam_init)

    for hh in range(HEADS_PER_STEP):
        qt = qt_ref[0, hh, 0]
        row = lax.broadcasted_iota(jnp.int32, qt.shape, 0)
        zero = jnp.zeros_like(qt)
        qst_sc[hh, :, :TB] = jnp.where(row < QK_DIM, qt, zero)
        qst_sc[hh, :, TB:] = jnp.where(row >= QK_DIM, qt, zero)

    acc_sc[...] = jnp.zeros(acc_sc.shape, F32)
    p_sc[1] = jnp.zeros(p_sc.shape[1:], BF16)
    n_strips = 2 * TB // LANES
    heads = range(HEADS_PER_STEP)
    assert TB == 2 * LANES

    def scores(j, slot):
        rows = pl.ds(pl.multiple_of(j * TB, TB), TB)
        for hh in heads:
            kb = k_ref[0, rows, hh * LANES:(hh + 1) * LANES]
            s = jnp.dot(kb, qst_sc[hh], preferred_element_type=F32)
            s_sc[slot, hh] = s
            bm_sc[slot, hh] = jnp.max(s, axis=0, keepdims=True)

    def values(j, slot):
        j = jnp.maximum(j, 0)
        for hh in heads:
            pv = jnp.dot(vt_ref[0, hh, j], p_sc[slot, hh], preferred_element_type=F32)
            acc_sc[hh] = st_sc[hh, 2] * acc_sc[hh] + pv

    def softmax(slot, diagonal):
        for hh in heads:
            for c in range(n_strips):
                lanes = slice(c * LANES, (c + 1) * LANES)
                m = st_sc[hh, 0, :, lanes]
                if not diagonal:
                    mc = jnp.maximum(m, bm_sc[slot, hh, :, lanes])
                    p = jnp.exp2(s_sc[slot, hh, :, lanes] - mc)
                    p_sc[slot, hh, :, lanes] = p.astype(BF16)
                    psum = jnp.sum(p, axis=0, keepdims=True)
                else:
                    q0 = (c * LANES) % TB
                    s = s_sc[slot, hh, :, lanes]
                    key = lax.broadcasted_iota(jnp.int32, s.shape, 0)
                    qry = lax.broadcasted_iota(jnp.int32, s.shape, 1) + q0
                    s = jnp.where(key <= qry, s, -jnp.inf)
                    mc = jnp.maximum(m, jnp.max(s, axis=0, keepdims=True))
                    p = jnp.exp2(s - mc)
                    p_sc[slot, hh, :, lanes] = p.astype(BF16)
                    psum = jnp.sum(p, axis=0, keepdims=True)
                ac = jnp.exp2(m - mc)
                st_sc[hh, 0, :, lanes] = mc
                st_sc[hh, 1, :, lanes] = ac * st_sc[hh, 1, :, lanes] + psum
                st_sc[hh, 2, :, lanes] = ac

    def pair(t, c):
        a = 2 * t
        scores(a + 1, 1)
        values(a - 1, 1)
        softmax(0, False)
        scores(a + 2, 0)
        values(a, 0)
        softmax(1, False)
        return c

    for hh in heads:
        st_sc[hh, 0] = jnp.full((1, 2 * TB), -jnp.inf, F32)
        st_sc[hh, 1] = jnp.zeros((1, 2 * TB), F32)
        st_sc[hh, 2] = jnp.ones((1, 2 * TB), F32)
    scores(0, 0)
    lax.fori_loop(0, qi // 2, pair, 0)

    @pl.when(qi % 2 == 1)
    def _():
        values(qi - 2, 1)
        scores(qi, 1)
        softmax(0, False)
        values(qi - 1, 0)
        softmax(1, True)
        values(qi, 1)

    @pl.when(qi % 2 == 0)
    def _():
        values(qi - 1, 1)
        softmax(0, True)
        values(qi, 0)

    l_fin = [st_sc[hh, 1] for hh in heads]

    for hh in range(HEADS_PER_STEP):
        o = acc_sc[hh] * (1.0 / l_fin[hh])
        o = o[:, :TB] - lam * o[:, TB:]
        ms = jnp.mean(o * o, axis=0, keepdims=True)
        o = o * lax.rsqrt(ms + LN_EPS) * (g_ref[...] * (1.0 - lam_init))
        o_ref[0, :, hh * LANES:(hh + 1) * LANES] = o.T.astype(o_ref.dtype)


def _diff_attn(qt, k, vt, lq1, lk1, lq2, lk2, subln_g, lam_init):
    B, L, _ = k.shape
    n_q = L // TB
    hp = HEADS_PER_STEP
    vec = pl.BlockSpec((1, QK_DIM), lambda b, h, i: (0, 0))
    return pl.pallas_call(
        functools.partial(_attn_kernel, lam_init=lam_init),
        grid=(B, HEADS // hp, n_q),
        in_specs=[
            vec, vec, vec, vec,
            pl.BlockSpec((V_DIM, 1), lambda b, h, i: (0, 0)),
            pl.BlockSpec((1, hp, 1, LANES, TB), lambda b, h, i: (b, h, i, 0, 0)),
            pl.BlockSpec((1, L, hp * LANES), lambda b, h, i: (b, 0, h)),
            pl.BlockSpec((1, hp, n_q, LANES, TB), lambda b, h, i: (b, h, 0, 0, 0)),
        ],
        out_specs=pl.BlockSpec((1, TB, hp * LANES), lambda b, h, i: (b, i, h)),
        out_shape=jax.ShapeDtypeStruct((B, L, ATT_WIDTH), BF16),
        scratch_shapes=[
            pltpu.VMEM((hp, V_DIM, 2 * TB), F32),
            pltpu.VMEM((hp, LANES, 2 * TB), BF16),
            pltpu.VMEM((2, hp, TB, 2 * TB), F32),
            pltpu.VMEM((2, hp, TB, 2 * TB), BF16),
            pltpu.VMEM((2, hp, 1, 2 * TB), F32),
            pltpu.VMEM((hp, 3, 1, 2 * TB), F32),
        ],
        compiler_params=_cparams(("parallel", "parallel", "arbitrary")),
        name="diff_attn",
    )(lq1, lk1, lq2, lk2, subln_g.reshape(V_DIM, 1), qt, k, vt)


def _block_transpose8(vs):
    lane_blk = lax.broadcasted_iota(jnp.int32, vs[0].shape, 1) // SSM_GROUP
    for d in (4, 2, 1):
        keep = (lane_blk & d) == 0
        new = list(vs)
        for a in range(8):
            if a & d == 0:
                lo, hi = vs[a], vs[a + d]
                new[a] = jnp.where(keep, lo, pltpu.roll(hi, SSM_GROUP * d, 1))
                new[a + d] = jnp.where(keep, pltpu.roll(lo, LANES - SSM_GROUP * d, 1), hi)
        vs = new
    return vs


def _ssm_kernel(u_ref, toep_ref, win_ref, wout_ref, a_ref, d_ref, y_ref, s_sc, x_sc, sp_sc, *, jt):
    rows = jt * SUBLANES

    @pl.when(pl.program_id(1) == 0)
    def _():
        s_sc[...] = jnp.zeros(s_sc.shape, F32)

    xs = [u_ref[0, :, r].reshape(rows, LANES) for r in range(CHUNK)]
    z_lo = _block_transpose8(xs[:8])
    z_hi = _block_transpose8(xs[8:])
    ug = [jnp.concatenate([z_lo[g], z_hi[g]], axis=1).astype(BF16) for g in range(GROUPS_PER_TILE)]

    y = [jnp.dot(ug[g], toep_ref[0, g], preferred_element_type=F32) for g in range(GROUPS_PER_TILE)]

    for m in range(PAIRS_PER_TILE):
        up = jnp.concatenate([ug[2 * m], ug[2 * m + 1]], axis=1)
        xp = jnp.dot(up, win_ref[0, m], preferred_element_type=F32)
        x_sc[0, :, m * LANES:(m + 1) * LANES] = xp[:, :LANES]
        x_sc[1, :, m * LANES:(m + 1) * LANES] = xp[:, LANES:]

    ar = a_ref[0, 0:1, :]
    ai = a_ref[0, 1:2, :]

    def step(j, carry):
        sr, si = carry
        r8 = pl.ds(pl.multiple_of(j * SUBLANES, SUBLANES), SUBLANES)
        sp_sc[0, r8, :] = sr
        sp_sc[1, r8, :] = si
        return (ar * sr - ai * si + x_sc[0, r8, :], ar * si + ai * sr + x_sc[1, r8, :])

    sr, si = lax.fori_loop(0, jt, step, (s_sc[0], s_sc[1]))
    s_sc[0] = sr
    s_sc[1] = si

    for m in range(PAIRS_PER_TILE):
        sp = jnp.concatenate([sp_sc[0, :, m * LANES:(m + 1) * LANES],
                              sp_sc[1, :, m * LANES:(m + 1) * LANES]], axis=1).astype(BF16)
        yc = jnp.dot(sp, wout_ref[0, m], preferred_element_type=F32)
        y[2 * m] = y[2 * m] + yc[:, :CHUNK_COLS]
        y[2 * m + 1] = y[2 * m + 1] + yc[:, CHUNK_COLS:]

    d = d_ref[0]
    for half in range(2):
        ws = _block_transpose8([y[g][:, half * LANES:(half + 1) * LANES] for g in range(GROUPS_PER_TILE)])
        for t8 in range(8):
            t = half * 8 + t8
            u_t = u_ref[0, :, t].reshape(rows, LANES)
            y_ref[0, :, t] = (ws[t8] + d * u_t).reshape(jt, SUBLANES, LANES)


def _ssm(u_tm, tables, d_skip, B, L, lt=1024):
    toep, win, wout, a16 = tables
    jt = lt // CHUNK
    n_l = L // lt
    u5 = u_tm.reshape(LANE_TILES, L // CHUNK, CHUNK, B, LANES)
    rows = jt * B
    y5 = pl.pallas_call(
        functools.partial(_ssm_kernel, jt=jt),
        grid=(LANE_TILES, n_l),
        in_specs=[
            pl.BlockSpec((1, jt, CHUNK, B, LANES), lambda q, l: (q, l, 0, 0, 0)),
            pl.BlockSpec((1, GROUPS_PER_TILE, CHUNK_COLS, CHUNK_COLS), lambda q, l: (q, 0, 0, 0)),
            pl.BlockSpec((1, PAIRS_PER_TILE, 2 * CHUNK_COLS, 2 * LANES), lambda q, l: (q, 0, 0, 0)),
            pl.BlockSpec((1, PAIRS_PER_TILE, 2 * LANES, 2 * CHUNK_COLS), lambda q, l: (q, 0, 0, 0)),
            pl.BlockSpec((1, 2, STATE_COLS), lambda q, l: (q, 0, 0)),
            pl.BlockSpec((1, 1, LANES), lambda q, l: (q, 0, 0)),
        ],
        out_specs=pl.BlockSpec((1, jt, CHUNK, B, LANES), lambda q, l: (q, l, 0, 0, 0)),
        out_shape=jax.ShapeDtypeStruct(u5.shape, F32),
        scratch_shapes=[
            pltpu.VMEM((2, B, STATE_COLS), F32),
            pltpu.VMEM((2, rows, STATE_COLS), F32),
            pltpu.VMEM((2, rows, STATE_COLS), F32),
        ],
        compiler_params=_cparams(("parallel", "arbitrary")),
        name="ssm",
    )(u5, toep, win, wout, a16, d_skip.reshape(LANE_TILES, 1, LANES))
    return y5.reshape(LANE_TILES, L * B, LANES)


def _cmul(a, b):
    return a[0] * b[0] - a[1] * b[1], a[0] * b[1] + a[1] * b[0]


def _zoh(lr, li, ldt):
    dt = jnp.exp(ldt)
    mag = jnp.exp(lr * dt)
    ab = (mag * jnp.cos(li * dt), mag * jnp.sin(li * dt))
    den = lr * lr + li * li
    zr, zi = ab[0] - 1.0, ab[1]
    return ab, ((zr * lr + zi * li) / den, (zi * lr - zr * li) / den)


def _cpow(base, exponent, bits):
    out = None
    for k in range(bits):
        on = (exponent & (1 << k)) != 0
        sel = (jnp.where(on, base[0], 1.0), jnp.where(on, base[1], 0.0))
        out = sel if out is None else _cmul(out, sel)
        if k + 1 < bits:
            base = _cmul(base, base)
    return out


def _ssm_prep_kernel(col_ref, row_ref, b_ref, ct_ref, bt_ref, toep_ref, win_ref, wout_ref, a_ref):
    bits = CHUNK.bit_length() - 1
    lane = lax.broadcasted_iota(jnp.int32, (1, CHUNK_COLS), 1)
    tau = lane // SSM_GROUP
    rowi = lax.broadcasted_iota(jnp.int32, (CHUNK_COLS, 1), 0)
    back = (CHUNK - 1) - rowi // SSM_GROUP
    lane2 = lax.broadcasted_iota(jnp.int32, (1, LANES), 1)
    a_rows = []
    for m in range(PAIRS_PER_TILE):
        abp, fp = _zoh(row_ref[0, m], row_ref[1, m], row_ref[2, m])
        pwr = _cpow(abp, back, bits)
        a16 = abp
        for _ in range(bits):
            a16 = _cmul(a16, a16)
        a_rows.append(a16)
        wout_rows = [[None, None], [None, None]]
        for j in range(2):
            g = 2 * m + j
            ab, f = _zoh(col_ref[0, g], col_ref[1, g], col_ref[2, g])
            bb = _cmul(f, (b_ref[0, g], b_ref[1, g]))
            pw = _cpow(ab, tau, bits)
            g0 = _cmul(pw, (ct_ref[0, g], ct_ref[1, g]))
            g1 = _cmul(g0, ab)
            kt = jnp.concatenate(
                [jnp.sum(bb[0][:, h:h + 1] * g0[0] - bb[1][:, h:h + 1] * g0[1], axis=0, keepdims=True)
                 for h in range(SSM_GROUP)], axis=0)
            toep_ref[0, g] = jnp.concatenate(
                [jnp.where(lane >= SSM_GROUP * r, pltpu.roll(kt, SSM_GROUP * r, 1), 0.0) if r else kt
                 for r in range(CHUNK)], axis=0).astype(BF16)
            bb2 = _cmul(fp, (bt_ref[0, g], bt_ref[1, g]))
            w = _cmul(pwr, tuple(jnp.concatenate([v] * CHUNK, axis=0) for v in bb2))
            mine = (lane2 < SSM_STATE) if j == 0 else (lane2 >= SSM_STATE)
            win_ref[0, m, j * CHUNK_COLS:(j + 1) * CHUNK_COLS, :] = jnp.concatenate(
                [jnp.where(mine, w[0], 0.0), jnp.where(mine, w[1], 0.0)], axis=1).astype(BF16)
            wout_rows[0][j] = g1[0]
            wout_rows[1][j] = -g1[1]
        zero = jnp.zeros((SSM_STATE, CHUNK_COLS), F32)
        wout_ref[0, m] = jnp.concatenate([
            jnp.concatenate([wout_rows[0][0], zero], axis=1),
            jnp.concatenate([zero, wout_rows[0][1]], axis=1),
            jnp.concatenate([wout_rows[1][0], zero], axis=1),
            jnp.concatenate([zero, wout_rows[1][1]], axis=1)], axis=0).astype(BF16)
    a_ref[0] = jnp.concatenate([jnp.concatenate([a[0] for a in a_rows], axis=1),
                                jnp.concatenate([a[1] for a in a_rows], axis=1)], axis=0)


def _ssm_prep(lam_re, lam_im, log_dt, b_re, b_im, c_re, c_im):
    G, P, H = b_re.shape
    ldt = jnp.broadcast_to(log_dt[:, None], (G, P))
    cols = jnp.stack([lam_re, lam_im, ldt])[..., None]
    rows = jnp.stack([lam_re, lam_im, ldt]).reshape(3, G // 2, 1, 2 * P)
    b = jnp.stack([b_re, b_im])
    ct = jnp.tile(jnp.stack([c_re, c_im]).transpose(0, 1, 3, 2), (1, 1, 1, CHUNK))
    bt = jnp.tile(b.transpose(0, 1, 3, 2), (1, 1, 1, 2))
    gt = GROUPS_PER_TILE
    return pl.pallas_call(
        _ssm_prep_kernel,
        grid=(LANE_TILES,),
        in_specs=[
            pl.BlockSpec((3, gt, P, 1), lambda q: (0, q, 0, 0)),
            pl.BlockSpec((3, PAIRS_PER_TILE, 1, 2 * P), lambda q: (0, q, 0, 0)),
            pl.BlockSpec((2, gt, P, H), lambda q: (0, q, 0, 0)),
            pl.BlockSpec((2, gt, P, CHUNK_COLS), lambda q: (0, q, 0, 0)),
            pl.BlockSpec((2, gt, H, 2 * P), lambda q: (0, q, 0, 0)),
        ],
        out_specs=[
            pl.BlockSpec((1, gt, CHUNK_COLS, CHUNK_COLS), lambda q: (q, 0, 0, 0)),
            pl.BlockSpec((1, PAIRS_PER_TILE, 2 * CHUNK_COLS, 2 * LANES), lambda q: (q, 0, 0, 0)),
            pl.BlockSpec((1, PAIRS_PER_TILE, 2 * LANES, 2 * CHUNK_COLS), lambda q: (q, 0, 0, 0)),
            pl.BlockSpec((1, 2, STATE_COLS), lambda q: (q, 0, 0)),
        ],
        out_shape=[
            jax.ShapeDtypeStruct((LANE_TILES, gt, CHUNK_COLS, CHUNK_COLS), BF16),
            jax.ShapeDtypeStruct((LANE_TILES, PAIRS_PER_TILE, 2 * CHUNK_COLS, 2 * LANES), BF16),
            jax.ShapeDtypeStruct((LANE_TILES, PAIRS_PER_TILE, 2 * LANES, 2 * CHUNK_COLS), BF16),
            jax.ShapeDtypeStruct((LANE_TILES, 2, STATE_COLS), F32),
        ],
        compiler_params=_cparams(("parallel",)),
        name="ssm_prep",
    )(cols, rows, b, ct, bt)


def _layer_norm(r, g, b):
    mu = jnp.mean(r, axis=-1, keepdims=True)
    c = r - mu
    var = jnp.mean(c * c, axis=-1, keepdims=True)
    return c * lax.rsqrt(var + LN_EPS) * g + b


MIX_ROWS = 256


def _gelu_tanh(y):
    return 0.5 * y * (1.0 + jnp.tanh(math.sqrt(2.0 / math.pi) * (y + 0.044715 * (y * y * y))))


def _mix_ln1_kernel(x_ref, attn_ref, y_ref, wglu_ref, bglu_ref, wo_ref, g_ref, b_ref, o_ref):
    nb, tl, d = x_ref.shape
    per = MIX_ROWS // tl
    n_pieces = nb // per

    def glu(n):
        y = jnp.concatenate(
            [jnp.concatenate([y_ref[q, pl.ds(b, tl, stride=nb), :] for q in range(LANE_TILES)], axis=1)
             for b in range(n * per, (n + 1) * per)], axis=0)
        gl = _gelu_tanh(y)
        z = gl * jax.nn.sigmoid(jnp.dot(gl.astype(BF16), wglu_ref[...], preferred_element_type=F32) + bglu_ref[...])
        return z.astype(BF16)

    def project(n, z):
        bs = slice(n * per, (n + 1) * per)
        mix = (jnp.dot(attn_ref[bs].reshape(MIX_ROWS, ATT_WIDTH), wo_ref[:ATT_WIDTH], preferred_element_type=F32)
               + jnp.dot(z, wo_ref[ATT_WIDTH:], preferred_element_type=F32))
        res = _layer_norm(ALPHA * x_ref[bs].reshape(MIX_ROWS, d) + mix, g_ref[...], b_ref[...])
        o_ref[bs] = res.reshape(per, tl, d)

    z = glu(0)
    for n in range(n_pieces):
        z_next = glu(n + 1) if n + 1 < n_pieces else None
        project(n, z)
        z = z_next


def _mix_ln1(x, attn, y_tm, wglu_bf, b_glu, wo_bf, ln_g, ln_b, tl=128):
    B, L, D = x.shape
    assert MIX_ROWS % tl == 0 and B % (MIX_ROWS // tl) == 0
    resident = lambda shape: pl.BlockSpec(shape, lambda l: (0, 0), pipeline_mode=pl.Buffered(1))
    return pl.pallas_call(
        _mix_ln1_kernel,
        grid=(L // tl,),
        in_specs=[
            pl.BlockSpec((B, tl, D), lambda l: (0, l, 0)),
            pl.BlockSpec((B, tl, ATT_WIDTH), lambda l: (0, l, 0)),
            pl.BlockSpec((LANE_TILES, tl * B, LANES), lambda l: (0, l, 0)),
            resident((SSM_WIDTH, SSM_WIDTH)),
            resident((1, SSM_WIDTH)),
            resident((D, D)),
            resident((1, D)),
            resident((1, D)),
        ],
        out_specs=pl.BlockSpec((B, tl, D), lambda l: (0, l, 0)),
        out_shape=jax.ShapeDtypeStruct((B, L, D), F32),
        compiler_params=_cparams(("parallel",)),
        name="mix_ln1",
    )(x, attn, y_tm, wglu_bf, b_glu, wo_bf, ln_g, ln_b)


HALO = 16
FFN_SUB = 256
FFN_ROWS = 128


def _ffn_kernel(x_ref, xh_ref, p_ref, wple_ref, wgate_ref, wup_ref, cw_ref, cb_ref, wd_ref, g_ref, b_ref, o_ref,
                x32_sc, xh32_sc, p32_sc, xb_sc, acc_sc, hg_sc, hv_sc, act_sc, *, tl):
    nb = x_ref.shape[0]
    tm = tl * nb
    d_slabs = x_ref.shape[2] // LANES
    hp = (CONV_WIDTH - 1)
    for bb in range(nb):
        for s in range(d_slabs):
            lanes = slice(s * LANES, (s + 1) * LANES)
            x32_sc[s, pl.ds(bb, tl, stride=nb), :] = x_ref[bb, :, lanes]
            xh32_sc[s, pl.ds(bb, hp, stride=nb), :] = xh_ref[bb, SUBLANES - hp:, lanes]
        for s in range(p_ref.shape[2] // LANES):
            p32_sc[s, pl.ds(bb, tl, stride=nb), :] = p_ref[bb, :, s * LANES:(s + 1) * LANES]
    x_tm = jnp.concatenate([x32_sc[s] for s in range(d_slabs)], axis=1)
    p_tm = jnp.concatenate([p32_sc[s] for s in range(p32_sc.shape[0])], axis=1)
    keep = jnp.where(pl.program_id(0) == 0, 0.0, 1.0)
    xb_sc[0:HALO] = (jnp.concatenate([xh32_sc[s] for s in range(d_slabs)], axis=1) * keep).astype(BF16)
    xb_sc[HALO:] = x_tm.astype(BF16)
    gate = jax.nn.sigmoid(jnp.dot(xb_sc[HALO:], wgate_ref[...], preferred_element_type=F32))
    ple = jnp.dot(p_tm.astype(BF16), wple_ref[...], preferred_element_type=F32) * gate
    acc_sc[...] = ALPHA * x_tm + ple

    def conv(h_sc, slot, cols, r0, rows):
        out = cb_ref[:, cols] + cw_ref[CONV_WIDTH - 1:CONV_WIDTH, cols] * h_sc[slot, HALO + r0:HALO + r0 + rows]
        for j in range(CONV_WIDTH - 1):
            back = (CONV_WIDTH - 1 - j) * nb
            out = out + cw_ref[j:j + 1, cols] * h_sc[slot, HALO + r0 - back:HALO + r0 - back + rows]
        return out

    n_sub = D_FF // FFN_SUB
    g_cols = lambda n: slice(n * FFN_SUB, (n + 1) * FFN_SUB)
    v_cols = lambda n: slice(D_FF + n * FFN_SUB, D_FF + (n + 1) * FFN_SUB)

    def up(n):
        xb = xb_sc[...]
        hg_sc[n % 2] = jnp.dot(xb, wup_ref[:, g_cols(n)], preferred_element_type=F32)
        hv_sc[n % 2] = jnp.dot(xb, wup_ref[:, v_cols(n)], preferred_element_type=F32)

    def gate_block(n):
        for r0 in range(0, tm, FFN_ROWS):
            gc = conv(hg_sc, n % 2, g_cols(n), r0, FFN_ROWS)
            vc = conv(hv_sc, n % 2, v_cols(n), r0, FFN_ROWS)
            act_sc[r0:r0 + FFN_ROWS, g_cols(n)] = (gc * jax.nn.sigmoid(gc) * vc).astype(BF16)

    def down(lo, hi):
        cols = slice(lo * FFN_SUB, hi * FFN_SUB)
        return jnp.dot(act_sc[:, cols], wd_ref[cols, :], preferred_element_type=F32)

    half = (n_sub + 1) // 2
    up(0)
    for n in range(n_sub):
        if n + 1 < n_sub:
            up(n + 1)
        gate_block(n)
        if n + 1 == half:
            acc_sc[...] += down(0, half)

    res = _layer_norm(acc_sc[...] + down(half, n_sub), g_ref[...], b_ref[...])
    for s in range(d_slabs):
        x32_sc[s] = res[:, s * LANES:(s + 1) * LANES]
    for bb in range(nb):
        for s in range(d_slabs):
            o_ref[bb, :, s * LANES:(s + 1) * LANES] = x32_sc[s, pl.ds(bb, tl, stride=nb), :]


def _ffn_ln2(x1, p, wple_bf, wgate_bf, wup_bf, conv_w, conv_b, wdown_bf, ln_g, ln_b, tl=64):
    B, L, D = x1.shape
    tm = tl * B
    assert D_FF % FFN_SUB == 0 and HALO == (CONV_WIDTH - 1) * B and tm % FFN_ROWS == 0
    const = lambda i: (0, 0)
    resident = lambda shape: pl.BlockSpec(shape, const, pipeline_mode=pl.Buffered(1))
    return pl.pallas_call(
        functools.partial(_ffn_kernel, tl=tl),
        grid=(L // tl,),
        in_specs=[
            pl.BlockSpec((B, tl, D), lambda i: (0, i, 0)),
            pl.BlockSpec((B, SUBLANES, D), lambda i: (0, jnp.maximum(i * (tl // SUBLANES) - 1, 0), 0)),
            pl.BlockSpec((B, tl, PLE_DIM), lambda i: (0, i, 0)),
            resident((PLE_DIM, D)),
            resident((D, D)),
            resident((D, 2 * D_FF)),
            resident((CONV_WIDTH, 2 * D_FF)),
            resident((1, 2 * D_FF)),
            resident((D_FF, D)),
            resident((1, D)),
            resident((1, D)),
        ],
        out_specs=pl.BlockSpec((B, tl, D), lambda i: (0, i, 0)),
        out_shape=jax.ShapeDtypeStruct((B, L, D), F32),
        scratch_shapes=[
            pltpu.VMEM((D // LANES, tm, LANES), F32),
            pltpu.VMEM((D // LANES, HALO, LANES), F32),
            pltpu.VMEM((PLE_DIM // LANES, tm, LANES), F32),
            pltpu.VMEM((tm + HALO, D), BF16),
            pltpu.VMEM((tm, D), F32),
            pltpu.VMEM((2, tm + HALO, FFN_SUB), F32),
            pltpu.VMEM((2, tm + HALO, FFN_SUB), F32),
            pltpu.VMEM((tm, D_FF), BF16),
        ],
        compiler_params=_cparams(("parallel",)),
        name="ffn_ln2",
    )(x1, x1, p, wple_bf, wgate_bf, wup_bf, conv_w, conv_b, wdown_bf, ln_g, ln_b)


def kernel(x, p, w_in, diff_lambda_q1, diff_lambda_k1, diff_lambda_q2, diff_lambda_k2, diff_subln_g, ssm_lambda_re, ssm_lambda_im, ssm_log_dt, ssm_b_re, ssm_b_im, ssm_c_re, ssm_c_im, ssm_d, ssm_w_glu, ssm_b_glu, w_o, ln1_g, ln1_b, ffn_w_up, ffn_conv_w, ffn_conv_b, ffn_w_down, w_ple, w_ple_gate, ln2_g, ln2_b):
    B, L, D = x.shape
    assert B == SUBLANES and D == D_MODEL and L % 512 == 0
    for i in range(DEPTH):
        lam_init = 0.8 - 0.6 * math.exp(-0.3 * i)
        row = lambda a: a[i].reshape(1, -1)
        qt, k, vt, u_tm = _in_proj(x, w_in[i])
        attn = _diff_attn(qt, k, vt, row(diff_lambda_q1), row(diff_lambda_k1), row(diff_lambda_q2),
                          row(diff_lambda_k2), diff_subln_g[i], lam_init)
        tables = _ssm_prep(ssm_lambda_re[i], ssm_lambda_im[i], ssm_log_dt[i], ssm_b_re[i], ssm_b_im[i],
                           ssm_c_re[i], ssm_c_im[i])
        y_tm = _ssm(u_tm, tables, ssm_d[i], B, L)
        x1 = _mix_ln1(x, attn, y_tm, ssm_w_glu[i].astype(BF16), row(ssm_b_glu), w_o[i].astype(BF16),
                      row(ln1_g), row(ln1_b))
        x = _ffn_ln2(x1, p[i], w_ple[i].astype(BF16), w_ple_gate[i].astype(BF16), ffn_w_up[i].astype(BF16),
                     ffn_conv_w[i], row(ffn_conv_b), ffn_w_down[i].astype(BF16), row(ln2_g), row(ln2_b))
    return x
```

```python
import functools
import math

import jax
import jax.numpy as jnp
from jax import lax
from jax.experimental import pallas as pl
from jax.experimental.pallas import tpu as pltpu

F32 = jnp.float32
BF16 = jnp.bfloat16

SUBLANES = 8
LANES = 128

DEPTH = 1
D_MODEL = 1024
PLE_DIM = 256
HEADS = 4
QK_DIM = 64
V_DIM = 2 * QK_DIM
QK_WIDTH = HEADS * 2 * QK_DIM
ATT_WIDTH = HEADS * V_DIM
SSM_WIDTH = D_MODEL - ATT_WIDTH
SSM_GROUP = 16
SSM_GROUPS = SSM_WIDTH // SSM_GROUP
SSM_STATE = 64
D_FF = 2816
CONV_WIDTH = 3
LN_EPS = 1e-5
ALPHA = (2 * DEPTH) ** 0.25
QK_SCALE = QK_DIM ** -0.5 * math.log2(math.e)

CHUNK = 16
CHUNK_COLS = CHUNK * SSM_GROUP
GROUPS_PER_TILE = LANES // SSM_GROUP
LANE_TILES = SSM_WIDTH // LANES
PAIRS_PER_TILE = GROUPS_PER_TILE // 2
STATE_COLS = GROUPS_PER_TILE * SSM_STATE

VMEM_LIMIT = 56 * 1024 * 1024


def _cparams(sem):
    return pltpu.CompilerParams(dimension_semantics=sem, vmem_limit_bytes=VMEM_LIMIT)


TB = 256
HEADS_PER_STEP = 4


def _in_proj_kernel(x_ref, wqt_ref, wk_ref, wvt_ref, wu_ref, qt_ref, k_ref, vt_ref, u_ref):
    nb, tl, d = x_ref.shape
    xb = x_ref[...].reshape(nb * tl, d).astype(BF16)
    nt = (((1,), (1,)), ((), ()))
    qt = lax.dot_general(wqt_ref[...], xb, nt, preferred_element_type=F32) * QK_SCALE
    vt = lax.dot_general(wvt_ref[...], xb, nt, preferred_element_type=F32)
    for b in range(nb):
        for h in range(HEADS):
            for c in range(tl // TB):
                cols = slice(b * tl + c * TB, b * tl + (c + 1) * TB)
                qt_ref[b, h, c] = qt[h * LANES:(h + 1) * LANES, cols].astype(BF16)
                vt_ref[b, h, c] = vt[h * LANES:(h + 1) * LANES, cols].astype(BF16)
    k_ref[...] = jnp.dot(xb, wk_ref[...], preferred_element_type=F32).astype(BF16).reshape(k_ref.shape)
    hu = jnp.dot(xb, wu_ref[...], preferred_element_type=F32)
    for b in range(nb):
        for q in range(LANE_TILES):
            u_ref[q, pl.ds(b, tl, stride=nb), :] = hu[b * tl:(b + 1) * tl, q * LANES:(q + 1) * LANES]


def _in_proj(x, w_in, tl=256):
    B, L, D = x.shape
    w = w_in.astype(BF16)
    wqt = w[:, :QK_WIDTH].T
    wk = w[:, QK_WIDTH:2 * QK_WIDTH]
    wvt = w[:, 2 * QK_WIDTH:2 * QK_WIDTH + ATT_WIDTH].T
    wu = w[:, 2 * QK_WIDTH + ATT_WIDTH:]
    resident = lambda shape: pl.BlockSpec(shape, lambda l: (0, 0), pipeline_mode=pl.Buffered(1))
    slab = pl.BlockSpec((B, HEADS, tl // TB, LANES, TB), lambda l: (0, 0, l, 0, 0))
    slab_shape = jax.ShapeDtypeStruct((B, HEADS, L // TB, LANES, TB), BF16)
    return pl.pallas_call(
        _in_proj_kernel,
        grid=(L // tl,),
        in_specs=[
            pl.BlockSpec((B, tl, D), lambda l: (0, l, 0)),
            resident((QK_WIDTH, D)),
            resident((D, QK_WIDTH)),
            resident((ATT_WIDTH, D)),
            resident((D, SSM_WIDTH)),
        ],
        out_specs=[
            slab,
            pl.BlockSpec((B, tl, QK_WIDTH), lambda l: (0, l, 0)),
            slab,
            pl.BlockSpec((LANE_TILES, tl * B, LANES), lambda l: (0, l, 0)),
        ],
        out_shape=[
            slab_shape,
            jax.ShapeDtypeStruct((B, L, QK_WIDTH), BF16),
            slab_shape,
            jax.ShapeDtypeStruct((LANE_TILES, L * B, LANES), F32),
        ],
        compiler_params=_cparams(("parallel",)),
        name="in_proj",
    )(x, wqt, wk, wvt, wu)


def _attn_flat_kernel(lq1_ref, lk1_ref, lq2_ref, lk2_ref, g_ref, qt_ref, k_ref, vt_ref, o_ref,
                      acc_sc, qst_sc, s_sc, p_sc, bm_sc, st_sc, al_sc, *, lam_init):
    n_q = qt_ref.shape[2]
    n_pairs = n_q * (n_q + 1) // 2
    assert TB == 2 * LANES and n_pairs % 2 == 0
    heads = range(HEADS)
    n_strips = 2 * TB // LANES
    lam = (jnp.exp(jnp.sum(lq1_ref[...] * lk1_ref[...], axis=-1, keepdims=True))
           - jnp.exp(jnp.sum(lq2_ref[...] * lk2_ref[...], axis=-1, keepdims=True)) + lam_init)

    def prepare(i, c):
        for hh in heads:
            qt = qt_ref[0, hh, i]
            row = lax.broadcasted_iota(jnp.int32, qt.shape, 0)
            zero = jnp.zeros_like(qt)
            qst_sc[i, hh, :, :TB] = jnp.where(row < QK_DIM, qt, zero)
            qst_sc[i, hh, :, TB:] = jnp.where(row >= QK_DIM, qt, zero)
            acc_sc[i, hh] = jnp.zeros(acc_sc.shape[2:], F32)
            st_sc[i, hh, 0] = jnp.full((1, 2 * TB), -jnp.inf, F32)
            st_sc[i, hh, 1] = jnp.zeros((1, 2 * TB), F32)
        return c

    lax.fori_loop(0, n_q, prepare, 0)
    p_sc[1] = jnp.zeros(p_sc.shape[1:], BF16)
    al_sc[1] = jnp.ones(al_sc.shape[1:], F32)

    def scores(pair, slot):
        i, j = pair
        rows = pl.ds(pl.multiple_of(j * TB, TB), TB)
        for hh in heads:
            kb = k_ref[0, rows, hh * LANES:(hh + 1) * LANES]
            s = jnp.dot(kb, qst_sc[i, hh], preferred_element_type=F32)
            s_sc[slot, hh] = s
            bm_sc[slot, hh] = jnp.max(s, axis=0, keepdims=True)

    def values(pair, slot):
        i, j = pair
        for hh in heads:
            pv = jnp.dot(vt_ref[0, hh, j], p_sc[slot, hh], preferred_element_type=F32)
            acc_sc[i, hh] = al_sc[slot, hh] * acc_sc[i, hh] + pv

    def softmax(pair, slot, diagonal):
        i, _ = pair
        for hh in heads:
            for c in range(n_strips):
                lanes = slice(c * LANES, (c + 1) * LANES)
                m = st_sc[i, hh, 0, :, lanes]
                if not diagonal:
                    mc = jnp.maximum(m, bm_sc[slot, hh, :, lanes])
                    p = jnp.exp2(s_sc[slot, hh, :, lanes] - mc)
                else:
                    s = s_sc[slot, hh, :, lanes]
                    key = lax.broadcasted_iota(jnp.int32, s.shape, 0)
                    qry = lax.broadcasted_iota(jnp.int32, s.shape, 1) + (c * LANES) % TB
                    s = jnp.where(key <= qry, s, -jnp.inf)
                    mc = jnp.maximum(m, jnp.max(s, axis=0, keepdims=True))
                    p = jnp.exp2(s - mc)
                p_sc[slot, hh, :, lanes] = p.astype(BF16)
                ac = jnp.exp2(m - mc)
                st_sc[i, hh, 0, :, lanes] = mc
                st_sc[i, hh, 1, :, lanes] = ac * st_sc[i, hh, 1, :, lanes] + jnp.sum(p, axis=0, keepdims=True)
                al_sc[slot, hh, :, lanes] = ac

    def succ(pair):
        i, j = pair
        last = j == i
        i2 = jnp.minimum(jnp.where(last, i + 1, i), n_q - 1)
        return i2, jnp.where(last, 0, j + 1)

    def two_steps(t, carry):
        prev, cur = carry
        nxt = succ(cur)
        nxt2 = succ(nxt)

        def body(diag0, diag1):
            scores(nxt, 1)
            values(prev, 1)
            softmax(cur, 0, diag0)
            scores(nxt2, 0)
            values(cur, 0)
            softmax(nxt, 1, diag1)

        d0 = cur[1] == cur[0]
        d1 = nxt[1] == nxt[0]
        pl.when(d0)(lambda: body(True, False))
        pl.when(d1)(lambda: body(False, True))
        pl.when(jnp.logical_not(jnp.logical_or(d0, d1)))(lambda: body(False, False))
        return nxt, nxt2

    zero = jnp.int32(0)
    first = (zero, zero)
    scores(first, 0)
    last, _ = lax.fori_loop(0, n_pairs // 2, two_steps, (first, first))
    values(last, 1)

    def finish(i, c):
        for hh in heads:
            o = acc_sc[i, hh] * (1.0 / st_sc[i, hh, 1])
            o = o[:, :TB] - lam * o[:, TB:]
            ms = jnp.mean(o * o, axis=0, keepdims=True)
            o = o * lax.rsqrt(ms + LN_EPS) * (g_ref[...] * (1.0 - lam_init))
            o_ref[0, pl.ds(pl.multiple_of(i * TB, TB), TB), hh * LANES:(hh + 1) * LANES] = o.T.astype(o_ref.dtype)
        return c

    lax.fori_loop(0, n_q, finish, 0)


def _diff_attn_flat(qt, k, vt, lq1, lk1, lq2, lk2, subln_g, lam_init):
    B, L, _ = k.shape
    n_q = L // TB
    vec = pl.BlockSpec((1, QK_DIM), lambda b: (0, 0))
    return pl.pallas_call(
        functools.partial(_attn_flat_kernel, lam_init=lam_init),
        grid=(B,),
        in_specs=[
            vec, vec, vec, vec,
            pl.BlockSpec((V_DIM, 1), lambda b: (0, 0)),
            pl.BlockSpec((1, HEADS, n_q, LANES, TB), lambda b: (b, 0, 0, 0, 0)),
            pl.BlockSpec((1, L, HEADS * LANES), lambda b: (b, 0, 0)),
            pl.BlockSpec((1, HEADS, n_q, LANES, TB), lambda b: (b, 0, 0, 0, 0)),
        ],
        out_specs=pl.BlockSpec((1, L, HEADS * LANES), lambda b: (b, 0, 0)),
        out_shape=jax.ShapeDtypeStruct((B, L, ATT_WIDTH), BF16),
        scratch_shapes=[
            pltpu.VMEM((n_q, HEADS, V_DIM, 2 * TB), F32),
            pltpu.VMEM((n_q, HEADS, LANES, 2 * TB), BF16),
            pltpu.VMEM((2, HEADS, TB, 2 * TB), F32),
            pltpu.VMEM((2, HEADS, TB, 2 * TB), BF16),
            pltpu.VMEM((2, HEADS, 1, 2 * TB), F32),
            pltpu.VMEM((n_q, HEADS, 2, 1, 2 * TB), F32),
            pltpu.VMEM((2, HEADS, 1, 2 * TB), F32),
        ],
        compiler_params=_cparams(("parallel",)),
        name="diff_attn",
    )(lq1, lk1, lq2, lk2, subln_g.reshape(V_DIM, 1), qt, k, vt)


def _attn_kernel(lq1_ref, lk1_ref, lq2_ref, lk2_ref, g_ref, qt_ref, k_ref, vt_ref, o_ref,
                 acc_sc, qst_sc, s_sc, p_sc, bm_sc, st_sc, *, lam_init):
    qi = pl.program_id(2)
    lam = (jnp.exp(jnp.sum(lq1_ref[...] * lk1_ref[...], axis=-1, keepdims=True))
           - jnp.exp(jnp.sum(lq2_ref[...] * lk2_ref[...], axis=-1, keepdims=True)) + lam_init)

    for hh in range(HEADS_PER_STEP):
        qt = qt_ref[0, hh, 0]
        row = lax.broadcasted_iota(jnp.int32, qt.shape, 0)
        zero = jnp.zeros_like(qt)
        qst_sc[hh, :, :TB] = jnp.where(row < QK_DIM, qt, zero)
        qst_sc[hh, :, TB:] = jnp.where(row >= QK_DIM, qt, zero)

    acc_sc[...] = jnp.zeros(acc_sc.shape, F32)
    p_sc[1] = jnp.zeros(p_sc.shape[1:], BF16)
    n_strips = 2 * TB // LANES
    heads = range(HEADS_PER_STEP)
    assert TB == 2 * LANES

    def scores(j, slot):
        rows = pl.ds(pl.multiple_of(j * TB, TB), TB)
        for hh in heads:
            kb = k_ref[0, rows, hh * LANES:(hh + 1) * LANES]
            s = jnp.dot(kb, qst_sc[hh], preferred_element_type=F32)
            s_sc[slot, hh] = s
            bm_sc[slot, hh] = jnp.max(s, axis=0, keepdims=True)

    def values(j, slot):
        j = jnp.maximum(j, 0)
        for hh in heads:
            pv = jnp.dot(vt_ref[0, hh, j], p_sc[slot, hh], preferred_element_type=F32)
            acc_sc[hh] = st_sc[hh, 2] * acc_sc[hh] + pv

    def softmax(slot, diagonal):
        for hh in heads:
            for c in range(n_strips):
                lanes = slice(c * LANES, (c + 1) * LANES)
                m = st_sc[hh, 0, :, lanes]
                if not diagonal:
                    mc = jnp.maximum(m, bm_sc[slot, hh, :, lanes])
                    p = jnp.exp2(s_sc[slot, hh, :, lanes] - mc)
                    p_sc[slot, hh, :, lanes] = p.astype(BF16)
                    psum = jnp.sum(p, axis=0, keepdims=True)
                else:
                    q0 = (c * LANES) % TB
                    s = s_sc[slot, hh, :, lanes]
                    key = lax.broadcasted_iota(jnp.int32, s.shape, 0)
                    qry = lax.broadcasted_iota(jnp.int32, s.shape, 1) + q0
                    s = jnp.where(key <= qry, s, -jnp.inf)
                    mc = jnp.maximum(m, jnp.max(s, axis=0, keepdims=True))
                    p = jnp.exp2(s - mc)
                    p_sc[slot, hh, :, lanes] = p.astype(BF16)
                    psum = jnp.sum(p, axis=0, keepdims=True)
                ac = jnp.exp2(m - mc)
                st_sc[hh, 0, :, lanes] = mc
                st_sc[hh, 1, :, lanes] = ac * st_sc[hh, 1, :, lanes] + psum
                st_sc[hh, 2, :, lanes] = ac

    def pair(t, c):
        a = 2 * t
        scores(a + 1, 1)
        values(a - 1, 1)
        softmax(0, False)
        scores(a + 2, 0)
        values(a, 0)
        softmax(1, False)
        return c

    for hh in heads:
        st_sc[hh, 0] = jnp.full((1, 2 * TB), -jnp.inf, F32)
        st_sc[hh, 1] = jnp.zeros((1, 2 * TB), F32)
        st_sc[hh, 2] = jnp.ones((1, 2 * TB), F32)
    scores(0, 0)
    lax.fori_loop(0, qi // 2, pair, 0)

    @pl.when(qi % 2 == 1)
    def _():
        values(qi - 2, 1)
        scores(qi, 1)
        softmax(0, False)
        values(qi - 1, 0)
        softmax(1, True)
        values(qi, 1)

    @pl.when(qi % 2 == 0)
    def _():
        values(qi - 1, 1)
        softmax(0, True)
        values(qi, 0)

    l_fin = [st_sc[hh, 1] for hh in heads]

    for hh in range(HEADS_PER_STEP):
        o = acc_sc[hh] * (1.0 / l_fin[hh])
        o = o[:, :TB] - lam * o[:, TB:]
        ms = jnp.mean(o * o, axis=0, keepdims=True)
        o = o * lax.rsqrt(ms + LN_EPS) * (g_ref[...] * (1.0 - lam_init))
        o_ref[0, :, hh * LANES:(hh + 1) * LANES] = o.T.astype(o_ref.dtype)


def _diff_attn(qt, k, vt, lq1, lk1, lq2, lk2, subln_g, lam_init):
    B, L, _ = k.shape
    n_q = L // TB
    hp = HEADS_PER_STEP
    vec = pl.BlockSpec((1, QK_DIM), lambda b, h, i: (0, 0))
    return pl.pallas_call(
        functools.partial(_attn_kernel, lam_init=lam_init),
        grid=(B, HEADS // hp, n_q),
        in_specs=[
            vec, vec, vec, vec,
            pl.BlockSpec((V_DIM, 1), lambda b, h, i: (0, 0)),
            pl.BlockSpec((1, hp, 1, LANES, TB), lambda b, h, i: (b, h, i, 0, 0)),
            pl.BlockSpec((1, L, hp * LANES), lambda b, h, i: (b, 0, h)),
            pl.BlockSpec((1, hp, n_q, LANES, TB), lambda b, h, i: (b, h, 0, 0, 0)),
        ],
        out_specs=pl.BlockSpec((1, TB, hp * LANES), lambda b, h, i: (b, i, h)),
        out_shape=jax.ShapeDtypeStruct((B, L, ATT_WIDTH), BF16),
        scratch_shapes=[
            pltpu.VMEM((hp, V_DIM, 2 * TB), F32),
            pltpu.VMEM((hp, LANES, 2 * TB), BF16),
            pltpu.VMEM((2, hp, TB, 2 * TB), F32),
            pltpu.VMEM((2, hp, TB, 2 * TB), BF16),
            pltpu.VMEM((2, hp, 1, 2 * TB), F32),
            pltpu.VMEM((hp, 3, 1, 2 * TB), F32),
        ],
        compiler_params=_cparams(("parallel", "parallel", "arbitrary")),
        name="diff_attn",
    )(lq1, lk1, lq2, lk2, subln_g.reshape(V_DIM, 1), qt, k, vt)


def _block_transpose8(vs):
    lane_blk = lax.broadcasted_iota(jnp.int32, vs[0].shape, 1) // SSM_GROUP
    for d in (4, 2, 1):
        keep = (lane_blk & d) == 0
        new = list(vs)
        for a in range(8):
            if a & d == 0:
                lo, hi = vs[a], vs[a + d]
                new[a] = jnp.where(keep, lo, pltpu.roll(hi, SSM_GROUP * d, 1))
                new[a + d] = jnp.where(keep, pltpu.roll(lo, LANES - SSM_GROUP * d, 1), hi)
        vs = new
    return vs


def _ssm_kernel(u_ref, toep_ref, win_ref, wout_ref, a_ref, d_ref, y_ref, s_sc, x_sc, sp_sc, *, jt):
    rows = jt * SUBLANES

    @pl.when(pl.program_id(1) == 0)
    def _():
        s_sc[...] = jnp.zeros(s_sc.shape, F32)

    xs = [u_ref[0, :, r].reshape(rows, LANES) for r in range(CHUNK)]
    z_lo = _block_transpose8(xs[:8])
    z_hi = _block_transpose8(xs[8:])
    ug = [jnp.concatenate([z_lo[g], z_hi[g]], axis=1).astype(BF16) for g in range(GROUPS_PER_TILE)]

    y = [jnp.dot(ug[g], toep_ref[0, g], preferred_element_type=F32) for g in range(GROUPS_PER_TILE)]

    for m in range(PAIRS_PER_TILE):
        up = jnp.concatenate([ug[2 * m], ug[2 * m + 1]], axis=1)
        xp = jnp.dot(up, win_ref[0, m], preferred_element_type=F32)
        x_sc[0, :, m * LANES:(m + 1) * LANES] = xp[:, :LANES]
        x_sc[1, :, m * LANES:(m + 1) * LANES] = xp[:, LANES:]

    ar = a_ref[0, 0:1, :]
    ai = a_ref[0, 1:2, :]

    def step(j, carry):
        sr, si = carry
        r8 = pl.ds(pl.multiple_of(j * SUBLANES, SUBLANES), SUBLANES)
        sp_sc[0, r8, :] = sr
        sp_sc[1, r8, :] = si
        return (ar * sr - ai * si + x_sc[0, r8, :], ar * si + ai * sr + x_sc[1, r8, :])

    sr, si = lax.fori_loop(0, jt, step, (s_sc[0], s_sc[1]))
    s_sc[0] = sr
    s_sc[1] = si

    for m in range(PAIRS_PER_TILE):
        sp = jnp.concatenate([sp_sc[0, :, m * LANES:(m + 1) * LANES],
                              sp_sc[1, :, m * LANES:(m + 1) * LANES]], axis=1).astype(BF16)
        yc = jnp.dot(sp, wout_ref[0, m], preferred_element_type=F32)
        y[2 * m] = y[2 * m] + yc[:, :CHUNK_COLS]
        y[2 * m + 1] = y[2 * m + 1] + yc[:, CHUNK_COLS:]

    d = d_ref[0]
    for half in range(2):
        ws = _block_transpose8([y[g][:, half * LANES:(half + 1) * LANES] for g in range(GROUPS_PER_TILE)])
        for t8 in range(8):
            t = half * 8 + t8
            u_t = u_ref[0, :, t].reshape(rows, LANES)
            y_ref[0, :, t] = (ws[t8] + d * u_t).reshape(jt, SUBLANES, LANES)


def _ssm(u_tm, tables, d_skip, B, L, lt=1024):
    toep, win, wout, a16 = tables
    jt = lt // CHUNK
    n_l = L // lt
    u5 = u_tm.reshape(LANE_TILES, L // CHUNK, CHUNK, B, LANES)
    rows = jt * B
    y5 = pl.pallas_call(
        functools.partial(_ssm_kernel, jt=jt),
        grid=(LANE_TILES, n_l),
        in_specs=[
            pl.BlockSpec((1, jt, CHUNK, B, LANES), lambda q, l: (q, l, 0, 0, 0)),
            pl.BlockSpec((1, GROUPS_PER_TILE, CHUNK_COLS, CHUNK_COLS), lambda q, l: (q, 0, 0, 0)),
            pl.BlockSpec((1, PAIRS_PER_TILE, 2 * CHUNK_COLS, 2 * LANES), lambda q, l: (q, 0, 0, 0)),
            pl.BlockSpec((1, PAIRS_PER_TILE, 2 * LANES, 2 * CHUNK_COLS), lambda q, l: (q, 0, 0, 0)),
            pl.BlockSpec((1, 2, STATE_COLS), lambda q, l: (q, 0, 0)),
            pl.BlockSpec((1, 1, LANES), lambda q, l: (q, 0, 0)),
        ],
        out_specs=pl.BlockSpec((1, jt, CHUNK, B, LANES), lambda q, l: (q, l, 0, 0, 0)),
        out_shape=jax.ShapeDtypeStruct(u5.shape, F32),
        scratch_shapes=[
            pltpu.VMEM((2, B, STATE_COLS), F32),
            pltpu.VMEM((2, rows, STATE_COLS), F32),
            pltpu.VMEM((2, rows, STATE_COLS), F32),
        ],
        compiler_params=_cparams(("parallel", "arbitrary")),
        name="ssm",
    )(u5, toep, win, wout, a16, d_skip.reshape(LANE_TILES, 1, LANES))
    return y5.reshape(LANE_TILES, L * B, LANES)


def _cmul(a, b):
    return a[0] * b[0] - a[1] * b[1], a[0] * b[1] + a[1] * b[0]


def _zoh(lr, li, ldt):
    dt = jnp.exp(ldt)
    mag = jnp.exp(lr * dt)
    ab = (mag * jnp.cos(li * dt), mag * jnp.sin(li * dt))
    den = lr * lr + li * li
    zr, zi = ab[0] - 1.0, ab[1]
    return ab, ((zr * lr + zi * li) / den, (zi * lr - zr * li) / den)


def _cpow(base, exponent, bits):
    out = None
    for k in range(bits):
        on = (exponent & (1 << k)) != 0
        sel = (jnp.where(on, base[0], 1.0), jnp.where(on, base[1], 0.0))
        out = sel if out is None else _cmul(out, sel)
        if k + 1 < bits:
            base = _cmul(base, base)
    return out


def _ssm_prep_kernel(col_ref, row_ref, b_ref, ct_ref, bt_ref, toep_ref, win_ref, wout_ref, a_ref):
    bits = CHUNK.bit_length() - 1
    lane = lax.broadcasted_iota(jnp.int32, (1, CHUNK_COLS), 1)
    tau = lane // SSM_GROUP
    rowi = lax.broadcasted_iota(jnp.int32, (CHUNK_COLS, 1), 0)
    back = (CHUNK - 1) - rowi // SSM_GROUP
    lane2 = lax.broadcasted_iota(jnp.int32, (1, LANES), 1)
    a_rows = []
    for m in range(PAIRS_PER_TILE):
        abp, fp = _zoh(row_ref[0, m], row_ref[1, m], row_ref[2, m])
        pwr = _cpow(abp, back, bits)
        a16 = abp
        for _ in range(bits):
            a16 = _cmul(a16, a16)
        a_rows.append(a16)
        wout_rows = [[None, None], [None, None]]
        for j in range(2):
            g = 2 * m + j
            ab, f = _zoh(col_ref[0, g], col_ref[1, g], col_ref[2, g])
            bb = _cmul(f, (b_ref[0, g], b_ref[1, g]))
            pw = _cpow(ab, tau, bits)
            g0 = _cmul(pw, (ct_ref[0, g], ct_ref[1, g]))
            g1 = _cmul(g0, ab)
            kt = jnp.concatenate(
                [jnp.sum(bb[0][:, h:h + 1] * g0[0] - bb[1][:, h:h + 1] * g0[1], axis=0, keepdims=True)
                 for h in range(SSM_GROUP)], axis=0)
            toep_ref[0, g] = jnp.concatenate(
                [jnp.where(lane >= SSM_GROUP * r, pltpu.roll(kt, SSM_GROUP * r, 1), 0.0) if r else kt
                 for r in range(CHUNK)], axis=0).astype(BF16)
            bb2 = _cmul(fp, (bt_ref[0, g], bt_ref[1, g]))
            w = _cmul(pwr, tuple(jnp.concatenate([v] * CHUNK, axis=0) for v in bb2))
            mine = (lane2 < SSM_STATE) if j == 0 else (lane2 >= SSM_STATE)
            win_ref[0, m, j * CHUNK_COLS:(j + 1) * CHUNK_COLS, :] = jnp.concatenate(
                [jnp.where(mine, w[0], 0.0), jnp.where(mine, w[1], 0.0)], axis=1).astype(BF16)
            wout_rows[0][j] = g1[0]
            wout_rows[1][j] = -g1[1]
        zero = jnp.zeros((SSM_STATE, CHUNK_COLS), F32)
        wout_ref[0, m] = jnp.concatenate([
            jnp.concatenate([wout_rows[0][0], zero], axis=1),
            jnp.concatenate([zero, wout_rows[0][1]], axis=1),
            jnp.concatenate([wout_rows[1][0], zero], axis=1),
            jnp.concatenate([zero, wout_rows[1][1]], axis=1)], axis=0).astype(BF16)
    a_ref[0] = jnp.concatenate([jnp.concatenate([a[0] for a in a_rows], axis=1),
                                jnp.concatenate([a[1] for a in a_rows], axis=1)], axis=0)


def _ssm_prep(lam_re, lam_im, log_dt, b_re, b_im, c_re, c_im):
    G, P, H = b_re.shape
    ldt = jnp.broadcast_to(log_dt[:, None], (G, P))
    cols = jnp.stack([lam_re, lam_im, ldt])[..., None]
    rows = jnp.stack([lam_re, lam_im, ldt]).reshape(3, G // 2, 1, 2 * P)
    b = jnp.stack([b_re, b_im])
    ct = jnp.tile(jnp.stack([c_re, c_im]).transpose(0, 1, 3, 2), (1, 1, 1, CHUNK))
    bt = jnp.tile(b.transpose(0, 1, 3, 2), (1, 1, 1, 2))
    gt = GROUPS_PER_TILE
    return pl.pallas_call(
        _ssm_prep_kernel,
        grid=(LANE_TILES,),
        in_specs=[
            pl.BlockSpec((3, gt, P, 1), lambda q: (0, q, 0, 0)),
            pl.BlockSpec((3, PAIRS_PER_TILE, 1, 2 * P), lambda q: (0, q, 0, 0)),
            pl.BlockSpec((2, gt, P, H), lambda q: (0, q, 0, 0)),
            pl.BlockSpec((2, gt, P, CHUNK_COLS), lambda q: (0, q, 0, 0)),
            pl.BlockSpec((2, gt, H, 2 * P), lambda q: (0, q, 0, 0)),
        ],
        out_specs=[
            pl.BlockSpec((1, gt, CHUNK_COLS, CHUNK_COLS), lambda q: (q, 0, 0, 0)),
            pl.BlockSpec((1, PAIRS_PER_TILE, 2 * CHUNK_COLS, 2 * LANES), lambda q: (q, 0, 0, 0)),
            pl.BlockSpec((1, PAIRS_PER_TILE, 2 * LANES, 2 * CHUNK_COLS), lambda q: (q, 0, 0, 0)),
            pl.BlockSpec((1, 2, STATE_COLS), lambda q: (q, 0, 0)),
        ],
        out_shape=[
            jax.ShapeDtypeStruct((LANE_TILES, gt, CHUNK_COLS, CHUNK_COLS), BF16),
            jax.ShapeDtypeStruct((LANE_TILES, PAIRS_PER_TILE, 2 * CHUNK_COLS, 2 * LANES), BF16),
            jax.ShapeDtypeStruct((LANE_TILES, PAIRS_PER_TILE, 2 * LANES, 2 * CHUNK_COLS), BF16),
            jax.ShapeDtypeStruct((LANE_TILES, 2, STATE_COLS), F32),
        ],
        compiler_params=_cparams(("parallel",)),
        name="ssm_prep",
    )(cols, rows, b, ct, bt)


def _layer_norm(r, g, b):
    mu = jnp.mean(r, axis=-1, keepdims=True)
    c = r - mu
    var = jnp.mean(c * c, axis=-1, keepdims=True)
    return c * lax.rsqrt(var + LN_EPS) * g + b


MIX_ROWS = 256


def _gelu_tanh(y):
    return 0.5 * y * (1.0 + jnp.tanh(math.sqrt(2.0 / math.pi) * (y + 0.044715 * (y * y * y))))


def _mix_ln1_kernel(x_ref, attn_ref, y_ref, wglu_ref, bglu_ref, wo_ref, g_ref, b_ref, o_ref):
    nb, tl, d = x_ref.shape
    per = MIX_ROWS // tl
    n_pieces = nb // per

    def glu(n):
        y = jnp.concatenate(
            [jnp.concatenate([y_ref[q, pl.ds(b, tl, stride=nb), :] for q in range(LANE_TILES)], axis=1)
             for b in range(n * per, (n + 1) * per)], axis=0)
        gl = _gelu_tanh(y)
        z = gl * jax.nn.sigmoid(jnp.dot(gl.astype(BF16), wglu_ref[...], preferred_element_type=F32) + bglu_ref[...])
        return z.astype(BF16)

    def project(n, z):
        bs = slice(n * per, (n + 1) * per)
        mix = (jnp.dot(attn_ref[bs].reshape(MIX_ROWS, ATT_WIDTH), wo_ref[:ATT_WIDTH], preferred_element_type=F32)
               + jnp.dot(z, wo_ref[ATT_WIDTH:], preferred_element_type=F32))
        res = _layer_norm(ALPHA * x_ref[bs].reshape(MIX_ROWS, d) + mix, g_ref[...], b_ref[...])
        o_ref[bs] = res.reshape(per, tl, d)

    z = glu(0)
    for n in range(n_pieces):
        z_next = glu(n + 1) if n + 1 < n_pieces else None
        project(n, z)
        z = z_next


def _mix_ln1(x, attn, y_tm, wglu_bf, b_glu, wo_bf, ln_g, ln_b, tl=128):
    B, L, D = x.shape
    assert MIX_ROWS % tl == 0 and B % (MIX_ROWS // tl) == 0
    resident = lambda shape: pl.BlockSpec(shape, lambda l: (0, 0), pipeline_mode=pl.Buffered(1))
    return pl.pallas_call(
        _mix_ln1_kernel,
        grid=(L // tl,),
        in_specs=[
            pl.BlockSpec((B, tl, D), lambda l: (0, l, 0)),
            pl.BlockSpec((B, tl, ATT_WIDTH), lambda l: (0, l, 0)),
            pl.BlockSpec((LANE_TILES, tl * B, LANES), lambda l: (0, l, 0)),
            resident((SSM_WIDTH, SSM_WIDTH)),
            resident((1, SSM_WIDTH)),
            resident((D, D)),
            resident((1, D)),
            resident((1, D)),
        ],
        out_specs=pl.BlockSpec((B, tl, D), lambda l: (0, l, 0)),
        out_shape=jax.ShapeDtypeStruct((B, L, D), F32),
        compiler_params=_cparams(("parallel",)),
        name="mix_ln1",
    )(x, attn, y_tm, wglu_bf, b_glu, wo_bf, ln_g, ln_b)


HALO = 16
FFN_SUB = 256
FFN_ROWS = 128


def _ffn_kernel(x_ref, xh_ref, p_ref, wple_ref, wgate_ref, wup_ref, cw_ref, cb_ref, wd_ref, g_ref, b_ref, o_ref,
                x32_sc, xh32_sc, p32_sc, xb_sc, acc_sc, hg_sc, hv_sc, act_sc, *, tl):
    nb = x_ref.shape[0]
    tm = tl * nb
    d_slabs = x_ref.shape[2] // LANES
    hp = (CONV_WIDTH - 1)
    for bb in range(nb):
        for s in range(d_slabs):
            lanes = slice(s * LANES, (s + 1) * LANES)
            x32_sc[s, pl.ds(bb, tl, stride=nb), :] = x_ref[bb, :, lanes]
            xh32_sc[s, pl.ds(bb, hp, stride=nb), :] = xh_ref[bb, SUBLANES - hp:, lanes]
        for s in range(p_ref.shape[2] // LANES):
            p32_sc[s, pl.ds(bb, tl, stride=nb), :] = p_ref[bb, :, s * LANES:(s + 1) * LANES]
    x_tm = jnp.concatenate([x32_sc[s] for s in range(d_slabs)], axis=1)
    p_tm = jnp.concatenate([p32_sc[s] for s in range(p32_sc.shape[0])], axis=1)
    keep = jnp.where(pl.program_id(0) == 0, 0.0, 1.0)
    xb_sc[0:HALO] = (jnp.concatenate([xh32_sc[s] for s in range(d_slabs)], axis=1) * keep).astype(BF16)
    xb_sc[HALO:] = x_tm.astype(BF16)
    gate = jax.nn.sigmoid(jnp.dot(xb_sc[HALO:], wgate_ref[...], preferred_element_type=F32))
    ple = jnp.dot(p_tm.astype(BF16), wple_ref[...], preferred_element_type=F32) * gate
    acc_sc[...] = ALPHA * x_tm + ple

    def conv(h_sc, slot, cols, r0, rows):
        out = cb_ref[:, cols] + cw_ref[CONV_WIDTH - 1:CONV_WIDTH, cols] * h_sc[slot, HALO + r0:HALO + r0 + rows]
        for j in range(CONV_WIDTH - 1):
            back = (CONV_WIDTH - 1 - j) * nb
            out = out + cw_ref[j:j + 1, cols] * h_sc[slot, HALO + r0 - back:HALO + r0 - back + rows]
        return out

    n_sub = D_FF // FFN_SUB
    g_cols = lambda n: slice(n * FFN_SUB, (n + 1) * FFN_SUB)
    v_cols = lambda n: slice(D_FF + n * FFN_SUB, D_FF + (n + 1) * FFN_SUB)

    def up(n):
        xb = xb_sc[...]
        hg_sc[n % 2] = jnp.dot(xb, wup_ref[:, g_cols(n)], preferred_element_type=F32)
        hv_sc[n % 2] = jnp.dot(xb, wup_ref[:, v_cols(n)], preferred_element_type=F32)

    def gate_block(n):
        for r0 in range(0, tm, FFN_ROWS):
            gc = conv(hg_sc, n % 2, g_cols(n), r0, FFN_ROWS)
            vc = conv(hv_sc, n % 2, v_cols(n), r0, FFN_ROWS)
            act_sc[r0:r0 + FFN_ROWS, g_cols(n)] = (gc * jax.nn.sigmoid(gc) * vc).astype(BF16)

    def down(lo, hi):
        cols = slice(lo * FFN_SUB, hi * FFN_SUB)
        return jnp.dot(act_sc[:, cols], wd_ref[cols, :], preferred_element_type=F32)

    half = (n_sub + 1) // 2
    up(0)
    for n in range(n_sub):
        if n + 1 < n_sub:
            up(n + 1)
        gate_block(n)
        if n + 1 == half:
            acc_sc[...] += down(0, half)

    res = _layer_norm(acc_sc[...] + down(half, n_sub), g_ref[...], b_ref[...])
    for s in range(d_slabs):
        x32_sc[s] = res[:, s * LANES:(s + 1) * LANES]
    for bb in range(nb):
        for s in range(d_slabs):
            o_ref[bb, :, s * LANES:(s + 1) * LANES] = x32_sc[s, pl.ds(bb, tl, stride=nb), :]


def _ffn_ln2(x1, p, wple_bf, wgate_bf, wup_bf, conv_w, conv_b, wdown_bf, ln_g, ln_b, tl=64):
    B, L, D = x1.shape
    tm = tl * B
    assert D_FF % FFN_SUB == 0 and HALO == (CONV_WIDTH - 1) * B and tm % FFN_ROWS == 0
    const = lambda i: (0, 0)
    resident = lambda shape: pl.BlockSpec(shape, const, pipeline_mode=pl.Buffered(1))
    return pl.pallas_call(
        functools.partial(_ffn_kernel, tl=tl),
        grid=(L // tl,),
        in_specs=[
            pl.BlockSpec((B, tl, D), lambda i: (0, i, 0)),
            pl.BlockSpec((B, SUBLANES, D), lambda i: (0, jnp.maximum(i * (tl // SUBLANES) - 1, 0), 0)),
            pl.BlockSpec((B, tl, PLE_DIM), lambda i: (0, i, 0)),
            resident((PLE_DIM, D)),
            resident((D, D)),
            resident((D, 2 * D_FF)),
            resident((CONV_WIDTH, 2 * D_FF)),
            resident((1, 2 * D_FF)),
            resident((D_FF, D)),
            resident((1, D)),
            resident((1, D)),
        ],
        out_specs=pl.BlockSpec((B, tl, D), lambda i: (0, i, 0)),
        out_shape=jax.ShapeDtypeStruct((B, L, D), F32),
        scratch_shapes=[
            pltpu.VMEM((D // LANES, tm, LANES), F32),
            pltpu.VMEM((D // LANES, HALO, LANES), F32),
            pltpu.VMEM((PLE_DIM // LANES, tm, LANES), F32),
            pltpu.VMEM((tm + HALO, D), BF16),
            pltpu.VMEM((tm, D), F32),
            pltpu.VMEM((2, tm + HALO, FFN_SUB), F32),
            pltpu.VMEM((2, tm + HALO, FFN_SUB), F32),
            pltpu.VMEM((tm, D_FF), BF16),
        ],
        compiler_params=_cparams(("parallel",)),
        name="ffn_ln2",
    )(x1, x1, p, wple_bf, wgate_bf, wup_bf, conv_w, conv_b, wdown_bf, ln_g, ln_b)


def kernel(x, p, w_in, diff_lambda_q1, diff_lambda_k1, diff_lambda_q2, diff_lambda_k2, diff_subln_g, ssm_lambda_re, ssm_lambda_im, ssm_log_dt, ssm_b_re, ssm_b_im, ssm_c_re, ssm_c_im, ssm_d, ssm_w_glu, ssm_b_glu, w_o, ln1_g, ln1_b, ffn_w_up, ffn_conv_w, ffn_conv_b, ffn_w_down, w_ple, w_ple_gate, ln2_g, ln2_b):
    B, L, D = x.shape
    assert B == SUBLANES and D == D_MODEL and L % 512 == 0
    for i in range(DEPTH):
        lam_init = 0.8 - 0.6 * math.exp(-0.3 * i)
        row = lambda a: a[i].reshape(1, -1)
        qt, k, vt, u_tm = _in_proj(x, w_in[i])
        attn = _diff_attn_flat(qt, k, vt, row(diff_lambda_q1), row(diff_lambda_k1), row(diff_lambda_q2),
                          row(diff_lambda_k2), diff_subln_g[i], lam_init)
        tables = _ssm_prep(ssm_lambda_re[i], ssm_lambda_im[i], ssm_log_dt[i], ssm_b_re[i], ssm_b_im[i],
                           ssm_c_re[i], ssm_c_im[i])
        y_tm = _ssm(u_tm, tables, ssm_d[i], B, L)
        x1 = _mix_ln1(x, attn, y_tm, ssm_w_glu[i].astype(BF16), row(ssm_b_glu), w_o[i].astype(BF16),
                      row(ln1_g), row(ln1_b))
        x = _ffn_ln2(x1, p[i], w_ple[i].astype(BF16), w_ple_gate[i].astype(BF16), ffn_w_up[i].astype(BF16),
                     ffn_conv_w[i], row(ffn_conv_b), ffn_w_down[i].astype(BF16), row(ln2_g), row(ln2_b))
    return x
```

```python
import functools
import math

import jax
import jax.numpy as jnp
from jax import lax
from jax.experimental import pallas as pl
from jax.experimental.pallas import tpu as pltpu

F32 = jnp.float32
BF16 = jnp.bfloat16

SUBLANES = 8
LANES = 128

DEPTH = 1
D_MODEL = 1024
PLE_DIM = 256
HEADS = 4
QK_DIM = 64
V_DIM = 2 * QK_DIM
QK_WIDTH = HEADS * 2 * QK_DIM
ATT_WIDTH = HEADS * V_DIM
SSM_WIDTH = D_MODEL - ATT_WIDTH
SSM_GROUP = 16
SSM_GROUPS = SSM_WIDTH // SSM_GROUP
SSM_STATE = 64
D_FF = 2816
CONV_WIDTH = 3
LN_EPS = 1e-5
ALPHA = (2 * DEPTH) ** 0.25
QK_SCALE = QK_DIM ** -0.5 * math.log2(math.e)

CHUNK = 16
CHUNK_COLS = CHUNK * SSM_GROUP
GROUPS_PER_TILE = LANES // SSM_GROUP
LANE_TILES = SSM_WIDTH // LANES
PAIRS_PER_TILE = GROUPS_PER_TILE // 2
STATE_COLS = GROUPS_PER_TILE * SSM_STATE

VMEM_LIMIT = 56 * 1024 * 1024


def _cparams(sem):
    return pltpu.CompilerParams(dimension_semantics=sem, vmem_limit_bytes=VMEM_LIMIT)


TB = 256
VT_ROWS = V_DIM + 16


def _in_proj_kernel(x_ref, wqt_ref, wk_ref, wvt_ref, wu_ref, qt_ref, k_ref, vt_ref, u_ref):
    nb, tl, d = x_ref.shape
    xb = x_ref[...].reshape(nb * tl, d).astype(BF16)
    nt = (((1,), (1,)), ((), ()))
    qt = lax.dot_general(wqt_ref[...], xb, nt, preferred_element_type=F32) * QK_SCALE
    vt = lax.dot_general(wvt_ref[...], xb, nt, preferred_element_type=F32)
    pad_row = lax.broadcasted_iota(jnp.int32, (VT_ROWS - V_DIM, TB), 0)
    ones_row = jnp.where(pad_row == 0, 1.0, 0.0).astype(BF16)
    for b in range(nb):
        for h in range(HEADS):
            for c in range(tl // TB):
                cols = slice(b * tl + c * TB, b * tl + (c + 1) * TB)
                qt_ref[b, h, c] = qt[h * LANES:(h + 1) * LANES, cols].astype(BF16)
                vt_ref[b, h, c, :V_DIM] = vt[h * LANES:(h + 1) * LANES, cols].astype(BF16)
                vt_ref[b, h, c, V_DIM:] = ones_row
    k_ref[...] = jnp.dot(xb, wk_ref[...], preferred_element_type=F32).astype(BF16).reshape(k_ref.shape)
    hu = jnp.dot(xb, wu_ref[...], preferred_element_type=F32)
    for b in range(nb):
        for q in range(LANE_TILES):
            u_ref[q, pl.ds(b, tl, stride=nb), :] = hu[b * tl:(b + 1) * tl, q * LANES:(q + 1) * LANES]


def _in_proj(x, w_in, tl=256):
    B, L, D = x.shape
    w = w_in.astype(BF16)
    wqt = w[:, :QK_WIDTH].T
    wk = w[:, QK_WIDTH:2 * QK_WIDTH]
    wvt = w[:, 2 * QK_WIDTH:2 * QK_WIDTH + ATT_WIDTH].T
    wu = w[:, 2 * QK_WIDTH + ATT_WIDTH:]
    resident = lambda shape: pl.BlockSpec(shape, lambda l: (0, 0), pipeline_mode=pl.Buffered(1))
    slab = lambda rows: pl.BlockSpec((B, HEADS, tl // TB, rows, TB), lambda l: (0, 0, l, 0, 0))
    slab_shape = lambda rows: jax.ShapeDtypeStruct((B, HEADS, L // TB, rows, TB), BF16)
    return pl.pallas_call(
        _in_proj_kernel,
        grid=(L // tl,),
        in_specs=[
            pl.BlockSpec((B, tl, D), lambda l: (0, l, 0)),
            resident((QK_WIDTH, D)),
            resident((D, QK_WIDTH)),
            resident((ATT_WIDTH, D)),
            resident((D, SSM_WIDTH)),
        ],
        out_specs=[
            slab(LANES),
            pl.BlockSpec((B, tl, QK_WIDTH), lambda l: (0, l, 0)),
            slab(VT_ROWS),
            pl.BlockSpec((LANE_TILES, tl * B, LANES), lambda l: (0, l, 0)),
        ],
        out_shape=[
            slab_shape(LANES),
            jax.ShapeDtypeStruct((B, L, QK_WIDTH), BF16),
            slab_shape(VT_ROWS),
            jax.ShapeDtypeStruct((LANE_TILES, L * B, LANES), F32),
        ],
        compiler_params=_cparams(("parallel",)),
        name="in_proj",
    )(x, wqt, wk, wvt, wu)


def _attn_flat_kernel(lq1_ref, lk1_ref, lq2_ref, lk2_ref, g_ref, qt_ref, k_ref, vt_ref, o_ref,
                      acc_sc, qst_sc, s_sc, p_sc, bm_sc, st_sc, al_sc, *, lam_init):
    n_q = qt_ref.shape[2]
    n_pairs = n_q * (n_q + 1) // 2
    assert TB == 2 * LANES and n_pairs % 2 == 0
    heads = range(HEADS)
    n_strips = 2 * TB // LANES
    lam = (jnp.exp(jnp.sum(lq1_ref[...] * lk1_ref[...], axis=-1, keepdims=True))
           - jnp.exp(jnp.sum(lq2_ref[...] * lk2_ref[...], axis=-1, keepdims=True)) + lam_init)

    def prepare(i, c):
        for hh in heads:
            qt = qt_ref[0, hh, i]
            row = lax.broadcasted_iota(jnp.int32, qt.shape, 0)
            zero = jnp.zeros_like(qt)
            qst_sc[i, hh, :, :TB] = jnp.where(row < QK_DIM, qt, zero)
            qst_sc[i, hh, :, TB:] = jnp.where(row >= QK_DIM, qt, zero)
            acc_sc[i, hh] = jnp.zeros(acc_sc.shape[2:], F32)
            st_sc[i, hh] = jnp.full((1, 2 * TB), -jnp.inf, F32)
        return c

    lax.fori_loop(0, n_q, prepare, 0)
    p_sc[1] = jnp.zeros(p_sc.shape[1:], BF16)
    al_sc[1] = jnp.ones(al_sc.shape[1:], F32)

    def diag_place(i):
        r = i % 4
        return jnp.where(jnp.logical_or(r == 1, r == 2), 1, 0)

    def is_diag(pair):
        return pair[1] == diag_place(pair[0])

    def block_of(pair):
        i, k = pair
        d = diag_place(i)
        return jnp.where(k == d, i, jnp.where(k < d, k, k - 1))

    def scores(pair, slot):
        i, j = pair[0], block_of(pair)
        rows = pl.ds(pl.multiple_of(j * TB, TB), TB)
        for hh in heads:
            kb = k_ref[0, rows, hh * LANES:(hh + 1) * LANES]
            s = jnp.dot(kb, qst_sc[i, hh], preferred_element_type=F32)
            s_sc[slot, hh] = s
            bm_sc[slot, hh] = jnp.max(s, axis=0, keepdims=True)

    def values(pair, slot):
        i, j = pair[0], block_of(pair)
        for hh in heads:
            pv = jnp.dot(vt_ref[0, hh, j], p_sc[slot, hh], preferred_element_type=F32)
            acc_sc[i, hh] = al_sc[slot, hh] * acc_sc[i, hh] + pv

    def softmax(pair, slot, diagonal):
        i, _ = pair
        for hh in heads:
            for c in range(n_strips):
                lanes = slice(c * LANES, (c + 1) * LANES)
                m = st_sc[i, hh, :, lanes]
                if not diagonal:
                    mc = jnp.maximum(m, bm_sc[slot, hh, :, lanes])
                    p = jnp.exp2(s_sc[slot, hh, :, lanes] - mc)
                else:
                    s = s_sc[slot, hh, :, lanes]
                    key = lax.broadcasted_iota(jnp.int32, s.shape, 0)
                    qry = lax.broadcasted_iota(jnp.int32, s.shape, 1) + (c * LANES) % TB
                    s = jnp.where(key <= qry, s, -jnp.inf)
                    mc = jnp.maximum(m, jnp.max(s, axis=0, keepdims=True))
                    p = jnp.exp2(s - mc)
                p_sc[slot, hh, :, lanes] = p.astype(BF16)
                st_sc[i, hh, :, lanes] = mc
                al_sc[slot, hh, :, lanes] = jnp.exp2(m - mc)

    def succ(pair):
        i, k = pair
        last = k == i
        i2 = jnp.minimum(jnp.where(last, i + 1, i), n_q - 1)
        return i2, jnp.where(last, 0, k + 1)

    def two_steps(t, carry):
        prev, cur = carry
        nxt = succ(cur)
        nxt2 = succ(nxt)

        def body(diag0):
            scores(nxt, 1)
            values(prev, 1)
            softmax(cur, 0, diag0)
            scores(nxt2, 0)
            values(cur, 0)
            softmax(nxt, 1, False)

        d0 = is_diag(cur)
        pl.when(d0)(lambda: body(True))
        pl.when(jnp.logical_not(d0))(lambda: body(False))
        return nxt, nxt2

    zero = jnp.int32(0)
    first = (zero, zero)
    scores(first, 0)
    last, _ = lax.fori_loop(0, n_pairs // 2, two_steps, (first, first))
    values(last, 1)

    def finish(i, c):
        for hh in heads:
            o = acc_sc[i, hh, :V_DIM] * (1.0 / acc_sc[i, hh, V_DIM:V_DIM + 1])
            o = o[:, :TB] - lam * o[:, TB:]
            ms = jnp.mean(o * o, axis=0, keepdims=True)
            o = o * lax.rsqrt(ms + LN_EPS) * (g_ref[...] * (1.0 - lam_init))
            o_ref[0, pl.ds(pl.multiple_of(i * TB, TB), TB), hh * LANES:(hh + 1) * LANES] = o.T.astype(o_ref.dtype)
        return c

    lax.fori_loop(0, n_q, finish, 0)


def _diff_attn_flat(qt, k, vt, lq1, lk1, lq2, lk2, subln_g, lam_init):
    B, L, _ = k.shape
    n_q = L // TB
    vec = pl.BlockSpec((1, QK_DIM), lambda b: (0, 0))
    return pl.pallas_call(
        functools.partial(_attn_flat_kernel, lam_init=lam_init),
        grid=(B,),
        in_specs=[
            vec, vec, vec, vec,
            pl.BlockSpec((V_DIM, 1), lambda b: (0, 0)),
            pl.BlockSpec((1, HEADS, n_q, LANES, TB), lambda b: (b, 0, 0, 0, 0)),
            pl.BlockSpec((1, L, HEADS * LANES), lambda b: (b, 0, 0)),
            pl.BlockSpec((1, HEADS, n_q, VT_ROWS, TB), lambda b: (b, 0, 0, 0, 0)),
        ],
        out_specs=pl.BlockSpec((1, L, HEADS * LANES), lambda b: (b, 0, 0)),
        out_shape=jax.ShapeDtypeStruct((B, L, ATT_WIDTH), BF16),
        scratch_shapes=[
            pltpu.VMEM((n_q, HEADS, VT_ROWS, 2 * TB), F32),
            pltpu.VMEM((n_q, HEADS, LANES, 2 * TB), BF16),
            pltpu.VMEM((2, HEADS, TB, 2 * TB), F32),
            pltpu.VMEM((2, HEADS, TB, 2 * TB), BF16),
            pltpu.VMEM((2, HEADS, 1, 2 * TB), F32),
            pltpu.VMEM((n_q, HEADS, 1, 2 * TB), F32),
            pltpu.VMEM((2, HEADS, 1, 2 * TB), F32),
        ],
        compiler_params=_cparams(("parallel",)),
        name="diff_attn",
    )(lq1, lk1, lq2, lk2, subln_g.reshape(V_DIM, 1), qt, k, vt)


def _block_transpose8(vs):
    lane_blk = lax.broadcasted_iota(jnp.int32, vs[0].shape, 1) // SSM_GROUP
    for d in (4, 2, 1):
        keep = (lane_blk & d) == 0
        new = list(vs)
        for a in range(8):
            if a & d == 0:
                lo, hi = vs[a], vs[a + d]
                new[a] = jnp.where(keep, lo, pltpu.roll(hi, SSM_GROUP * d, 1))
                new[a + d] = jnp.where(keep, pltpu.roll(lo, LANES - SSM_GROUP * d, 1), hi)
        vs = new
    return vs


def _ssm_kernel(u_ref, toep_ref, win_ref, wout_ref, a_ref, d_ref, y_ref, s_sc, x_sc, sp_sc, *, jt):
    rows = jt * SUBLANES

    @pl.when(pl.program_id(1) == 0)
    def _():
        s_sc[...] = jnp.zeros(s_sc.shape, F32)

    xs = [u_ref[0, :, r].reshape(rows, LANES) for r in range(CHUNK)]
    z_lo = _block_transpose8(xs[:8])
    z_hi = _block_transpose8(xs[8:])
    ug = [jnp.concatenate([z_lo[g], z_hi[g]], axis=1).astype(BF16) for g in range(GROUPS_PER_TILE)]

    y = [jnp.dot(ug[g], toep_ref[0, g], preferred_element_type=F32) for g in range(GROUPS_PER_TILE)]

    for m in range(PAIRS_PER_TILE):
        up = jnp.concatenate([ug[2 * m], ug[2 * m + 1]], axis=1)
        xp = jnp.dot(up, win_ref[0, m], preferred_element_type=F32)
        x_sc[0, :, m * LANES:(m + 1) * LANES] = xp[:, :LANES]
        x_sc[1, :, m * LANES:(m + 1) * LANES] = xp[:, LANES:]

    ar = a_ref[0, 0:1, :]
    ai = a_ref[0, 1:2, :]

    def step(j, carry):
        sr, si = carry
        r8 = pl.ds(pl.multiple_of(j * SUBLANES, SUBLANES), SUBLANES)
        sp_sc[0, r8, :] = sr
        sp_sc[1, r8, :] = si
        return (ar * sr - ai * si + x_sc[0, r8, :], ar * si + ai * sr + x_sc[1, r8, :])

    sr, si = lax.fori_loop(0, jt, step, (s_sc[0], s_sc[1]))
    s_sc[0] = sr
    s_sc[1] = si

    for m in range(PAIRS_PER_TILE):
        sp = jnp.concatenate([sp_sc[0, :, m * LANES:(m + 1) * LANES],
                              sp_sc[1, :, m * LANES:(m + 1) * LANES]], axis=1).astype(BF16)
        yc = jnp.dot(sp, wout_ref[0, m], preferred_element_type=F32)
        y[2 * m] = y[2 * m] + yc[:, :CHUNK_COLS]
        y[2 * m + 1] = y[2 * m + 1] + yc[:, CHUNK_COLS:]

    d = d_ref[0]
    for half in range(2):
        ws = _block_transpose8([y[g][:, half * LANES:(half + 1) * LANES] for g in range(GROUPS_PER_TILE)])
        for t8 in range(8):
            t = half * 8 + t8
            u_t = u_ref[0, :, t].reshape(rows, LANES)
            y_ref[0, :, t] = (ws[t8] + d * u_t).reshape(jt, SUBLANES, LANES)


def _ssm(u_tm, tables, d_skip, B, L, lt=1024):
    toep, win, wout, a16 = tables
    jt = lt // CHUNK
    n_l = L // lt
    u5 = u_tm.reshape(LANE_TILES, L // CHUNK, CHUNK, B, LANES)
    rows = jt * B
    y5 = pl.pallas_call(
        functools.partial(_ssm_kernel, jt=jt),
        grid=(LANE_TILES, n_l),
        in_specs=[
            pl.BlockSpec((1, jt, CHUNK, B, LANES), lambda q, l: (q, l, 0, 0, 0)),
            pl.BlockSpec((1, GROUPS_PER_TILE, CHUNK_COLS, CHUNK_COLS), lambda q, l: (q, 0, 0, 0)),
            pl.BlockSpec((1, PAIRS_PER_TILE, 2 * CHUNK_COLS, 2 * LANES), lambda q, l: (q, 0, 0, 0)),
            pl.BlockSpec((1, PAIRS_PER_TILE, 2 * LANES, 2 * CHUNK_COLS), lambda q, l: (q, 0, 0, 0)),
            pl.BlockSpec((1, 2, STATE_COLS), lambda q, l: (q, 0, 0)),
            pl.BlockSpec((1, 1, LANES), lambda q, l: (q, 0, 0)),
        ],
        out_specs=pl.BlockSpec((1, jt, CHUNK, B, LANES), lambda q, l: (q, l, 0, 0, 0)),
        out_shape=jax.ShapeDtypeStruct(u5.shape, F32),
        scratch_shapes=[
            pltpu.VMEM((2, B, STATE_COLS), F32),
            pltpu.VMEM((2, rows, STATE_COLS), F32),
            pltpu.VMEM((2, rows, STATE_COLS), F32),
        ],
        compiler_params=_cparams(("parallel", "arbitrary")),
        name="ssm",
    )(u5, toep, win, wout, a16, d_skip.reshape(LANE_TILES, 1, LANES))
    return y5.reshape(LANE_TILES, L * B, LANES)


def _cmul(a, b):
    return a[0] * b[0] - a[1] * b[1], a[0] * b[1] + a[1] * b[0]


def _zoh(lr, li, ldt):
    dt = jnp.exp(ldt)
    mag = jnp.exp(lr * dt)
    ab = (mag * jnp.cos(li * dt), mag * jnp.sin(li * dt))
    den = lr * lr + li * li
    zr, zi = ab[0] - 1.0, ab[1]
    return ab, ((zr * lr + zi * li) / den, (zi * lr - zr * li) / den)


def _cpow(base, exponent, bits):
    out = None
    for k in range(bits):
        on = (exponent & (1 << k)) != 0
        sel = (jnp.where(on, base[0], 1.0), jnp.where(on, base[1], 0.0))
        out = sel if out is None else _cmul(out, sel)
        if k + 1 < bits:
            base = _cmul(base, base)
    return out


def _ssm_prep_kernel(row_ref, ct_ref, bt_ref, toep_ref, win_ref, wout_ref, a_ref):
    bits = CHUNK.bit_length() - 1
    lane = lax.broadcasted_iota(jnp.int32, (1, CHUNK_COLS), 1)
    tau = lane // SSM_GROUP
    rowi = lax.broadcasted_iota(jnp.int32, (CHUNK_COLS, 1), 0)
    back = (CHUNK - 1) - rowi // SSM_GROUP
    lane2 = lax.broadcasted_iota(jnp.int32, (1, LANES), 1)
    exact = functools.partial(jnp.dot, precision=lax.Precision.HIGHEST, preferred_element_type=F32)
    zoh = [_zoh(row_ref[0, m], row_ref[1, m], row_ref[2, m]) for m in range(PAIRS_PER_TILE)]
    pad = jnp.zeros((SUBLANES - PAIRS_PER_TILE, LANES), F32)
    ab_cols = tuple(jnp.concatenate([z[0][c] for z in zoh] + [pad], axis=0).T for c in range(2))
    a_rows = []
    for m in range(PAIRS_PER_TILE):
        abp, fp = zoh[m]
        pwr = _cpow(abp, back, bits)
        a16 = abp
        for _ in range(bits):
            a16 = _cmul(a16, a16)
        a_rows.append(a16)
        wout_rows = [[None, None], [None, None]]
        for j in range(2):
            g = 2 * m + j
            states = slice(j * SSM_STATE, (j + 1) * SSM_STATE)
            ab = tuple(v[states, m:m + 1] for v in ab_cols)
            pw = _cpow(ab, tau, bits)
            g0 = _cmul(pw, (ct_ref[0, g], ct_ref[1, g]))
            g1 = _cmul(g0, ab)
            mine = (lane2 < SSM_STATE) if j == 0 else (lane2 >= SSM_STATE)
            bb2 = _cmul(fp, (bt_ref[0, g], bt_ref[1, g]))
            bb2 = tuple(jnp.where(mine, v, 0.0) for v in bb2)
            none = jnp.zeros_like(g0[0])
            stack = (lambda v: jnp.concatenate([v, none], axis=0)) if j == 0 else \
                    (lambda v: jnp.concatenate([none, v], axis=0))
            kt = exact(bb2[0], stack(g0[0])) - exact(bb2[1], stack(g0[1]))
            toep_ref[0, g] = jnp.concatenate(
                [jnp.where(lane >= SSM_GROUP * r, pltpu.roll(kt, SSM_GROUP * r, 1), 0.0) if r else kt
                 for r in range(CHUNK)], axis=0).astype(BF16)
            w = _cmul(pwr, tuple(jnp.concatenate([v] * CHUNK, axis=0) for v in bb2))
            win_ref[0, m, j * CHUNK_COLS:(j + 1) * CHUNK_COLS, :] = jnp.concatenate(w, axis=1).astype(BF16)
            wout_rows[0][j] = g1[0]
            wout_rows[1][j] = -g1[1]
        zero = jnp.zeros((SSM_STATE, CHUNK_COLS), F32)
        wout_ref[0, m] = jnp.concatenate([
            jnp.concatenate([wout_rows[0][0], zero], axis=1),
            jnp.concatenate([zero, wout_rows[0][1]], axis=1),
            jnp.concatenate([wout_rows[1][0], zero], axis=1),
            jnp.concatenate([zero, wout_rows[1][1]], axis=1)], axis=0).astype(BF16)
    a_ref[0] = jnp.concatenate([jnp.concatenate([a[0] for a in a_rows], axis=1),
                                jnp.concatenate([a[1] for a in a_rows], axis=1)], axis=0)


def _ssm_prep(lam_re, lam_im, log_dt, b_re, b_im, c_re, c_im):
    G, P, H = b_re.shape
    ldt = jnp.broadcast_to(log_dt[:, None], (G, P))
    rows = jnp.stack([lam_re, lam_im, ldt]).reshape(3, G // 2, 1, 2 * P)
    ct = jnp.tile(jnp.stack([c_re, c_im]).transpose(0, 1, 3, 2), (1, 1, 1, CHUNK))
    bt = jnp.tile(jnp.stack([b_re, b_im]).transpose(0, 1, 3, 2), (1, 1, 1, 2))
    gt = GROUPS_PER_TILE
    return pl.pallas_call(
        _ssm_prep_kernel,
        grid=(LANE_TILES,),
        in_specs=[
            pl.BlockSpec((3, PAIRS_PER_TILE, 1, 2 * P), lambda q: (0, q, 0, 0)),
            pl.BlockSpec((2, gt, P, CHUNK_COLS), lambda q: (0, q, 0, 0)),
            pl.BlockSpec((2, gt, H, 2 * P), lambda q: (0, q, 0, 0)),
        ],
        out_specs=[
            pl.BlockSpec((1, gt, CHUNK_COLS, CHUNK_COLS), lambda q: (q, 0, 0, 0)),
            pl.BlockSpec((1, PAIRS_PER_TILE, 2 * CHUNK_COLS, 2 * LANES), lambda q: (q, 0, 0, 0)),
            pl.BlockSpec((1, PAIRS_PER_TILE, 2 * LANES, 2 * CHUNK_COLS), lambda q: (q, 0, 0, 0)),
            pl.BlockSpec((1, 2, STATE_COLS), lambda q: (q, 0, 0)),
        ],
        out_shape=[
            jax.ShapeDtypeStruct((LANE_TILES, gt, CHUNK_COLS, CHUNK_COLS), BF16),
            jax.ShapeDtypeStruct((LANE_TILES, PAIRS_PER_TILE, 2 * CHUNK_COLS, 2 * LANES), BF16),
            jax.ShapeDtypeStruct((LANE_TILES, PAIRS_PER_TILE, 2 * LANES, 2 * CHUNK_COLS), BF16),
            jax.ShapeDtypeStruct((LANE_TILES, 2, STATE_COLS), F32),
        ],
        compiler_params=_cparams(("parallel",)),
        name="ssm_prep",
    )(rows, ct, bt)


def _layer_norm(r, g, b):
    mu = jnp.mean(r, axis=-1, keepdims=True)
    c = r - mu
    var = jnp.mean(c * c, axis=-1, keepdims=True)
    return c * lax.rsqrt(var + LN_EPS) * g + b


MIX_ROWS = 256


def _gelu_tanh(y):
    return 0.5 * y * (1.0 + jnp.tanh(math.sqrt(2.0 / math.pi) * (y + 0.044715 * (y * y * y))))


def _mix_ln1_kernel(x_ref, attn_ref, y_ref, wglu_ref, bglu_ref, wo_ref, g_ref, b_ref, o_ref):
    nb, tl, d = x_ref.shape
    per = MIX_ROWS // tl
    n_pieces = nb // per

    def glu(n):
        y = jnp.concatenate(
            [jnp.concatenate([y_ref[q, pl.ds(b, tl, stride=nb), :] for q in range(LANE_TILES)], axis=1)
             for b in range(n * per, (n + 1) * per)], axis=0)
        gl = _gelu_tanh(y)
        z = gl * jax.nn.sigmoid(jnp.dot(gl.astype(BF16), wglu_ref[...], preferred_element_type=F32) + bglu_ref[...])
        return z.astype(BF16)

    def project(n, z):
        bs = slice(n * per, (n + 1) * per)
        mix = (jnp.dot(attn_ref[bs].reshape(MIX_ROWS, ATT_WIDTH), wo_ref[:ATT_WIDTH], preferred_element_type=F32)
               + jnp.dot(z, wo_ref[ATT_WIDTH:], preferred_element_type=F32))
        res = _layer_norm(ALPHA * x_ref[bs].reshape(MIX_ROWS, d) + mix, g_ref[...], b_ref[...])
        o_ref[bs] = res.reshape(per, tl, d)

    z = glu(0)
    for n in range(n_pieces):
        z_next = glu(n + 1) if n + 1 < n_pieces else None
        project(n, z)
        z = z_next


def _mix_ln1(x, attn, y_tm, wglu_bf, b_glu, wo_bf, ln_g, ln_b, tl=128):
    B, L, D = x.shape
    assert MIX_ROWS % tl == 0 and B % (MIX_ROWS // tl) == 0
    resident = lambda shape: pl.BlockSpec(shape, lambda l: (0, 0), pipeline_mode=pl.Buffered(1))
    return pl.pallas_call(
        _mix_ln1_kernel,
        grid=(L // tl,),
        in_specs=[
            pl.BlockSpec((B, tl, D), lambda l: (0, l, 0)),
            pl.BlockSpec((B, tl, ATT_WIDTH), lambda l: (0, l, 0)),
            pl.BlockSpec((LANE_TILES, tl * B, LANES), lambda l: (0, l, 0)),
            resident((SSM_WIDTH, SSM_WIDTH)),
            resident((1, SSM_WIDTH)),
            resident((D, D)),
            resident((1, D)),
            resident((1, D)),
        ],
        out_specs=pl.BlockSpec((B, tl, D), lambda l: (0, l, 0)),
        out_shape=jax.ShapeDtypeStruct((B, L, D), F32),
        compiler_params=_cparams(("parallel",)),
        name="mix_ln1",
    )(x, attn, y_tm, wglu_bf, b_glu, wo_bf, ln_g, ln_b)


HALO = 16
FFN_SUB = 256
FFN_ROWS = 128


def _ffn_kernel(x_ref, xh_ref, p_ref, wple_ref, wgate_ref, wup_ref, cw_ref, cb_ref, wd_ref, g_ref, b_ref, o_ref,
                x32_sc, xh32_sc, p32_sc, xb_sc, acc_sc, hg_sc, hv_sc, act_sc, *, tl):
    nb = x_ref.shape[0]
    tm = tl * nb
    d_slabs = x_ref.shape[2] // LANES
    hp = (CONV_WIDTH - 1)
    for bb in range(nb):
        for s in range(d_slabs):
            lanes = slice(s * LANES, (s + 1) * LANES)
            x32_sc[s, pl.ds(bb, tl, stride=nb), :] = x_ref[bb, :, lanes]
            xh32_sc[s, pl.ds(bb, hp, stride=nb), :] = xh_ref[bb, SUBLANES - hp:, lanes]
        for s in range(p_ref.shape[2] // LANES):
            p32_sc[s, pl.ds(bb, tl, stride=nb), :] = p_ref[bb, :, s * LANES:(s + 1) * LANES]
    x_tm = jnp.concatenate([x32_sc[s] for s in range(d_slabs)], axis=1)
    p_tm = jnp.concatenate([p32_sc[s] for s in range(p32_sc.shape[0])], axis=1)
    keep = jnp.where(pl.program_id(0) == 0, 0.0, 1.0)
    xb_sc[0:HALO] = (jnp.concatenate([xh32_sc[s] for s in range(d_slabs)], axis=1) * keep).astype(BF16)
    xb_sc[HALO:] = x_tm.astype(BF16)
    gate = jax.nn.sigmoid(jnp.dot(xb_sc[HALO:], wgate_ref[...], preferred_element_type=F32))
    ple = jnp.dot(p_tm.astype(BF16), wple_ref[...], preferred_element_type=F32) * gate
    acc_sc[...] = ALPHA * x_tm + ple

    def conv(h_sc, slot, cols, r0, rows):
        out = cb_ref[:, cols] + cw_ref[CONV_WIDTH - 1:CONV_WIDTH, cols] * h_sc[slot, HALO + r0:HALO + r0 + rows]
        for j in range(CONV_WIDTH - 1):
            back = (CONV_WIDTH - 1 - j) * nb
            out = out + cw_ref[j:j + 1, cols] * h_sc[slot, HALO + r0 - back:HALO + r0 - back + rows]
        return out

    n_sub = D_FF // FFN_SUB
    g_cols = lambda n: slice(n * FFN_SUB, (n + 1) * FFN_SUB)
    v_cols = lambda n: slice(D_FF + n * FFN_SUB, D_FF + (n + 1) * FFN_SUB)

    def up(n):
        xb = xb_sc[...]
        hg_sc[n % 2] = jnp.dot(xb, wup_ref[:, g_cols(n)], preferred_element_type=F32)
        hv_sc[n % 2] = jnp.dot(xb, wup_ref[:, v_cols(n)], preferred_element_type=F32)

    def gate_block(n):
        for r0 in range(0, tm, FFN_ROWS):
            gc = conv(hg_sc, n % 2, g_cols(n), r0, FFN_ROWS)
            vc = conv(hv_sc, n % 2, v_cols(n), r0, FFN_ROWS)
            act_sc[r0:r0 + FFN_ROWS, g_cols(n)] = (gc * jax.nn.sigmoid(gc) * vc).astype(BF16)

    def down(lo, hi):
        cols = slice(lo * FFN_SUB, hi * FFN_SUB)
        return jnp.dot(act_sc[:, cols], wd_ref[cols, :], preferred_element_type=F32)

    half = (n_sub + 1) // 2
    up(0)
    for n in range(n_sub):
        if n + 1 < n_sub:
            up(n + 1)
        gate_block(n)
        if n + 1 == half:
            acc_sc[...] += down(0, half)

    res = _layer_norm(acc_sc[...] + down(half, n_sub), g_ref[...], b_ref[...])
    for s in range(d_slabs):
        x32_sc[s] = res[:, s * LANES:(s + 1) * LANES]
    for bb in range(nb):
        for s in range(d_slabs):
            o_ref[bb, :, s * LANES:(s + 1) * LANES] = x32_sc[s, pl.ds(bb, tl, stride=nb), :]


def _ffn_ln2(x1, p, wple_bf, wgate_bf, wup_bf, conv_w, conv_b, wdown_bf, ln_g, ln_b, tl=64):
    B, L, D = x1.shape
    tm = tl * B
    assert D_FF % FFN_SUB == 0 and HALO == (CONV_WIDTH - 1) * B and tm % FFN_ROWS == 0
    const = lambda i: (0, 0)
    resident = lambda shape: pl.BlockSpec(shape, const, pipeline_mode=pl.Buffered(1))
    return pl.pallas_call(
        functools.partial(_ffn_kernel, tl=tl),
        grid=(L // tl,),
        in_specs=[
            pl.BlockSpec((B, tl, D), lambda i: (0, i, 0)),
            pl.BlockSpec((B, SUBLANES, D), lambda i: (0, jnp.maximum(i * (tl // SUBLANES) - 1, 0), 0)),
            pl.BlockSpec((B, tl, PLE_DIM), lambda i: (0, i, 0)),
            resident((PLE_DIM, D)),
            resident((D, D)),
            resident((D, 2 * D_FF)),
            resident((CONV_WIDTH, 2 * D_FF)),
            resident((1, 2 * D_FF)),
            resident((D_FF, D)),
            resident((1, D)),
            resident((1, D)),
        ],
        out_specs=pl.BlockSpec((B, tl, D), lambda i: (0, i, 0)),
        out_shape=jax.ShapeDtypeStruct((B, L, D), F32),
        scratch_shapes=[
            pltpu.VMEM((D // LANES, tm, LANES), F32),
            pltpu.VMEM((D // LANES, HALO, LANES), F32),
            pltpu.VMEM((PLE_DIM // LANES, tm, LANES), F32),
            pltpu.VMEM((tm + HALO, D), BF16),
            pltpu.VMEM((tm, D), F32),
            pltpu.VMEM((2, tm + HALO, FFN_SUB), F32),
            pltpu.VMEM((2, tm + HALO, FFN_SUB), F32),
            pltpu.VMEM((tm, D_FF), BF16),
        ],
        compiler_params=_cparams(("parallel",)),
        name="ffn_ln2",
    )(x1, x1, p, wple_bf, wgate_bf, wup_bf, conv_w, conv_b, wdown_bf, ln_g, ln_b)


def kernel(x, p, w_in, diff_lambda_q1, diff_lambda_k1, diff_lambda_q2, diff_lambda_k2, diff_subln_g, ssm_lambda_re, ssm_lambda_im, ssm_log_dt, ssm_b_re, ssm_b_im, ssm_c_re, ssm_c_im, ssm_d, ssm_w_glu, ssm_b_glu, w_o, ln1_g, ln1_b, ffn_w_up, ffn_conv_w, ffn_conv_b, ffn_w_down, w_ple, w_ple_gate, ln2_g, ln2_b):
    B, L, D = x.shape
    assert B == SUBLANES and D == D_MODEL and L % 512 == 0
    for i in range(DEPTH):
        lam_init = 0.8 - 0.6 * math.exp(-0.3 * i)
        row = lambda a: a[i].reshape(1, -1)
        qt, k, vt, u_tm = _in_proj(x, w_in[i])
        attn = _diff_attn_flat(qt, k, vt, row(diff_lambda_q1), row(diff_lambda_k1), row(diff_lambda_q2),
                          row(diff_lambda_k2), diff_subln_g[i], lam_init)
        tables = _ssm_prep(ssm_lambda_re[i], ssm_lambda_im[i], ssm_log_dt[i], ssm_b_re[i], ssm_b_im[i],
                           ssm_c_re[i], ssm_c_im[i])
        y_tm = _ssm(u_tm, tables, ssm_d[i], B, L)
        x1 = _mix_ln1(x, attn, y_tm, ssm_w_glu[i].astype(BF16), row(ssm_b_glu), w_o[i].astype(BF16),
                      row(ln1_g), row(ln1_b))
        x = _ffn_ln2(x1, p[i], w_ple[i].astype(BF16), w_ple_gate[i].astype(BF16), ffn_w_up[i].astype(BF16),
                     ffn_conv_w[i], row(ffn_conv_b), ffn_w_down[i].astype(BF16), row(ln2_g), row(ln2_b))
    return x
```

```python
import functools
import math

import jax
import jax.numpy as jnp
from jax import lax
from jax.experimental import pallas as pl
from jax.experimental.pallas import tpu as pltpu

F32 = jnp.float32
BF16 = jnp.bfloat16

SUBLANES = 8
LANES = 128

DEPTH = 1
D_MODEL = 1024
PLE_DIM = 256
HEADS = 4
QK_DIM = 64
V_DIM = 2 * QK_DIM
QK_WIDTH = HEADS * 2 * QK_DIM
ATT_WIDTH = HEADS * V_DIM
SSM_WIDTH = D_MODEL - ATT_WIDTH
SSM_GROUP = 16
SSM_GROUPS = SSM_WIDTH // SSM_GROUP
SSM_STATE = 64
D_FF = 2816
CONV_WIDTH = 3
LN_EPS = 1e-5
ALPHA = (2 * DEPTH) ** 0.25
QK_SCALE = QK_DIM ** -0.5 * math.log2(math.e)

CHUNK = 16
CHUNK_COLS = CHUNK * SSM_GROUP
GROUPS_PER_TILE = LANES // SSM_GROUP
LANE_TILES = SSM_WIDTH // LANES
PAIRS_PER_TILE = GROUPS_PER_TILE // 2
STATE_COLS = GROUPS_PER_TILE * SSM_STATE

VMEM_LIMIT = 56 * 1024 * 1024


def _cparams(sem):
    return pltpu.CompilerParams(dimension_semantics=sem, vmem_limit_bytes=VMEM_LIMIT)


TB = 256
VT_ROWS = V_DIM + 16


def _in_proj_kernel(x_ref, wqt_ref, wk_ref, wvt_ref, wu_ref, qt_ref, k_ref, vt_ref, u_ref, u_sc):
    nb, tl, d = x_ref.shape
    jt = tl // CHUNK
    xb = x_ref[...].reshape(nb * tl, d).astype(BF16)
    hu = jnp.dot(xb, wu_ref[...], preferred_element_type=F32)
    for b in range(nb):
        for q in range(LANE_TILES):
            u_sc[q, pl.ds(b, tl, stride=nb), :] = hu[b * tl:(b + 1) * tl, q * LANES:(q + 1) * LANES]
    for q in range(LANE_TILES):
        tiles = u_sc[q].reshape(jt, CHUNK, nb, LANES)
        slabs = _chunk_slabs([tiles[:, r].reshape(jt * nb, LANES) for r in range(CHUNK)])
        for g in range(GROUPS_PER_TILE):
            u_ref[q, g] = slabs[g]
    nt = (((1,), (1,)), ((), ()))
    qt = lax.dot_general(wqt_ref[...], xb, nt, preferred_element_type=F32) * QK_SCALE
    vt = lax.dot_general(wvt_ref[...], xb, nt, preferred_element_type=F32)
    pad_row = lax.broadcasted_iota(jnp.int32, (VT_ROWS - V_DIM, TB), 0)
    ones_row = jnp.where(pad_row == 0, 1.0, 0.0).astype(BF16)
    for b in range(nb):
        for h in range(HEADS):
            for c in range(tl // TB):
                cols = slice(b * tl + c * TB, b * tl + (c + 1) * TB)
                qt_ref[b, h, c] = qt[h * LANES:(h + 1) * LANES, cols].astype(BF16)
                vt_ref[b, h, c, :V_DIM] = vt[h * LANES:(h + 1) * LANES, cols].astype(BF16)
                vt_ref[b, h, c, V_DIM:] = ones_row
    k_ref[...] = jnp.dot(xb, wk_ref[...], preferred_element_type=F32).astype(BF16).reshape(k_ref.shape)


def _in_proj(x, w_in, tl=256):
    B, L, D = x.shape
    w = w_in.astype(BF16)
    wqt = w[:, :QK_WIDTH].T
    wk = w[:, QK_WIDTH:2 * QK_WIDTH]
    wvt = w[:, 2 * QK_WIDTH:2 * QK_WIDTH + ATT_WIDTH].T
    wu = w[:, 2 * QK_WIDTH + ATT_WIDTH:]
    resident = lambda shape: pl.BlockSpec(shape, lambda l: (0, 0), pipeline_mode=pl.Buffered(1))
    slab = lambda rows: pl.BlockSpec((B, HEADS, tl // TB, rows, TB), lambda l: (0, 0, l, 0, 0))
    slab_shape = lambda rows: jax.ShapeDtypeStruct((B, HEADS, L // TB, rows, TB), BF16)
    return pl.pallas_call(
        _in_proj_kernel,
        grid=(L // tl,),
        in_specs=[
            pl.BlockSpec((B, tl, D), lambda l: (0, l, 0)),
            resident((QK_WIDTH, D)),
            resident((D, QK_WIDTH)),
            resident((ATT_WIDTH, D)),
            resident((D, SSM_WIDTH)),
        ],
        out_specs=[
            slab(LANES),
            pl.BlockSpec((B, tl, QK_WIDTH), lambda l: (0, l, 0)),
            slab(VT_ROWS),
            pl.BlockSpec((LANE_TILES, GROUPS_PER_TILE, tl // CHUNK * B, CHUNK_COLS), lambda l: (0, 0, l, 0)),
        ],
        out_shape=[
            slab_shape(LANES),
            jax.ShapeDtypeStruct((B, L, QK_WIDTH), BF16),
            slab_shape(VT_ROWS),
            jax.ShapeDtypeStruct((LANE_TILES, GROUPS_PER_TILE, L // CHUNK * B, CHUNK_COLS), F32),
        ],
        scratch_shapes=[pltpu.VMEM((LANE_TILES, tl * B, LANES), F32)],
        compiler_params=_cparams(("parallel",)),
        name="in_proj",
    )(x, wqt, wk, wvt, wu)


def _attn_flat_kernel(lq1_ref, lk1_ref, lq2_ref, lk2_ref, g_ref, qt_ref, k_ref, vt_ref, o_ref,
                      acc_sc, qst_sc, s_sc, p_sc, bm_sc, st_sc, al_sc, *, lam_init):
    n_q = qt_ref.shape[2]
    n_pairs = n_q * (n_q + 1) // 2
    assert TB == 2 * LANES and n_pairs % 2 == 0
    heads = range(HEADS)
    n_strips = 2 * TB // LANES
    lam = (jnp.exp(jnp.sum(lq1_ref[...] * lk1_ref[...], axis=-1, keepdims=True))
           - jnp.exp(jnp.sum(lq2_ref[...] * lk2_ref[...], axis=-1, keepdims=True)) + lam_init)

    def prepare(i, c):
        for hh in heads:
            qt = qt_ref[0, hh, i]
            row = lax.broadcasted_iota(jnp.int32, qt.shape, 0)
            zero = jnp.zeros_like(qt)
            qst_sc[i, hh, :, :TB] = jnp.where(row < QK_DIM, qt, zero)
            qst_sc[i, hh, :, TB:] = jnp.where(row >= QK_DIM, qt, zero)
            acc_sc[i, hh] = jnp.zeros(acc_sc.shape[2:], F32)
            st_sc[i, hh] = jnp.full((1, 2 * TB), -jnp.inf, F32)
        return c

    lax.fori_loop(0, n_q, prepare, 0)
    p_sc[1] = jnp.zeros(p_sc.shape[1:], BF16)
    al_sc[1] = jnp.ones(al_sc.shape[1:], F32)

    def diag_place(i):
        r = i % 4
        return jnp.where(jnp.logical_or(r == 1, r == 2), 1, 0)

    def is_diag(pair):
        return pair[1] == diag_place(pair[0])

    def block_of(pair):
        i, k = pair
        d = diag_place(i)
        return jnp.where(k == d, i, jnp.where(k < d, k, k - 1))

    def scores(pair, slot):
        i, j = pair[0], block_of(pair)
        rows = pl.ds(pl.multiple_of(j * TB, TB), TB)
        for hh in heads:
            kb = k_ref[0, rows, hh * LANES:(hh + 1) * LANES]
            s = jnp.dot(kb, qst_sc[i, hh], preferred_element_type=F32)
            s_sc[slot, hh] = s
            bm_sc[slot, hh] = jnp.max(s, axis=0, keepdims=True)

    def values(pair, slot):
        i, j = pair[0], block_of(pair)
        for hh in heads:
            pv = jnp.dot(vt_ref[0, hh, j], p_sc[slot, hh], preferred_element_type=F32)
            acc_sc[i, hh] = al_sc[slot, hh] * acc_sc[i, hh] + pv

    def softmax(pair, slot, diagonal):
        i, _ = pair
        for hh in heads:
            for c in range(n_strips):
                lanes = slice(c * LANES, (c + 1) * LANES)
                m = st_sc[i, hh, :, lanes]
                if not diagonal:
                    mc = jnp.maximum(m, bm_sc[slot, hh, :, lanes])
                    p = jnp.exp2(s_sc[slot, hh, :, lanes] - mc)
                else:
                    s = s_sc[slot, hh, :, lanes]
                    key = lax.broadcasted_iota(jnp.int32, s.shape, 0)
                    qry = lax.broadcasted_iota(jnp.int32, s.shape, 1) + (c * LANES) % TB
                    s = jnp.where(key <= qry, s, -jnp.inf)
                    mc = jnp.maximum(m, jnp.max(s, axis=0, keepdims=True))
                    p = jnp.exp2(s - mc)
                p_sc[slot, hh, :, lanes] = p.astype(BF16)
                st_sc[i, hh, :, lanes] = mc
                al_sc[slot, hh, :, lanes] = jnp.exp2(m - mc)

    def succ(pair):
        i, k = pair
        last = k == i
        i2 = jnp.minimum(jnp.where(last, i + 1, i), n_q - 1)
        return i2, jnp.where(last, 0, k + 1)

    def two_steps(t, carry):
        prev, cur = carry
        nxt = succ(cur)
        nxt2 = succ(nxt)

        def body(diag0):
            scores(nxt, 1)
            values(prev, 1)
            softmax(cur, 0, diag0)
            scores(nxt2, 0)
            values(cur, 0)
            softmax(nxt, 1, False)

        d0 = is_diag(cur)
        pl.when(d0)(lambda: body(True))
        pl.when(jnp.logical_not(d0))(lambda: body(False))
        return nxt, nxt2

    zero = jnp.int32(0)
    first = (zero, zero)
    scores(first, 0)
    last, _ = lax.fori_loop(0, n_pairs // 2, two_steps, (first, first))
    values(last, 1)

    def finish(i, c):
        for hh in heads:
            o = acc_sc[i, hh, :V_DIM] * (1.0 / acc_sc[i, hh, V_DIM:V_DIM + 1])
            o = o[:, :TB] - lam * o[:, TB:]
            ms = jnp.mean(o * o, axis=0, keepdims=True)
            o = o * lax.rsqrt(ms + LN_EPS) * (g_ref[...] * (1.0 - lam_init))
            o_ref[0, pl.ds(pl.multiple_of(i * TB, TB), TB), hh * LANES:(hh + 1) * LANES] = o.T.astype(o_ref.dtype)
        return c

    lax.fori_loop(0, n_q, finish, 0)


def _diff_attn_flat(qt, k, vt, lq1, lk1, lq2, lk2, subln_g, lam_init):
    B, L, _ = k.shape
    n_q = L // TB
    vec = pl.BlockSpec((1, QK_DIM), lambda b: (0, 0))
    return pl.pallas_call(
        functools.partial(_attn_flat_kernel, lam_init=lam_init),
        grid=(B,),
        in_specs=[
            vec, vec, vec, vec,
            pl.BlockSpec((V_DIM, 1), lambda b: (0, 0)),
            pl.BlockSpec((1, HEADS, n_q, LANES, TB), lambda b: (b, 0, 0, 0, 0)),
            pl.BlockSpec((1, L, HEADS * LANES), lambda b: (b, 0, 0)),
            pl.BlockSpec((1, HEADS, n_q, VT_ROWS, TB), lambda b: (b, 0, 0, 0, 0)),
        ],
        out_specs=pl.BlockSpec((1, L, HEADS * LANES), lambda b: (b, 0, 0)),
        out_shape=jax.ShapeDtypeStruct((B, L, ATT_WIDTH), BF16),
        scratch_shapes=[
            pltpu.VMEM((n_q, HEADS, VT_ROWS, 2 * TB), F32),
            pltpu.VMEM((n_q, HEADS, LANES, 2 * TB), BF16),
            pltpu.VMEM((2, HEADS, TB, 2 * TB), F32),
            pltpu.VMEM((2, HEADS, TB, 2 * TB), BF16),
            pltpu.VMEM((2, HEADS, 1, 2 * TB), F32),
            pltpu.VMEM((n_q, HEADS, 1, 2 * TB), F32),
            pltpu.VMEM((2, HEADS, 1, 2 * TB), F32),
        ],
        compiler_params=_cparams(("parallel",)),
        name="diff_attn",
    )(lq1, lk1, lq2, lk2, subln_g.reshape(V_DIM, 1), qt, k, vt)


def _block_transpose8(vs):
    lane_blk = lax.broadcasted_iota(jnp.int32, vs[0].shape, 1) // SSM_GROUP
    for d in (4, 2, 1):
        keep = (lane_blk & d) == 0
        new = list(vs)
        for a in range(8):
            if a & d == 0:
                lo, hi = vs[a], vs[a + d]
                new[a] = jnp.where(keep, lo, pltpu.roll(hi, SSM_GROUP * d, 1))
                new[a + d] = jnp.where(keep, pltpu.roll(lo, LANES - SSM_GROUP * d, 1), hi)
        vs = new
    return vs


def _ssm_kernel(u_ref, toep_ref, win_ref, wout_ref, a_ref, d_ref, y_ref, s_sc, x_sc, sp_sc, *, jt):
    rows = jt * SUBLANES

    @pl.when(pl.program_id(1) == 0)
    def _():
        s_sc[...] = jnp.zeros(s_sc.shape, F32)

    ug = [u_ref[0, g].astype(BF16) for g in range(GROUPS_PER_TILE)]

    y = [jnp.dot(ug[g], toep_ref[0, g], preferred_element_type=F32) for g in range(GROUPS_PER_TILE)]

    for m in range(PAIRS_PER_TILE):
        up = jnp.concatenate([ug[2 * m], ug[2 * m + 1]], axis=1)
        xp = jnp.dot(up, win_ref[0, m], preferred_element_type=F32)
        x_sc[0, :, m * LANES:(m + 1) * LANES] = xp[:, :LANES]
        x_sc[1, :, m * LANES:(m + 1) * LANES] = xp[:, LANES:]

    ar = a_ref[0, 0:1, :]
    ai = a_ref[0, 1:2, :]

    def step(j, carry):
        sr, si = carry
        r8 = pl.ds(pl.multiple_of(j * SUBLANES, SUBLANES), SUBLANES)
        sp_sc[0, r8, :] = sr
        sp_sc[1, r8, :] = si
        return (ar * sr - ai * si + x_sc[0, r8, :], ar * si + ai * sr + x_sc[1, r8, :])

    sr, si = lax.fori_loop(0, jt, step, (s_sc[0], s_sc[1]))
    s_sc[0] = sr
    s_sc[1] = si

    for m in range(PAIRS_PER_TILE):
        sp = jnp.concatenate([sp_sc[0, :, m * LANES:(m + 1) * LANES],
                              sp_sc[1, :, m * LANES:(m + 1) * LANES]], axis=1).astype(BF16)
        yc = jnp.dot(sp, wout_ref[0, m], preferred_element_type=F32)
        y[2 * m] = y[2 * m] + yc[:, :CHUNK_COLS]
        y[2 * m + 1] = y[2 * m + 1] + yc[:, CHUNK_COLS:]

    for g in range(GROUPS_PER_TILE):
        y_ref[0, g] = y[g] + d_ref[0, g] * u_ref[0, g]


def _chunk_slabs(tiles):
    z_lo = _block_transpose8(tiles[:8])
    z_hi = _block_transpose8(tiles[8:])
    return [jnp.concatenate([z_lo[g], z_hi[g]], axis=1) for g in range(GROUPS_PER_TILE)]


def _chunk_tiles(slabs):
    out = []
    for half in range(2):
        out += _block_transpose8([s[:, half * LANES:(half + 1) * LANES] for s in slabs])
    return out


def _ssm(u_slab, tables, d_skip, B, L, lt=1024):
    toep, win, wout, a16 = tables
    jt = lt // CHUNK
    n_l = L // lt
    rows = jt * B
    slab = pl.BlockSpec((1, GROUPS_PER_TILE, rows, CHUNK_COLS), lambda q, l: (q, 0, l, 0))
    d_slab = jnp.tile(d_skip.reshape(LANE_TILES, GROUPS_PER_TILE, 1, SSM_GROUP), (1, 1, 1, CHUNK))
    return pl.pallas_call(
        functools.partial(_ssm_kernel, jt=jt),
        grid=(LANE_TILES, n_l),
        in_specs=[
            slab,
            pl.BlockSpec((1, GROUPS_PER_TILE, CHUNK_COLS, CHUNK_COLS), lambda q, l: (q, 0, 0, 0)),
            pl.BlockSpec((1, PAIRS_PER_TILE, 2 * CHUNK_COLS, 2 * LANES), lambda q, l: (q, 0, 0, 0)),
            pl.BlockSpec((1, PAIRS_PER_TILE, 2 * LANES, 2 * CHUNK_COLS), lambda q, l: (q, 0, 0, 0)),
            pl.BlockSpec((1, 2, STATE_COLS), lambda q, l: (q, 0, 0)),
            pl.BlockSpec((1, GROUPS_PER_TILE, 1, CHUNK_COLS), lambda q, l: (q, 0, 0, 0)),
        ],
        out_specs=slab,
        out_shape=jax.ShapeDtypeStruct(u_slab.shape, F32),
        scratch_shapes=[
            pltpu.VMEM((2, B, STATE_COLS), F32),
            pltpu.VMEM((2, rows, STATE_COLS), F32),
            pltpu.VMEM((2, rows, STATE_COLS), F32),
        ],
        compiler_params=_cparams(("parallel", "arbitrary")),
        name="ssm",
    )(u_slab, toep, win, wout, a16, d_slab)


def _cmul(a, b):
    return a[0] * b[0] - a[1] * b[1], a[0] * b[1] + a[1] * b[0]


def _zoh(lr, li, ldt):
    dt = jnp.exp(ldt)
    mag = jnp.exp(lr * dt)
    ab = (mag * jnp.cos(li * dt), mag * jnp.sin(li * dt))
    den = lr * lr + li * li
    zr, zi = ab[0] - 1.0, ab[1]
    return ab, ((zr * lr + zi * li) / den, (zi * lr - zr * li) / den)


def _cpow(base, exponent, bits):
    out = None
    for k in range(bits):
        on = (exponent & (1 << k)) != 0
        sel = (jnp.where(on, base[0], 1.0), jnp.where(on, base[1], 0.0))
        out = sel if out is None else _cmul(out, sel)
        if k + 1 < bits:
            base = _cmul(base, base)
    return out


def _ssm_prep_kernel(row_ref, ct_ref, bt_ref, toep_ref, win_ref, wout_ref, a_ref):
    bits = CHUNK.bit_length() - 1
    lane = lax.broadcasted_iota(jnp.int32, (1, CHUNK_COLS), 1)
    tau = lane // SSM_GROUP
    rowi = lax.broadcasted_iota(jnp.int32, (CHUNK_COLS, 1), 0)
    back = (CHUNK - 1) - rowi // SSM_GROUP
    lane2 = lax.broadcasted_iota(jnp.int32, (1, LANES), 1)
    exact = functools.partial(jnp.dot, precision=lax.Precision.HIGHEST, preferred_element_type=F32)
    zoh = [_zoh(row_ref[0, m], row_ref[1, m], row_ref[2, m]) for m in range(PAIRS_PER_TILE)]
    pad = jnp.zeros((SUBLANES - PAIRS_PER_TILE, LANES), F32)
    ab_cols = tuple(jnp.concatenate([z[0][c] for z in zoh] + [pad], axis=0).T for c in range(2))
    a_rows = []
    for m in range(PAIRS_PER_TILE):
        abp, fp = zoh[m]
        pwr = _cpow(abp, back, bits)
        a16 = abp
        for _ in range(bits):
            a16 = _cmul(a16, a16)
        a_rows.append(a16)
        wout_rows = [[None, None], [None, None]]
        for j in range(2):
            g = 2 * m + j
            states = slice(j * SSM_STATE, (j + 1) * SSM_STATE)
            ab = tuple(v[states, m:m + 1] for v in ab_cols)
            pw = _cpow(ab, tau, bits)
            g0 = _cmul(pw, (ct_ref[0, g], ct_ref[1, g]))
            g1 = _cmul(g0, ab)
            mine = (lane2 < SSM_STATE) if j == 0 else (lane2 >= SSM_STATE)
            bb2 = _cmul(fp, (bt_ref[0, g], bt_ref[1, g]))
            bb2 = tuple(jnp.where(mine, v, 0.0) for v in bb2)
            none = jnp.zeros_like(g0[0])
            stack = (lambda v: jnp.concatenate([v, none], axis=0)) if j == 0 else \
                    (lambda v: jnp.concatenate([none, v], axis=0))
            kt = exact(bb2[0], stack(g0[0])) - exact(bb2[1], stack(g0[1]))
            toep_ref[0, g] = jnp.concatenate(
                [jnp.where(lane >= SSM_GROUP * r, pltpu.roll(kt, SSM_GROUP * r, 1), 0.0) if r else kt
                 for r in range(CHUNK)], axis=0).astype(BF16)
            w = _cmul(pwr, tuple(jnp.concatenate([v] * CHUNK, axis=0) for v in bb2))
            win_ref[0, m, j * CHUNK_COLS:(j + 1) * CHUNK_COLS, :] = jnp.concatenate(w, axis=1).astype(BF16)
            wout_rows[0][j] = g1[0]
            wout_rows[1][j] = -g1[1]
        zero = jnp.zeros((SSM_STATE, CHUNK_COLS), F32)
        wout_ref[0, m] = jnp.concatenate([
            jnp.concatenate([wout_rows[0][0], zero], axis=1),
            jnp.concatenate([zero, wout_rows[0][1]], axis=1),
            jnp.concatenate([wout_rows[1][0], zero], axis=1),
            jnp.concatenate([zero, wout_rows[1][1]], axis=1)], axis=0).astype(BF16)
    a_ref[0] = jnp.concatenate([jnp.concatenate([a[0] for a in a_rows], axis=1),
                                jnp.concatenate([a[1] for a in a_rows], axis=1)], axis=0)


def _ssm_prep(lam_re, lam_im, log_dt, b_re, b_im, c_re, c_im):
    G, P, H = b_re.shape
    ldt = jnp.broadcast_to(log_dt[:, None], (G, P))
    rows = jnp.stack([lam_re, lam_im, ldt]).reshape(3, G // 2, 1, 2 * P)
    ct = jnp.tile(jnp.stack([c_re, c_im]).transpose(0, 1, 3, 2), (1, 1, 1, CHUNK))
    bt = jnp.tile(jnp.stack([b_re, b_im]).transpose(0, 1, 3, 2), (1, 1, 1, 2))
    gt = GROUPS_PER_TILE
    return pl.pallas_call(
        _ssm_prep_kernel,
        grid=(LANE_TILES,),
        in_specs=[
            pl.BlockSpec((3, PAIRS_PER_TILE, 1, 2 * P), lambda q: (0, q, 0, 0)),
            pl.BlockSpec((2, gt, P, CHUNK_COLS), lambda q: (0, q, 0, 0)),
            pl.BlockSpec((2, gt, H, 2 * P), lambda q: (0, q, 0, 0)),
        ],
        out_specs=[
            pl.BlockSpec((1, gt, CHUNK_COLS, CHUNK_COLS), lambda q: (q, 0, 0, 0)),
            pl.BlockSpec((1, PAIRS_PER_TILE, 2 * CHUNK_COLS, 2 * LANES), lambda q: (q, 0, 0, 0)),
            pl.BlockSpec((1, PAIRS_PER_TILE, 2 * LANES, 2 * CHUNK_COLS), lambda q: (q, 0, 0, 0)),
            pl.BlockSpec((1, 2, STATE_COLS), lambda q: (q, 0, 0)),
        ],
        out_shape=[
            jax.ShapeDtypeStruct((LANE_TILES, gt, CHUNK_COLS, CHUNK_COLS), BF16),
            jax.ShapeDtypeStruct((LANE_TILES, PAIRS_PER_TILE, 2 * CHUNK_COLS, 2 * LANES), BF16),
            jax.ShapeDtypeStruct((LANE_TILES, PAIRS_PER_TILE, 2 * LANES, 2 * CHUNK_COLS), BF16),
            jax.ShapeDtypeStruct((LANE_TILES, 2, STATE_COLS), F32),
        ],
        compiler_params=_cparams(("parallel",)),
        name="ssm_prep",
    )(rows, ct, bt)


def _layer_norm(r, g, b):
    mu = jnp.mean(r, axis=-1, keepdims=True)
    c = r - mu
    var = jnp.mean(c * c, axis=-1, keepdims=True)
    return c * lax.rsqrt(var + LN_EPS) * g + b


MIX_ROWS = 256


def _gelu_tanh(y):
    return 0.5 * y * (1.0 + jnp.tanh(math.sqrt(2.0 / math.pi) * (y + 0.044715 * (y * y * y))))


def _mix_ln1_kernel(x_ref, attn_ref, y_ref, wglu_ref, bglu_ref, wo_ref, g_ref, b_ref, o_ref, y_sc):
    nb, tl, d = x_ref.shape
    jt = tl // CHUNK
    per = MIX_ROWS // tl
    n_pieces = nb // per
    for q in range(LANE_TILES):
        tiles = _chunk_tiles([y_ref[q, g] for g in range(GROUPS_PER_TILE)])
        y_sc[q] = jnp.stack([t.reshape(jt, nb, LANES) for t in tiles], axis=1).reshape(tl * nb, LANES)

    def glu(n):
        y = jnp.concatenate(
            [jnp.concatenate([y_sc[q, pl.ds(b, tl, stride=nb), :] for q in range(LANE_TILES)], axis=1)
             for b in range(n * per, (n + 1) * per)], axis=0)
        gl = _gelu_tanh(y)
        z = gl * jax.nn.sigmoid(jnp.dot(gl.astype(BF16), wglu_ref[...], preferred_element_type=F32) + bglu_ref[...])
        return z.astype(BF16)

    def project(n, z):
        bs = slice(n * per, (n + 1) * per)
        mix = (jnp.dot(attn_ref[bs].reshape(MIX_ROWS, ATT_WIDTH), wo_ref[:ATT_WIDTH], preferred_element_type=F32)
               + jnp.dot(z, wo_ref[ATT_WIDTH:], preferred_element_type=F32))
        res = _layer_norm(ALPHA * x_ref[bs].reshape(MIX_ROWS, d) + mix, g_ref[...], b_ref[...])
        o_ref[bs] = res.reshape(per, tl, d)

    z = glu(0)
    for n in range(n_pieces):
        z_next = glu(n + 1) if n + 1 < n_pieces else None
        project(n, z)
        z = z_next


def _mix_ln1(x, attn, y_tm, wglu_bf, b_glu, wo_bf, ln_g, ln_b, tl=128):
    B, L, D = x.shape
    assert MIX_ROWS % tl == 0 and B % (MIX_ROWS // tl) == 0
    resident = lambda shape: pl.BlockSpec(shape, lambda l: (0, 0), pipeline_mode=pl.Buffered(1))
    return pl.pallas_call(
        _mix_ln1_kernel,
        grid=(L // tl,),
        in_specs=[
            pl.BlockSpec((B, tl, D), lambda l: (0, l, 0)),
            pl.BlockSpec((B, tl, ATT_WIDTH), lambda l: (0, l, 0)),
            pl.BlockSpec((LANE_TILES, GROUPS_PER_TILE, tl // CHUNK * B, CHUNK_COLS), lambda l: (0, 0, l, 0)),
            resident((SSM_WIDTH, SSM_WIDTH)),
            resident((1, SSM_WIDTH)),
            resident((D, D)),
            resident((1, D)),
            resident((1, D)),
        ],
        out_specs=pl.BlockSpec((B, tl, D), lambda l: (0, l, 0)),
        out_shape=jax.ShapeDtypeStruct((B, L, D), F32),
        scratch_shapes=[pltpu.VMEM((LANE_TILES, tl * B, LANES), F32)],
        compiler_params=_cparams(("parallel",)),
        name="mix_ln1",
    )(x, attn, y_tm, wglu_bf, b_glu, wo_bf, ln_g, ln_b)


HALO = 16
FFN_SUB = 256
FFN_ROWS = 128


def _ffn_kernel(x_ref, xh_ref, p_ref, wple_ref, wgate_ref, wup_ref, cw_ref, cb_ref, wd_ref, g_ref, b_ref, o_ref,
                x32_sc, xh32_sc, p32_sc, xb_sc, acc_sc, hg_sc, hv_sc, act_sc, *, tl):
    nb = x_ref.shape[0]
    tm = tl * nb
    d_slabs = x_ref.shape[2] // LANES
    hp = (CONV_WIDTH - 1)
    for bb in range(nb):
        for s in range(d_slabs):
            lanes = slice(s * LANES, (s + 1) * LANES)
            x32_sc[s, pl.ds(bb, tl, stride=nb), :] = x_ref[bb, :, lanes]
            xh32_sc[s, pl.ds(bb, hp, stride=nb), :] = xh_ref[bb, SUBLANES - hp:, lanes]
        for s in range(p_ref.shape[2] // LANES):
            p32_sc[s, pl.ds(bb, tl, stride=nb), :] = p_ref[bb, :, s * LANES:(s + 1) * LANES]
    x_tm = jnp.concatenate([x32_sc[s] for s in range(d_slabs)], axis=1)
    p_tm = jnp.concatenate([p32_sc[s] for s in range(p32_sc.shape[0])], axis=1)
    keep = jnp.where(pl.program_id(0) == 0, 0.0, 1.0)
    xb_sc[0:HALO] = (jnp.concatenate([xh32_sc[s] for s in range(d_slabs)], axis=1) * keep).astype(BF16)
    xb_sc[HALO:] = x_tm.astype(BF16)
    gate = jax.nn.sigmoid(jnp.dot(xb_sc[HALO:], wgate_ref[...], preferred_element_type=F32))
    ple = jnp.dot(p_tm.astype(BF16), wple_ref[...], preferred_element_type=F32) * gate
    acc_sc[...] = ALPHA * x_tm + ple

    def conv(h_sc, slot, cols, r0, rows):
        out = cb_ref[:, cols] + cw_ref[CONV_WIDTH - 1:CONV_WIDTH, cols] * h_sc[slot, HALO + r0:HALO + r0 + rows]
        for j in range(CONV_WIDTH - 1):
            back = (CONV_WIDTH - 1 - j) * nb
            out = out + cw_ref[j:j + 1, cols] * h_sc[slot, HALO + r0 - back:HALO + r0 - back + rows]
        return out

    n_sub = D_FF // FFN_SUB
    g_cols = lambda n: slice(n * FFN_SUB, (n + 1) * FFN_SUB)
    v_cols = lambda n: slice(D_FF + n * FFN_SUB, D_FF + (n + 1) * FFN_SUB)

    def up(n):
        xb = xb_sc[...]
        hg_sc[n % 2] = jnp.dot(xb, wup_ref[:, g_cols(n)], preferred_element_type=F32)
        hv_sc[n % 2] = jnp.dot(xb, wup_ref[:, v_cols(n)], preferred_element_type=F32)

    def gate_block(n):
        for r0 in range(0, tm, FFN_ROWS):
            gc = conv(hg_sc, n % 2, g_cols(n), r0, FFN_ROWS)
            vc = conv(hv_sc, n % 2, v_cols(n), r0, FFN_ROWS)
            act_sc[r0:r0 + FFN_ROWS, g_cols(n)] = (gc * jax.nn.sigmoid(gc) * vc).astype(BF16)

    def down(lo, hi):
        cols = slice(lo * FFN_SUB, hi * FFN_SUB)
        return jnp.dot(act_sc[:, cols], wd_ref[cols, :], preferred_element_type=F32)

    half = (n_sub + 1) // 2
    up(0)
    for n in range(n_sub):
        if n + 1 < n_sub:
            up(n + 1)
        gate_block(n)
        if n + 1 == half:
            acc_sc[...] += down(0, half)

    res = _layer_norm(acc_sc[...] + down(half, n_sub), g_ref[...], b_ref[...])
    for s in range(d_slabs):
        x32_sc[s] = res[:, s * LANES:(s + 1) * LANES]
    for bb in range(nb):
        for s in range(d_slabs):
            o_ref[bb, :, s * LANES:(s + 1) * LANES] = x32_sc[s, pl.ds(bb, tl, stride=nb), :]


def _ffn_ln2(x1, p, wple_bf, wgate_bf, wup_bf, conv_w, conv_b, wdown_bf, ln_g, ln_b, tl=64):
    B, L, D = x1.shape
    tm = tl * B
    assert D_FF % FFN_SUB == 0 and HALO == (CONV_WIDTH - 1) * B and tm % FFN_ROWS == 0
    const = lambda i: (0, 0)
    resident = lambda shape: pl.BlockSpec(shape, const, pipeline_mode=pl.Buffered(1))
    return pl.pallas_call(
        functools.partial(_ffn_kernel, tl=tl),
        grid=(L // tl,),
        in_specs=[
            pl.BlockSpec((B, tl, D), lambda i: (0, i, 0)),
            pl.BlockSpec((B, SUBLANES, D), lambda i: (0, jnp.maximum(i * (tl // SUBLANES) - 1, 0), 0)),
            pl.BlockSpec((B, tl, PLE_DIM), lambda i: (0, i, 0)),
            resident((PLE_DIM, D)),
            resident((D, D)),
            resident((D, 2 * D_FF)),
            resident((CONV_WIDTH, 2 * D_FF)),
            resident((1, 2 * D_FF)),
            resident((D_FF, D)),
            resident((1, D)),
            resident((1, D)),
        ],
        out_specs=pl.BlockSpec((B, tl, D), lambda i: (0, i, 0)),
        out_shape=jax.ShapeDtypeStruct((B, L, D), F32),
        scratch_shapes=[
            pltpu.VMEM((D // LANES, tm, LANES), F32),
            pltpu.VMEM((D // LANES, HALO, LANES), F32),
            pltpu.VMEM((PLE_DIM // LANES, tm, LANES), F32),
            pltpu.VMEM((tm + HALO, D), BF16),
            pltpu.VMEM((tm, D), F32),
            pltpu.VMEM((2, tm + HALO, FFN_SUB), F32),
            pltpu.VMEM((2, tm + HALO, FFN_SUB), F32),
            pltpu.VMEM((tm, D_FF), BF16),
        ],
        compiler_params=_cparams(("parallel",)),
        name="ffn_ln2",
    )(x1, x1, p, wple_bf, wgate_bf, wup_bf, conv_w, conv_b, wdown_bf, ln_g, ln_b)


def kernel(x, p, w_in, diff_lambda_q1, diff_lambda_k1, diff_lambda_q2, diff_lambda_k2, diff_subln_g, ssm_lambda_re, ssm_lambda_im, ssm_log_dt, ssm_b_re, ssm_b_im, ssm_c_re, ssm_c_im, ssm_d, ssm_w_glu, ssm_b_glu, w_o, ln1_g, ln1_b, ffn_w_up, ffn_conv_w, ffn_conv_b, ffn_w_down, w_ple, w_ple_gate, ln2_g, ln2_b):
    B, L, D = x.shape
    assert B == SUBLANES and D == D_MODEL and L % 512 == 0
    for i in range(DEPTH):
        lam_init = 0.8 - 0.6 * math.exp(-0.3 * i)
        row = lambda a: a[i].reshape(1, -1)
        qt, k, vt, u_tm = _in_proj(x, w_in[i])
        attn = _diff_attn_flat(qt, k, vt, row(diff_lambda_q1), row(diff_lambda_k1), row(diff_lambda_q2),
                          row(diff_lambda_k2), diff_subln_g[i], lam_init)
        tables = _ssm_prep(ssm_lambda_re[i], ssm_lambda_im[i], ssm_log_dt[i], ssm_b_re[i], ssm_b_im[i],
                           ssm_c_re[i], ssm_c_im[i])
        y_tm = _ssm(u_tm, tables, ssm_d[i], B, L)
        x1 = _mix_ln1(x, attn, y_tm, ssm_w_glu[i].astype(BF16), row(ssm_b_glu), w_o[i].astype(BF16),
                      row(ln1_g), row(ln1_b))
        x = _ffn_ln2(x1, p[i], w_ple[i].astype(BF16), w_ple_gate[i].astype(BF16), ffn_w_up[i].astype(BF16),
                     ffn_conv_w[i], row(ffn_conv_b), ffn_w_down[i].astype(BF16), row(ln2_g), row(ln2_b))
    return x
```

```python
import functools
import math

import jax
import jax.numpy as jnp
from jax import lax
from jax.experimental import pallas as pl
from jax.experimental.pallas import tpu as pltpu

F32 = jnp.float32
BF16 = jnp.bfloat16

SUBLANES = 8
LANES = 128

DEPTH = 1
D_MODEL = 1024
PLE_DIM = 256
HEADS = 4
QK_DIM = 64
V_DIM = 2 * QK_DIM
QK_WIDTH = HEADS * 2 * QK_DIM
ATT_WIDTH = HEADS * V_DIM
SSM_WIDTH = D_MODEL - ATT_WIDTH
SSM_GROUP = 16
SSM_GROUPS = SSM_WIDTH // SSM_GROUP
SSM_STATE = 64
D_FF = 2816
CONV_WIDTH = 3
LN_EPS = 1e-5
ALPHA = (2 * DEPTH) ** 0.25
QK_SCALE = QK_DIM ** -0.5 * math.log2(math.e)

CHUNK = 16
CHUNK_COLS = CHUNK * SSM_GROUP
GROUPS_PER_TILE = LANES // SSM_GROUP
LANE_TILES = SSM_WIDTH // LANES
PAIRS_PER_TILE = GROUPS_PER_TILE // 2
STATE_COLS = GROUPS_PER_TILE * SSM_STATE

VMEM_LIMIT = 56 * 1024 * 1024


def _cparams(sem):
    return pltpu.CompilerParams(dimension_semantics=sem, vmem_limit_bytes=VMEM_LIMIT)


TB = 256
VT_ROWS = V_DIM + 16


def _in_proj_kernel(x_ref, wqt_ref, wk_ref, wvt_ref, wu_ref, qt_ref, k_ref, vt_ref, u_ref, u_sc):
    nb, tl, d = x_ref.shape
    jt = tl // CHUNK
    xb = x_ref[...].reshape(nb * tl, d).astype(BF16)
    hu = jnp.dot(xb, wu_ref[...], preferred_element_type=F32)
    for b in range(nb):
        for q in range(LANE_TILES):
            u_sc[q, pl.ds(b, tl, stride=nb), :] = hu[b * tl:(b + 1) * tl, q * LANES:(q + 1) * LANES]
    for q in range(LANE_TILES):
        tiles = u_sc[q].reshape(jt, CHUNK, nb, LANES)
        slabs = _chunk_slabs([tiles[:, r].reshape(jt * nb, LANES) for r in range(CHUNK)])
        for g in range(GROUPS_PER_TILE):
            u_ref[q, g] = slabs[g]
    nt = (((1,), (1,)), ((), ()))
    qt = lax.dot_general(wqt_ref[...], xb, nt, preferred_element_type=F32) * QK_SCALE
    vt = lax.dot_general(wvt_ref[...], xb, nt, preferred_element_type=F32)
    pad_row = lax.broadcasted_iota(jnp.int32, (VT_ROWS - V_DIM, TB), 0)
    ones_row = jnp.where(pad_row == 0, 1.0, 0.0).astype(BF16)
    for b in range(nb):
        for h in range(HEADS):
            for c in range(tl // TB):
                cols = slice(b * tl + c * TB, b * tl + (c + 1) * TB)
                qt_ref[b, h, c] = qt[h * LANES:(h + 1) * LANES, cols].astype(BF16)
                vt_ref[b, h, c, :V_DIM] = vt[h * LANES:(h + 1) * LANES, cols].astype(BF16)
                vt_ref[b, h, c, V_DIM:] = ones_row
    k_ref[...] = jnp.dot(xb, wk_ref[...], preferred_element_type=F32).astype(BF16).reshape(k_ref.shape)


def _in_proj(x, w_in, tl=256):
    B, L, D = x.shape
    w = w_in.astype(BF16)
    wqt = w[:, :QK_WIDTH].T
    wk = w[:, QK_WIDTH:2 * QK_WIDTH]
    wvt = w[:, 2 * QK_WIDTH:2 * QK_WIDTH + ATT_WIDTH].T
    wu = w[:, 2 * QK_WIDTH + ATT_WIDTH:]
    resident = lambda shape: pl.BlockSpec(shape, lambda l: (0, 0), pipeline_mode=pl.Buffered(1))
    slab = lambda rows: pl.BlockSpec((B, HEADS, tl // TB, rows, TB), lambda l: (0, 0, l, 0, 0))
    slab_shape = lambda rows: jax.ShapeDtypeStruct((B, HEADS, L // TB, rows, TB), BF16)
    return pl.pallas_call(
        _in_proj_kernel,
        grid=(L // tl,),
        in_specs=[
            pl.BlockSpec((B, tl, D), lambda l: (0, l, 0)),
            resident((QK_WIDTH, D)),
            resident((D, QK_WIDTH)),
            resident((ATT_WIDTH, D)),
            resident((D, SSM_WIDTH)),
        ],
        out_specs=[
            slab(LANES),
            pl.BlockSpec((B, tl, QK_WIDTH), lambda l: (0, l, 0)),
            slab(VT_ROWS),
            pl.BlockSpec((LANE_TILES, GROUPS_PER_TILE, tl // CHUNK * B, CHUNK_COLS), lambda l: (0, 0, l, 0)),
        ],
        out_shape=[
            slab_shape(LANES),
            jax.ShapeDtypeStruct((B, L, QK_WIDTH), BF16),
            slab_shape(VT_ROWS),
            jax.ShapeDtypeStruct((LANE_TILES, GROUPS_PER_TILE, L // CHUNK * B, CHUNK_COLS), F32),
        ],
        scratch_shapes=[pltpu.VMEM((LANE_TILES, tl * B, LANES), F32)],
        compiler_params=_cparams(("parallel",)),
        name="in_proj",
    )(x, wqt, wk, wvt, wu)


def _attn_flat_kernel(lq1_ref, lk1_ref, lq2_ref, lk2_ref, g_ref, qt_ref, k_ref, vt_ref, o_ref,
                      acc_sc, qst_sc, s_sc, p_sc, bm_sc, st_sc, al_sc, *, lam_init):
    n_q = qt_ref.shape[2]
    n_pairs = n_q * (n_q + 1) // 2
    assert TB == 2 * LANES and n_pairs % 2 == 0
    heads = range(HEADS)
    n_strips = 2 * TB // LANES
    lam = (jnp.exp(jnp.sum(lq1_ref[...] * lk1_ref[...], axis=-1, keepdims=True))
           - jnp.exp(jnp.sum(lq2_ref[...] * lk2_ref[...], axis=-1, keepdims=True)) + lam_init)

    def prepare(i, c):
        for hh in heads:
            qt = qt_ref[0, hh, i]
            row = lax.broadcasted_iota(jnp.int32, qt.shape, 0)
            zero = jnp.zeros_like(qt)
            qst_sc[i, hh, :, :TB] = jnp.where(row < QK_DIM, qt, zero)
            qst_sc[i, hh, :, TB:] = jnp.where(row >= QK_DIM, qt, zero)
            acc_sc[i, hh] = jnp.zeros(acc_sc.shape[2:], F32)
            st_sc[i, hh] = jnp.full((1, 2 * TB), -jnp.inf, F32)
        return c

    lax.fori_loop(0, n_q, prepare, 0)
    p_sc[1] = jnp.zeros(p_sc.shape[1:], BF16)
    al_sc[1] = jnp.ones(al_sc.shape[1:], F32)

    def diag_place(i):
        r = i % 4
        return jnp.where(jnp.logical_or(r == 1, r == 2), 1, 0)

    def is_diag(pair):
        return pair[1] == diag_place(pair[0])

    def block_of(pair):
        i, k = pair
        d = diag_place(i)
        return jnp.where(k == d, i, jnp.where(k < d, k, k - 1))

    def scores(pair, slot):
        i, j = pair[0], block_of(pair)
        rows = pl.ds(pl.multiple_of(j * TB, TB), TB)
        for hh in heads:
            kb = k_ref[0, rows, hh * LANES:(hh + 1) * LANES]
            s = jnp.dot(kb, qst_sc[i, hh], preferred_element_type=F32)
            s_sc[slot, hh] = s
            bm_sc[slot, hh] = jnp.max(s, axis=0, keepdims=True)

    def values(pair, slot):
        i, j = pair[0], block_of(pair)
        for hh in heads:
            pv = jnp.dot(vt_ref[0, hh, j], p_sc[slot, hh], preferred_element_type=F32)
            acc_sc[i, hh] = al_sc[slot, hh] * acc_sc[i, hh] + pv

    def softmax(pair, slot, diagonal):
        i, _ = pair
        for hh in heads:
            for c in range(n_strips):
                lanes = slice(c * LANES, (c + 1) * LANES)
                m = st_sc[i, hh, :, lanes]
                if not diagonal:
                    mc = jnp.maximum(m, bm_sc[slot, hh, :, lanes])
                    p = jnp.exp2(s_sc[slot, hh, :, lanes] - mc)
                else:
                    s = s_sc[slot, hh, :, lanes]
                    key = lax.broadcasted_iota(jnp.int32, s.shape, 0)
                    qry = lax.broadcasted_iota(jnp.int32, s.shape, 1) + (c * LANES) % TB
                    s = jnp.where(key <= qry, s, -jnp.inf)
                    mc = jnp.maximum(m, jnp.max(s, axis=0, keepdims=True))
                    p = jnp.exp2(s - mc)
                p_sc[slot, hh, :, lanes] = p.astype(BF16)
                st_sc[i, hh, :, lanes] = mc
                al_sc[slot, hh, :, lanes] = jnp.exp2(m - mc)

    def succ(pair):
        i, k = pair
        last = k == i
        i2 = jnp.minimum(jnp.where(last, i + 1, i), n_q - 1)
        return i2, jnp.where(last, 0, k + 1)

    def two_steps(t, carry):
        prev, cur = carry
        nxt = succ(cur)
        nxt2 = succ(nxt)

        def body(diag0):
            scores(nxt, 1)
            values(prev, 1)
            softmax(cur, 0, diag0)
            scores(nxt2, 0)
            values(cur, 0)
            softmax(nxt, 1, False)

        d0 = is_diag(cur)
        pl.when(d0)(lambda: body(True))
        pl.when(jnp.logical_not(d0))(lambda: body(False))
        return nxt, nxt2

    zero = jnp.int32(0)
    first = (zero, zero)
    scores(first, 0)
    last, _ = lax.fori_loop(0, n_pairs // 2, two_steps, (first, first))
    values(last, 1)

    def finish(i, c):
        for hh in heads:
            o = acc_sc[i, hh, :V_DIM] * (1.0 / acc_sc[i, hh, V_DIM:V_DIM + 1])
            o = o[:, :TB] - lam * o[:, TB:]
            ms = jnp.mean(o * o, axis=0, keepdims=True)
            o = o * lax.rsqrt(ms + LN_EPS) * (g_ref[...] * (1.0 - lam_init))
            o_ref[0, pl.ds(pl.multiple_of(i * TB, TB), TB), hh * LANES:(hh + 1) * LANES] = o.T.astype(o_ref.dtype)
        return c

    lax.fori_loop(0, n_q, finish, 0)


def _diff_attn_flat(qt, k, vt, lq1, lk1, lq2, lk2, subln_g, lam_init):
    B, L, _ = k.shape
    n_q = L // TB
    vec = pl.BlockSpec((1, QK_DIM), lambda b: (0, 0))
    return pl.pallas_call(
        functools.partial(_attn_flat_kernel, lam_init=lam_init),
        grid=(B,),
        in_specs=[
            vec, vec, vec, vec,
            pl.BlockSpec((V_DIM, 1), lambda b: (0, 0)),
            pl.BlockSpec((1, HEADS, n_q, LANES, TB), lambda b: (b, 0, 0, 0, 0)),
            pl.BlockSpec((1, L, HEADS * LANES), lambda b: (b, 0, 0)),
            pl.BlockSpec((1, HEADS, n_q, VT_ROWS, TB), lambda b: (b, 0, 0, 0, 0)),
        ],
        out_specs=pl.BlockSpec((1, L, HEADS * LANES), lambda b: (b, 0, 0)),
        out_shape=jax.ShapeDtypeStruct((B, L, ATT_WIDTH), BF16),
        scratch_shapes=[
            pltpu.VMEM((n_q, HEADS, VT_ROWS, 2 * TB), F32),
            pltpu.VMEM((n_q, HEADS, LANES, 2 * TB), BF16),
            pltpu.VMEM((2, HEADS, TB, 2 * TB), F32),
            pltpu.VMEM((2, HEADS, TB, 2 * TB), BF16),
            pltpu.VMEM((2, HEADS, 1, 2 * TB), F32),
            pltpu.VMEM((n_q, HEADS, 1, 2 * TB), F32),
            pltpu.VMEM((2, HEADS, 1, 2 * TB), F32),
        ],
        compiler_params=_cparams(("parallel",)),
        name="diff_attn",
    )(lq1, lk1, lq2, lk2, subln_g.reshape(V_DIM, 1), qt, k, vt)


def _block_transpose8(vs):
    lane_blk = lax.broadcasted_iota(jnp.int32, vs[0].shape, 1) // SSM_GROUP
    for d in (4, 2, 1):
        keep = (lane_blk & d) == 0
        new = list(vs)
        for a in range(8):
            if a & d == 0:
                lo, hi = vs[a], vs[a + d]
                new[a] = jnp.where(keep, lo, pltpu.roll(hi, SSM_GROUP * d, 1))
                new[a + d] = jnp.where(keep, pltpu.roll(lo, LANES - SSM_GROUP * d, 1), hi)
        vs = new
    return vs


def _ssm_kernel(u_ref, toep_ref, win_ref, wout_ref, a_ref, d_ref, y_ref, s_sc, x_sc, sp_sc, *, jt):
    rows = jt * SUBLANES

    @pl.when(pl.program_id(1) == 0)
    def _():
        s_sc[...] = jnp.zeros(s_sc.shape, F32)

    ug = [u_ref[0, g].astype(BF16) for g in range(GROUPS_PER_TILE)]

    y = [jnp.dot(ug[g], toep_ref[0, g], preferred_element_type=F32) for g in range(GROUPS_PER_TILE)]

    for m in range(PAIRS_PER_TILE):
        up = jnp.concatenate([ug[2 * m], ug[2 * m + 1]], axis=1)
        xp = jnp.dot(up, win_ref[0, m], preferred_element_type=F32)
        x_sc[0, :, m * LANES:(m + 1) * LANES] = xp[:, :LANES]
        x_sc[1, :, m * LANES:(m + 1) * LANES] = xp[:, LANES:]

    ar = a_ref[0, 0:1, :]
    ai = a_ref[0, 1:2, :]

    def step(j, carry):
        sr, si = carry
        r8 = pl.ds(pl.multiple_of(j * SUBLANES, SUBLANES), SUBLANES)
        sp_sc[0, r8, :] = sr
        sp_sc[1, r8, :] = si
        return (ar * sr - ai * si + x_sc[0, r8, :], ar * si + ai * sr + x_sc[1, r8, :])

    sr, si = lax.fori_loop(0, jt, step, (s_sc[0], s_sc[1]))
    s_sc[0] = sr
    s_sc[1] = si

    for m in range(PAIRS_PER_TILE):
        sp = jnp.concatenate([sp_sc[0, :, m * LANES:(m + 1) * LANES],
                              sp_sc[1, :, m * LANES:(m + 1) * LANES]], axis=1).astype(BF16)
        yc = jnp.dot(sp, wout_ref[0, m], preferred_element_type=F32)
        y[2 * m] = y[2 * m] + yc[:, :CHUNK_COLS]
        y[2 * m + 1] = y[2 * m + 1] + yc[:, CHUNK_COLS:]

    for g in range(GROUPS_PER_TILE):
        y_ref[0, g] = y[g] + d_ref[0, g] * u_ref[0, g]


def _chunk_slabs(tiles):
    z_lo = _block_transpose8(tiles[:8])
    z_hi = _block_transpose8(tiles[8:])
    return [jnp.concatenate([z_lo[g], z_hi[g]], axis=1) for g in range(GROUPS_PER_TILE)]


def _chunk_tiles(slabs):
    out = []
    for half in range(2):
        out += _block_transpose8([s[:, half * LANES:(half + 1) * LANES] for s in slabs])
    return out


def _ssm(u_slab, tables, d_skip, B, L, lt=1024):
    toep, win, wout, a16 = tables
    jt = lt // CHUNK
    n_l = L // lt
    rows = jt * B
    slab = pl.BlockSpec((1, GROUPS_PER_TILE, rows, CHUNK_COLS), lambda q, l: (q, 0, l, 0))
    d_slab = jnp.tile(d_skip.reshape(LANE_TILES, GROUPS_PER_TILE, 1, SSM_GROUP), (1, 1, 1, CHUNK))
    return pl.pallas_call(
        functools.partial(_ssm_kernel, jt=jt),
        grid=(LANE_TILES, n_l),
        in_specs=[
            slab,
            pl.BlockSpec((1, GROUPS_PER_TILE, CHUNK_COLS, CHUNK_COLS), lambda q, l: (q, 0, 0, 0)),
            pl.BlockSpec((1, PAIRS_PER_TILE, 2 * CHUNK_COLS, 2 * LANES), lambda q, l: (q, 0, 0, 0)),
            pl.BlockSpec((1, PAIRS_PER_TILE, 2 * LANES, 2 * CHUNK_COLS), lambda q, l: (q, 0, 0, 0)),
            pl.BlockSpec((1, 2, STATE_COLS), lambda q, l: (q, 0, 0)),
            pl.BlockSpec((1, GROUPS_PER_TILE, 1, CHUNK_COLS), lambda q, l: (q, 0, 0, 0)),
        ],
        out_specs=slab,
        out_shape=jax.ShapeDtypeStruct(u_slab.shape, F32),
        scratch_shapes=[
            pltpu.VMEM((2, B, STATE_COLS), F32),
            pltpu.VMEM((2, rows, STATE_COLS), F32),
            pltpu.VMEM((2, rows, STATE_COLS), F32),
        ],
        compiler_params=_cparams(("parallel", "arbitrary")),
        name="ssm",
    )(u_slab, toep, win, wout, a16, d_slab)


def _cmul(a, b):
    return a[0] * b[0] - a[1] * b[1], a[0] * b[1] + a[1] * b[0]


def _zoh(lr, li, ldt):
    dt = jnp.exp(ldt)
    mag = jnp.exp(lr * dt)
    ab = (mag * jnp.cos(li * dt), mag * jnp.sin(li * dt))
    den = lr * lr + li * li
    zr, zi = ab[0] - 1.0, ab[1]
    return ab, ((zr * lr + zi * li) / den, (zi * lr - zr * li) / den)


def _cpow(base, exponent, bits):
    out = None
    for k in range(bits):
        on = (exponent & (1 << k)) != 0
        sel = (jnp.where(on, base[0], 1.0), jnp.where(on, base[1], 0.0))
        out = sel if out is None else _cmul(out, sel)
        if k + 1 < bits:
            base = _cmul(base, base)
    return out


def _ssm_prep_kernel(row_ref, ct_ref, bt_ref, toep_ref, win_ref, wout_ref, a_ref):
    bits = CHUNK.bit_length() - 1
    lane = lax.broadcasted_iota(jnp.int32, (1, CHUNK_COLS), 1)
    tau = lane // SSM_GROUP
    rowi = lax.broadcasted_iota(jnp.int32, (CHUNK_COLS, 1), 0)
    back = (CHUNK - 1) - rowi // SSM_GROUP
    lane2 = lax.broadcasted_iota(jnp.int32, (1, LANES), 1)
    exact = functools.partial(jnp.dot, precision=lax.Precision.HIGHEST, preferred_element_type=F32)
    zoh = [_zoh(row_ref[0, m], row_ref[1, m], row_ref[2, m]) for m in range(PAIRS_PER_TILE)]
    pad = jnp.zeros((SUBLANES - PAIRS_PER_TILE, LANES), F32)
    ab_cols = tuple(jnp.concatenate([z[0][c] for z in zoh] + [pad], axis=0).T for c in range(2))
    a_rows = []
    for m in range(PAIRS_PER_TILE):
        abp, fp = zoh[m]
        pwr = _cpow(abp, back, bits)
        a16 = abp
        for _ in range(bits):
            a16 = _cmul(a16, a16)
        a_rows.append(a16)
        wout_rows = [[None, None], [None, None]]
        for j in range(2):
            g = 2 * m + j
            states = slice(j * SSM_STATE, (j + 1) * SSM_STATE)
            ab = tuple(v[states, m:m + 1] for v in ab_cols)
            pw = _cpow(ab, tau, bits)
            g0 = _cmul(pw, (ct_ref[0, g], ct_ref[1, g]))
            g1 = _cmul(g0, ab)
            mine = (lane2 < SSM_STATE) if j == 0 else (lane2 >= SSM_STATE)
            bb2 = _cmul(fp, (bt_ref[0, g], bt_ref[1, g]))
            bb2 = tuple(jnp.where(mine, v, 0.0) for v in bb2)
            none = jnp.zeros_like(g0[0])
            stack = (lambda v: jnp.concatenate([v, none], axis=0)) if j == 0 else \
                    (lambda v: jnp.concatenate([none, v], axis=0))
            kt = exact(bb2[0], stack(g0[0])) - exact(bb2[1], stack(g0[1]))
            toep_ref[0, g] = jnp.concatenate(
                [jnp.where(lane >= SSM_GROUP * r, pltpu.roll(kt, SSM_GROUP * r, 1), 0.0) if r else kt
                 for r in range(CHUNK)], axis=0).astype(BF16)
            w = _cmul(pwr, tuple(jnp.concatenate([v] * CHUNK, axis=0) for v in bb2))
            win_ref[0, m, j * CHUNK_COLS:(j + 1) * CHUNK_COLS, :] = jnp.concatenate(w, axis=1).astype(BF16)
            wout_rows[0][j] = g1[0]
            wout_rows[1][j] = -g1[1]
        zero = jnp.zeros((SSM_STATE, CHUNK_COLS), F32)
        wout_ref[0, m] = jnp.concatenate([
            jnp.concatenate([wout_rows[0][0], zero], axis=1),
            jnp.concatenate([zero, wout_rows[0][1]], axis=1),
            jnp.concatenate([wout_rows[1][0], zero], axis=1),
            jnp.concatenate([zero, wout_rows[1][1]], axis=1)], axis=0).astype(BF16)
    a_ref[0] = jnp.concatenate([jnp.concatenate([a[0] for a in a_rows], axis=1),
                                jnp.concatenate([a[1] for a in a_rows], axis=1)], axis=0)


def _ssm_prep(lam_re, lam_im, log_dt, b_re, b_im, c_re, c_im):
    G, P, H = b_re.shape
    ldt = jnp.broadcast_to(log_dt[:, None], (G, P))
    rows = jnp.stack([lam_re, lam_im, ldt]).reshape(3, G // 2, 1, 2 * P)
    ct = jnp.tile(jnp.stack([c_re, c_im]).transpose(0, 1, 3, 2), (1, 1, 1, CHUNK))
    bt = jnp.tile(jnp.stack([b_re, b_im]).transpose(0, 1, 3, 2), (1, 1, 1, 2))
    gt = GROUPS_PER_TILE
    return pl.pallas_call(
        _ssm_prep_kernel,
        grid=(LANE_TILES,),
        in_specs=[
            pl.BlockSpec((3, PAIRS_PER_TILE, 1, 2 * P), lambda q: (0, q, 0, 0)),
            pl.BlockSpec((2, gt, P, CHUNK_COLS), lambda q: (0, q, 0, 0)),
            pl.BlockSpec((2, gt, H, 2 * P), lambda q: (0, q, 0, 0)),
        ],
        out_specs=[
            pl.BlockSpec((1, gt, CHUNK_COLS, CHUNK_COLS), lambda q: (q, 0, 0, 0)),
            pl.BlockSpec((1, PAIRS_PER_TILE, 2 * CHUNK_COLS, 2 * LANES), lambda q: (q, 0, 0, 0)),
            pl.BlockSpec((1, PAIRS_PER_TILE, 2 * LANES, 2 * CHUNK_COLS), lambda q: (q, 0, 0, 0)),
            pl.BlockSpec((1, 2, STATE_COLS), lambda q: (q, 0, 0)),
        ],
        out_shape=[
            jax.ShapeDtypeStruct((LANE_TILES, gt, CHUNK_COLS, CHUNK_COLS), BF16),
            jax.ShapeDtypeStruct((LANE_TILES, PAIRS_PER_TILE, 2 * CHUNK_COLS, 2 * LANES), BF16),
            jax.ShapeDtypeStruct((LANE_TILES, PAIRS_PER_TILE, 2 * LANES, 2 * CHUNK_COLS), BF16),
            jax.ShapeDtypeStruct((LANE_TILES, 2, STATE_COLS), F32),
        ],
        compiler_params=_cparams(("parallel",)),
        name="ssm_prep",
    )(rows, ct, bt)


def _layer_norm(r, g, b):
    mu = jnp.mean(r, axis=-1, keepdims=True)
    c = r - mu
    var = jnp.mean(c * c, axis=-1, keepdims=True)
    return c * lax.rsqrt(var + LN_EPS) * g + b


MIX_ROWS = 256


def _gelu_tanh(y):
    return 0.5 * y * (1.0 + jnp.tanh(math.sqrt(2.0 / math.pi) * (y + 0.044715 * (y * y * y))))


def _mix_ln1_kernel(x_ref, attn_ref, y_ref, wglu_ref, bglu_ref, wo_ref, g_ref, b_ref, o_ref, y_sc):
    nb, tl, d = x_ref.shape
    pos = MIX_ROWS // nb
    n_pieces = tl // pos
    jp = pos // CHUNK

    def relayout(n):
        rows = slice(n * jp * nb, (n + 1) * jp * nb)
        for q in range(LANE_TILES):
            tiles = _chunk_tiles([y_ref[q, g, rows] for g in range(GROUPS_PER_TILE)])
            y_sc[q, n * MIX_ROWS:(n + 1) * MIX_ROWS] = jnp.stack(
                [t.reshape(jp, nb, LANES) for t in tiles], axis=1).reshape(MIX_ROWS, LANES)

    def glu(n):
        y = jnp.concatenate(
            [jnp.concatenate([y_sc[q, pl.ds(n * MIX_ROWS + b, pos, stride=nb), :] for q in range(LANE_TILES)], axis=1)
             for b in range(nb)], axis=0)
        gl = _gelu_tanh(y)
        z = gl * jax.nn.sigmoid(jnp.dot(gl.astype(BF16), wglu_ref[...], preferred_element_type=F32) + bglu_ref[...])
        return z.astype(BF16)

    def project(n, z):
        ls = slice(n * pos, (n + 1) * pos)
        mix = (jnp.dot(attn_ref[:, ls].reshape(MIX_ROWS, ATT_WIDTH), wo_ref[:ATT_WIDTH], preferred_element_type=F32)
               + jnp.dot(z, wo_ref[ATT_WIDTH:], preferred_element_type=F32))
        res = _layer_norm(ALPHA * x_ref[:, ls].reshape(MIX_ROWS, d) + mix, g_ref[...], b_ref[...])
        o_ref[:, ls] = res.reshape(nb, pos, d)

    relayout(0)
    z = glu(0)
    for n in range(n_pieces):
        if n + 1 < n_pieces:
            relayout(n + 1)
            z_next = glu(n + 1)
        project(n, z)
        z = z_next


def _mix_ln1(x, attn, y_tm, wglu_bf, b_glu, wo_bf, ln_g, ln_b, tl=128):
    B, L, D = x.shape
    assert MIX_ROWS % (B * CHUNK) == 0 and tl % (MIX_ROWS // B) == 0
    resident = lambda shape: pl.BlockSpec(shape, lambda l: (0, 0), pipeline_mode=pl.Buffered(1))
    return pl.pallas_call(
        _mix_ln1_kernel,
        grid=(L // tl,),
        in_specs=[
            pl.BlockSpec((B, tl, D), lambda l: (0, l, 0)),
            pl.BlockSpec((B, tl, ATT_WIDTH), lambda l: (0, l, 0)),
            pl.BlockSpec((LANE_TILES, GROUPS_PER_TILE, tl // CHUNK * B, CHUNK_COLS), lambda l: (0, 0, l, 0)),
            resident((SSM_WIDTH, SSM_WIDTH)),
            resident((1, SSM_WIDTH)),
            resident((D, D)),
            resident((1, D)),
            resident((1, D)),
        ],
        out_specs=pl.BlockSpec((B, tl, D), lambda l: (0, l, 0)),
        out_shape=jax.ShapeDtypeStruct((B, L, D), F32),
        scratch_shapes=[pltpu.VMEM((LANE_TILES, tl * B, LANES), F32)],
        compiler_params=_cparams(("parallel",)),
        name="mix_ln1",
    )(x, attn, y_tm, wglu_bf, b_glu, wo_bf, ln_g, ln_b)


HALO = 16
FFN_SUB = 256
FFN_ROWS = 128


def _ffn_kernel(x_ref, xh_ref, p_ref, wple_ref, wgate_ref, wup_ref, cw_ref, cb_ref, wd_ref, g_ref, b_ref, o_ref,
                x32_sc, xh32_sc, p32_sc, xb_sc, acc_sc, hg_sc, hv_sc, act_sc, *, tl):
    nb = x_ref.shape[0]
    tm = tl * nb
    d_slabs = x_ref.shape[2] // LANES
    hp = (CONV_WIDTH - 1)
    for bb in range(nb):
        for s in range(d_slabs):
            lanes = slice(s * LANES, (s + 1) * LANES)
            x32_sc[s, pl.ds(bb, tl, stride=nb), :] = x_ref[bb, :, lanes]
            xh32_sc[s, pl.ds(bb, hp, stride=nb), :] = xh_ref[bb, SUBLANES - hp:, lanes]
        for s in range(p_ref.shape[2] // LANES):
            p32_sc[s, pl.ds(bb, tl, stride=nb), :] = p_ref[bb, :, s * LANES:(s + 1) * LANES]
    x_tm = jnp.concatenate([x32_sc[s] for s in range(d_slabs)], axis=1)
    p_tm = jnp.concatenate([p32_sc[s] for s in range(p32_sc.shape[0])], axis=1)
    keep = jnp.where(pl.program_id(0) == 0, 0.0, 1.0)
    xb_sc[0:HALO] = (jnp.concatenate([xh32_sc[s] for s in range(d_slabs)], axis=1) * keep).astype(BF16)
    xb_sc[HALO:] = x_tm.astype(BF16)
    gate = jax.nn.sigmoid(jnp.dot(xb_sc[HALO:], wgate_ref[...], preferred_element_type=F32))
    ple = jnp.dot(p_tm.astype(BF16), wple_ref[...], preferred_element_type=F32) * gate
    acc_sc[...] = ALPHA * x_tm + ple

    def conv(h_sc, slot, cols, r0, rows):
        out = cb_ref[:, cols] + cw_ref[CONV_WIDTH - 1:CONV_WIDTH, cols] * h_sc[slot, HALO + r0:HALO + r0 + rows]
        for j in range(CONV_WIDTH - 1):
            back = (CONV_WIDTH - 1 - j) * nb
            out = out + cw_ref[j:j + 1, cols] * h_sc[slot, HALO + r0 - back:HALO + r0 - back + rows]
        return out

    n_sub = D_FF // FFN_SUB
    g_cols = lambda n: slice(n * FFN_SUB, (n + 1) * FFN_SUB)
    v_cols = lambda n: slice(D_FF + n * FFN_SUB, D_FF + (n + 1) * FFN_SUB)

    def up(n):
        xb = xb_sc[...]
        hg_sc[n % 2] = jnp.dot(xb, wup_ref[:, g_cols(n)], preferred_element_type=F32)
        hv_sc[n % 2] = jnp.dot(xb, wup_ref[:, v_cols(n)], preferred_element_type=F32)

    def gate_block(n):
        for r0 in range(0, tm, FFN_ROWS):
            gc = conv(hg_sc, n % 2, g_cols(n), r0, FFN_ROWS)
            vc = conv(hv_sc, n % 2, v_cols(n), r0, FFN_ROWS)
            act_sc[r0:r0 + FFN_ROWS, g_cols(n)] = (gc * jax.nn.sigmoid(gc) * vc).astype(BF16)

    def down(lo, hi):
        cols = slice(lo * FFN_SUB, hi * FFN_SUB)
        return jnp.dot(act_sc[:, cols], wd_ref[cols, :], preferred_element_type=F32)

    half = (n_sub + 1) // 2
    up(0)
    for n in range(n_sub):
        if n + 1 < n_sub:
            up(n + 1)
        gate_block(n)
        if n + 1 == half:
            acc_sc[...] += down(0, half)

    res = _layer_norm(acc_sc[...] + down(half, n_sub), g_ref[...], b_ref[...])
    for s in range(d_slabs):
        x32_sc[s] = res[:, s * LANES:(s + 1) * LANES]
    for bb in range(nb):
        for s in range(d_slabs):
            o_ref[bb, :, s * LANES:(s + 1) * LANES] = x32_sc[s, pl.ds(bb, tl, stride=nb), :]


def _ffn_ln2(x1, p, wple_bf, wgate_bf, wup_bf, conv_w, conv_b, wdown_bf, ln_g, ln_b, tl=64):
    B, L, D = x1.shape
    tm = tl * B
    assert D_FF % FFN_SUB == 0 and HALO == (CONV_WIDTH - 1) * B and tm % FFN_ROWS == 0
    const = lambda i: (0, 0)
    resident = lambda shape: pl.BlockSpec(shape, const, pipeline_mode=pl.Buffered(1))
    return pl.pallas_call(
        functools.partial(_ffn_kernel, tl=tl),
        grid=(L // tl,),
        in_specs=[
            pl.BlockSpec((B, tl, D), lambda i: (0, i, 0)),
            pl.BlockSpec((B, SUBLANES, D), lambda i: (0, jnp.maximum(i * (tl // SUBLANES) - 1, 0), 0)),
            pl.BlockSpec((B, tl, PLE_DIM), lambda i: (0, i, 0)),
            resident((PLE_DIM, D)),
            resident((D, D)),
            resident((D, 2 * D_FF)),
            resident((CONV_WIDTH, 2 * D_FF)),
            resident((1, 2 * D_FF)),
            resident((D_FF, D)),
            resident((1, D)),
            resident((1, D)),
        ],
        out_specs=pl.BlockSpec((B, tl, D), lambda i: (0, i, 0)),
        out_shape=jax.ShapeDtypeStruct((B, L, D), F32),
        scratch_shapes=[
            pltpu.VMEM((D // LANES, tm, LANES), F32),
            pltpu.VMEM((D // LANES, HALO, LANES), F32),
            pltpu.VMEM((PLE_DIM // LANES, tm, LANES), F32),
            pltpu.VMEM((tm + HALO, D), BF16),
            pltpu.VMEM((tm, D), F32),
            pltpu.VMEM((2, tm + HALO, FFN_SUB), F32),
            pltpu.VMEM((2, tm + HALO, FFN_SUB), F32),
            pltpu.VMEM((tm, D_FF), BF16),
        ],
        compiler_params=_cparams(("parallel",)),
        name="ffn_ln2",
    )(x1, x1, p, wple_bf, wgate_bf, wup_bf, conv_w, conv_b, wdown_bf, ln_g, ln_b)


def kernel(x, p, w_in, diff_lambda_q1, diff_lambda_k1, diff_lambda_q2, diff_lambda_k2, diff_subln_g, ssm_lambda_re, ssm_lambda_im, ssm_log_dt, ssm_b_re, ssm_b_im, ssm_c_re, ssm_c_im, ssm_d, ssm_w_glu, ssm_b_glu, w_o, ln1_g, ln1_b, ffn_w_up, ffn_conv_w, ffn_conv_b, ffn_w_down, w_ple, w_ple_gate, ln2_g, ln2_b):
    B, L, D = x.shape
    assert B == SUBLANES and D == D_MODEL and L % 512 == 0
    for i in range(DEPTH):
        lam_init = 0.8 - 0.6 * math.exp(-0.3 * i)
        row = lambda a: a[i].reshape(1, -1)
        qt, k, vt, u_tm = _in_proj(x, w_in[i])
        attn = _diff_attn_flat(qt, k, vt, row(diff_lambda_q1), row(diff_lambda_k1), row(diff_lambda_q2),
                          row(diff_lambda_k2), diff_subln_g[i], lam_init)
        tables = _ssm_prep(ssm_lambda_re[i], ssm_lambda_im[i], ssm_log_dt[i], ssm_b_re[i], ssm_b_im[i],
                           ssm_c_re[i], ssm_c_im[i])
        y_tm = _ssm(u_tm, tables, ssm_d[i], B, L)
        x1 = _mix_ln1(x, attn, y_tm, ssm_w_glu[i].astype(BF16), row(ssm_b_glu), w_o[i].astype(BF16),
                      row(ln1_g), row(ln1_b))
        x = _ffn_ln2(x1, p[i], w_ple[i].astype(BF16), w_ple_gate[i].astype(BF16), ffn_w_up[i].astype(BF16),
                     ffn_conv_w[i], row(ffn_conv_b), ffn_w_down[i].astype(BF16), row(ln2_g), row(ln2_b))
    return x
```

```python
import functools
import math

import jax
import jax.numpy as jnp
from jax import lax
from jax.experimental import pallas as pl
from jax.experimental.pallas import tpu as pltpu

F32 = jnp.float32
BF16 = jnp.bfloat16

SUBLANES = 8
LANES = 128

DEPTH = 1
D_MODEL = 1024
PLE_DIM = 256
HEADS = 4
QK_DIM = 64
V_DIM = 2 * QK_DIM
QK_WIDTH = HEADS * 2 * QK_DIM
ATT_WIDTH = HEADS * V_DIM
SSM_WIDTH = D_MODEL - ATT_WIDTH
SSM_GROUP = 16
SSM_GROUPS = SSM_WIDTH // SSM_GROUP
SSM_STATE = 64
D_FF = 2816
CONV_WIDTH = 3
LN_EPS = 1e-5
ALPHA = (2 * DEPTH) ** 0.25
QK_SCALE = QK_DIM ** -0.5 * math.log2(math.e)

CHUNK = 16
CHUNK_COLS = CHUNK * SSM_GROUP
GROUPS_PER_TILE = LANES // SSM_GROUP
LANE_TILES = SSM_WIDTH // LANES
PAIRS_PER_TILE = GROUPS_PER_TILE // 2
STATE_COLS = GROUPS_PER_TILE * SSM_STATE

VMEM_LIMIT = 56 * 1024 * 1024


def _cparams(sem):
    return pltpu.CompilerParams(dimension_semantics=sem, vmem_limit_bytes=VMEM_LIMIT)


TB = 256
VT_ROWS = V_DIM + 16


def _in_proj_kernel(x_ref, w_ref, qt_ref, k_ref, vt_ref, u_ref, u_sc):
    wq_ref = w_ref.at[:, :QK_WIDTH]
    wk_ref = w_ref.at[:, QK_WIDTH:2 * QK_WIDTH]
    wv_ref = w_ref.at[:, 2 * QK_WIDTH:2 * QK_WIDTH + ATT_WIDTH]
    wu_ref = w_ref.at[:, 2 * QK_WIDTH + ATT_WIDTH:]
    nb, tl, d = x_ref.shape
    jt = tl // CHUNK
    xb = x_ref[...].reshape(nb * tl, d).astype(BF16)
    hu = jnp.dot(xb, wu_ref[...], preferred_element_type=F32)
    for b in range(nb):
        for q in range(LANE_TILES):
            u_sc[q, pl.ds(b, tl, stride=nb), :] = hu[b * tl:(b + 1) * tl, q * LANES:(q + 1) * LANES]
    for q in range(LANE_TILES):
        tiles = u_sc[q].reshape(jt, CHUNK, nb, LANES)
        slabs = _chunk_slabs([tiles[:, r].reshape(jt * nb, LANES) for r in range(CHUNK)])
        for g in range(GROUPS_PER_TILE):
            u_ref[q, g] = slabs[g]
    tn = (((0,), (1,)), ((), ()))
    qt = lax.dot_general(wq_ref[...], xb, tn, preferred_element_type=F32) * QK_SCALE
    vt = lax.dot_general(wv_ref[...], xb, tn, preferred_element_type=F32)
    pad_row = lax.broadcasted_iota(jnp.int32, (VT_ROWS - V_DIM, TB), 0)
    ones_row = jnp.where(pad_row == 0, 1.0, 0.0).astype(BF16)
    for b in range(nb):
        for h in range(HEADS):
            for c in range(tl // TB):
                cols = slice(b * tl + c * TB, b * tl + (c + 1) * TB)
                qt_ref[b, h, c] = qt[h * LANES:(h + 1) * LANES, cols].astype(BF16)
                vt_ref[b, h, c, :V_DIM] = vt[h * LANES:(h + 1) * LANES, cols].astype(BF16)
                vt_ref[b, h, c, V_DIM:] = ones_row
    k_ref[...] = jnp.dot(xb, wk_ref[...], preferred_element_type=F32).astype(BF16).reshape(k_ref.shape)


def _in_proj(x, w_in, tl=256):
    B, L, D = x.shape
    slab = lambda rows: pl.BlockSpec((B, HEADS, tl // TB, rows, TB), lambda l: (0, 0, l, 0, 0))
    slab_shape = lambda rows: jax.ShapeDtypeStruct((B, HEADS, L // TB, rows, TB), BF16)
    return pl.pallas_call(
        _in_proj_kernel,
        grid=(L // tl,),
        in_specs=[
            pl.BlockSpec((B, tl, D), lambda l: (0, l, 0)),
            pl.BlockSpec(w_in.shape, lambda l: (0, 0), pipeline_mode=pl.Buffered(1)),
        ],
        out_specs=[
            slab(LANES),
            pl.BlockSpec((B, tl, QK_WIDTH), lambda l: (0, l, 0)),
            slab(VT_ROWS),
            pl.BlockSpec((LANE_TILES, GROUPS_PER_TILE, tl // CHUNK * B, CHUNK_COLS), lambda l: (0, 0, l, 0)),
        ],
        out_shape=[
            slab_shape(LANES),
            jax.ShapeDtypeStruct((B, L, QK_WIDTH), BF16),
            slab_shape(VT_ROWS),
            jax.ShapeDtypeStruct((LANE_TILES, GROUPS_PER_TILE, L // CHUNK * B, CHUNK_COLS), F32),
        ],
        scratch_shapes=[pltpu.VMEM((LANE_TILES, tl * B, LANES), F32)],
        compiler_params=_cparams(("parallel",)),
        name="in_proj",
    )(x, w_in.astype(BF16))


def _attn_flat_kernel(lq1_ref, lk1_ref, lq2_ref, lk2_ref, g_ref, qt_ref, k_ref, vt_ref, o_ref,
                      acc_sc, qst_sc, s_sc, p_sc, bm_sc, st_sc, al_sc, *, lam_init):
    n_q = qt_ref.shape[2]
    n_pairs = n_q * (n_q + 1) // 2
    assert TB == 2 * LANES and n_pairs % 2 == 0
    heads = range(HEADS)
    n_strips = 2 * TB // LANES
    lam = (jnp.exp(jnp.sum(lq1_ref[...] * lk1_ref[...], axis=-1, keepdims=True))
           - jnp.exp(jnp.sum(lq2_ref[...] * lk2_ref[...], axis=-1, keepdims=True)) + lam_init)

    def prepare(i, c):
        for hh in heads:
            qt = qt_ref[0, hh, i]
            row = lax.broadcasted_iota(jnp.int32, qt.shape, 0)
            zero = jnp.zeros_like(qt)
            qst_sc[i, hh, :, :TB] = jnp.where(row < QK_DIM, qt, zero)
            qst_sc[i, hh, :, TB:] = jnp.where(row >= QK_DIM, qt, zero)
            acc_sc[i, hh] = jnp.zeros(acc_sc.shape[2:], F32)
            st_sc[i, hh] = jnp.full((1, 2 * TB), -jnp.inf, F32)
        return c

    lax.fori_loop(0, n_q, prepare, 0)
    p_sc[1] = jnp.zeros(p_sc.shape[1:], BF16)
    al_sc[1] = jnp.ones(al_sc.shape[1:], F32)

    def diag_place(i):
        r = i % 4
        return jnp.where(jnp.logical_or(r == 1, r == 2), 1, 0)

    def is_diag(pair):
        return pair[1] == diag_place(pair[0])

    def block_of(pair):
        i, k = pair
        d = diag_place(i)
        return jnp.where(k == d, i, jnp.where(k < d, k, k - 1))

    def scores(pair, slot):
        i, j = pair[0], block_of(pair)
        rows = pl.ds(pl.multiple_of(j * TB, TB), TB)
        for hh in heads:
            kb = k_ref[0, rows, hh * LANES:(hh + 1) * LANES]
            s = jnp.dot(kb, qst_sc[i, hh], preferred_element_type=F32)
            s_sc[slot, hh] = s
            bm_sc[slot, hh] = jnp.max(s, axis=0, keepdims=True)

    def values(pair, slot):
        i, j = pair[0], block_of(pair)
        for hh in heads:
            pv = jnp.dot(vt_ref[0, hh, j], p_sc[slot, hh], preferred_element_type=F32)
            acc_sc[i, hh] = al_sc[slot, hh] * acc_sc[i, hh] + pv

    def softmax(pair, slot, diagonal):
        i, _ = pair
        for hh in heads:
            for c in range(n_strips):
                lanes = slice(c * LANES, (c + 1) * LANES)
                m = st_sc[i, hh, :, lanes]
                if not diagonal:
                    mc = jnp.maximum(m, bm_sc[slot, hh, :, lanes])
                    p = jnp.exp2(s_sc[slot, hh, :, lanes] - mc)
                else:
                    s = s_sc[slot, hh, :, lanes]
                    key = lax.broadcasted_iota(jnp.int32, s.shape, 0)
                    qry = lax.broadcasted_iota(jnp.int32, s.shape, 1) + (c * LANES) % TB
                    s = jnp.where(key <= qry, s, -jnp.inf)
                    mc = jnp.maximum(m, jnp.max(s, axis=0, keepdims=True))
                    p = jnp.exp2(s - mc)
                p_sc[slot, hh, :, lanes] = p.astype(BF16)
                st_sc[i, hh, :, lanes] = mc
                al_sc[slot, hh, :, lanes] = jnp.exp2(m - mc)

    def succ(pair):
        i, k = pair
        last = k == i
        i2 = jnp.minimum(jnp.where(last, i + 1, i), n_q - 1)
        return i2, jnp.where(last, 0, k + 1)

    def two_steps(t, carry):
        prev, cur = carry
        nxt = succ(cur)
        nxt2 = succ(nxt)

        def body(diag0):
            scores(nxt, 1)
            values(prev, 1)
            softmax(cur, 0, diag0)
            scores(nxt2, 0)
            values(cur, 0)
            softmax(nxt, 1, False)

        d0 = is_diag(cur)
        pl.when(d0)(lambda: body(True))
        pl.when(jnp.logical_not(d0))(lambda: body(False))
        return nxt, nxt2

    zero = jnp.int32(0)
    first = (zero, zero)
    scores(first, 0)
    last, _ = lax.fori_loop(0, n_pairs // 2, two_steps, (first, first))
    values(last, 1)

    def finish(i, c):
        for hh in heads:
            o = acc_sc[i, hh, :V_DIM] * (1.0 / acc_sc[i, hh, V_DIM:V_DIM + 1])
            o = o[:, :TB] - lam * o[:, TB:]
            ms = jnp.mean(o * o, axis=0, keepdims=True)
            o = o * lax.rsqrt(ms + LN_EPS) * (g_ref[...] * (1.0 - lam_init))
            o_ref[0, pl.ds(pl.multiple_of(i * TB, TB), TB), hh * LANES:(hh + 1) * LANES] = o.T.astype(o_ref.dtype)
        return c

    lax.fori_loop(0, n_q, finish, 0)


def _diff_attn_flat(qt, k, vt, lq1, lk1, lq2, lk2, subln_g, lam_init):
    B, L, _ = k.shape
    n_q = L // TB
    vec = pl.BlockSpec((1, QK_DIM), lambda b: (0, 0))
    return pl.pallas_call(
        functools.partial(_attn_flat_kernel, lam_init=lam_init),
        grid=(B,),
        in_specs=[
            vec, vec, vec, vec,
            pl.BlockSpec((V_DIM, 1), lambda b: (0, 0)),
            pl.BlockSpec((1, HEADS, n_q, LANES, TB), lambda b: (b, 0, 0, 0, 0)),
            pl.BlockSpec((1, L, HEADS * LANES), lambda b: (b, 0, 0)),
            pl.BlockSpec((1, HEADS, n_q, VT_ROWS, TB), lambda b: (b, 0, 0, 0, 0)),
        ],
        out_specs=pl.BlockSpec((1, L, HEADS * LANES), lambda b: (b, 0, 0)),
        out_shape=jax.ShapeDtypeStruct((B, L, ATT_WIDTH), BF16),
        scratch_shapes=[
            pltpu.VMEM((n_q, HEADS, VT_ROWS, 2 * TB), F32),
            pltpu.VMEM((n_q, HEADS, LANES, 2 * TB), BF16),
            pltpu.VMEM((2, HEADS, TB, 2 * TB), F32),
            pltpu.VMEM((2, HEADS, TB, 2 * TB), BF16),
            pltpu.VMEM((2, HEADS, 1, 2 * TB), F32),
            pltpu.VMEM((n_q, HEADS, 1, 2 * TB), F32),
            pltpu.VMEM((2, HEADS, 1, 2 * TB), F32),
        ],
        compiler_params=_cparams(("parallel",)),
        name="diff_attn",
    )(lq1, lk1, lq2, lk2, subln_g.reshape(V_DIM, 1), qt, k, vt)


def _block_transpose8(vs):
    lane_blk = lax.broadcasted_iota(jnp.int32, vs[0].shape, 1) // SSM_GROUP
    for d in (4, 2, 1):
        keep = (lane_blk & d) == 0
        new = list(vs)
        for a in range(8):
            if a & d == 0:
                lo, hi = vs[a], vs[a + d]
                new[a] = jnp.where(keep, lo, pltpu.roll(hi, SSM_GROUP * d, 1))
                new[a + d] = jnp.where(keep, pltpu.roll(lo, LANES - SSM_GROUP * d, 1), hi)
        vs = new
    return vs


def _ssm_kernel(u_ref, toep_ref, win_ref, wout_ref, a_ref, d_ref, y_ref, s_sc, x_sc, sp_sc, *, jt):
    rows = jt * SUBLANES

    @pl.when(pl.program_id(1) == 0)
    def _():
        s_sc[...] = jnp.zeros(s_sc.shape, F32)

    ug = [u_ref[0, g].astype(BF16) for g in range(GROUPS_PER_TILE)]

    y = [jnp.dot(ug[g], toep_ref[0, g], preferred_element_type=F32) for g in range(GROUPS_PER_TILE)]

    for m in range(PAIRS_PER_TILE):
        up = jnp.concatenate([ug[2 * m], ug[2 * m + 1]], axis=1)
        xp = jnp.dot(up, win_ref[0, m], preferred_element_type=F32)
        x_sc[0, :, m * LANES:(m + 1) * LANES] = xp[:, :LANES]
        x_sc[1, :, m * LANES:(m + 1) * LANES] = xp[:, LANES:]

    ar = a_ref[0, 0:1, :]
    ai = a_ref[0, 1:2, :]

    def step(j, carry):
        sr, si = carry
        r8 = pl.ds(pl.multiple_of(j * SUBLANES, SUBLANES), SUBLANES)
        sp_sc[0, r8, :] = sr
        sp_sc[1, r8, :] = si
        return (ar * sr - ai * si + x_sc[0, r8, :], ar * si + ai * sr + x_sc[1, r8, :])

    sr, si = lax.fori_loop(0, jt, step, (s_sc[0], s_sc[1]))
    s_sc[0] = sr
    s_sc[1] = si

    for m in range(PAIRS_PER_TILE):
        sp = jnp.concatenate([sp_sc[0, :, m * LANES:(m + 1) * LANES],
                              sp_sc[1, :, m * LANES:(m + 1) * LANES]], axis=1).astype(BF16)
        yc = jnp.dot(sp, wout_ref[0, m], preferred_element_type=F32)
        y[2 * m] = y[2 * m] + yc[:, :CHUNK_COLS]
        y[2 * m + 1] = y[2 * m + 1] + yc[:, CHUNK_COLS:]

    for g in range(GROUPS_PER_TILE):
        y_ref[0, g] = y[g] + d_ref[0, g] * u_ref[0, g]


def _chunk_slabs(tiles):
    z_lo = _block_transpose8(tiles[:8])
    z_hi = _block_transpose8(tiles[8:])
    return [jnp.concatenate([z_lo[g], z_hi[g]], axis=1) for g in range(GROUPS_PER_TILE)]


def _chunk_tiles(slabs):
    out = []
    for half in range(2):
        out += _block_transpose8([s[:, half * LANES:(half + 1) * LANES] for s in slabs])
    return out


def _ssm(u_slab, tables, d_skip, B, L, lt=1024):
    toep, win, wout, a16 = tables
    jt = lt // CHUNK
    n_l = L // lt
    rows = jt * B
    slab = pl.BlockSpec((1, GROUPS_PER_TILE, rows, CHUNK_COLS), lambda q, l: (q, 0, l, 0))
    d_slab = jnp.tile(d_skip.reshape(LANE_TILES, GROUPS_PER_TILE, 1, SSM_GROUP), (1, 1, 1, CHUNK))
    return pl.pallas_call(
        functools.partial(_ssm_kernel, jt=jt),
        grid=(LANE_TILES, n_l),
        in_specs=[
            slab,
            pl.BlockSpec((1, GROUPS_PER_TILE, CHUNK_COLS, CHUNK_COLS), lambda q, l: (q, 0, 0, 0)),
            pl.BlockSpec((1, PAIRS_PER_TILE, 2 * CHUNK_COLS, 2 * LANES), lambda q, l: (q, 0, 0, 0)),
            pl.BlockSpec((1, PAIRS_PER_TILE, 2 * LANES, 2 * CHUNK_COLS), lambda q, l: (q, 0, 0, 0)),
            pl.BlockSpec((1, 2, STATE_COLS), lambda q, l: (q, 0, 0)),
            pl.BlockSpec((1, GROUPS_PER_TILE, 1, CHUNK_COLS), lambda q, l: (q, 0, 0, 0)),
        ],
        out_specs=slab,
        out_shape=jax.ShapeDtypeStruct(u_slab.shape, F32),
        scratch_shapes=[
            pltpu.VMEM((2, B, STATE_COLS), F32),
            pltpu.VMEM((2, rows, STATE_COLS), F32),
            pltpu.VMEM((2, rows, STATE_COLS), F32),
        ],
        compiler_params=_cparams(("parallel", "arbitrary")),
        name="ssm",
    )(u_slab, toep, win, wout, a16, d_slab)


def _cmul(a, b):
    return a[0] * b[0] - a[1] * b[1], a[0] * b[1] + a[1] * b[0]


def _zoh(lr, li, ldt):
    dt = jnp.exp(ldt)
    mag = jnp.exp(lr * dt)
    ab = (mag * jnp.cos(li * dt), mag * jnp.sin(li * dt))
    den = lr * lr + li * li
    zr, zi = ab[0] - 1.0, ab[1]
    return ab, ((zr * lr + zi * li) / den, (zi * lr - zr * li) / den)


def _cpow(base, exponent, bits):
    out = None
    for k in range(bits):
        on = (exponent & (1 << k)) != 0
        sel = (jnp.where(on, base[0], 1.0), jnp.where(on, base[1], 0.0))
        out = sel if out is None else _cmul(out, sel)
        if k + 1 < bits:
            base = _cmul(base, base)
    return out


def _ssm_prep_kernel(row_ref, ct_ref, bt_ref, toep_ref, win_ref, wout_ref, a_ref):
    bits = CHUNK.bit_length() - 1
    lane = lax.broadcasted_iota(jnp.int32, (1, CHUNK_COLS), 1)
    tau = lane // SSM_GROUP
    rowi = lax.broadcasted_iota(jnp.int32, (CHUNK_COLS, 1), 0)
    back = (CHUNK - 1) - rowi // SSM_GROUP
    lane2 = lax.broadcasted_iota(jnp.int32, (1, LANES), 1)
    exact = functools.partial(jnp.dot, precision=lax.Precision.HIGHEST, preferred_element_type=F32)
    zoh = [_zoh(row_ref[0, m], row_ref[1, m], row_ref[2, m]) for m in range(PAIRS_PER_TILE)]
    pad = jnp.zeros((SUBLANES - PAIRS_PER_TILE, LANES), F32)
    ab_cols = tuple(jnp.concatenate([z[0][c] for z in zoh] + [pad], axis=0).T for c in range(2))
    a_rows = []
    for m in range(PAIRS_PER_TILE):
        abp, fp = zoh[m]
        pwr = _cpow(abp, back, bits)
        a16 = abp
        for _ in range(bits):
            a16 = _cmul(a16, a16)
        a_rows.append(a16)
        wout_rows = [[None, None], [None, None]]
        for j in range(2):
            g = 2 * m + j
            states = slice(j * SSM_STATE, (j + 1) * SSM_STATE)
            ab = tuple(v[states, m:m + 1] for v in ab_cols)
            pw = _cpow(ab, tau, bits)
            g0 = _cmul(pw, (ct_ref[0, g], ct_ref[1, g]))
            g1 = _cmul(g0, ab)
            mine = (lane2 < SSM_STATE) if j == 0 else (lane2 >= SSM_STATE)
            bb2 = _cmul(fp, (bt_ref[0, g], bt_ref[1, g]))
            bb2 = tuple(jnp.where(mine, v, 0.0) for v in bb2)
            none = jnp.zeros_like(g0[0])
            stack = (lambda v: jnp.concatenate([v, none], axis=0)) if j == 0 else \
                    (lambda v: jnp.concatenate([none, v], axis=0))
            kt = exact(bb2[0], stack(g0[0])) - exact(bb2[1], stack(g0[1]))
            toep_ref[0, g] = jnp.concatenate(
                [jnp.where(lane >= SSM_GROUP * r, pltpu.roll(kt, SSM_GROUP * r, 1), 0.0) if r else kt
                 for r in range(CHUNK)], axis=0).astype(BF16)
            w = _cmul(pwr, tuple(jnp.concatenate([v] * CHUNK, axis=0) for v in bb2))
            win_ref[0, m, j * CHUNK_COLS:(j + 1) * CHUNK_COLS, :] = jnp.concatenate(w, axis=1).astype(BF16)
            wout_rows[0][j] = g1[0]
            wout_rows[1][j] = -g1[1]
        zero = jnp.zeros((SSM_STATE, CHUNK_COLS), F32)
        wout_ref[0, m] = jnp.concatenate([
            jnp.concatenate([wout_rows[0][0], zero], axis=1),
            jnp.concatenate([zero, wout_rows[0][1]], axis=1),
            jnp.concatenate([wout_rows[1][0], zero], axis=1),
            jnp.concatenate([zero, wout_rows[1][1]], axis=1)], axis=0).astype(BF16)
    a_ref[0] = jnp.concatenate([jnp.concatenate([a[0] for a in a_rows], axis=1),
                                jnp.concatenate([a[1] for a in a_rows], axis=1)], axis=0)


def _ssm_prep(lam_re, lam_im, log_dt, b_re, b_im, c_re, c_im):
    G, P, H = b_re.shape
    ldt = jnp.broadcast_to(log_dt[:, None], (G, P))
    rows = jnp.stack([lam_re, lam_im, ldt]).reshape(3, G // 2, 1, 2 * P)
    ct = jnp.tile(jnp.stack([c_re, c_im]).transpose(0, 1, 3, 2), (1, 1, 1, CHUNK))
    bt = jnp.tile(jnp.stack([b_re, b_im]).transpose(0, 1, 3, 2), (1, 1, 1, 2))
    gt = GROUPS_PER_TILE
    return pl.pallas_call(
        _ssm_prep_kernel,
        grid=(LANE_TILES,),
        in_specs=[
            pl.BlockSpec((3, PAIRS_PER_TILE, 1, 2 * P), lambda q: (0, q, 0, 0)),
            pl.BlockSpec((2, gt, P, CHUNK_COLS), lambda q: (0, q, 0, 0)),
            pl.BlockSpec((2, gt, H, 2 * P), lambda q: (0, q, 0, 0)),
        ],
        out_specs=[
            pl.BlockSpec((1, gt, CHUNK_COLS, CHUNK_COLS), lambda q: (q, 0, 0, 0)),
            pl.BlockSpec((1, PAIRS_PER_TILE, 2 * CHUNK_COLS, 2 * LANES), lambda q: (q, 0, 0, 0)),
            pl.BlockSpec((1, PAIRS_PER_TILE, 2 * LANES, 2 * CHUNK_COLS), lambda q: (q, 0, 0, 0)),
            pl.BlockSpec((1, 2, STATE_COLS), lambda q: (q, 0, 0)),
        ],
        out_shape=[
            jax.ShapeDtypeStruct((LANE_TILES, gt, CHUNK_COLS, CHUNK_COLS), BF16),
            jax.ShapeDtypeStruct((LANE_TILES, PAIRS_PER_TILE, 2 * CHUNK_COLS, 2 * LANES), BF16),
            jax.ShapeDtypeStruct((LANE_TILES, PAIRS_PER_TILE, 2 * LANES, 2 * CHUNK_COLS), BF16),
            jax.ShapeDtypeStruct((LANE_TILES, 2, STATE_COLS), F32),
        ],
        compiler_params=_cparams(("parallel",)),
        name="ssm_prep",
    )(rows, ct, bt)


def _layer_norm(r, g, b):
    mu = jnp.mean(r, axis=-1, keepdims=True)
    c = r - mu
    var = jnp.mean(c * c, axis=-1, keepdims=True)
    return c * lax.rsqrt(var + LN_EPS) * g + b


MIX_ROWS = 256


def _gelu_tanh(y):
    return 0.5 * y * (1.0 + jnp.tanh(math.sqrt(2.0 / math.pi) * (y + 0.044715 * (y * y * y))))


def _mix_ln1_kernel(x_ref, attn_ref, y_ref, wglu_ref, bglu_ref, wo_ref, g_ref, b_ref, o_ref, y_sc):
    nb, tl, d = x_ref.shape
    pos = MIX_ROWS // nb
    n_pieces = tl // pos
    jp = pos // CHUNK

    def relayout(n):
        rows = slice(n * jp * nb, (n + 1) * jp * nb)
        for q in range(LANE_TILES):
            tiles = _chunk_tiles([y_ref[q, g, rows] for g in range(GROUPS_PER_TILE)])
            y_sc[q, n * MIX_ROWS:(n + 1) * MIX_ROWS] = jnp.stack(
                [t.reshape(jp, nb, LANES) for t in tiles], axis=1).reshape(MIX_ROWS, LANES)

    def glu(n):
        y = jnp.concatenate(
            [jnp.concatenate([y_sc[q, pl.ds(n * MIX_ROWS + b, pos, stride=nb), :] for q in range(LANE_TILES)], axis=1)
             for b in range(nb)], axis=0)
        gl = _gelu_tanh(y)
        z = gl * jax.nn.sigmoid(jnp.dot(gl.astype(BF16), wglu_ref[...], preferred_element_type=F32) + bglu_ref[...])
        return z.astype(BF16)

    def project(n, z):
        ls = slice(n * pos, (n + 1) * pos)
        mix = (jnp.dot(attn_ref[:, ls].reshape(MIX_ROWS, ATT_WIDTH), wo_ref[:ATT_WIDTH], preferred_element_type=F32)
               + jnp.dot(z, wo_ref[ATT_WIDTH:], preferred_element_type=F32))
        res = _layer_norm(ALPHA * x_ref[:, ls].reshape(MIX_ROWS, d) + mix, g_ref[...], b_ref[...])
        o_ref[:, ls] = res.reshape(nb, pos, d)

    relayout(0)
    z = glu(0)
    for n in range(n_pieces):
        if n + 1 < n_pieces:
            relayout(n + 1)
            z_next = glu(n + 1)
        project(n, z)
        z = z_next


def _mix_ln1(x, attn, y_tm, wglu_bf, b_glu, wo_bf, ln_g, ln_b, tl=128):
    B, L, D = x.shape
    assert MIX_ROWS % (B * CHUNK) == 0 and tl % (MIX_ROWS // B) == 0
    resident = lambda shape: pl.BlockSpec(shape, lambda l: (0, 0), pipeline_mode=pl.Buffered(1))
    return pl.pallas_call(
        _mix_ln1_kernel,
        grid=(L // tl,),
        in_specs=[
            pl.BlockSpec((B, tl, D), lambda l: (0, l, 0)),
            pl.BlockSpec((B, tl, ATT_WIDTH), lambda l: (0, l, 0)),
            pl.BlockSpec((LANE_TILES, GROUPS_PER_TILE, tl // CHUNK * B, CHUNK_COLS), lambda l: (0, 0, l, 0)),
            resident((SSM_WIDTH, SSM_WIDTH)),
            resident((1, SSM_WIDTH)),
            resident((D, D)),
            resident((1, D)),
            resident((1, D)),
        ],
        out_specs=pl.BlockSpec((B, tl, D), lambda l: (0, l, 0)),
        out_shape=jax.ShapeDtypeStruct((B, L, D), F32),
        scratch_shapes=[pltpu.VMEM((LANE_TILES, tl * B, LANES), F32)],
        compiler_params=_cparams(("parallel",)),
        name="mix_ln1",
    )(x, attn, y_tm, wglu_bf, b_glu, wo_bf, ln_g, ln_b)


HALO = 16
FFN_SUB = 256
FFN_DOWN_PARTS = 2
FFN_ROWS = 128


def _ffn_kernel(x_ref, xh_ref, p_ref, wple_ref, wgate_ref, wup_ref, cw_ref, cb_ref, wd_ref, g_ref, b_ref, o_ref,
                x32_sc, xh32_sc, p32_sc, xb_sc, acc_sc, hg_sc, hv_sc, act_sc, *, tl):
    nb = x_ref.shape[0]
    tm = tl * nb
    d_slabs = x_ref.shape[2] // LANES
    hp = (CONV_WIDTH - 1)
    for bb in range(nb):
        for s in range(d_slabs):
            lanes = slice(s * LANES, (s + 1) * LANES)
            x32_sc[s, pl.ds(bb, tl, stride=nb), :] = x_ref[bb, :, lanes]
            xh32_sc[s, pl.ds(bb, hp, stride=nb), :] = xh_ref[bb, SUBLANES - hp:, lanes]
        for s in range(p_ref.shape[2] // LANES):
            p32_sc[s, pl.ds(bb, tl, stride=nb), :] = p_ref[bb, :, s * LANES:(s + 1) * LANES]
    x_tm = jnp.concatenate([x32_sc[s] for s in range(d_slabs)], axis=1)
    p_tm = jnp.concatenate([p32_sc[s] for s in range(p32_sc.shape[0])], axis=1)
    keep = jnp.where(pl.program_id(0) == 0, 0.0, 1.0)
    xb_sc[0:HALO] = (jnp.concatenate([xh32_sc[s] for s in range(d_slabs)], axis=1) * keep).astype(BF16)
    xb_sc[HALO:] = x_tm.astype(BF16)
    gate = jax.nn.sigmoid(jnp.dot(xb_sc[HALO:], wgate_ref[...], preferred_element_type=F32))
    ple = jnp.dot(p_tm.astype(BF16), wple_ref[...], preferred_element_type=F32) * gate
    acc_sc[...] = ALPHA * x_tm + ple

    def conv(h_sc, slot, cols, r0, rows):
        out = cb_ref[:, cols] + cw_ref[CONV_WIDTH - 1:CONV_WIDTH, cols] * h_sc[slot, HALO + r0:HALO + r0 + rows]
        for j in range(CONV_WIDTH - 1):
            back = (CONV_WIDTH - 1 - j) * nb
            out = out + cw_ref[j:j + 1, cols] * h_sc[slot, HALO + r0 - back:HALO + r0 - back + rows]
        return out

    n_sub = D_FF // FFN_SUB
    g_cols = lambda n: slice(n * FFN_SUB, (n + 1) * FFN_SUB)
    v_cols = lambda n: slice(D_FF + n * FFN_SUB, D_FF + (n + 1) * FFN_SUB)

    def up(n):
        xb = xb_sc[...]
        hg_sc[n % 2] = jnp.dot(xb, wup_ref[:, g_cols(n)], preferred_element_type=F32)
        hv_sc[n % 2] = jnp.dot(xb, wup_ref[:, v_cols(n)], preferred_element_type=F32)

    def gate_block(n):
        for r0 in range(0, tm, FFN_ROWS):
            gc = conv(hg_sc, n % 2, g_cols(n), r0, FFN_ROWS)
            vc = conv(hv_sc, n % 2, v_cols(n), r0, FFN_ROWS)
            act_sc[r0:r0 + FFN_ROWS, g_cols(n)] = (gc * jax.nn.sigmoid(gc) * vc).astype(BF16)

    def down(lo, hi):
        cols = slice(lo * FFN_SUB, hi * FFN_SUB)
        return jnp.dot(act_sc[:, cols], wd_ref[cols, :], preferred_element_type=F32)

    cuts = [n_sub * k // FFN_DOWN_PARTS for k in range(FFN_DOWN_PARTS + 1)]
    up(0)
    for n in range(n_sub):
        if n + 1 < n_sub:
            up(n + 1)
        gate_block(n)
        if n + 1 in cuts[1:-1]:
            lo = cuts[cuts.index(n + 1) - 1]
            acc_sc[...] += down(lo, n + 1)

    res = _layer_norm(acc_sc[...] + down(cuts[-2], n_sub), g_ref[...], b_ref[...])
    for s in range(d_slabs):
        x32_sc[s] = res[:, s * LANES:(s + 1) * LANES]
    for bb in range(nb):
        for s in range(d_slabs):
            o_ref[bb, :, s * LANES:(s + 1) * LANES] = x32_sc[s, pl.ds(bb, tl, stride=nb), :]


def _ffn_ln2(x1, p, wple_bf, wgate_bf, wup_bf, conv_w, conv_b, wdown_bf, ln_g, ln_b, tl=64):
    B, L, D = x1.shape
    tm = tl * B
    assert D_FF % FFN_SUB == 0 and HALO == (CONV_WIDTH - 1) * B and tm % FFN_ROWS == 0
    const = lambda i: (0, 0)
    resident = lambda shape: pl.BlockSpec(shape, const, pipeline_mode=pl.Buffered(1))
    return pl.pallas_call(
        functools.partial(_ffn_kernel, tl=tl),
        grid=(L // tl,),
        in_specs=[
            pl.BlockSpec((B, tl, D), lambda i: (0, i, 0)),
            pl.BlockSpec((B, SUBLANES, D), lambda i: (0, jnp.maximum(i * (tl // SUBLANES) - 1, 0), 0)),
            pl.BlockSpec((B, tl, PLE_DIM), lambda i: (0, i, 0)),
            resident((PLE_DIM, D)),
            resident((D, D)),
            resident((D, 2 * D_FF)),
            resident((CONV_WIDTH, 2 * D_FF)),
            resident((1, 2 * D_FF)),
            resident((D_FF, D)),
            resident((1, D)),
            resident((1, D)),
        ],
        out_specs=pl.BlockSpec((B, tl, D), lambda i: (0, i, 0)),
        out_shape=jax.ShapeDtypeStruct((B, L, D), F32),
        scratch_shapes=[
            pltpu.VMEM((D // LANES, tm, LANES), F32),
            pltpu.VMEM((D // LANES, HALO, LANES), F32),
            pltpu.VMEM((PLE_DIM // LANES, tm, LANES), F32),
            pltpu.VMEM((tm + HALO, D), BF16),
            pltpu.VMEM((tm, D), F32),
            pltpu.VMEM((2, tm + HALO, FFN_SUB), F32),
            pltpu.VMEM((2, tm + HALO, FFN_SUB), F32),
            pltpu.VMEM((tm, D_FF), BF16),
        ],
        compiler_params=_cparams(("parallel",)),
        name="ffn_ln2",
    )(x1, x1, p, wple_bf, wgate_bf, wup_bf, conv_w, conv_b, wdown_bf, ln_g, ln_b)


def kernel(x, p, w_in, diff_lambda_q1, diff_lambda_k1, diff_lambda_q2, diff_lambda_k2, diff_subln_g, ssm_lambda_re, ssm_lambda_im, ssm_log_dt, ssm_b_re, ssm_b_im, ssm_c_re, ssm_c_im, ssm_d, ssm_w_glu, ssm_b_glu, w_o, ln1_g, ln1_b, ffn_w_up, ffn_conv_w, ffn_conv_b, ffn_w_down, w_ple, w_ple_gate, ln2_g, ln2_b):
    B, L, D = x.shape
    assert B == SUBLANES and D == D_MODEL and L % 512 == 0
    for i in range(DEPTH):
        lam_init = 0.8 - 0.6 * math.exp(-0.3 * i)
        row = lambda a: a[i].reshape(1, -1)
        qt, k, vt, u_tm = _in_proj(x, w_in[i])
        attn = _diff_attn_flat(qt, k, vt, row(diff_lambda_q1), row(diff_lambda_k1), row(diff_lambda_q2),
                          row(diff_lambda_k2), diff_subln_g[i], lam_init)
        tables = _ssm_prep(ssm_lambda_re[i], ssm_lambda_im[i], ssm_log_dt[i], ssm_b_re[i], ssm_b_im[i],
                           ssm_c_re[i], ssm_c_im[i])
        y_tm = _ssm(u_tm, tables, ssm_d[i], B, L)
        x1 = _mix_ln1(x, attn, y_tm, ssm_w_glu[i].astype(BF16), row(ssm_b_glu), w_o[i].astype(BF16),
                      row(ln1_g), row(ln1_b))
        x = _ffn_ln2(x1, p[i], w_ple[i].astype(BF16), w_ple_gate[i].astype(BF16), ffn_w_up[i].astype(BF16),
                     ffn_conv_w[i], row(ffn_conv_b), ffn_w_down[i].astype(BF16), row(ln2_g), row(ln2_b))
    return x
```

```python
import functools
import math

import jax
import jax.numpy as jnp
from jax import lax
from jax.experimental import pallas as pl
from jax.experimental.pallas import tpu as pltpu

F32 = jnp.float32
BF16 = jnp.bfloat16

SUBLANES = 8
LANES = 128

DEPTH = 1
D_MODEL = 1024
PLE_DIM = 256
HEADS = 4
QK_DIM = 64
V_DIM = 2 * QK_DIM
QK_WIDTH = HEADS * 2 * QK_DIM
ATT_WIDTH = HEADS * V_DIM
SSM_WIDTH = D_MODEL - ATT_WIDTH
SSM_GROUP = 16
SSM_GROUPS = SSM_WIDTH // SSM_GROUP
SSM_STATE = 64
D_FF = 2816
CONV_WIDTH = 3
LN_EPS = 1e-5
ALPHA = (2 * DEPTH) ** 0.25
QK_SCALE = QK_DIM ** -0.5 * math.log2(math.e)

CHUNK = 16
CHUNK_COLS = CHUNK * SSM_GROUP
GROUPS_PER_TILE = LANES // SSM_GROUP
LANE_TILES = SSM_WIDTH // LANES
PAIRS_PER_TILE = GROUPS_PER_TILE // 2
STATE_COLS = GROUPS_PER_TILE * SSM_STATE

VMEM_LIMIT = 56 * 1024 * 1024


def _cparams(sem):
    return pltpu.CompilerParams(dimension_semantics=sem, vmem_limit_bytes=VMEM_LIMIT)


TB = 256
VT_ROWS = V_DIM + 16


def _in_proj_kernel(x_ref, w_ref, qt_ref, k_ref, vt_ref, u_ref, u_sc):
    wq_ref = w_ref.at[:, :QK_WIDTH]
    wk_ref = w_ref.at[:, QK_WIDTH:2 * QK_WIDTH]
    wv_ref = w_ref.at[:, 2 * QK_WIDTH:2 * QK_WIDTH + ATT_WIDTH]
    wu_ref = w_ref.at[:, 2 * QK_WIDTH + ATT_WIDTH:]
    nb, tl, d = x_ref.shape
    jt = tl // CHUNK
    xb = x_ref[...].reshape(nb * tl, d).astype(BF16)
    hu = jnp.dot(xb, wu_ref[...], preferred_element_type=F32)
    for b in range(nb):
        for q in range(LANE_TILES):
            u_sc[q, pl.ds(b, tl, stride=nb), :] = hu[b * tl:(b + 1) * tl, q * LANES:(q + 1) * LANES]
    for q in range(LANE_TILES):
        tiles = u_sc[q].reshape(jt, CHUNK, nb, LANES)
        slabs = _chunk_slabs([tiles[:, r].reshape(jt * nb, LANES) for r in range(CHUNK)])
        for g in range(GROUPS_PER_TILE):
            u_ref[q, g] = slabs[g]
    tn = (((0,), (1,)), ((), ()))
    qt = lax.dot_general(wq_ref[...], xb, tn, preferred_element_type=F32) * QK_SCALE
    vt = lax.dot_general(wv_ref[...], xb, tn, preferred_element_type=F32)
    pad_row = lax.broadcasted_iota(jnp.int32, (VT_ROWS - V_DIM, TB), 0)
    ones_row = jnp.where(pad_row == 0, 1.0, 0.0).astype(BF16)
    for b in range(nb):
        for h in range(HEADS):
            for c in range(tl // TB):
                cols = slice(b * tl + c * TB, b * tl + (c + 1) * TB)
                qt_ref[b, h, c] = qt[h * LANES:(h + 1) * LANES, cols].astype(BF16)
                vt_ref[b, h, c, :V_DIM] = vt[h * LANES:(h + 1) * LANES, cols].astype(BF16)
                vt_ref[b, h, c, V_DIM:] = ones_row
    k_ref[...] = jnp.dot(xb, wk_ref[...], preferred_element_type=F32).astype(BF16).reshape(k_ref.shape)


def _in_proj(x, w_in, tl=256):
    B, L, D = x.shape
    slab = lambda rows: pl.BlockSpec((B, HEADS, tl // TB, rows, TB), lambda l: (0, 0, l, 0, 0))
    slab_shape = lambda rows: jax.ShapeDtypeStruct((B, HEADS, L // TB, rows, TB), BF16)
    return pl.pallas_call(
        _in_proj_kernel,
        grid=(L // tl,),
        in_specs=[
            pl.BlockSpec((B, tl, D), lambda l: (0, l, 0)),
            pl.BlockSpec(w_in.shape, lambda l: (0, 0), pipeline_mode=pl.Buffered(1)),
        ],
        out_specs=[
            slab(LANES),
            pl.BlockSpec((B, tl, QK_WIDTH), lambda l: (0, l, 0)),
            slab(VT_ROWS),
            pl.BlockSpec((LANE_TILES, GROUPS_PER_TILE, tl // CHUNK * B, CHUNK_COLS), lambda l: (0, 0, l, 0)),
        ],
        out_shape=[
            slab_shape(LANES),
            jax.ShapeDtypeStruct((B, L, QK_WIDTH), BF16),
            slab_shape(VT_ROWS),
            jax.ShapeDtypeStruct((LANE_TILES, GROUPS_PER_TILE, L // CHUNK * B, CHUNK_COLS), F32),
        ],
        scratch_shapes=[pltpu.VMEM((LANE_TILES, tl * B, LANES), F32)],
        compiler_params=_cparams(("parallel",)),
        name="in_proj",
    )(x, w_in.astype(BF16))


def _attn_flat_kernel(lq1_ref, lk1_ref, lq2_ref, lk2_ref, g_ref, qt_ref, k_ref, vt_ref, o_ref,
                      acc_sc, qst_sc, s_sc, p_sc, bm_sc, st_sc, al_sc, *, lam_init):
    n_q = qt_ref.shape[2]
    n_pairs = n_q * (n_q + 1) // 2
    assert TB == 2 * LANES and n_pairs % 2 == 0
    heads = range(HEADS)
    n_strips = 2 * TB // LANES
    lam = (jnp.exp(jnp.sum(lq1_ref[...] * lk1_ref[...], axis=-1, keepdims=True))
           - jnp.exp(jnp.sum(lq2_ref[...] * lk2_ref[...], axis=-1, keepdims=True)) + lam_init)

    def prepare(i, c):
        for hh in heads:
            qt = qt_ref[0, hh, i]
            row = lax.broadcasted_iota(jnp.int32, qt.shape, 0)
            zero = jnp.zeros_like(qt)
            qst_sc[i, hh, :, :TB] = jnp.where(row < QK_DIM, qt, zero)
            qst_sc[i, hh, :, TB:] = jnp.where(row >= QK_DIM, qt, zero)
            acc_sc[i, hh] = jnp.zeros(acc_sc.shape[2:], F32)
            st_sc[i, hh] = jnp.full((1, 2 * TB), -jnp.inf, F32)
        return c

    lax.fori_loop(0, n_q, prepare, 0)
    p_sc[1] = jnp.zeros(p_sc.shape[1:], BF16)
    al_sc[1] = jnp.ones(al_sc.shape[1:], F32)

    def diag_place(i):
        r = i % 4
        return jnp.where(jnp.logical_or(r == 1, r == 2), 1, 0)

    def is_diag(pair):
        return pair[1] == diag_place(pair[0])

    def block_of(pair):
        i, k = pair
        d = diag_place(i)
        return jnp.where(k == d, i, jnp.where(k < d, k, k - 1))

    def scores(pair, slot):
        i, j = pair[0], block_of(pair)
        rows = pl.ds(pl.multiple_of(j * TB, TB), TB)
        for hh in heads:
            kb = k_ref[0, rows, hh * LANES:(hh + 1) * LANES]
            s = jnp.dot(kb, qst_sc[i, hh], preferred_element_type=F32)
            s_sc[slot, hh] = s
            bm_sc[slot, hh] = jnp.max(s, axis=0, keepdims=True)

    def values(pair, slot):
        i, j = pair[0], block_of(pair)
        for hh in heads:
            pv = jnp.dot(vt_ref[0, hh, j], p_sc[slot, hh], preferred_element_type=F32)
            acc_sc[i, hh] = al_sc[slot, hh] * acc_sc[i, hh] + pv

    def softmax(pair, slot, diagonal):
        i, _ = pair
        for hh in heads:
            for c in range(n_strips):
                lanes = slice(c * LANES, (c + 1) * LANES)
                m = st_sc[i, hh, :, lanes]
                if not diagonal:
                    mc = jnp.maximum(m, bm_sc[slot, hh, :, lanes])
                    p = jnp.exp2(s_sc[slot, hh, :, lanes] - mc)
                else:
                    s = s_sc[slot, hh, :, lanes]
                    key = lax.broadcasted_iota(jnp.int32, s.shape, 0)
                    qry = lax.broadcasted_iota(jnp.int32, s.shape, 1) + (c * LANES) % TB
                    s = jnp.where(key <= qry, s, -jnp.inf)
                    mc = jnp.maximum(m, jnp.max(s, axis=0, keepdims=True))
                    p = jnp.exp2(s - mc)
                p_sc[slot, hh, :, lanes] = p.astype(BF16)
                st_sc[i, hh, :, lanes] = mc
                al_sc[slot, hh, :, lanes] = jnp.exp2(m - mc)

    def succ(pair):
        i, k = pair
        last = k == i
        i2 = jnp.minimum(jnp.where(last, i + 1, i), n_q - 1)
        return i2, jnp.where(last, 0, k + 1)

    def two_steps(t, carry):
        prev, cur = carry
        nxt = succ(cur)
        nxt2 = succ(nxt)

        def body(diag0):
            scores(nxt, 1)
            values(prev, 1)
            softmax(cur, 0, diag0)
            scores(nxt2, 0)
            values(cur, 0)
            softmax(nxt, 1, False)

        d0 = is_diag(cur)
        pl.when(d0)(lambda: body(True))
        pl.when(jnp.logical_not(d0))(lambda: body(False))
        return nxt, nxt2

    zero = jnp.int32(0)
    first = (zero, zero)
    scores(first, 0)
    last, _ = lax.fori_loop(0, n_pairs // 2, two_steps, (first, first))
    values(last, 1)

    def finish(i, c):
        for hh in heads:
            o = acc_sc[i, hh, :V_DIM] * (1.0 / acc_sc[i, hh, V_DIM:V_DIM + 1])
            o = o[:, :TB] - lam * o[:, TB:]
            ms = jnp.mean(o * o, axis=0, keepdims=True)
            o = o * lax.rsqrt(ms + LN_EPS) * (g_ref[...] * (1.0 - lam_init))
            o_ref[0, pl.ds(pl.multiple_of(i * TB, TB), TB), hh * LANES:(hh + 1) * LANES] = o.T.astype(o_ref.dtype)
        return c

    lax.fori_loop(0, n_q, finish, 0)


def _diff_attn_flat(qt, k, vt, lq1, lk1, lq2, lk2, subln_g, lam_init):
    B, L, _ = k.shape
    n_q = L // TB
    vec = pl.BlockSpec((1, QK_DIM), lambda b: (0, 0))
    return pl.pallas_call(
        functools.partial(_attn_flat_kernel, lam_init=lam_init),
        grid=(B,),
        in_specs=[
            vec, vec, vec, vec,
            pl.BlockSpec((V_DIM, 1), lambda b: (0, 0)),
            pl.BlockSpec((1, HEADS, n_q, LANES, TB), lambda b: (b, 0, 0, 0, 0)),
            pl.BlockSpec((1, L, HEADS * LANES), lambda b: (b, 0, 0)),
            pl.BlockSpec((1, HEADS, n_q, VT_ROWS, TB), lambda b: (b, 0, 0, 0, 0)),
        ],
        out_specs=pl.BlockSpec((1, L, HEADS * LANES), lambda b: (b, 0, 0)),
        out_shape=jax.ShapeDtypeStruct((B, L, ATT_WIDTH), BF16),
        scratch_shapes=[
            pltpu.VMEM((n_q, HEADS, VT_ROWS, 2 * TB), F32),
            pltpu.VMEM((n_q, HEADS, LANES, 2 * TB), BF16),
            pltpu.VMEM((2, HEADS, TB, 2 * TB), F32),
            pltpu.VMEM((2, HEADS, TB, 2 * TB), BF16),
            pltpu.VMEM((2, HEADS, 1, 2 * TB), F32),
            pltpu.VMEM((n_q, HEADS, 1, 2 * TB), F32),
            pltpu.VMEM((2, HEADS, 1, 2 * TB), F32),
        ],
        compiler_params=_cparams(("parallel",)),
        name="diff_attn",
    )(lq1, lk1, lq2, lk2, subln_g.reshape(V_DIM, 1), qt, k, vt)


def _block_transpose8(vs):
    lane_blk = lax.broadcasted_iota(jnp.int32, vs[0].shape, 1) // SSM_GROUP
    for d in (4, 2, 1):
        keep = (lane_blk & d) == 0
        new = list(vs)
        for a in range(8):
            if a & d == 0:
                lo, hi = vs[a], vs[a + d]
                new[a] = jnp.where(keep, lo, pltpu.roll(hi, SSM_GROUP * d, 1))
                new[a + d] = jnp.where(keep, pltpu.roll(lo, LANES - SSM_GROUP * d, 1), hi)
        vs = new
    return vs


def _ssm_kernel(u_ref, toep_ref, win_ref, wout_ref, a_ref, d_ref, y_ref, s_sc, x_sc, sp_sc, *, jt):
    rows = jt * SUBLANES

    @pl.when(pl.program_id(1) == 0)
    def _():
        s_sc[...] = jnp.zeros(s_sc.shape, F32)

    ug = [u_ref[0, g].astype(BF16) for g in range(GROUPS_PER_TILE)]

    y = [jnp.dot(ug[g], toep_ref[0, g], preferred_element_type=F32) for g in range(GROUPS_PER_TILE)]

    for m in range(PAIRS_PER_TILE):
        up = jnp.concatenate([ug[2 * m], ug[2 * m + 1]], axis=1)
        xp = jnp.dot(up, win_ref[0, m], preferred_element_type=F32)
        x_sc[0, :, m * LANES:(m + 1) * LANES] = xp[:, :LANES]
        x_sc[1, :, m * LANES:(m + 1) * LANES] = xp[:, LANES:]

    ar = a_ref[0, 0:1, :]
    ai = a_ref[0, 1:2, :]

    def step(j, carry):
        sr, si = carry
        r8 = pl.ds(pl.multiple_of(j * SUBLANES, SUBLANES), SUBLANES)
        sp_sc[0, r8, :] = sr
        sp_sc[1, r8, :] = si
        return (ar * sr - ai * si + x_sc[0, r8, :], ar * si + ai * sr + x_sc[1, r8, :])

    sr, si = lax.fori_loop(0, jt, step, (s_sc[0], s_sc[1]))
    s_sc[0] = sr
    s_sc[1] = si

    for m in range(PAIRS_PER_TILE):
        sp = jnp.concatenate([sp_sc[0, :, m * LANES:(m + 1) * LANES],
                              sp_sc[1, :, m * LANES:(m + 1) * LANES]], axis=1).astype(BF16)
        yc = jnp.dot(sp, wout_ref[0, m], preferred_element_type=F32)
        y[2 * m] = y[2 * m] + yc[:, :CHUNK_COLS]
        y[2 * m + 1] = y[2 * m + 1] + yc[:, CHUNK_COLS:]

    for g in range(GROUPS_PER_TILE):
        y_ref[0, g] = y[g] + d_ref[0, g] * u_ref[0, g]


def _chunk_slabs(tiles):
    z_lo = _block_transpose8(tiles[:8])
    z_hi = _block_transpose8(tiles[8:])
    return [jnp.concatenate([z_lo[g], z_hi[g]], axis=1) for g in range(GROUPS_PER_TILE)]


def _chunk_tiles(slabs):
    out = []
    for half in range(2):
        out += _block_transpose8([s[:, half * LANES:(half + 1) * LANES] for s in slabs])
    return out


def _ssm(u_slab, tables, d_skip, B, L, lt=1024):
    toep, win, wout, a16 = tables
    jt = lt // CHUNK
    n_l = L // lt
    rows = jt * B
    slab = pl.BlockSpec((1, GROUPS_PER_TILE, rows, CHUNK_COLS), lambda q, l: (q, 0, l, 0))
    d_slab = jnp.tile(d_skip.reshape(LANE_TILES, GROUPS_PER_TILE, 1, SSM_GROUP), (1, 1, 1, CHUNK))
    return pl.pallas_call(
        functools.partial(_ssm_kernel, jt=jt),
        grid=(LANE_TILES, n_l),
        in_specs=[
            slab,
            pl.BlockSpec((1, GROUPS_PER_TILE, CHUNK_COLS, CHUNK_COLS), lambda q, l: (q, 0, 0, 0)),
            pl.BlockSpec((1, PAIRS_PER_TILE, 2 * CHUNK_COLS, 2 * LANES), lambda q, l: (q, 0, 0, 0)),
            pl.BlockSpec((1, PAIRS_PER_TILE, 2 * LANES, 2 * CHUNK_COLS), lambda q, l: (q, 0, 0, 0)),
            pl.BlockSpec((1, 2, STATE_COLS), lambda q, l: (q, 0, 0)),
            pl.BlockSpec((1, GROUPS_PER_TILE, 1, CHUNK_COLS), lambda q, l: (q, 0, 0, 0)),
        ],
        out_specs=slab,
        out_shape=jax.ShapeDtypeStruct(u_slab.shape, F32),
        scratch_shapes=[
            pltpu.VMEM((2, B, STATE_COLS), F32),
            pltpu.VMEM((2, rows, STATE_COLS), F32),
            pltpu.VMEM((2, rows, STATE_COLS), F32),
        ],
        compiler_params=_cparams(("parallel", "arbitrary")),
        name="ssm",
    )(u_slab, toep, win, wout, a16, d_slab)


def _cmul(a, b):
    return a[0] * b[0] - a[1] * b[1], a[0] * b[1] + a[1] * b[0]


def _zoh(lr, li, ldt):
    dt = jnp.exp(ldt)
    mag = jnp.exp(lr * dt)
    ab = (mag * jnp.cos(li * dt), mag * jnp.sin(li * dt))
    den = lr * lr + li * li
    zr, zi = ab[0] - 1.0, ab[1]
    return ab, ((zr * lr + zi * li) / den, (zi * lr - zr * li) / den)


def _cpow(base, exponent, bits):
    out = None
    for k in range(bits):
        on = (exponent & (1 << k)) != 0
        sel = (jnp.where(on, base[0], 1.0), jnp.where(on, base[1], 0.0))
        out = sel if out is None else _cmul(out, sel)
        if k + 1 < bits:
            base = _cmul(base, base)
    return out


def _ssm_prep_kernel(row_ref, ct_ref, bt_ref, toep_ref, win_ref, wout_ref, a_ref):
    bits = CHUNK.bit_length() - 1
    lane = lax.broadcasted_iota(jnp.int32, (1, CHUNK_COLS), 1)
    tau = lane // SSM_GROUP
    rowi = lax.broadcasted_iota(jnp.int32, (CHUNK_COLS, 1), 0)
    back = (CHUNK - 1) - rowi // SSM_GROUP
    lane2 = lax.broadcasted_iota(jnp.int32, (1, LANES), 1)
    exact = functools.partial(jnp.dot, precision=lax.Precision.HIGHEST, preferred_element_type=F32)
    h_row = lax.broadcasted_iota(jnp.int32, (SSM_GROUP, CHUNK_COLS), 0)
    repeat = jnp.where(lane % SSM_GROUP == h_row, 1.0, 0.0)
    zoh = [_zoh(row_ref[0, m], row_ref[1, m], row_ref[2, m]) for m in range(PAIRS_PER_TILE)]
    pad = jnp.zeros((SUBLANES - PAIRS_PER_TILE, LANES), F32)
    ab_cols = tuple(jnp.concatenate([z[0][c] for z in zoh] + [pad], axis=0).T for c in range(2))
    a_rows = []
    for m in range(PAIRS_PER_TILE):
        abp, fp = zoh[m]
        pwr = _cpow(abp, back, bits)
        a16 = abp
        for _ in range(bits):
            a16 = _cmul(a16, a16)
        a_rows.append(a16)
        wout_rows = [[None, None], [None, None]]
        for j in range(2):
            g = 2 * m + j
            states = slice(j * SSM_STATE, (j + 1) * SSM_STATE)
            ab = tuple(v[states, m:m + 1] for v in ab_cols)
            pw = _cpow(ab, tau, bits)
            c_rep = tuple(exact(ct_ref[c, g], repeat) for c in range(2))
            g0 = _cmul(pw, c_rep)
            g1 = _cmul(g0, ab)
            mine = (lane2 < SSM_STATE) if j == 0 else (lane2 >= SSM_STATE)
            bb2 = _cmul(fp, (bt_ref[0, g], bt_ref[1, g]))
            bb2 = tuple(jnp.where(mine, v, 0.0) for v in bb2)
            none = jnp.zeros_like(g0[0])
            stack = (lambda v: jnp.concatenate([v, none], axis=0)) if j == 0 else \
                    (lambda v: jnp.concatenate([none, v], axis=0))
            kt = exact(bb2[0], stack(g0[0])) - exact(bb2[1], stack(g0[1]))
            toep_ref[0, g] = jnp.concatenate(
                [jnp.where(lane >= SSM_GROUP * r, pltpu.roll(kt, SSM_GROUP * r, 1), 0.0) if r else kt
                 for r in range(CHUNK)], axis=0).astype(BF16)
            w = _cmul(pwr, tuple(jnp.concatenate([v] * CHUNK, axis=0) for v in bb2))
            win_ref[0, m, j * CHUNK_COLS:(j + 1) * CHUNK_COLS, :] = jnp.concatenate(w, axis=1).astype(BF16)
            wout_rows[0][j] = g1[0]
            wout_rows[1][j] = -g1[1]
        zero = jnp.zeros((SSM_STATE, CHUNK_COLS), F32)
        wout_ref[0, m] = jnp.concatenate([
            jnp.concatenate([wout_rows[0][0], zero], axis=1),
            jnp.concatenate([zero, wout_rows[0][1]], axis=1),
            jnp.concatenate([wout_rows[1][0], zero], axis=1),
            jnp.concatenate([zero, wout_rows[1][1]], axis=1)], axis=0).astype(BF16)
    a_ref[0] = jnp.concatenate([jnp.concatenate([a[0] for a in a_rows], axis=1),
                                jnp.concatenate([a[1] for a in a_rows], axis=1)], axis=0)


def _ssm_prep(lam_re, lam_im, log_dt, b_re, b_im, c_re, c_im):
    G, P, H = b_re.shape
    ldt = jnp.broadcast_to(log_dt[:, None], (G, P))
    rows = jnp.stack([lam_re, lam_im, ldt]).reshape(3, G // 2, 1, 2 * P)
    ct = jnp.stack([c_re, c_im]).transpose(0, 1, 3, 2)
    bt = jnp.tile(jnp.stack([b_re, b_im]).transpose(0, 1, 3, 2), (1, 1, 1, 2))
    gt = GROUPS_PER_TILE
    return pl.pallas_call(
        _ssm_prep_kernel,
        grid=(LANE_TILES,),
        in_specs=[
            pl.BlockSpec((3, PAIRS_PER_TILE, 1, 2 * P), lambda q: (0, q, 0, 0)),
            pl.BlockSpec((2, gt, P, H), lambda q: (0, q, 0, 0)),
            pl.BlockSpec((2, gt, H, 2 * P), lambda q: (0, q, 0, 0)),
        ],
        out_specs=[
            pl.BlockSpec((1, gt, CHUNK_COLS, CHUNK_COLS), lambda q: (q, 0, 0, 0)),
            pl.BlockSpec((1, PAIRS_PER_TILE, 2 * CHUNK_COLS, 2 * LANES), lambda q: (q, 0, 0, 0)),
            pl.BlockSpec((1, PAIRS_PER_TILE, 2 * LANES, 2 * CHUNK_COLS), lambda q: (q, 0, 0, 0)),
            pl.BlockSpec((1, 2, STATE_COLS), lambda q: (q, 0, 0)),
        ],
        out_shape=[
            jax.ShapeDtypeStruct((LANE_TILES, gt, CHUNK_COLS, CHUNK_COLS), BF16),
            jax.ShapeDtypeStruct((LANE_TILES, PAIRS_PER_TILE, 2 * CHUNK_COLS, 2 * LANES), BF16),
            jax.ShapeDtypeStruct((LANE_TILES, PAIRS_PER_TILE, 2 * LANES, 2 * CHUNK_COLS), BF16),
            jax.ShapeDtypeStruct((LANE_TILES, 2, STATE_COLS), F32),
        ],
        compiler_params=_cparams(("parallel",)),
        name="ssm_prep",
    )(rows, ct, bt)


def _layer_norm(r, g, b):
    mu = jnp.mean(r, axis=-1, keepdims=True)
    c = r - mu
    var = jnp.mean(c * c, axis=-1, keepdims=True)
    return c * lax.rsqrt(var + LN_EPS) * g + b


MIX_ROWS = 256


def _gelu_tanh(y):
    return 0.5 * y * (1.0 + jnp.tanh(math.sqrt(2.0 / math.pi) * (y + 0.044715 * (y * y * y))))


def _mix_ln1_kernel(x_ref, attn_ref, y_ref, wglu_ref, bglu_ref, wo_ref, g_ref, b_ref, o_ref, y_sc):
    nb, tl, d = x_ref.shape
    pos = MIX_ROWS // nb
    n_pieces = tl // pos
    jp = pos // CHUNK

    def relayout(n):
        rows = slice(n * jp * nb, (n + 1) * jp * nb)
        for q in range(LANE_TILES):
            tiles = _chunk_tiles([y_ref[q, g, rows] for g in range(GROUPS_PER_TILE)])
            y_sc[q, n * MIX_ROWS:(n + 1) * MIX_ROWS] = jnp.stack(
                [t.reshape(jp, nb, LANES) for t in tiles], axis=1).reshape(MIX_ROWS, LANES)

    def glu(n):
        y = jnp.concatenate(
            [jnp.concatenate([y_sc[q, pl.ds(n * MIX_ROWS + b, pos, stride=nb), :] for q in range(LANE_TILES)], axis=1)
             for b in range(nb)], axis=0)
        gl = _gelu_tanh(y)
        z = gl * jax.nn.sigmoid(jnp.dot(gl.astype(BF16), wglu_ref[...], preferred_element_type=F32) + bglu_ref[...])
        return z.astype(BF16)

    def project(n, z):
        ls = slice(n * pos, (n + 1) * pos)
        mix = (jnp.dot(attn_ref[:, ls].reshape(MIX_ROWS, ATT_WIDTH), wo_ref[:ATT_WIDTH], preferred_element_type=F32)
               + jnp.dot(z, wo_ref[ATT_WIDTH:], preferred_element_type=F32))
        res = _layer_norm(ALPHA * x_ref[:, ls].reshape(MIX_ROWS, d) + mix, g_ref[...], b_ref[...])
        o_ref[:, ls] = res.reshape(nb, pos, d)

    relayout(0)
    z = glu(0)
    for n in range(n_pieces):
        if n + 1 < n_pieces:
            relayout(n + 1)
            z_next = glu(n + 1)
        project(n, z)
        z = z_next


def _mix_ln1(x, attn, y_tm, wglu_bf, b_glu, wo_bf, ln_g, ln_b, tl=128):
    B, L, D = x.shape
    assert MIX_ROWS % (B * CHUNK) == 0 and tl % (MIX_ROWS // B) == 0
    resident = lambda shape: pl.BlockSpec(shape, lambda l: (0, 0), pipeline_mode=pl.Buffered(1))
    return pl.pallas_call(
        _mix_ln1_kernel,
        grid=(L // tl,),
        in_specs=[
            pl.BlockSpec((B, tl, D), lambda l: (0, l, 0)),
            pl.BlockSpec((B, tl, ATT_WIDTH), lambda l: (0, l, 0)),
            pl.BlockSpec((LANE_TILES, GROUPS_PER_TILE, tl // CHUNK * B, CHUNK_COLS), lambda l: (0, 0, l, 0)),
            resident((SSM_WIDTH, SSM_WIDTH)),
            resident((1, SSM_WIDTH)),
            resident((D, D)),
            resident((1, D)),
            resident((1, D)),
        ],
        out_specs=pl.BlockSpec((B, tl, D), lambda l: (0, l, 0)),
        out_shape=jax.ShapeDtypeStruct((B, L, D), F32),
        scratch_shapes=[pltpu.VMEM((LANE_TILES, tl * B, LANES), F32)],
        compiler_params=_cparams(("parallel",)),
        name="mix_ln1",
    )(x, attn, y_tm, wglu_bf, b_glu, wo_bf, ln_g, ln_b)


HALO = 16
FFN_SUB = 256
FFN_DOWN_PARTS = 2
FFN_ROWS = 128


def _ffn_kernel(x_ref, xh_ref, p_ref, wple_ref, wgate_ref, wup_ref, cw_ref, cb_ref, wd_ref, g_ref, b_ref, o_ref,
                x32_sc, xh32_sc, p32_sc, xb_sc, acc_sc, hg_sc, hv_sc, act_sc, *, tl):
    nb = x_ref.shape[0]
    tm = tl * nb
    d_slabs = x_ref.shape[2] // LANES
    hp = (CONV_WIDTH - 1)
    for bb in range(nb):
        for s in range(d_slabs):
            lanes = slice(s * LANES, (s + 1) * LANES)
            x32_sc[s, pl.ds(bb, tl, stride=nb), :] = x_ref[bb, :, lanes]
            xh32_sc[s, pl.ds(bb, hp, stride=nb), :] = xh_ref[bb, SUBLANES - hp:, lanes]
        for s in range(p_ref.shape[2] // LANES):
            p32_sc[s, pl.ds(bb, tl, stride=nb), :] = p_ref[bb, :, s * LANES:(s + 1) * LANES]
    x_tm = jnp.concatenate([x32_sc[s] for s in range(d_slabs)], axis=1)
    p_tm = jnp.concatenate([p32_sc[s] for s in range(p32_sc.shape[0])], axis=1)
    keep = jnp.where(pl.program_id(0) == 0, 0.0, 1.0)
    xb_sc[0:HALO] = (jnp.concatenate([xh32_sc[s] for s in range(d_slabs)], axis=1) * keep).astype(BF16)
    xb_sc[HALO:] = x_tm.astype(BF16)
    gate = jax.nn.sigmoid(jnp.dot(xb_sc[HALO:], wgate_ref[...], preferred_element_type=F32))
    ple = jnp.dot(p_tm.astype(BF16), wple_ref[...], preferred_element_type=F32) * gate
    acc_sc[...] = ALPHA * x_tm + ple

    def conv(h_sc, slot, cols, r0, rows):
        out = cb_ref[:, cols] + cw_ref[CONV_WIDTH - 1:CONV_WIDTH, cols] * h_sc[slot, HALO + r0:HALO + r0 + rows]
        for j in range(CONV_WIDTH - 1):
            back = (CONV_WIDTH - 1 - j) * nb
            out = out + cw_ref[j:j + 1, cols] * h_sc[slot, HALO + r0 - back:HALO + r0 - back + rows]
        return out

    n_sub = D_FF // FFN_SUB
    g_cols = lambda n: slice(n * FFN_SUB, (n + 1) * FFN_SUB)
    v_cols = lambda n: slice(D_FF + n * FFN_SUB, D_FF + (n + 1) * FFN_SUB)

    def up(n):
        xb = xb_sc[...]
        hg_sc[n % 2] = jnp.dot(xb, wup_ref[:, g_cols(n)], preferred_element_type=F32)
        hv_sc[n % 2] = jnp.dot(xb, wup_ref[:, v_cols(n)], preferred_element_type=F32)

    def gate_block(n):
        for r0 in range(0, tm, FFN_ROWS):
            gc = conv(hg_sc, n % 2, g_cols(n), r0, FFN_ROWS)
            vc = conv(hv_sc, n % 2, v_cols(n), r0, FFN_ROWS)
            act_sc[r0:r0 + FFN_ROWS, g_cols(n)] = (gc * jax.nn.sigmoid(gc) * vc).astype(BF16)

    def down(lo, hi):
        cols = slice(lo * FFN_SUB, hi * FFN_SUB)
        return jnp.dot(act_sc[:, cols], wd_ref[cols, :], preferred_element_type=F32)

    cuts = [n_sub * k // FFN_DOWN_PARTS for k in range(FFN_DOWN_PARTS + 1)]
    up(0)
    for n in range(n_sub):
        if n + 1 < n_sub:
            up(n + 1)
        gate_block(n)
        if n + 1 in cuts[1:-1]:
            lo = cuts[cuts.index(n + 1) - 1]
            acc_sc[...] += down(lo, n + 1)

    res = _layer_norm(acc_sc[...] + down(cuts[-2], n_sub), g_ref[...], b_ref[...])
    for s in range(d_slabs):
        x32_sc[s] = res[:, s * LANES:(s + 1) * LANES]
    for bb in range(nb):
        for s in range(d_slabs):
            o_ref[bb, :, s * LANES:(s + 1) * LANES] = x32_sc[s, pl.ds(bb, tl, stride=nb), :]


def _ffn_ln2(x1, p, wple_bf, wgate_bf, wup_bf, conv_w, conv_b, wdown_bf, ln_g, ln_b, tl=64):
    B, L, D = x1.shape
    tm = tl * B
    assert D_FF % FFN_SUB == 0 and HALO == (CONV_WIDTH - 1) * B and tm % FFN_ROWS == 0
    const = lambda i: (0, 0)
    resident = lambda shape: pl.BlockSpec(shape, const, pipeline_mode=pl.Buffered(1))
    return pl.pallas_call(
        functools.partial(_ffn_kernel, tl=tl),
        grid=(L // tl,),
        in_specs=[
            pl.BlockSpec((B, tl, D), lambda i: (0, i, 0)),
            pl.BlockSpec((B, SUBLANES, D), lambda i: (0, jnp.maximum(i * (tl // SUBLANES) - 1, 0), 0)),
            pl.BlockSpec((B, tl, PLE_DIM), lambda i: (0, i, 0)),
            resident((PLE_DIM, D)),
            resident((D, D)),
            resident((D, 2 * D_FF)),
            resident((CONV_WIDTH, 2 * D_FF)),
            resident((1, 2 * D_FF)),
            resident((D_FF, D)),
            resident((1, D)),
            resident((1, D)),
        ],
        out_specs=pl.BlockSpec((B, tl, D), lambda i: (0, i, 0)),
        out_shape=jax.ShapeDtypeStruct((B, L, D), F32),
        scratch_shapes=[
            pltpu.VMEM((D // LANES, tm, LANES), F32),
            pltpu.VMEM((D // LANES, HALO, LANES), F32),
            pltpu.VMEM((PLE_DIM // LANES, tm, LANES), F32),
            pltpu.VMEM((tm + HALO, D), BF16),
            pltpu.VMEM((tm, D), F32),
            pltpu.VMEM((2, tm + HALO, FFN_SUB), F32),
            pltpu.VMEM((2, tm + HALO, FFN_SUB), F32),
            pltpu.VMEM((tm, D_FF), BF16),
        ],
        compiler_params=_cparams(("parallel",)),
        name="ffn_ln2",
    )(x1, x1, p, wple_bf, wgate_bf, wup_bf, conv_w, conv_b, wdown_bf, ln_g, ln_b)


def kernel(x, p, w_in, diff_lambda_q1, diff_lambda_k1, diff_lambda_q2, diff_lambda_k2, diff_subln_g, ssm_lambda_re, ssm_lambda_im, ssm_log_dt, ssm_b_re, ssm_b_im, ssm_c_re, ssm_c_im, ssm_d, ssm_w_glu, ssm_b_glu, w_o, ln1_g, ln1_b, ffn_w_up, ffn_conv_w, ffn_conv_b, ffn_w_down, w_ple, w_ple_gate, ln2_g, ln2_b):
    B, L, D = x.shape
    assert B == SUBLANES and D == D_MODEL and L % 512 == 0
    for i in range(DEPTH):
        lam_init = 0.8 - 0.6 * math.exp(-0.3 * i)
        row = lambda a: a[i].reshape(1, -1)
        qt, k, vt, u_tm = _in_proj(x, w_in[i])
        attn = _diff_attn_flat(qt, k, vt, row(diff_lambda_q1), row(diff_lambda_k1), row(diff_lambda_q2),
                          row(diff_lambda_k2), diff_subln_g[i], lam_init)
        tables = _ssm_prep(ssm_lambda_re[i], ssm_lambda_im[i], ssm_log_dt[i], ssm_b_re[i], ssm_b_im[i],
                           ssm_c_re[i], ssm_c_im[i])
        y_tm = _ssm(u_tm, tables, ssm_d[i], B, L)
        x1 = _mix_ln1(x, attn, y_tm, ssm_w_glu[i].astype(BF16), row(ssm_b_glu), w_o[i].astype(BF16),
                      row(ln1_g), row(ln1_b))
        x = _ffn_ln2(x1, p[i], w_ple[i].astype(BF16), w_ple_gate[i].astype(BF16), ffn_w_up[i].astype(BF16),
                     ffn_conv_w[i], row(ffn_conv_b), ffn_w_down[i].astype(BF16), row(ln2_g), row(ln2_b))
    return x
```

```python
import functools
import math

import jax
import jax.numpy as jnp
from jax import lax
from jax.experimental import pallas as pl
from jax.experimental.pallas import tpu as pltpu

F32 = jnp.float32
BF16 = jnp.bfloat16

SUBLANES = 8
LANES = 128

DEPTH = 1
D_MODEL = 1024
PLE_DIM = 256
HEADS = 4
QK_DIM = 64
V_DIM = 2 * QK_DIM
QK_WIDTH = HEADS * 2 * QK_DIM
ATT_WIDTH = HEADS * V_DIM
SSM_WIDTH = D_MODEL - ATT_WIDTH
SSM_GROUP = 16
SSM_GROUPS = SSM_WIDTH // SSM_GROUP
SSM_STATE = 64
D_FF = 2816
CONV_WIDTH = 3
LN_EPS = 1e-5
ALPHA = (2 * DEPTH) ** 0.25
QK_SCALE = QK_DIM ** -0.5 * math.log2(math.e)

CHUNK = 16
CHUNK_COLS = CHUNK * SSM_GROUP
GROUPS_PER_TILE = LANES // SSM_GROUP
LANE_TILES = SSM_WIDTH // LANES
PAIRS_PER_TILE = GROUPS_PER_TILE // 2
STATE_COLS = GROUPS_PER_TILE * SSM_STATE

VMEM_LIMIT = 56 * 1024 * 1024


def _cparams(sem):
    return pltpu.CompilerParams(dimension_semantics=sem, vmem_limit_bytes=VMEM_LIMIT)


TB = 256
VT_ROWS = V_DIM + 16


def _in_proj_kernel(x_ref, w_ref, qt_ref, k_ref, vt_ref, u_ref, u_sc):
    wq_ref = w_ref.at[:, :QK_WIDTH]
    wk_ref = w_ref.at[:, QK_WIDTH:2 * QK_WIDTH]
    wv_ref = w_ref.at[:, 2 * QK_WIDTH:2 * QK_WIDTH + ATT_WIDTH]
    wu_ref = w_ref.at[:, 2 * QK_WIDTH + ATT_WIDTH:]
    nb, tl, d = x_ref.shape
    jt = tl // CHUNK
    xb = x_ref[...].reshape(nb * tl, d).astype(BF16)
    hu = jnp.dot(xb, wu_ref[...], preferred_element_type=F32)
    for b in range(nb):
        for q in range(LANE_TILES):
            u_sc[q, pl.ds(b, tl, stride=nb), :] = hu[b * tl:(b + 1) * tl, q * LANES:(q + 1) * LANES]
    for q in range(LANE_TILES):
        tiles = u_sc[q].reshape(jt, CHUNK, nb, LANES)
        slabs = _chunk_slabs([tiles[:, r].reshape(jt * nb, LANES) for r in range(CHUNK)])
        for g in range(GROUPS_PER_TILE):
            u_ref[q, g] = slabs[g]
    tn = (((0,), (1,)), ((), ()))
    qt = lax.dot_general(wq_ref[...], xb, tn, preferred_element_type=F32) * QK_SCALE
    vt = lax.dot_general(wv_ref[...], xb, tn, preferred_element_type=F32)
    pad_row = lax.broadcasted_iota(jnp.int32, (VT_ROWS - V_DIM, TB), 0)
    ones_row = jnp.where(pad_row == 0, 1.0, 0.0).astype(BF16)
    for b in range(nb):
        for h in range(HEADS):
            for c in range(tl // TB):
                cols = slice(b * tl + c * TB, b * tl + (c + 1) * TB)
                qt_ref[b, h, c] = qt[h * LANES:(h + 1) * LANES, cols].astype(BF16)
                vt_ref[b, h, c, :V_DIM] = vt[h * LANES:(h + 1) * LANES, cols].astype(BF16)
                vt_ref[b, h, c, V_DIM:] = ones_row
    k_ref[...] = jnp.dot(xb, wk_ref[...], preferred_element_type=F32).astype(BF16).reshape(k_ref.shape)


def _in_proj(x, w_in, tl=256):
    B, L, D = x.shape
    slab = lambda rows: pl.BlockSpec((B, HEADS, tl // TB, rows, TB), lambda l: (0, 0, l, 0, 0))
    slab_shape = lambda rows: jax.ShapeDtypeStruct((B, HEADS, L // TB, rows, TB), BF16)
    return pl.pallas_call(
        _in_proj_kernel,
        grid=(L // tl,),
        in_specs=[
            pl.BlockSpec((B, tl, D), lambda l: (0, l, 0)),
            pl.BlockSpec(w_in.shape, lambda l: (0, 0), pipeline_mode=pl.Buffered(1)),
        ],
        out_specs=[
            slab(LANES),
            pl.BlockSpec((B, tl, QK_WIDTH), lambda l: (0, l, 0)),
            slab(VT_ROWS),
            pl.BlockSpec((LANE_TILES, GROUPS_PER_TILE, tl // CHUNK * B, CHUNK_COLS), lambda l: (0, 0, l, 0)),
        ],
        out_shape=[
            slab_shape(LANES),
            jax.ShapeDtypeStruct((B, L, QK_WIDTH), BF16),
            slab_shape(VT_ROWS),
            jax.ShapeDtypeStruct((LANE_TILES, GROUPS_PER_TILE, L // CHUNK * B, CHUNK_COLS), F32),
        ],
        scratch_shapes=[pltpu.VMEM((LANE_TILES, tl * B, LANES), F32)],
        compiler_params=_cparams(("parallel",)),
        name="in_proj",
    )(x, w_in.astype(BF16))


def _attn_flat_kernel(lq1_ref, lk1_ref, lq2_ref, lk2_ref, g_ref, qt_ref, k_ref, vt_ref, o_ref,
                      acc_sc, qst_sc, s_sc, p_sc, bm_sc, st_sc, al_sc, *, lam_init):
    n_q = qt_ref.shape[2]
    n_pairs = n_q * (n_q + 1) // 2
    assert TB == 2 * LANES and n_pairs % 2 == 0
    heads = range(HEADS)
    n_strips = 2 * TB // LANES
    lam = (jnp.exp(jnp.sum(lq1_ref[...] * lk1_ref[...], axis=-1, keepdims=True))
           - jnp.exp(jnp.sum(lq2_ref[...] * lk2_ref[...], axis=-1, keepdims=True)) + lam_init)

    def prepare(i, c):
        for hh in heads:
            qt = qt_ref[0, hh, i]
            row = lax.broadcasted_iota(jnp.int32, qt.shape, 0)
            zero = jnp.zeros_like(qt)
            qst_sc[i, hh, :, :TB] = jnp.where(row < QK_DIM, qt, zero)
            qst_sc[i, hh, :, TB:] = jnp.where(row >= QK_DIM, qt, zero)
            acc_sc[i, hh] = jnp.zeros(acc_sc.shape[2:], F32)
            st_sc[i, hh] = jnp.full((1, 2 * TB), -jnp.inf, F32)
        return c

    lax.fori_loop(0, n_q, prepare, 0)
    p_sc[1] = jnp.zeros(p_sc.shape[1:], BF16)
    al_sc[1] = jnp.ones(al_sc.shape[1:], F32)

    def diag_place(i):
        r = i % 4
        return jnp.where(jnp.logical_or(r == 1, r == 2), 1, 0)

    def is_diag(pair):
        return pair[1] == diag_place(pair[0])

    def block_of(pair):
        i, k = pair
        d = diag_place(i)
        return jnp.where(k == d, i, jnp.where(k < d, k, k - 1))

    def scores(pair, slot):
        i, j = pair[0], block_of(pair)
        rows = pl.ds(pl.multiple_of(j * TB, TB), TB)
        for hh in heads:
            kb = k_ref[0, rows, hh * LANES:(hh + 1) * LANES]
            s = jnp.dot(kb, qst_sc[i, hh], preferred_element_type=F32)
            s_sc[slot, hh] = s
            bm_sc[slot, hh] = jnp.max(s, axis=0, keepdims=True)

    def values(pair, slot):
        i, j = pair[0], block_of(pair)
        for hh in heads:
            pv = jnp.dot(vt_ref[0, hh, j], p_sc[slot, hh], preferred_element_type=F32)
            acc_sc[i, hh] = al_sc[slot, hh] * acc_sc[i, hh] + pv

    def softmax(pair, slot, diagonal):
        i, _ = pair
        for hh in heads:
            for c in range(n_strips):
                lanes = slice(c * LANES, (c + 1) * LANES)
                m = st_sc[i, hh, :, lanes]
                if not diagonal:
                    mc = jnp.maximum(m, bm_sc[slot, hh, :, lanes])
                    p = jnp.exp2(s_sc[slot, hh, :, lanes] - mc)
                else:
                    s = s_sc[slot, hh, :, lanes]
                    key = lax.broadcasted_iota(jnp.int32, s.shape, 0)
                    qry = lax.broadcasted_iota(jnp.int32, s.shape, 1) + (c * LANES) % TB
                    s = jnp.where(key <= qry, s, -jnp.inf)
                    mc = jnp.maximum(m, jnp.max(s, axis=0, keepdims=True))
                    p = jnp.exp2(s - mc)
                p_sc[slot, hh, :, lanes] = p.astype(BF16)
                st_sc[i, hh, :, lanes] = mc
                al_sc[slot, hh, :, lanes] = jnp.exp2(m - mc)

    def succ(pair):
        i, k = pair
        last = k == i
        i2 = jnp.minimum(jnp.where(last, i + 1, i), n_q - 1)
        return i2, jnp.where(last, 0, k + 1)

    def two_steps(t, carry):
        prev, cur = carry
        nxt = succ(cur)
        nxt2 = succ(nxt)

        def body(diag0):
            scores(nxt, 1)
            values(prev, 1)
            softmax(cur, 0, diag0)
            scores(nxt2, 0)
            values(cur, 0)
            softmax(nxt, 1, False)

        d0 = is_diag(cur)
        pl.when(d0)(lambda: body(True))
        pl.when(jnp.logical_not(d0))(lambda: body(False))
        return nxt, nxt2

    zero = jnp.int32(0)
    first = (zero, zero)
    scores(first, 0)
    last, _ = lax.fori_loop(0, n_pairs // 2, two_steps, (first, first))
    values(last, 1)

    def finish(i, c):
        for hh in heads:
            o = acc_sc[i, hh, :V_DIM] * (1.0 / acc_sc[i, hh, V_DIM:V_DIM + 1])
            o = o[:, :TB] - lam * o[:, TB:]
            ms = jnp.mean(o * o, axis=0, keepdims=True)
            o = o * lax.rsqrt(ms + LN_EPS) * (g_ref[...] * (1.0 - lam_init))
            o_ref[0, pl.ds(pl.multiple_of(i * TB, TB), TB), hh * LANES:(hh + 1) * LANES] = o.T.astype(o_ref.dtype)
        return c

    lax.fori_loop(0, n_q, finish, 0)


def _diff_attn_flat(qt, k, vt, lq1, lk1, lq2, lk2, subln_g, lam_init):
    B, L, _ = k.shape
    n_q = L // TB
    vec = pl.BlockSpec((1, QK_DIM), lambda b: (0, 0))
    return pl.pallas_call(
        functools.partial(_attn_flat_kernel, lam_init=lam_init),
        grid=(B,),
        in_specs=[
            vec, vec, vec, vec,
            pl.BlockSpec((V_DIM, 1), lambda b: (0, 0)),
            pl.BlockSpec((1, HEADS, n_q, LANES, TB), lambda b: (b, 0, 0, 0, 0)),
            pl.BlockSpec((1, L, HEADS * LANES), lambda b: (b, 0, 0)),
            pl.BlockSpec((1, HEADS, n_q, VT_ROWS, TB), lambda b: (b, 0, 0, 0, 0)),
        ],
        out_specs=pl.BlockSpec((1, L, HEADS * LANES), lambda b: (b, 0, 0)),
        out_shape=jax.ShapeDtypeStruct((B, L, ATT_WIDTH), BF16),
        scratch_shapes=[
            pltpu.VMEM((n_q, HEADS, VT_ROWS, 2 * TB), F32),
            pltpu.VMEM((n_q, HEADS, LANES, 2 * TB), BF16),
            pltpu.VMEM((2, HEADS, TB, 2 * TB), F32),
            pltpu.VMEM((2, HEADS, TB, 2 * TB), BF16),
            pltpu.VMEM((2, HEADS, 1, 2 * TB), F32),
            pltpu.VMEM((n_q, HEADS, 1, 2 * TB), F32),
            pltpu.VMEM((2, HEADS, 1, 2 * TB), F32),
        ],
        compiler_params=_cparams(("parallel",)),
        name="diff_attn",
    )(lq1, lk1, lq2, lk2, subln_g.reshape(V_DIM, 1), qt, k, vt)


def _block_transpose8(vs):
    lane_blk = lax.broadcasted_iota(jnp.int32, vs[0].shape, 1) // SSM_GROUP
    for d in (4, 2, 1):
        keep = (lane_blk & d) == 0
        new = list(vs)
        for a in range(8):
            if a & d == 0:
                lo, hi = vs[a], vs[a + d]
                new[a] = jnp.where(keep, lo, pltpu.roll(hi, SSM_GROUP * d, 1))
                new[a + d] = jnp.where(keep, pltpu.roll(lo, LANES - SSM_GROUP * d, 1), hi)
        vs = new
    return vs


def _ssm_kernel(u_ref, toep_ref, win_ref, wout_ref, a_ref, d_ref, y_ref, s_sc, x_sc, sp_sc, *, jt):
    rows = jt * SUBLANES

    @pl.when(pl.program_id(1) == 0)
    def _():
        s_sc[...] = jnp.zeros(s_sc.shape, F32)

    ug = [u_ref[0, g].astype(BF16) for g in range(GROUPS_PER_TILE)]

    y = [jnp.dot(ug[g], toep_ref[0, g], preferred_element_type=F32) for g in range(GROUPS_PER_TILE)]

    for m in range(PAIRS_PER_TILE):
        up = jnp.concatenate([ug[2 * m], ug[2 * m + 1]], axis=1)
        xp = jnp.dot(up, win_ref[0, m], preferred_element_type=F32)
        x_sc[0, :, m * LANES:(m + 1) * LANES] = xp[:, :LANES]
        x_sc[1, :, m * LANES:(m + 1) * LANES] = xp[:, LANES:]

    ar = a_ref[0, 0:1, :]
    ai = a_ref[0, 1:2, :]

    def step(j, carry):
        sr, si = carry
        r8 = pl.ds(pl.multiple_of(j * SUBLANES, SUBLANES), SUBLANES)
        sp_sc[0, r8, :] = sr
        sp_sc[1, r8, :] = si
        return (ar * sr - ai * si + x_sc[0, r8, :], ar * si + ai * sr + x_sc[1, r8, :])

    sr, si = lax.fori_loop(0, jt, step, (s_sc[0], s_sc[1]))
    s_sc[0] = sr
    s_sc[1] = si

    for m in range(PAIRS_PER_TILE):
        sp = jnp.concatenate([sp_sc[0, :, m * LANES:(m + 1) * LANES],
                              sp_sc[1, :, m * LANES:(m + 1) * LANES]], axis=1).astype(BF16)
        yc = jnp.dot(sp, wout_ref[0, m], preferred_element_type=F32)
        y[2 * m] = y[2 * m] + yc[:, :CHUNK_COLS]
        y[2 * m + 1] = y[2 * m + 1] + yc[:, CHUNK_COLS:]

    for g in range(GROUPS_PER_TILE):
        y_ref[0, g] = y[g] + d_ref[0, g] * u_ref[0, g]


def _chunk_slabs(tiles):
    z_lo = _block_transpose8(tiles[:8])
    z_hi = _block_transpose8(tiles[8:])
    return [jnp.concatenate([z_lo[g], z_hi[g]], axis=1) for g in range(GROUPS_PER_TILE)]


def _chunk_tiles(slabs):
    out = []
    for half in range(2):
        out += _block_transpose8([s[:, half * LANES:(half + 1) * LANES] for s in slabs])
    return out


def _ssm(u_slab, tables, d_skip, B, L, lt=1024):
    toep, win, wout, a16 = tables
    jt = lt // CHUNK
    n_l = L // lt
    rows = jt * B
    slab = pl.BlockSpec((1, GROUPS_PER_TILE, rows, CHUNK_COLS), lambda q, l: (q, 0, l, 0))
    d_slab = jnp.tile(d_skip.reshape(LANE_TILES, GROUPS_PER_TILE, 1, SSM_GROUP), (1, 1, 1, CHUNK))
    return pl.pallas_call(
        functools.partial(_ssm_kernel, jt=jt),
        grid=(LANE_TILES, n_l),
        in_specs=[
            slab,
            pl.BlockSpec((1, GROUPS_PER_TILE, CHUNK_COLS, CHUNK_COLS), lambda q, l: (q, 0, 0, 0)),
            pl.BlockSpec((1, PAIRS_PER_TILE, 2 * CHUNK_COLS, 2 * LANES), lambda q, l: (q, 0, 0, 0)),
            pl.BlockSpec((1, PAIRS_PER_TILE, 2 * LANES, 2 * CHUNK_COLS), lambda q, l: (q, 0, 0, 0)),
            pl.BlockSpec((1, 2, STATE_COLS), lambda q, l: (q, 0, 0)),
            pl.BlockSpec((1, GROUPS_PER_TILE, 1, CHUNK_COLS), lambda q, l: (q, 0, 0, 0)),
        ],
        out_specs=slab,
        out_shape=jax.ShapeDtypeStruct(u_slab.shape, F32),
        scratch_shapes=[
            pltpu.VMEM((2, B, STATE_COLS), F32),
            pltpu.VMEM((2, rows, STATE_COLS), F32),
            pltpu.VMEM((2, rows, STATE_COLS), F32),
        ],
        compiler_params=_cparams(("parallel", "arbitrary")),
        name="ssm",
    )(u_slab, toep, win, wout, a16, d_slab)


def _cmul(a, b):
    return a[0] * b[0] - a[1] * b[1], a[0] * b[1] + a[1] * b[0]


def _zoh(lr, li, ldt):
    dt = jnp.exp(ldt)
    mag = jnp.exp(lr * dt)
    ab = (mag * jnp.cos(li * dt), mag * jnp.sin(li * dt))
    den = lr * lr + li * li
    zr, zi = ab[0] - 1.0, ab[1]
    return ab, ((zr * lr + zi * li) / den, (zi * lr - zr * li) / den)


def _cpow(base, exponent, bits):
    out = None
    for k in range(bits):
        on = (exponent & (1 << k)) != 0
        sel = (jnp.where(on, base[0], 1.0), jnp.where(on, base[1], 0.0))
        out = sel if out is None else _cmul(out, sel)
        if k + 1 < bits:
            base = _cmul(base, base)
    return out


def _ssm_prep_kernel(row_ref, ct_ref, bt_ref, toep_ref, win_ref, wout_ref, a_ref):
    bits = CHUNK.bit_length() - 1
    lane = lax.broadcasted_iota(jnp.int32, (1, CHUNK_COLS), 1)
    tau = lane // SSM_GROUP
    rowi = lax.broadcasted_iota(jnp.int32, (CHUNK_COLS, 1), 0)
    back = (CHUNK - 1) - rowi // SSM_GROUP
    lane2 = lax.broadcasted_iota(jnp.int32, (1, LANES), 1)
    exact = functools.partial(jnp.dot, precision=lax.Precision.HIGHEST, preferred_element_type=F32)
    h_row = lax.broadcasted_iota(jnp.int32, (SSM_GROUP, CHUNK_COLS), 0)
    repeat = jnp.where(lane % SSM_GROUP == h_row, 1.0, 0.0)
    zoh = [_zoh(row_ref[0, m], row_ref[1, m], row_ref[2, m]) for m in range(PAIRS_PER_TILE)]
    pad = jnp.zeros((SUBLANES - PAIRS_PER_TILE, LANES), F32)
    ab_cols = tuple(jnp.concatenate([z[0][c] for z in zoh] + [pad], axis=0).T for c in range(2))
    a_rows = []
    for m in range(PAIRS_PER_TILE):
        abp, fp = zoh[m]
        pwr = _cpow(abp, back, bits)
        a16 = abp
        for _ in range(bits):
            a16 = _cmul(a16, a16)
        a_rows.append(a16)
        wout_rows = [[None, None], [None, None]]
        for j in range(2):
            g = 2 * m + j
            states = slice(j * SSM_STATE, (j + 1) * SSM_STATE)
            ab = tuple(v[states, m:m + 1] for v in ab_cols)
            pw = _cpow(ab, tau, bits)
            c_rep = tuple(exact(ct_ref[c, g], repeat) for c in range(2))
            g0 = _cmul(pw, c_rep)
            g1 = _cmul(g0, ab)
            mine = (lane2 < SSM_STATE) if j == 0 else (lane2 >= SSM_STATE)
            bb2 = _cmul(fp, (bt_ref[0, g], bt_ref[1, g]))
            bb2 = tuple(jnp.where(mine, v, 0.0) for v in bb2)
            none = jnp.zeros_like(g0[0])
            stack = (lambda v: jnp.concatenate([v, none], axis=0)) if j == 0 else \
                    (lambda v: jnp.concatenate([none, v], axis=0))
            kt = exact(bb2[0], stack(g0[0])) - exact(bb2[1], stack(g0[1]))
            toep_ref[0, g] = jnp.concatenate(
                [jnp.where(lane >= SSM_GROUP * r, pltpu.roll(kt, SSM_GROUP * r, 1), 0.0) if r else kt
                 for r in range(CHUNK)], axis=0).astype(BF16)
            w = _cmul(pwr, tuple(jnp.concatenate([v] * CHUNK, axis=0) for v in bb2))
            win_ref[0, m, j * CHUNK_COLS:(j + 1) * CHUNK_COLS, :] = jnp.concatenate(w, axis=1).astype(BF16)
            wout_rows[0][j] = g1[0]
            wout_rows[1][j] = -g1[1]
        zero = jnp.zeros((SSM_STATE, CHUNK_COLS), F32)
        wout_ref[0, m] = jnp.concatenate([
            jnp.concatenate([wout_rows[0][0], zero], axis=1),
            jnp.concatenate([zero, wout_rows[0][1]], axis=1),
            jnp.concatenate([wout_rows[1][0], zero], axis=1),
            jnp.concatenate([zero, wout_rows[1][1]], axis=1)], axis=0).astype(BF16)
    a_ref[0] = jnp.concatenate([jnp.concatenate([a[0] for a in a_rows], axis=1),
                                jnp.concatenate([a[1] for a in a_rows], axis=1)], axis=0)


def _ssm_prep(lam_re, lam_im, log_dt, b_re, b_im, c_re, c_im):
    G, P, H = b_re.shape
    ldt = jnp.broadcast_to(log_dt[:, None], (G, P))
    rows = jnp.stack([lam_re, lam_im, ldt]).reshape(3, G // 2, 1, 2 * P)
    ct = jnp.stack([c_re, c_im]).transpose(0, 1, 3, 2)
    bt = jnp.tile(jnp.stack([b_re, b_im]).transpose(0, 1, 3, 2), (1, 1, 1, 2))
    gt = GROUPS_PER_TILE
    return pl.pallas_call(
        _ssm_prep_kernel,
        grid=(LANE_TILES,),
        in_specs=[
            pl.BlockSpec((3, PAIRS_PER_TILE, 1, 2 * P), lambda q: (0, q, 0, 0)),
            pl.BlockSpec((2, gt, P, H), lambda q: (0, q, 0, 0)),
            pl.BlockSpec((2, gt, H, 2 * P), lambda q: (0, q, 0, 0)),
        ],
        out_specs=[
            pl.BlockSpec((1, gt, CHUNK_COLS, CHUNK_COLS), lambda q: (q, 0, 0, 0)),
            pl.BlockSpec((1, PAIRS_PER_TILE, 2 * CHUNK_COLS, 2 * LANES), lambda q: (q, 0, 0, 0)),
            pl.BlockSpec((1, PAIRS_PER_TILE, 2 * LANES, 2 * CHUNK_COLS), lambda q: (q, 0, 0, 0)),
            pl.BlockSpec((1, 2, STATE_COLS), lambda q: (q, 0, 0)),
        ],
        out_shape=[
            jax.ShapeDtypeStruct((LANE_TILES, gt, CHUNK_COLS, CHUNK_COLS), BF16),
            jax.ShapeDtypeStruct((LANE_TILES, PAIRS_PER_TILE, 2 * CHUNK_COLS, 2 * LANES), BF16),
            jax.ShapeDtypeStruct((LANE_TILES, PAIRS_PER_TILE, 2 * LANES, 2 * CHUNK_COLS), BF16),
            jax.ShapeDtypeStruct((LANE_TILES, 2, STATE_COLS), F32),
        ],
        compiler_params=_cparams(("parallel",)),
        name="ssm_prep",
    )(rows, ct, bt)


def _layer_norm(r, g, b):
    mu = jnp.mean(r, axis=-1, keepdims=True)
    c = r - mu
    var = jnp.mean(c * c, axis=-1, keepdims=True)
    return c * lax.rsqrt(var + LN_EPS) * g + b


MIX_ROWS = 256


def _gelu_tanh(y):
    return 0.5 * y * (1.0 + jnp.tanh(math.sqrt(2.0 / math.pi) * (y + 0.044715 * (y * y * y))))


def _mix_ln1_kernel(x_ref, attn_ref, y_ref, wglu_ref, bglu_ref, wo_ref, g_ref, b_ref, o_ref, y_sc):
    nb, tl, d = x_ref.shape
    pos = MIX_ROWS // nb
    n_pieces = tl // pos
    jp = pos // CHUNK

    def relayout(n):
        rows = slice(n * jp * nb, (n + 1) * jp * nb)
        for q in range(LANE_TILES):
            tiles = _chunk_tiles([y_ref[q, g, rows] for g in range(GROUPS_PER_TILE)])
            y_sc[q, n * MIX_ROWS:(n + 1) * MIX_ROWS] = jnp.stack(
                [t.reshape(jp, nb, LANES) for t in tiles], axis=1).reshape(MIX_ROWS, LANES)

    def glu(n):
        y = jnp.concatenate(
            [jnp.concatenate([y_sc[q, pl.ds(n * MIX_ROWS + b, pos, stride=nb), :] for q in range(LANE_TILES)], axis=1)
             for b in range(nb)], axis=0)
        gl = _gelu_tanh(y)
        z = gl * jax.nn.sigmoid(jnp.dot(gl.astype(BF16), wglu_ref[...], preferred_element_type=F32) + bglu_ref[...])
        return z.astype(BF16)

    def project(n, z):
        ls = slice(n * pos, (n + 1) * pos)
        mix = (jnp.dot(attn_ref[:, ls].reshape(MIX_ROWS, ATT_WIDTH), wo_ref[:ATT_WIDTH], preferred_element_type=F32)
               + jnp.dot(z, wo_ref[ATT_WIDTH:], preferred_element_type=F32))
        res = _layer_norm(ALPHA * x_ref[:, ls].reshape(MIX_ROWS, d) + mix, g_ref[...], b_ref[...])
        o_ref[:, ls] = res.reshape(nb, pos, d)

    relayout(0)
    z = glu(0)
    for n in range(n_pieces):
        if n + 1 < n_pieces:
            relayout(n + 1)
            z_next = glu(n + 1)
        project(n, z)
        z = z_next


def _mix_ln1(x, attn, y_tm, wglu_bf, b_glu, wo_bf, ln_g, ln_b, tl=256):
    B, L, D = x.shape
    assert MIX_ROWS % (B * CHUNK) == 0 and tl % (MIX_ROWS // B) == 0
    resident = lambda shape: pl.BlockSpec(shape, lambda l: (0, 0), pipeline_mode=pl.Buffered(1))
    return pl.pallas_call(
        _mix_ln1_kernel,
        grid=(L // tl,),
        in_specs=[
            pl.BlockSpec((B, tl, D), lambda l: (0, l, 0)),
            pl.BlockSpec((B, tl, ATT_WIDTH), lambda l: (0, l, 0)),
            pl.BlockSpec((LANE_TILES, GROUPS_PER_TILE, tl // CHUNK * B, CHUNK_COLS), lambda l: (0, 0, l, 0)),
            resident((SSM_WIDTH, SSM_WIDTH)),
            resident((1, SSM_WIDTH)),
            resident((D, D)),
            resident((1, D)),
            resident((1, D)),
        ],
        out_specs=pl.BlockSpec((B, tl, D), lambda l: (0, l, 0)),
        out_shape=jax.ShapeDtypeStruct((B, L, D), F32),
        scratch_shapes=[pltpu.VMEM((LANE_TILES, tl * B, LANES), F32)],
        compiler_params=_cparams(("parallel",)),
        name="mix_ln1",
    )(x, attn, y_tm, wglu_bf, b_glu, wo_bf, ln_g, ln_b)


HALO = 16
FFN_SUB = 256
FFN_DOWN_PARTS = 2
FFN_ROWS = 128


def _ffn_kernel(x_ref, xh_ref, p_ref, wple_ref, wgate_ref, wup_ref, cw_ref, cb_ref, wd_ref, g_ref, b_ref, o_ref,
                x32_sc, xh32_sc, p32_sc, xb_sc, acc_sc, hg_sc, hv_sc, act_sc, *, tl):
    nb = x_ref.shape[0]
    tm = tl * nb
    d_slabs = x_ref.shape[2] // LANES
    hp = (CONV_WIDTH - 1)
    for bb in range(nb):
        for s in range(d_slabs):
            lanes = slice(s * LANES, (s + 1) * LANES)
            x32_sc[s, pl.ds(bb, tl, stride=nb), :] = x_ref[bb, :, lanes]
            xh32_sc[s, pl.ds(bb, hp, stride=nb), :] = xh_ref[bb, SUBLANES - hp:, lanes]
        for s in range(p_ref.shape[2] // LANES):
            p32_sc[s, pl.ds(bb, tl, stride=nb), :] = p_ref[bb, :, s * LANES:(s + 1) * LANES]
    x_tm = jnp.concatenate([x32_sc[s] for s in range(d_slabs)], axis=1)
    p_tm = jnp.concatenate([p32_sc[s] for s in range(p32_sc.shape[0])], axis=1)
    keep = jnp.where(pl.program_id(0) == 0, 0.0, 1.0)
    xb_sc[0:HALO] = (jnp.concatenate([xh32_sc[s] for s in range(d_slabs)], axis=1) * keep).astype(BF16)
    xb_sc[HALO:] = x_tm.astype(BF16)
    gate = jax.nn.sigmoid(jnp.dot(xb_sc[HALO:], wgate_ref[...], preferred_element_type=F32))
    ple = jnp.dot(p_tm.astype(BF16), wple_ref[...], preferred_element_type=F32) * gate
    acc_sc[...] = ALPHA * x_tm + ple

    def conv(h_sc, slot, cols, r0, rows):
        out = cb_ref[:, cols] + cw_ref[CONV_WIDTH - 1:CONV_WIDTH, cols] * h_sc[slot, HALO + r0:HALO + r0 + rows]
        for j in range(CONV_WIDTH - 1):
            back = (CONV_WIDTH - 1 - j) * nb
            out = out + cw_ref[j:j + 1, cols] * h_sc[slot, HALO + r0 - back:HALO + r0 - back + rows]
        return out

    n_sub = D_FF // FFN_SUB
    g_cols = lambda n: slice(n * FFN_SUB, (n + 1) * FFN_SUB)
    v_cols = lambda n: slice(D_FF + n * FFN_SUB, D_FF + (n + 1) * FFN_SUB)

    def up(n):
        xb = xb_sc[...]
        hg_sc[n % 2] = jnp.dot(xb, wup_ref[:, g_cols(n)], preferred_element_type=F32)
        hv_sc[n % 2] = jnp.dot(xb, wup_ref[:, v_cols(n)], preferred_element_type=F32)

    def gate_block(n):
        for r0 in range(0, tm, FFN_ROWS):
            gc = conv(hg_sc, n % 2, g_cols(n), r0, FFN_ROWS)
            vc = conv(hv_sc, n % 2, v_cols(n), r0, FFN_ROWS)
            act_sc[r0:r0 + FFN_ROWS, g_cols(n)] = (gc * jax.nn.sigmoid(gc) * vc).astype(BF16)

    def down(lo, hi):
        cols = slice(lo * FFN_SUB, hi * FFN_SUB)
        return jnp.dot(act_sc[:, cols], wd_ref[cols, :], preferred_element_type=F32)

    cuts = [n_sub * k // FFN_DOWN_PARTS for k in range(FFN_DOWN_PARTS + 1)]
    up(0)
    for n in range(n_sub):
        if n + 1 < n_sub:
            up(n + 1)
        gate_block(n)
        if n + 1 in cuts[1:-1]:
            lo = cuts[cuts.index(n + 1) - 1]
            acc_sc[...] += down(lo, n + 1)

    res = _layer_norm(acc_sc[...] + down(cuts[-2], n_sub), g_ref[...], b_ref[...])
    for s in range(d_slabs):
        x32_sc[s] = res[:, s * LANES:(s + 1) * LANES]
    for bb in range(nb):
        for s in range(d_slabs):
            o_ref[bb, :, s * LANES:(s + 1) * LANES] = x32_sc[s, pl.ds(bb, tl, stride=nb), :]


def _ffn_ln2(x1, p, wple_bf, wgate_bf, wup_bf, conv_w, conv_b, wdown_bf, ln_g, ln_b, tl=64):
    B, L, D = x1.shape
    tm = tl * B
    assert D_FF % FFN_SUB == 0 and HALO == (CONV_WIDTH - 1) * B and tm % FFN_ROWS == 0
    const = lambda i: (0, 0)
    resident = lambda shape: pl.BlockSpec(shape, const, pipeline_mode=pl.Buffered(1))
    return pl.pallas_call(
        functools.partial(_ffn_kernel, tl=tl),
        grid=(L // tl,),
        in_specs=[
            pl.BlockSpec((B, tl, D), lambda i: (0, i, 0)),
            pl.BlockSpec((B, SUBLANES, D), lambda i: (0, jnp.maximum(i * (tl // SUBLANES) - 1, 0), 0)),
            pl.BlockSpec((B, tl, PLE_DIM), lambda i: (0, i, 0)),
            resident((PLE_DIM, D)),
            resident((D, D)),
            resident((D, 2 * D_FF)),
            resident((CONV_WIDTH, 2 * D_FF)),
            resident((1, 2 * D_FF)),
            resident((D_FF, D)),
            resident((1, D)),
            resident((1, D)),
        ],
        out_specs=pl.BlockSpec((B, tl, D), lambda i: (0, i, 0)),
        out_shape=jax.ShapeDtypeStruct((B, L, D), F32),
        scratch_shapes=[
            pltpu.VMEM((D // LANES, tm, LANES), F32),
            pltpu.VMEM((D // LANES, HALO, LANES), F32),
            pltpu.VMEM((PLE_DIM // LANES, tm, LANES), F32),
            pltpu.VMEM((tm + HALO, D), BF16),
            pltpu.VMEM((tm, D), F32),
            pltpu.VMEM((2, tm + HALO, FFN_SUB), F32),
            pltpu.VMEM((2, tm + HALO, FFN_SUB), F32),
            pltpu.VMEM((tm, D_FF), BF16),
        ],
        compiler_params=_cparams(("parallel",)),
        name="ffn_ln2",
    )(x1, x1, p, wple_bf, wgate_bf, wup_bf, conv_w, conv_b, wdown_bf, ln_g, ln_b)


def kernel(x, p, w_in, diff_lambda_q1, diff_lambda_k1, diff_lambda_q2, diff_lambda_k2, diff_subln_g, ssm_lambda_re, ssm_lambda_im, ssm_log_dt, ssm_b_re, ssm_b_im, ssm_c_re, ssm_c_im, ssm_d, ssm_w_glu, ssm_b_glu, w_o, ln1_g, ln1_b, ffn_w_up, ffn_conv_w, ffn_conv_b, ffn_w_down, w_ple, w_ple_gate, ln2_g, ln2_b):
    B, L, D = x.shape
    assert B == SUBLANES and D == D_MODEL and L % 512 == 0
    for i in range(DEPTH):
        lam_init = 0.8 - 0.6 * math.exp(-0.3 * i)
        row = lambda a: a[i].reshape(1, -1)
        qt, k, vt, u_tm = _in_proj(x, w_in[i])
        attn = _diff_attn_flat(qt, k, vt, row(diff_lambda_q1), row(diff_lambda_k1), row(diff_lambda_q2),
                          row(diff_lambda_k2), diff_subln_g[i], lam_init)
        tables = _ssm_prep(ssm_lambda_re[i], ssm_lambda_im[i], ssm_log_dt[i], ssm_b_re[i], ssm_b_im[i],
                           ssm_c_re[i], ssm_c_im[i])
        y_tm = _ssm(u_tm, tables, ssm_d[i], B, L)
        x1 = _mix_ln1(x, attn, y_tm, ssm_w_glu[i].astype(BF16), row(ssm_b_glu), w_o[i].astype(BF16),
                      row(ln1_g), row(ln1_b))
        x = _ffn_ln2(x1, p[i], w_ple[i].astype(BF16), w_ple_gate[i].astype(BF16), ffn_w_up[i].astype(BF16),
                     ffn_conv_w[i], row(ffn_conv_b), ffn_w_down[i].astype(BF16), row(ln2_g), row(ln2_b))
    return x
```

```python
import functools
import math

import jax
import jax.numpy as jnp
from jax import lax
from jax.experimental import pallas as pl
from jax.experimental.pallas import tpu as pltpu

F32 = jnp.float32
BF16 = jnp.bfloat16

SUBLANES = 8
LANES = 128

DEPTH = 1
D_MODEL = 1024
PLE_DIM = 256
HEADS = 4
QK_DIM = 64
V_DIM = 2 * QK_DIM
QK_WIDTH = HEADS * 2 * QK_DIM
ATT_WIDTH = HEADS * V_DIM
SSM_WIDTH = D_MODEL - ATT_WIDTH
SSM_GROUP = 16
SSM_GROUPS = SSM_WIDTH // SSM_GROUP
SSM_STATE = 64
D_FF = 2816
CONV_WIDTH = 3
LN_EPS = 1e-5
ALPHA = (2 * DEPTH) ** 0.25
QK_SCALE = QK_DIM ** -0.5 * math.log2(math.e)

CHUNK = 16
CHUNK_COLS = CHUNK * SSM_GROUP
GROUPS_PER_TILE = LANES // SSM_GROUP
LANE_TILES = SSM_WIDTH // LANES
PAIRS_PER_TILE = GROUPS_PER_TILE // 2
STATE_COLS = GROUPS_PER_TILE * SSM_STATE

VMEM_LIMIT = 56 * 1024 * 1024


def _cparams(sem):
    return pltpu.CompilerParams(dimension_semantics=sem, vmem_limit_bytes=VMEM_LIMIT)


TB = 256
VT_ROWS = V_DIM + 16


def _in_proj_kernel(x_ref, w_ref, qt_ref, k_ref, vt_ref, u_ref, u_sc):
    wq_ref = w_ref.at[:, :QK_WIDTH]
    wk_ref = w_ref.at[:, QK_WIDTH:2 * QK_WIDTH]
    wv_ref = w_ref.at[:, 2 * QK_WIDTH:2 * QK_WIDTH + ATT_WIDTH]
    wu_ref = w_ref.at[:, 2 * QK_WIDTH + ATT_WIDTH:]
    nb, tl, d = x_ref.shape
    jt = tl // CHUNK
    xb = x_ref[...].reshape(nb * tl, d).astype(BF16)
    hu = jnp.dot(xb, wu_ref[...], preferred_element_type=F32)
    for b in range(nb):
        for q in range(LANE_TILES):
            u_sc[q, pl.ds(b, tl, stride=nb), :] = hu[b * tl:(b + 1) * tl, q * LANES:(q + 1) * LANES]
    for q in range(LANE_TILES):
        tiles = u_sc[q].reshape(jt, CHUNK, nb, LANES)
        slabs = _chunk_slabs([tiles[:, r].reshape(jt * nb, LANES) for r in range(CHUNK)])
        for g in range(GROUPS_PER_TILE):
            u_ref[q, g] = slabs[g]
    tn = (((0,), (1,)), ((), ()))
    qt = lax.dot_general(wq_ref[...], xb, tn, preferred_element_type=F32) * QK_SCALE
    vt = lax.dot_general(wv_ref[...], xb, tn, preferred_element_type=F32)
    pad_row = lax.broadcasted_iota(jnp.int32, (VT_ROWS - V_DIM, TB), 0)
    ones_row = jnp.where(pad_row == 0, 1.0, 0.0).astype(BF16)
    for b in range(nb):
        for h in range(HEADS):
            for c in range(tl // TB):
                cols = slice(b * tl + c * TB, b * tl + (c + 1) * TB)
                qt_ref[b, h, c] = qt[h * LANES:(h + 1) * LANES, cols].astype(BF16)
                vt_ref[b, h, c, :V_DIM] = vt[h * LANES:(h + 1) * LANES, cols].astype(BF16)
                vt_ref[b, h, c, V_DIM:] = ones_row
    k_ref[...] = jnp.dot(xb, wk_ref[...], preferred_element_type=F32).astype(BF16).reshape(k_ref.shape)


def _in_proj(x, w_in, tl=256):
    B, L, D = x.shape
    slab = lambda rows: pl.BlockSpec((B, HEADS, tl // TB, rows, TB), lambda l: (0, 0, l, 0, 0))
    slab_shape = lambda rows: jax.ShapeDtypeStruct((B, HEADS, L // TB, rows, TB), BF16)
    return pl.pallas_call(
        _in_proj_kernel,
        grid=(L // tl,),
        in_specs=[
            pl.BlockSpec((B, tl, D), lambda l: (0, l, 0)),
            pl.BlockSpec(w_in.shape, lambda l: (0, 0), pipeline_mode=pl.Buffered(1)),
        ],
        out_specs=[
            slab(LANES),
            pl.BlockSpec((B, tl, QK_WIDTH), lambda l: (0, l, 0)),
            slab(VT_ROWS),
            pl.BlockSpec((LANE_TILES, GROUPS_PER_TILE, tl // CHUNK * B, CHUNK_COLS), lambda l: (0, 0, l, 0)),
        ],
        out_shape=[
            slab_shape(LANES),
            jax.ShapeDtypeStruct((B, L, QK_WIDTH), BF16),
            slab_shape(VT_ROWS),
            jax.ShapeDtypeStruct((LANE_TILES, GROUPS_PER_TILE, L // CHUNK * B, CHUNK_COLS), F32),
        ],
        scratch_shapes=[pltpu.VMEM((LANE_TILES, tl * B, LANES), F32)],
        compiler_params=_cparams(("parallel",)),
        name="in_proj",
    )(x, w_in.astype(BF16))


def _attn_flat_kernel(lq1_ref, lk1_ref, lq2_ref, lk2_ref, g_ref, qt_ref, k_ref, vt_ref, o_ref,
                      acc_sc, qst_sc, s_sc, p_sc, bm_sc, st_sc, al_sc, *, lam_init):
    n_q = qt_ref.shape[2]
    n_pairs = n_q * (n_q + 1) // 2
    assert TB == 2 * LANES and n_pairs % 2 == 0
    heads = range(HEADS)
    n_strips = 2 * TB // LANES
    lam = (jnp.exp(jnp.sum(lq1_ref[...] * lk1_ref[...], axis=-1, keepdims=True))
           - jnp.exp(jnp.sum(lq2_ref[...] * lk2_ref[...], axis=-1, keepdims=True)) + lam_init)

    def prepare(i, c):
        for hh in heads:
            qt = qt_ref[0, hh, i]
            row = lax.broadcasted_iota(jnp.int32, qt.shape, 0)
            zero = jnp.zeros_like(qt)
            qst_sc[i, hh, :, :TB] = jnp.where(row < QK_DIM, qt, zero)
            qst_sc[i, hh, :, TB:] = jnp.where(row >= QK_DIM, qt, zero)
            acc_sc[i, hh] = jnp.zeros(acc_sc.shape[2:], F32)
            st_sc[i, hh] = jnp.full((1, 2 * TB), -jnp.inf, F32)
        return c

    lax.fori_loop(0, n_q, prepare, 0)
    p_sc[1] = jnp.zeros(p_sc.shape[1:], BF16)
    al_sc[1] = jnp.ones(al_sc.shape[1:], F32)

    def diag_place(i):
        r = i % 4
        return jnp.where(jnp.logical_or(r == 1, r == 2), 1, 0)

    def is_diag(pair):
        return pair[1] == diag_place(pair[0])

    def block_of(pair):
        i, k = pair
        d = diag_place(i)
        return jnp.where(k == d, i, jnp.where(k < d, k, k - 1))

    def scores(pair, slot):
        i, j = pair[0], block_of(pair)
        rows = pl.ds(pl.multiple_of(j * TB, TB), TB)
        for hh in heads:
            kb = k_ref[0, rows, hh * LANES:(hh + 1) * LANES]
            s = jnp.dot(kb, qst_sc[i, hh], preferred_element_type=F32)
            s_sc[slot, hh] = s
            bm_sc[slot, hh] = jnp.max(s, axis=0, keepdims=True)

    def values(pair, slot):
        i, j = pair[0], block_of(pair)
        for hh in heads:
            pv = jnp.dot(vt_ref[0, hh, j], p_sc[slot, hh], preferred_element_type=F32)
            acc_sc[i, hh] = al_sc[slot, hh] * acc_sc[i, hh] + pv

    def softmax(pair, slot, diagonal):
        i, _ = pair
        for hh in heads:
            for c in range(n_strips):
                lanes = slice(c * LANES, (c + 1) * LANES)
                m = st_sc[i, hh, :, lanes]
                if not diagonal:
                    mc = jnp.maximum(m, bm_sc[slot, hh, :, lanes])
                    p = jnp.exp2(s_sc[slot, hh, :, lanes] - mc)
                else:
                    s = s_sc[slot, hh, :, lanes]
                    key = lax.broadcasted_iota(jnp.int32, s.shape, 0)
                    qry = lax.broadcasted_iota(jnp.int32, s.shape, 1) + (c * LANES) % TB
                    s = jnp.where(key <= qry, s, -jnp.inf)
                    mc = jnp.maximum(m, jnp.max(s, axis=0, keepdims=True))
                    p = jnp.exp2(s - mc)
                p_sc[slot, hh, :, lanes] = p.astype(BF16)
                st_sc[i, hh, :, lanes] = mc
                al_sc[slot, hh, :, lanes] = jnp.exp2(m - mc)

    def succ(pair):
        i, k = pair
        last = k == i
        i2 = jnp.minimum(jnp.where(last, i + 1, i), n_q - 1)
        return i2, jnp.where(last, 0, k + 1)

    def two_steps(t, carry):
        prev, cur = carry
        nxt = succ(cur)
        nxt2 = succ(nxt)

        def body(diag0):
            scores(nxt, 1)
            values(prev, 1)
            softmax(cur, 0, diag0)
            scores(nxt2, 0)
            values(cur, 0)
            softmax(nxt, 1, False)

        d0 = is_diag(cur)
        pl.when(d0)(lambda: body(True))
        pl.when(jnp.logical_not(d0))(lambda: body(False))
        return nxt, nxt2

    zero = jnp.int32(0)
    first = (zero, zero)
    scores(first, 0)
    last, _ = lax.fori_loop(0, n_pairs // 2, two_steps, (first, first))
    values(last, 1)

    def finish(i, c):
        for hh in heads:
            o = acc_sc[i, hh, :V_DIM] * (1.0 / acc_sc[i, hh, V_DIM:V_DIM + 1])
            o = o[:, :TB] - lam * o[:, TB:]
            ms = jnp.mean(o * o, axis=0, keepdims=True)
            o = o * lax.rsqrt(ms + LN_EPS) * (g_ref[...] * (1.0 - lam_init))
            o_ref[0, pl.ds(pl.multiple_of(i * TB, TB), TB), hh * LANES:(hh + 1) * LANES] = o.T.astype(o_ref.dtype)
        return c

    lax.fori_loop(0, n_q, finish, 0)


def _diff_attn_flat(qt, k, vt, lq1, lk1, lq2, lk2, subln_g, lam_init):
    B, L, _ = k.shape
    n_q = L // TB
    vec = pl.BlockSpec((1, QK_DIM), lambda b: (0, 0))
    return pl.pallas_call(
        functools.partial(_attn_flat_kernel, lam_init=lam_init),
        grid=(B,),
        in_specs=[
            vec, vec, vec, vec,
            pl.BlockSpec((V_DIM, 1), lambda b: (0, 0)),
            pl.BlockSpec((1, HEADS, n_q, LANES, TB), lambda b: (b, 0, 0, 0, 0)),
            pl.BlockSpec((1, L, HEADS * LANES), lambda b: (b, 0, 0)),
            pl.BlockSpec((1, HEADS, n_q, VT_ROWS, TB), lambda b: (b, 0, 0, 0, 0)),
        ],
        out_specs=pl.BlockSpec((1, L, HEADS * LANES), lambda b: (b, 0, 0)),
        out_shape=jax.ShapeDtypeStruct((B, L, ATT_WIDTH), BF16),
        scratch_shapes=[
            pltpu.VMEM((n_q, HEADS, VT_ROWS, 2 * TB), F32),
            pltpu.VMEM((n_q, HEADS, LANES, 2 * TB), BF16),
            pltpu.VMEM((2, HEADS, TB, 2 * TB), F32),
            pltpu.VMEM((2, HEADS, TB, 2 * TB), BF16),
            pltpu.VMEM((2, HEADS, 1, 2 * TB), F32),
            pltpu.VMEM((n_q, HEADS, 1, 2 * TB), F32),
            pltpu.VMEM((2, HEADS, 1, 2 * TB), F32),
        ],
        compiler_params=_cparams(("parallel",)),
        name="diff_attn",
    )(lq1, lk1, lq2, lk2, subln_g.reshape(V_DIM, 1), qt, k, vt)


def _block_transpose8(vs):
    lane_blk = lax.broadcasted_iota(jnp.int32, vs[0].shape, 1) // SSM_GROUP
    for d in (4, 2, 1):
        keep = (lane_blk & d) == 0
        new = list(vs)
        for a in range(8):
            if a & d == 0:
                lo, hi = vs[a], vs[a + d]
                new[a] = jnp.where(keep, lo, pltpu.roll(hi, SSM_GROUP * d, 1))
                new[a + d] = jnp.where(keep, pltpu.roll(lo, LANES - SSM_GROUP * d, 1), hi)
        vs = new
    return vs


def _ssm_kernel(u_ref, toep_ref, win_ref, wout_ref, a_ref, d_ref, y_ref, s_sc, x_sc, sp_sc, *, jt):
    rows = jt * SUBLANES

    @pl.when(pl.program_id(1) == 0)
    def _():
        s_sc[...] = jnp.zeros(s_sc.shape, F32)

    ug = [u_ref[0, g].astype(BF16) for g in range(GROUPS_PER_TILE)]

    y = [jnp.dot(ug[g], toep_ref[0, g], preferred_element_type=F32) for g in range(GROUPS_PER_TILE)]

    for m in range(PAIRS_PER_TILE):
        up = jnp.concatenate([ug[2 * m], ug[2 * m + 1]], axis=1)
        xp = jnp.dot(up, win_ref[0, m], preferred_element_type=F32)
        x_sc[0, :, m * LANES:(m + 1) * LANES] = xp[:, :LANES]
        x_sc[1, :, m * LANES:(m + 1) * LANES] = xp[:, LANES:]

    ar = a_ref[0, 0:1, :]
    ai = a_ref[0, 1:2, :]

    def step(j, carry):
        sr, si = carry
        r8 = pl.ds(pl.multiple_of(j * SUBLANES, SUBLANES), SUBLANES)
        sp_sc[0, r8, :] = sr
        sp_sc[1, r8, :] = si
        return (ar * sr - ai * si + x_sc[0, r8, :], ar * si + ai * sr + x_sc[1, r8, :])

    sr, si = lax.fori_loop(0, jt, step, (s_sc[0], s_sc[1]))
    s_sc[0] = sr
    s_sc[1] = si

    for m in range(PAIRS_PER_TILE):
        sp = jnp.concatenate([sp_sc[0, :, m * LANES:(m + 1) * LANES],
                              sp_sc[1, :, m * LANES:(m + 1) * LANES]], axis=1).astype(BF16)
        yc = jnp.dot(sp, wout_ref[0, m], preferred_element_type=F32)
        y[2 * m] = y[2 * m] + yc[:, :CHUNK_COLS]
        y[2 * m + 1] = y[2 * m + 1] + yc[:, CHUNK_COLS:]

    for g in range(GROUPS_PER_TILE):
        y_ref[0, g] = y[g] + d_ref[0, g] * u_ref[0, g]


def _chunk_slabs(tiles):
    z_lo = _block_transpose8(tiles[:8])
    z_hi = _block_transpose8(tiles[8:])
    return [jnp.concatenate([z_lo[g], z_hi[g]], axis=1) for g in range(GROUPS_PER_TILE)]


def _chunk_tiles(slabs):
    out = []
    for half in range(2):
        out += _block_transpose8([s[:, half * LANES:(half + 1) * LANES] for s in slabs])
    return out


def _ssm(u_slab, tables, d_skip, B, L, lt=1024):
    toep, win, wout, a16 = tables
    jt = lt // CHUNK
    n_l = L // lt
    rows = jt * B
    slab = pl.BlockSpec((1, GROUPS_PER_TILE, rows, CHUNK_COLS), lambda q, l: (q, 0, l, 0))
    d_slab = jnp.tile(d_skip.reshape(LANE_TILES, GROUPS_PER_TILE, 1, SSM_GROUP), (1, 1, 1, CHUNK))
    return pl.pallas_call(
        functools.partial(_ssm_kernel, jt=jt),
        grid=(LANE_TILES, n_l),
        in_specs=[
            slab,
            pl.BlockSpec((1, GROUPS_PER_TILE, CHUNK_COLS, CHUNK_COLS), lambda q, l: (q, 0, 0, 0)),
            pl.BlockSpec((1, PAIRS_PER_TILE, 2 * CHUNK_COLS, 2 * LANES), lambda q, l: (q, 0, 0, 0)),
            pl.BlockSpec((1, PAIRS_PER_TILE, 2 * LANES, 2 * CHUNK_COLS), lambda q, l: (q, 0, 0, 0)),
            pl.BlockSpec((1, 2, STATE_COLS), lambda q, l: (q, 0, 0)),
            pl.BlockSpec((1, GROUPS_PER_TILE, 1, CHUNK_COLS), lambda q, l: (q, 0, 0, 0)),
        ],
        out_specs=slab,
        out_shape=jax.ShapeDtypeStruct(u_slab.shape, F32),
        scratch_shapes=[
            pltpu.VMEM((2, B, STATE_COLS), F32),
            pltpu.VMEM((2, rows, STATE_COLS), F32),
            pltpu.VMEM((2, rows, STATE_COLS), F32),
        ],
        compiler_params=_cparams(("parallel", "arbitrary")),
        name="ssm",
    )(u_slab, toep, win, wout, a16, d_slab)


def _cmul(a, b):
    return a[0] * b[0] - a[1] * b[1], a[0] * b[1] + a[1] * b[0]


def _zoh(lr, li, ldt):
    dt = jnp.exp(ldt)
    mag = jnp.exp(lr * dt)
    ab = (mag * jnp.cos(li * dt), mag * jnp.sin(li * dt))
    den = lr * lr + li * li
    zr, zi = ab[0] - 1.0, ab[1]
    return ab, ((zr * lr + zi * li) / den, (zi * lr - zr * li) / den)


def _cpow(base, exponent, bits):
    out = None
    for k in range(bits):
        on = (exponent & (1 << k)) != 0
        sel = (jnp.where(on, base[0], 1.0), jnp.where(on, base[1], 0.0))
        out = sel if out is None else _cmul(out, sel)
        if k + 1 < bits:
            base = _cmul(base, base)
    return out


def _ssm_prep_kernel(row_ref, ct_ref, bt_ref, toep_ref, win_ref, wout_ref, a_ref):
    bits = CHUNK.bit_length() - 1
    lane = lax.broadcasted_iota(jnp.int32, (1, CHUNK_COLS), 1)
    tau = lane // SSM_GROUP
    rowi = lax.broadcasted_iota(jnp.int32, (CHUNK_COLS, 1), 0)
    back = (CHUNK - 1) - rowi // SSM_GROUP
    lane2 = lax.broadcasted_iota(jnp.int32, (1, LANES), 1)
    exact = functools.partial(jnp.dot, precision=lax.Precision.HIGHEST, preferred_element_type=F32)
    h_row = lax.broadcasted_iota(jnp.int32, (SSM_GROUP, CHUNK_COLS), 0)
    repeat = jnp.where(lane % SSM_GROUP == h_row, 1.0, 0.0)
    zoh = [_zoh(row_ref[0, m], row_ref[1, m], row_ref[2, m]) for m in range(PAIRS_PER_TILE)]
    pad = jnp.zeros((SUBLANES - PAIRS_PER_TILE, LANES), F32)
    ab_cols = tuple(jnp.concatenate([z[0][c] for z in zoh] + [pad], axis=0).T for c in range(2))
    a_rows = []
    for m in range(PAIRS_PER_TILE):
        abp, fp = zoh[m]
        pwr = _cpow(abp, back, bits)
        a16 = abp
        for _ in range(bits):
            a16 = _cmul(a16, a16)
        a_rows.append(a16)
        wout_rows = [[None, None], [None, None]]
        for j in range(2):
            g = 2 * m + j
            states = slice(j * SSM_STATE, (j + 1) * SSM_STATE)
            ab = tuple(v[states, m:m + 1] for v in ab_cols)
            pw = _cpow(ab, tau, bits)
            c_rep = tuple(exact(ct_ref[c, g], repeat) for c in range(2))
            g0 = _cmul(pw, c_rep)
            g1 = _cmul(g0, ab)
            mine = (lane2 < SSM_STATE) if j == 0 else (lane2 >= SSM_STATE)
            bb2 = _cmul(fp, (bt_ref[0, g], bt_ref[1, g]))
            bb2 = tuple(jnp.where(mine, v, 0.0) for v in bb2)
            none = jnp.zeros_like(g0[0])
            stack = (lambda v: jnp.concatenate([v, none], axis=0)) if j == 0 else \
                    (lambda v: jnp.concatenate([none, v], axis=0))
            kt = exact(bb2[0], stack(g0[0])) - exact(bb2[1], stack(g0[1]))
            toep_ref[0, g] = jnp.concatenate(
                [jnp.where(lane >= SSM_GROUP * r, pltpu.roll(kt, SSM_GROUP * r, 1), 0.0) if r else kt
                 for r in range(CHUNK)], axis=0).astype(BF16)
            w = _cmul(pwr, tuple(jnp.concatenate([v] * CHUNK, axis=0) for v in bb2))
            win_ref[0, m, j * CHUNK_COLS:(j + 1) * CHUNK_COLS, :] = jnp.concatenate(w, axis=1).astype(BF16)
            wout_rows[0][j] = g1[0]
            wout_rows[1][j] = -g1[1]
        zero = jnp.zeros((SSM_STATE, CHUNK_COLS), F32)
        wout_ref[0, m] = jnp.concatenate([
            jnp.concatenate([wout_rows[0][0], zero], axis=1),
            jnp.concatenate([zero, wout_rows[0][1]], axis=1),
            jnp.concatenate([wout_rows[1][0], zero], axis=1),
            jnp.concatenate([zero, wout_rows[1][1]], axis=1)], axis=0).astype(BF16)
    a_ref[0] = jnp.concatenate([jnp.concatenate([a[0] for a in a_rows], axis=1),
                                jnp.concatenate([a[1] for a in a_rows], axis=1)], axis=0)


def _ssm_prep(lam_re, lam_im, log_dt, b_re, b_im, c_re, c_im):
    G, P, H = b_re.shape
    ldt = jnp.broadcast_to(log_dt[:, None], (G, P))
    rows = jnp.stack([lam_re, lam_im, ldt]).reshape(3, G // 2, 1, 2 * P)
    ct = jnp.stack([c_re, c_im]).transpose(0, 1, 3, 2)
    bt = jnp.tile(jnp.stack([b_re, b_im]).transpose(0, 1, 3, 2), (1, 1, 1, 2))
    gt = GROUPS_PER_TILE
    return pl.pallas_call(
        _ssm_prep_kernel,
        grid=(LANE_TILES,),
        in_specs=[
            pl.BlockSpec((3, PAIRS_PER_TILE, 1, 2 * P), lambda q: (0, q, 0, 0)),
            pl.BlockSpec((2, gt, P, H), lambda q: (0, q, 0, 0)),
            pl.BlockSpec((2, gt, H, 2 * P), lambda q: (0, q, 0, 0)),
        ],
        out_specs=[
            pl.BlockSpec((1, gt, CHUNK_COLS, CHUNK_COLS), lambda q: (q, 0, 0, 0)),
            pl.BlockSpec((1, PAIRS_PER_TILE, 2 * CHUNK_COLS, 2 * LANES), lambda q: (q, 0, 0, 0)),
            pl.BlockSpec((1, PAIRS_PER_TILE, 2 * LANES, 2 * CHUNK_COLS), lambda q: (q, 0, 0, 0)),
            pl.BlockSpec((1, 2, STATE_COLS), lambda q: (q, 0, 0)),
        ],
        out_shape=[
            jax.ShapeDtypeStruct((LANE_TILES, gt, CHUNK_COLS, CHUNK_COLS), BF16),
            jax.ShapeDtypeStruct((LANE_TILES, PAIRS_PER_TILE, 2 * CHUNK_COLS, 2 * LANES), BF16),
            jax.ShapeDtypeStruct((LANE_TILES, PAIRS_PER_TILE, 2 * LANES, 2 * CHUNK_COLS), BF16),
            jax.ShapeDtypeStruct((LANE_TILES, 2, STATE_COLS), F32),
        ],
        compiler_params=_cparams(("parallel",)),
        name="ssm_prep",
    )(rows, ct, bt)


def _layer_norm(r, g, b):
    mu = jnp.mean(r, axis=-1, keepdims=True)
    c = r - mu
    var = jnp.mean(c * c, axis=-1, keepdims=True)
    return c * lax.rsqrt(var + LN_EPS) * g + b


MIX_ROWS = 256


def _gelu_tanh(y):
    return 0.5 * y * (1.0 + jnp.tanh(math.sqrt(2.0 / math.pi) * (y + 0.044715 * (y * y * y))))


def _mix_ln1_kernel(x_ref, attn_ref, y_ref, wglu_ref, bglu_ref, wo_ref, g_ref, b_ref, o_ref, y_sc):
    nb, tl, d = x_ref.shape
    pos = MIX_ROWS // nb
    n_pieces = tl // pos
    jp = pos // CHUNK

    def relayout(n):
        rows = slice(n * jp * nb, (n + 1) * jp * nb)
        for q in range(LANE_TILES):
            tiles = _chunk_tiles([y_ref[q, g, rows] for g in range(GROUPS_PER_TILE)])
            y_sc[q, n * MIX_ROWS:(n + 1) * MIX_ROWS] = jnp.stack(
                [t.reshape(jp, nb, LANES) for t in tiles], axis=1).reshape(MIX_ROWS, LANES)

    def glu(n):
        y = jnp.concatenate(
            [jnp.concatenate([y_sc[q, pl.ds(n * MIX_ROWS + b, pos, stride=nb), :] for q in range(LANE_TILES)], axis=1)
             for b in range(nb)], axis=0)
        gl = _gelu_tanh(y)
        z = gl * jax.nn.sigmoid(jnp.dot(gl.astype(BF16), wglu_ref[...], preferred_element_type=F32) + bglu_ref[...])
        return z.astype(BF16)

    def project(n, z):
        ls = slice(n * pos, (n + 1) * pos)
        mix = (jnp.dot(attn_ref[:, ls].reshape(MIX_ROWS, ATT_WIDTH), wo_ref[:ATT_WIDTH], preferred_element_type=F32)
               + jnp.dot(z, wo_ref[ATT_WIDTH:], preferred_element_type=F32))
        res = _layer_norm(ALPHA * x_ref[:, ls].reshape(MIX_ROWS, d) + mix, g_ref[...], b_ref[...])
        o_ref[:, ls] = res.reshape(nb, pos, d)

    relayout(0)
    z = glu(0)
    for n in range(n_pieces):
        if n + 1 < n_pieces:
            relayout(n + 1)
            z_next = glu(n + 1)
        project(n, z)
        z = z_next


def _mix_ln1(x, attn, y_tm, wglu_bf, b_glu, wo_bf, ln_g, ln_b, tl=256):
    B, L, D = x.shape
    assert MIX_ROWS % (B * CHUNK) == 0 and tl % (MIX_ROWS // B) == 0
    resident = lambda shape: pl.BlockSpec(shape, lambda l: (0, 0), pipeline_mode=pl.Buffered(1))
    return pl.pallas_call(
        _mix_ln1_kernel,
        grid=(L // tl,),
        in_specs=[
            pl.BlockSpec((B, tl, D), lambda l: (0, l, 0)),
            pl.BlockSpec((B, tl, ATT_WIDTH), lambda l: (0, l, 0)),
            pl.BlockSpec((LANE_TILES, GROUPS_PER_TILE, tl // CHUNK * B, CHUNK_COLS), lambda l: (0, 0, l, 0)),
            resident((SSM_WIDTH, SSM_WIDTH)),
            resident((1, SSM_WIDTH)),
            resident((D, D)),
            resident((1, D)),
            resident((1, D)),
        ],
        out_specs=pl.BlockSpec((B, tl, D), lambda l: (0, l, 0)),
        out_shape=jax.ShapeDtypeStruct((B, L, D), F32),
        scratch_shapes=[pltpu.VMEM((LANE_TILES, tl * B, LANES), F32)],
        compiler_params=_cparams(("parallel",)),
        name="mix_ln1",
    )(x, attn, y_tm, wglu_bf, b_glu, wo_bf, ln_g, ln_b)


HALO = 16
FFN_SUB = 256
FFN_DOWN_PARTS = 2
FFN_ROWS = 128


def _stage_bf16(src_hbm, dst_sc, stage_sc, sems, n_chunks, window):
    def copy(c):
        return pltpu.make_async_copy(src_hbm.at[window(c)], stage_sc.at[c % 2], sems.at[c % 2])

    copy(0).start()
    for c in range(n_chunks):
        if c + 1 < n_chunks:
            copy(c + 1).start()
        copy(c).wait()
        dst_sc[window(c)] = stage_sc[c % 2].astype(BF16)


def _ffn_kernel(x_ref, xh_ref, p_ref, wple_ref, wgate_ref, wup_hbm, cw_ref, cb_ref, wd_hbm, g_ref, b_ref, o_ref,
                x32_sc, xh32_sc, p32_sc, xb_sc, acc_sc, hg_sc, hv_sc, act_sc,
                wup_ref, wd_ref, up_stage, dn_stage, sems, *, tl):
    @pl.when(pl.program_id(0) == 0)
    def _():
        up_w = up_stage.shape[2]
        _stage_bf16(wup_hbm, wup_ref, up_stage, sems.at[0], wup_ref.shape[1] // up_w,
                    lambda c: (slice(None), pl.ds(c * up_w, up_w)))
        dn_h = dn_stage.shape[1]
        _stage_bf16(wd_hbm, wd_ref, dn_stage, sems.at[1], wd_ref.shape[0] // dn_h,
                    lambda c: (pl.ds(c * dn_h, dn_h), slice(None)))

    nb = x_ref.shape[0]
    tm = tl * nb
    d_slabs = x_ref.shape[2] // LANES
    hp = (CONV_WIDTH - 1)
    for bb in range(nb):
        for s in range(d_slabs):
            lanes = slice(s * LANES, (s + 1) * LANES)
            x32_sc[s, pl.ds(bb, tl, stride=nb), :] = x_ref[bb, :, lanes]
            xh32_sc[s, pl.ds(bb, hp, stride=nb), :] = xh_ref[bb, SUBLANES - hp:, lanes]
        for s in range(p_ref.shape[2] // LANES):
            p32_sc[s, pl.ds(bb, tl, stride=nb), :] = p_ref[bb, :, s * LANES:(s + 1) * LANES]
    x_tm = jnp.concatenate([x32_sc[s] for s in range(d_slabs)], axis=1)
    p_tm = jnp.concatenate([p32_sc[s] for s in range(p32_sc.shape[0])], axis=1)
    keep = jnp.where(pl.program_id(0) == 0, 0.0, 1.0)
    xb_sc[0:HALO] = (jnp.concatenate([xh32_sc[s] for s in range(d_slabs)], axis=1) * keep).astype(BF16)
    xb_sc[HALO:] = x_tm.astype(BF16)
    gate = jax.nn.sigmoid(jnp.dot(xb_sc[HALO:], wgate_ref[...], preferred_element_type=F32))
    ple = jnp.dot(p_tm.astype(BF16), wple_ref[...], preferred_element_type=F32) * gate
    acc_sc[...] = ALPHA * x_tm + ple

    def conv(h_sc, slot, cols, r0, rows):
        out = cb_ref[:, cols] + cw_ref[CONV_WIDTH - 1:CONV_WIDTH, cols] * h_sc[slot, HALO + r0:HALO + r0 + rows]
        for j in range(CONV_WIDTH - 1):
            back = (CONV_WIDTH - 1 - j) * nb
            out = out + cw_ref[j:j + 1, cols] * h_sc[slot, HALO + r0 - back:HALO + r0 - back + rows]
        return out

    n_sub = D_FF // FFN_SUB
    g_cols = lambda n: slice(n * FFN_SUB, (n + 1) * FFN_SUB)
    v_cols = lambda n: slice(D_FF + n * FFN_SUB, D_FF + (n + 1) * FFN_SUB)

    def up(n):
        xb = xb_sc[...]
        hg_sc[n % 2] = jnp.dot(xb, wup_ref[:, g_cols(n)], preferred_element_type=F32)
        hv_sc[n % 2] = jnp.dot(xb, wup_ref[:, v_cols(n)], preferred_element_type=F32)

    def gate_block(n):
        for r0 in range(0, tm, FFN_ROWS):
            gc = conv(hg_sc, n % 2, g_cols(n), r0, FFN_ROWS)
            vc = conv(hv_sc, n % 2, v_cols(n), r0, FFN_ROWS)
            act_sc[r0:r0 + FFN_ROWS, g_cols(n)] = (gc * jax.nn.sigmoid(gc) * vc).astype(BF16)

    def down(lo, hi):
        cols = slice(lo * FFN_SUB, hi * FFN_SUB)
        return jnp.dot(act_sc[:, cols], wd_ref[cols, :], preferred_element_type=F32)

    cuts = [n_sub * k // FFN_DOWN_PARTS for k in range(FFN_DOWN_PARTS + 1)]
    up(0)
    for n in range(n_sub):
        if n + 1 < n_sub:
            up(n + 1)
        gate_block(n)
        if n + 1 in cuts[1:-1]:
            lo = cuts[cuts.index(n + 1) - 1]
            acc_sc[...] += down(lo, n + 1)

    res = _layer_norm(acc_sc[...] + down(cuts[-2], n_sub), g_ref[...], b_ref[...])
    for s in range(d_slabs):
        x32_sc[s] = res[:, s * LANES:(s + 1) * LANES]
    for bb in range(nb):
        for s in range(d_slabs):
            o_ref[bb, :, s * LANES:(s + 1) * LANES] = x32_sc[s, pl.ds(bb, tl, stride=nb), :]


FFN_STAGE = 512


def _ffn_ln2(x1, p, wple_bf, wgate_bf, w_up, conv_w, conv_b, w_down, ln_g, ln_b, tl=64):
    B, L, D = x1.shape
    tm = tl * B
    assert D_FF % FFN_SUB == 0 and HALO == (CONV_WIDTH - 1) * B and tm % FFN_ROWS == 0
    const = lambda i: (0, 0)
    resident = lambda shape: pl.BlockSpec(shape, const, pipeline_mode=pl.Buffered(1))
    return pl.pallas_call(
        functools.partial(_ffn_kernel, tl=tl),
        grid=(L // tl,),
        in_specs=[
            pl.BlockSpec((B, tl, D), lambda i: (0, i, 0)),
            pl.BlockSpec((B, SUBLANES, D), lambda i: (0, jnp.maximum(i * (tl // SUBLANES) - 1, 0), 0)),
            pl.BlockSpec((B, tl, PLE_DIM), lambda i: (0, i, 0)),
            resident((PLE_DIM, D)),
            resident((D, D)),
            pl.BlockSpec(memory_space=pl.ANY),
            resident((CONV_WIDTH, 2 * D_FF)),
            resident((1, 2 * D_FF)),
            pl.BlockSpec(memory_space=pl.ANY),
            resident((1, D)),
            resident((1, D)),
        ],
        out_specs=pl.BlockSpec((B, tl, D), lambda i: (0, i, 0)),
        out_shape=jax.ShapeDtypeStruct((B, L, D), F32),
        scratch_shapes=[
            pltpu.VMEM((D // LANES, tm, LANES), F32),
            pltpu.VMEM((D // LANES, HALO, LANES), F32),
            pltpu.VMEM((PLE_DIM // LANES, tm, LANES), F32),
            pltpu.VMEM((tm + HALO, D), BF16),
            pltpu.VMEM((tm, D), F32),
            pltpu.VMEM((2, tm + HALO, FFN_SUB), F32),
            pltpu.VMEM((2, tm + HALO, FFN_SUB), F32),
            pltpu.VMEM((tm, D_FF), BF16),
            pltpu.VMEM((D, 2 * D_FF), BF16),
            pltpu.VMEM((D_FF, D), BF16),
            pltpu.VMEM((2, D, FFN_STAGE), F32),
            pltpu.VMEM((2, FFN_STAGE // 2, D), F32),
            pltpu.SemaphoreType.DMA((2, 2)),
        ],
        compiler_params=_cparams(("arbitrary",)),
        name="ffn_ln2",
    )(x1, x1, p, wple_bf, wgate_bf, w_up, conv_w, conv_b, w_down, ln_g, ln_b)


def kernel(x, p, w_in, diff_lambda_q1, diff_lambda_k1, diff_lambda_q2, diff_lambda_k2, diff_subln_g, ssm_lambda_re, ssm_lambda_im, ssm_log_dt, ssm_b_re, ssm_b_im, ssm_c_re, ssm_c_im, ssm_d, ssm_w_glu, ssm_b_glu, w_o, ln1_g, ln1_b, ffn_w_up, ffn_conv_w, ffn_conv_b, ffn_w_down, w_ple, w_ple_gate, ln2_g, ln2_b):
    B, L, D = x.shape
    assert B == SUBLANES and D == D_MODEL and L % 512 == 0
    for i in range(DEPTH):
        lam_init = 0.8 - 0.6 * math.exp(-0.3 * i)
        row = lambda a: a[i].reshape(1, -1)
        qt, k, vt, u_tm = _in_proj(x, w_in[i])
        attn = _diff_attn_flat(qt, k, vt, row(diff_lambda_q1), row(diff_lambda_k1), row(diff_lambda_q2),
                          row(diff_lambda_k2), diff_subln_g[i], lam_init)
        tables = _ssm_prep(ssm_lambda_re[i], ssm_lambda_im[i], ssm_log_dt[i], ssm_b_re[i], ssm_b_im[i],
                           ssm_c_re[i], ssm_c_im[i])
        y_tm = _ssm(u_tm, tables, ssm_d[i], B, L)
        x1 = _mix_ln1(x, attn, y_tm, ssm_w_glu[i].astype(BF16), row(ssm_b_glu), w_o[i].astype(BF16),
                      row(ln1_g), row(ln1_b))
        x = _ffn_ln2(x1, p[i], w_ple[i].astype(BF16), w_ple_gate[i].astype(BF16), ffn_w_up[i],
                     ffn_conv_w[i], row(ffn_conv_b), ffn_w_down[i], row(ln2_g), row(ln2_b))
    return x
```

```python
import functools
import math

import jax
import jax.numpy as jnp
from jax import lax
from jax.experimental import pallas as pl
from jax.experimental.pallas import tpu as pltpu

F32 = jnp.float32
BF16 = jnp.bfloat16

SUBLANES = 8
LANES = 128

DEPTH = 1
D_MODEL = 1024
PLE_DIM = 256
HEADS = 4
QK_DIM = 64
V_DIM = 2 * QK_DIM
QK_WIDTH = HEADS * 2 * QK_DIM
ATT_WIDTH = HEADS * V_DIM
SSM_WIDTH = D_MODEL - ATT_WIDTH
SSM_GROUP = 16
SSM_GROUPS = SSM_WIDTH // SSM_GROUP
SSM_STATE = 64
D_FF = 2816
CONV_WIDTH = 3
LN_EPS = 1e-5
ALPHA = (2 * DEPTH) ** 0.25
QK_SCALE = QK_DIM ** -0.5 * math.log2(math.e)

CHUNK = 16
CHUNK_COLS = CHUNK * SSM_GROUP
GROUPS_PER_TILE = LANES // SSM_GROUP
LANE_TILES = SSM_WIDTH // LANES
PAIRS_PER_TILE = GROUPS_PER_TILE // 2
STATE_COLS = GROUPS_PER_TILE * SSM_STATE

VMEM_LIMIT = 56 * 1024 * 1024


def _cparams(sem):
    return pltpu.CompilerParams(dimension_semantics=sem, vmem_limit_bytes=VMEM_LIMIT)


TB = 256
VT_ROWS = V_DIM + 16


def _in_proj_kernel(x_ref, w_ref, qt_ref, k_ref, vt_ref, u_ref, u_sc):
    wq_ref = w_ref.at[:, :QK_WIDTH]
    wk_ref = w_ref.at[:, QK_WIDTH:2 * QK_WIDTH]
    wv_ref = w_ref.at[:, 2 * QK_WIDTH:2 * QK_WIDTH + ATT_WIDTH]
    wu_ref = w_ref.at[:, 2 * QK_WIDTH + ATT_WIDTH:]
    nb, tl, d = x_ref.shape
    jt = tl // CHUNK
    xb = x_ref[...].reshape(nb * tl, d).astype(BF16)
    hu = jnp.dot(xb, wu_ref[...], preferred_element_type=F32)
    for b in range(nb):
        for q in range(LANE_TILES):
            u_sc[q, pl.ds(b, tl, stride=nb), :] = hu[b * tl:(b + 1) * tl, q * LANES:(q + 1) * LANES]
    for q in range(LANE_TILES):
        tiles = u_sc[q].reshape(jt, CHUNK, nb, LANES)
        slabs = _chunk_slabs([tiles[:, r].reshape(jt * nb, LANES) for r in range(CHUNK)])
        for g in range(GROUPS_PER_TILE):
            u_ref[q, g] = slabs[g]
    tn = (((0,), (1,)), ((), ()))
    qt = lax.dot_general(wq_ref[...], xb, tn, preferred_element_type=F32) * QK_SCALE
    vt = lax.dot_general(wv_ref[...], xb, tn, preferred_element_type=F32)
    pad_row = lax.broadcasted_iota(jnp.int32, (VT_ROWS - V_DIM, TB), 0)
    ones_row = jnp.where(pad_row == 0, 1.0, 0.0).astype(BF16)
    for b in range(nb):
        for h in range(HEADS):
            for c in range(tl // TB):
                cols = slice(b * tl + c * TB, b * tl + (c + 1) * TB)
                qt_ref[b, h, c] = qt[h * LANES:(h + 1) * LANES, cols].astype(BF16)
                vt_ref[b, h, c, :V_DIM] = vt[h * LANES:(h + 1) * LANES, cols].astype(BF16)
                vt_ref[b, h, c, V_DIM:] = ones_row
    k_ref[...] = jnp.dot(xb, wk_ref[...], preferred_element_type=F32).astype(BF16).reshape(k_ref.shape)


def _in_proj(x, w_in, tl=256):
    B, L, D = x.shape
    slab = lambda rows: pl.BlockSpec((B, HEADS, tl // TB, rows, TB), lambda l: (0, 0, l, 0, 0))
    slab_shape = lambda rows: jax.ShapeDtypeStruct((B, HEADS, L // TB, rows, TB), BF16)
    return pl.pallas_call(
        _in_proj_kernel,
        grid=(L // tl,),
        in_specs=[
            pl.BlockSpec((B, tl, D), lambda l: (0, l, 0)),
            pl.BlockSpec(w_in.shape, lambda l: (0, 0), pipeline_mode=pl.Buffered(1)),
        ],
        out_specs=[
            slab(LANES),
            pl.BlockSpec((B, tl, QK_WIDTH), lambda l: (0, l, 0)),
            slab(VT_ROWS),
            pl.BlockSpec((LANE_TILES, GROUPS_PER_TILE, tl // CHUNK * B, CHUNK_COLS), lambda l: (0, 0, l, 0)),
        ],
        out_shape=[
            slab_shape(LANES),
            jax.ShapeDtypeStruct((B, L, QK_WIDTH), BF16),
            slab_shape(VT_ROWS),
            jax.ShapeDtypeStruct((LANE_TILES, GROUPS_PER_TILE, L // CHUNK * B, CHUNK_COLS), F32),
        ],
        scratch_shapes=[pltpu.VMEM((LANE_TILES, tl * B, LANES), F32)],
        compiler_params=_cparams(("parallel",)),
        name="in_proj",
    )(x, w_in.astype(BF16))


def _attn_flat_kernel(lq1_ref, lk1_ref, lq2_ref, lk2_ref, g_ref, qt_ref, k_ref, vt_ref, o_ref,
                      acc_sc, qst_sc, s_sc, p_sc, bm_sc, st_sc, al_sc, *, lam_init):
    n_q = qt_ref.shape[2]
    n_pairs = n_q * (n_q + 1) // 2
    assert TB == 2 * LANES and n_pairs % 2 == 0
    heads = range(HEADS)
    n_strips = 2 * TB // LANES
    lam = (jnp.exp(jnp.sum(lq1_ref[...] * lk1_ref[...], axis=-1, keepdims=True))
           - jnp.exp(jnp.sum(lq2_ref[...] * lk2_ref[...], axis=-1, keepdims=True)) + lam_init)

    def prepare(i, c):
        for hh in heads:
            qt = qt_ref[0, hh, i]
            row = lax.broadcasted_iota(jnp.int32, qt.shape, 0)
            zero = jnp.zeros_like(qt)
            qst_sc[i, hh, :, :TB] = jnp.where(row < QK_DIM, qt, zero)
            qst_sc[i, hh, :, TB:] = jnp.where(row >= QK_DIM, qt, zero)
            acc_sc[i, hh] = jnp.zeros(acc_sc.shape[2:], F32)
            st_sc[i, hh] = jnp.full((1, 2 * TB), -jnp.inf, F32)
        return c

    lax.fori_loop(0, n_q, prepare, 0)
    p_sc[1] = jnp.zeros(p_sc.shape[1:], BF16)
    al_sc[1] = jnp.ones(al_sc.shape[1:], F32)

    def diag_place(i):
        r = i % 4
        return jnp.where(jnp.logical_or(r == 1, r == 2), 1, 0)

    def is_diag(pair):
        return pair[1] == diag_place(pair[0])

    def block_of(pair):
        i, k = pair
        d = diag_place(i)
        return jnp.where(k == d, i, jnp.where(k < d, k, k - 1))

    def scores(pair, slot):
        i, j = pair[0], block_of(pair)
        rows = pl.ds(pl.multiple_of(j * TB, TB), TB)
        for hh in heads:
            kb = k_ref[0, rows, hh * LANES:(hh + 1) * LANES]
            s = jnp.dot(kb, qst_sc[i, hh], preferred_element_type=F32)
            s_sc[slot, hh] = s
            bm_sc[slot, hh] = jnp.max(s, axis=0, keepdims=True)

    def values(pair, slot):
        i, j = pair[0], block_of(pair)
        for hh in heads:
            pv = jnp.dot(vt_ref[0, hh, j], p_sc[slot, hh], preferred_element_type=F32)
            acc_sc[i, hh] = al_sc[slot, hh] * acc_sc[i, hh] + pv

    def softmax(pair, slot, diagonal):
        i, _ = pair
        for hh in heads:
            for c in range(n_strips):
                lanes = slice(c * LANES, (c + 1) * LANES)
                m = st_sc[i, hh, :, lanes]
                if not diagonal:
                    mc = jnp.maximum(m, bm_sc[slot, hh, :, lanes])
                    p = jnp.exp2(s_sc[slot, hh, :, lanes] - mc)
                else:
                    s = s_sc[slot, hh, :, lanes]
                    key = lax.broadcasted_iota(jnp.int32, s.shape, 0)
                    qry = lax.broadcasted_iota(jnp.int32, s.shape, 1) + (c * LANES) % TB
                    s = jnp.where(key <= qry, s, -jnp.inf)
                    mc = jnp.maximum(m, jnp.max(s, axis=0, keepdims=True))
                    p = jnp.exp2(s - mc)
                p_sc[slot, hh, :, lanes] = p.astype(BF16)
                st_sc[i, hh, :, lanes] = mc
                al_sc[slot, hh, :, lanes] = jnp.exp2(m - mc)

    def succ(pair):
        i, k = pair
        last = k == i
        i2 = jnp.minimum(jnp.where(last, i + 1, i), n_q - 1)
        return i2, jnp.where(last, 0, k + 1)

    def two_steps(t, carry):
        prev, cur = carry
        nxt = succ(cur)
        nxt2 = succ(nxt)

        def body(diag0):
            scores(nxt, 1)
            values(prev, 1)
            softmax(cur, 0, diag0)
            scores(nxt2, 0)
            values(cur, 0)
            softmax(nxt, 1, False)

        d0 = is_diag(cur)
        pl.when(d0)(lambda: body(True))
        pl.when(jnp.logical_not(d0))(lambda: body(False))
        return nxt, nxt2

    zero = jnp.int32(0)
    first = (zero, zero)
    scores(first, 0)
    last, _ = lax.fori_loop(0, n_pairs // 2, two_steps, (first, first))
    values(last, 1)

    def finish(t, c):
        for i in (2 * t, 2 * t + 1):
            for hh in heads:
                o = acc_sc[i, hh, :V_DIM] * (1.0 / acc_sc[i, hh, V_DIM:V_DIM + 1])
                o = o[:, :TB] - lam * o[:, TB:]
                ms = jnp.mean(o * o, axis=0, keepdims=True)
                o = o * lax.rsqrt(ms + LN_EPS) * (g_ref[...] * (1.0 - lam_init))
                rows = pl.ds(pl.multiple_of(i * TB, TB), TB)
                o_ref[0, rows, hh * LANES:(hh + 1) * LANES] = o.T.astype(o_ref.dtype)
        return c

    lax.fori_loop(0, n_q // 2, finish, 0)


def _diff_attn_flat(qt, k, vt, lq1, lk1, lq2, lk2, subln_g, lam_init):
    B, L, _ = k.shape
    n_q = L // TB
    vec = pl.BlockSpec((1, QK_DIM), lambda b: (0, 0))
    return pl.pallas_call(
        functools.partial(_attn_flat_kernel, lam_init=lam_init),
        grid=(B,),
        in_specs=[
            vec, vec, vec, vec,
            pl.BlockSpec((V_DIM, 1), lambda b: (0, 0)),
            pl.BlockSpec((1, HEADS, n_q, LANES, TB), lambda b: (b, 0, 0, 0, 0)),
            pl.BlockSpec((1, L, HEADS * LANES), lambda b: (b, 0, 0)),
            pl.BlockSpec((1, HEADS, n_q, VT_ROWS, TB), lambda b: (b, 0, 0, 0, 0)),
        ],
        out_specs=pl.BlockSpec((1, L, HEADS * LANES), lambda b: (b, 0, 0)),
        out_shape=jax.ShapeDtypeStruct((B, L, ATT_WIDTH), BF16),
        scratch_shapes=[
            pltpu.VMEM((n_q, HEADS, VT_ROWS, 2 * TB), F32),
            pltpu.VMEM((n_q, HEADS, LANES, 2 * TB), BF16),
            pltpu.VMEM((2, HEADS, TB, 2 * TB), F32),
            pltpu.VMEM((2, HEADS, TB, 2 * TB), BF16),
            pltpu.VMEM((2, HEADS, 1, 2 * TB), F32),
            pltpu.VMEM((n_q, HEADS, 1, 2 * TB), F32),
            pltpu.VMEM((2, HEADS, 1, 2 * TB), F32),
        ],
        compiler_params=_cparams(("parallel",)),
        name="diff_attn",
    )(lq1, lk1, lq2, lk2, subln_g.reshape(V_DIM, 1), qt, k, vt)


def _block_transpose8(vs):
    lane_blk = lax.broadcasted_iota(jnp.int32, vs[0].shape, 1) // SSM_GROUP
    for d in (4, 2, 1):
        keep = (lane_blk & d) == 0
        new = list(vs)
        for a in range(8):
            if a & d == 0:
                lo, hi = vs[a], vs[a + d]
                new[a] = jnp.where(keep, lo, pltpu.roll(hi, SSM_GROUP * d, 1))
                new[a + d] = jnp.where(keep, pltpu.roll(lo, LANES - SSM_GROUP * d, 1), hi)
        vs = new
    return vs


def _ssm_kernel(u_ref, toep_ref, win_ref, wout_ref, a_ref, d_ref, y_ref, s_sc, x_sc, sp_sc, *, jt):
    rows = jt * SUBLANES

    @pl.when(pl.program_id(1) == 0)
    def _():
        s_sc[...] = jnp.zeros(s_sc.shape, F32)

    ug = [u_ref[0, g].astype(BF16) for g in range(GROUPS_PER_TILE)]

    y = [jnp.dot(ug[g], toep_ref[0, g], preferred_element_type=F32) for g in range(GROUPS_PER_TILE)]

    for m in range(PAIRS_PER_TILE):
        up = jnp.concatenate([ug[2 * m], ug[2 * m + 1]], axis=1)
        xp = jnp.dot(up, win_ref[0, m], preferred_element_type=F32)
        x_sc[0, :, m * LANES:(m + 1) * LANES] = xp[:, :LANES]
        x_sc[1, :, m * LANES:(m + 1) * LANES] = xp[:, LANES:]

    ar = a_ref[0, 0:1, :]
    ai = a_ref[0, 1:2, :]

    def step(j, carry):
        sr, si = carry
        r8 = pl.ds(pl.multiple_of(j * SUBLANES, SUBLANES), SUBLANES)
        sp_sc[0, r8, :] = sr
        sp_sc[1, r8, :] = si
        return (ar * sr - ai * si + x_sc[0, r8, :], ar * si + ai * sr + x_sc[1, r8, :])

    sr, si = lax.fori_loop(0, jt, step, (s_sc[0], s_sc[1]))
    s_sc[0] = sr
    s_sc[1] = si

    for m in range(PAIRS_PER_TILE):
        sp = jnp.concatenate([sp_sc[0, :, m * LANES:(m + 1) * LANES],
                              sp_sc[1, :, m * LANES:(m + 1) * LANES]], axis=1).astype(BF16)
        yc = jnp.dot(sp, wout_ref[0, m], preferred_element_type=F32)
        y[2 * m] = y[2 * m] + yc[:, :CHUNK_COLS]
        y[2 * m + 1] = y[2 * m + 1] + yc[:, CHUNK_COLS:]

    for g in range(GROUPS_PER_TILE):
        y_ref[0, g] = y[g] + d_ref[0, g] * u_ref[0, g]


def _chunk_slabs(tiles):
    z_lo = _block_transpose8(tiles[:8])
    z_hi = _block_transpose8(tiles[8:])
    return [jnp.concatenate([z_lo[g], z_hi[g]], axis=1) for g in range(GROUPS_PER_TILE)]


def _chunk_tiles(slabs):
    out = []
    for half in range(2):
        out += _block_transpose8([s[:, half * LANES:(half + 1) * LANES] for s in slabs])
    return out


def _ssm(u_slab, tables, d_skip, B, L, lt=512):
    toep, win, wout, a16 = tables
    jt = lt // CHUNK
    n_l = L // lt
    rows = jt * B
    slab = pl.BlockSpec((1, GROUPS_PER_TILE, rows, CHUNK_COLS), lambda q, l: (q, 0, l, 0))
    d_slab = jnp.tile(d_skip.reshape(LANE_TILES, GROUPS_PER_TILE, 1, SSM_GROUP), (1, 1, 1, CHUNK))
    return pl.pallas_call(
        functools.partial(_ssm_kernel, jt=jt),
        grid=(LANE_TILES, n_l),
        in_specs=[
            slab,
            pl.BlockSpec((1, GROUPS_PER_TILE, CHUNK_COLS, CHUNK_COLS), lambda q, l: (q, 0, 0, 0)),
            pl.BlockSpec((1, PAIRS_PER_TILE, 2 * CHUNK_COLS, 2 * LANES), lambda q, l: (q, 0, 0, 0)),
            pl.BlockSpec((1, PAIRS_PER_TILE, 2 * LANES, 2 * CHUNK_COLS), lambda q, l: (q, 0, 0, 0)),
            pl.BlockSpec((1, 2, STATE_COLS), lambda q, l: (q, 0, 0)),
            pl.BlockSpec((1, GROUPS_PER_TILE, 1, CHUNK_COLS), lambda q, l: (q, 0, 0, 0)),
        ],
        out_specs=slab,
        out_shape=jax.ShapeDtypeStruct(u_slab.shape, F32),
        scratch_shapes=[
            pltpu.VMEM((2, B, STATE_COLS), F32),
            pltpu.VMEM((2, rows, STATE_COLS), F32),
            pltpu.VMEM((2, rows, STATE_COLS), F32),
        ],
        compiler_params=_cparams(("parallel", "arbitrary")),
        name="ssm",
    )(u_slab, toep, win, wout, a16, d_slab)


def _cmul(a, b):
    return a[0] * b[0] - a[1] * b[1], a[0] * b[1] + a[1] * b[0]


def _zoh(lr, li, ldt):
    dt = jnp.exp(ldt)
    mag = jnp.exp(lr * dt)
    ab = (mag * jnp.cos(li * dt), mag * jnp.sin(li * dt))
    den = lr * lr + li * li
    zr, zi = ab[0] - 1.0, ab[1]
    return ab, ((zr * lr + zi * li) / den, (zi * lr - zr * li) / den)


def _cpow(base, exponent, bits):
    out = None
    for k in range(bits):
        on = (exponent & (1 << k)) != 0
        sel = (jnp.where(on, base[0], 1.0), jnp.where(on, base[1], 0.0))
        out = sel if out is None else _cmul(out, sel)
        if k + 1 < bits:
            base = _cmul(base, base)
    return out


def _ssm_prep_kernel(row_ref, ct_ref, bt_ref, toep_ref, win_ref, wout_ref, a_ref):
    bits = CHUNK.bit_length() - 1
    lane = lax.broadcasted_iota(jnp.int32, (1, CHUNK_COLS), 1)
    tau = lane // SSM_GROUP
    rowi = lax.broadcasted_iota(jnp.int32, (CHUNK_COLS, 1), 0)
    back = (CHUNK - 1) - rowi // SSM_GROUP
    lane2 = lax.broadcasted_iota(jnp.int32, (1, LANES), 1)
    exact = functools.partial(jnp.dot, precision=lax.Precision.HIGHEST, preferred_element_type=F32)
    h_row = lax.broadcasted_iota(jnp.int32, (SSM_GROUP, CHUNK_COLS), 0)
    repeat = jnp.where(lane % SSM_GROUP == h_row, 1.0, 0.0)
    zoh = [_zoh(row_ref[0, m], row_ref[1, m], row_ref[2, m]) for m in range(PAIRS_PER_TILE)]
    pad = jnp.zeros((SUBLANES - PAIRS_PER_TILE, LANES), F32)
    ab_cols = tuple(jnp.concatenate([z[0][c] for z in zoh] + [pad], axis=0).T for c in range(2))
    a_rows = []
    for m in range(PAIRS_PER_TILE):
        abp, fp = zoh[m]
        pwr = _cpow(abp, back, bits)
        a16 = abp
        for _ in range(bits):
            a16 = _cmul(a16, a16)
        a_rows.append(a16)
        wout_rows = [[None, None], [None, None]]
        for j in range(2):
            g = 2 * m + j
            states = slice(j * SSM_STATE, (j + 1) * SSM_STATE)
            ab = tuple(v[states, m:m + 1] for v in ab_cols)
            pw = _cpow(ab, tau, bits)
            c_rep = tuple(exact(ct_ref[c, g], repeat) for c in range(2))
            g0 = _cmul(pw, c_rep)
            g1 = _cmul(g0, ab)
            mine = (lane2 < SSM_STATE) if j == 0 else (lane2 >= SSM_STATE)
            bb2 = _cmul(fp, (bt_ref[0, g], bt_ref[1, g]))
            bb2 = tuple(jnp.where(mine, v, 0.0) for v in bb2)
            none = jnp.zeros_like(g0[0])
            stack = (lambda v: jnp.concatenate([v, none], axis=0)) if j == 0 else \
                    (lambda v: jnp.concatenate([none, v], axis=0))
            kt = exact(bb2[0], stack(g0[0])) - exact(bb2[1], stack(g0[1]))
            toep_ref[0, g] = jnp.concatenate(
                [jnp.where(lane >= SSM_GROUP * r, pltpu.roll(kt, SSM_GROUP * r, 1), 0.0) if r else kt
                 for r in range(CHUNK)], axis=0).astype(BF16)
            w = _cmul(pwr, tuple(jnp.concatenate([v] * CHUNK, axis=0) for v in bb2))
            win_ref[0, m, j * CHUNK_COLS:(j + 1) * CHUNK_COLS, :] = jnp.concatenate(w, axis=1).astype(BF16)
            wout_rows[0][j] = g1[0]
            wout_rows[1][j] = -g1[1]
        zero = jnp.zeros((SSM_STATE, CHUNK_COLS), F32)
        wout_ref[0, m] = jnp.concatenate([
            jnp.concatenate([wout_rows[0][0], zero], axis=1),
            jnp.concatenate([zero, wout_rows[0][1]], axis=1),
            jnp.concatenate([wout_rows[1][0], zero], axis=1),
            jnp.concatenate([zero, wout_rows[1][1]], axis=1)], axis=0).astype(BF16)
    a_ref[0] = jnp.concatenate([jnp.concatenate([a[0] for a in a_rows], axis=1),
                                jnp.concatenate([a[1] for a in a_rows], axis=1)], axis=0)


def _ssm_prep(lam_re, lam_im, log_dt, b_re, b_im, c_re, c_im):
    G, P, H = b_re.shape
    ldt = jnp.broadcast_to(log_dt[:, None], (G, P))
    rows = jnp.stack([lam_re, lam_im, ldt]).reshape(3, G // 2, 1, 2 * P)
    ct = jnp.stack([c_re, c_im]).transpose(0, 1, 3, 2)
    bt = jnp.tile(jnp.stack([b_re, b_im]).transpose(0, 1, 3, 2), (1, 1, 1, 2))
    gt = GROUPS_PER_TILE
    return pl.pallas_call(
        _ssm_prep_kernel,
        grid=(LANE_TILES,),
        in_specs=[
            pl.BlockSpec((3, PAIRS_PER_TILE, 1, 2 * P), lambda q: (0, q, 0, 0)),
            pl.BlockSpec((2, gt, P, H), lambda q: (0, q, 0, 0)),
            pl.BlockSpec((2, gt, H, 2 * P), lambda q: (0, q, 0, 0)),
        ],
        out_specs=[
            pl.BlockSpec((1, gt, CHUNK_COLS, CHUNK_COLS), lambda q: (q, 0, 0, 0)),
            pl.BlockSpec((1, PAIRS_PER_TILE, 2 * CHUNK_COLS, 2 * LANES), lambda q: (q, 0, 0, 0)),
            pl.BlockSpec((1, PAIRS_PER_TILE, 2 * LANES, 2 * CHUNK_COLS), lambda q: (q, 0, 0, 0)),
            pl.BlockSpec((1, 2, STATE_COLS), lambda q: (q, 0, 0)),
        ],
        out_shape=[
            jax.ShapeDtypeStruct((LANE_TILES, gt, CHUNK_COLS, CHUNK_COLS), BF16),
            jax.ShapeDtypeStruct((LANE_TILES, PAIRS_PER_TILE, 2 * CHUNK_COLS, 2 * LANES), BF16),
            jax.ShapeDtypeStruct((LANE_TILES, PAIRS_PER_TILE, 2 * LANES, 2 * CHUNK_COLS), BF16),
            jax.ShapeDtypeStruct((LANE_TILES, 2, STATE_COLS), F32),
        ],
        compiler_params=_cparams(("parallel",)),
        name="ssm_prep",
    )(rows, ct, bt)


def _layer_norm(r, g, b):
    mu = jnp.mean(r, axis=-1, keepdims=True)
    c = r - mu
    var = jnp.mean(c * c, axis=-1, keepdims=True)
    return c * lax.rsqrt(var + LN_EPS) * g + b


MIX_ROWS = 256


def _gelu_tanh(y):
    return 0.5 * y * (1.0 + jnp.tanh(math.sqrt(2.0 / math.pi) * (y + 0.044715 * (y * y * y))))


def _mix_ln1_kernel(x_ref, attn_ref, y_ref, wglu_ref, bglu_ref, wo_ref, g_ref, b_ref, o_ref, y_sc):
    nb, tl, d = x_ref.shape
    pos = MIX_ROWS // nb
    n_pieces = tl // pos
    jp = pos // CHUNK

    def relayout(n):
        rows = slice(n * jp * nb, (n + 1) * jp * nb)
        for q in range(LANE_TILES):
            tiles = _chunk_tiles([y_ref[q, g, rows] for g in range(GROUPS_PER_TILE)])
            y_sc[q, n * MIX_ROWS:(n + 1) * MIX_ROWS] = jnp.stack(
                [t.reshape(jp, nb, LANES) for t in tiles], axis=1).reshape(MIX_ROWS, LANES)

    def glu(n):
        y = jnp.concatenate(
            [jnp.concatenate([y_sc[q, pl.ds(n * MIX_ROWS + b, pos, stride=nb), :] for q in range(LANE_TILES)], axis=1)
             for b in range(nb)], axis=0)
        gl = _gelu_tanh(y)
        z = gl * jax.nn.sigmoid(jnp.dot(gl.astype(BF16), wglu_ref[...], preferred_element_type=F32) + bglu_ref[...])
        return z.astype(BF16)

    def project(n, z):
        ls = slice(n * pos, (n + 1) * pos)
        mix = (jnp.dot(attn_ref[:, ls].reshape(MIX_ROWS, ATT_WIDTH), wo_ref[:ATT_WIDTH], preferred_element_type=F32)
               + jnp.dot(z, wo_ref[ATT_WIDTH:], preferred_element_type=F32))
        res = _layer_norm(ALPHA * x_ref[:, ls].reshape(MIX_ROWS, d) + mix, g_ref[...], b_ref[...])
        o_ref[:, ls] = res.reshape(nb, pos, d)

    relayout(0)
    z = glu(0)
    for n in range(n_pieces):
        if n + 1 < n_pieces:
            relayout(n + 1)
            z_next = glu(n + 1)
        project(n, z)
        z = z_next


def _mix_ln1(x, attn, y_tm, wglu_bf, b_glu, wo_bf, ln_g, ln_b, tl=256):
    B, L, D = x.shape
    assert MIX_ROWS % (B * CHUNK) == 0 and tl % (MIX_ROWS // B) == 0
    resident = lambda shape: pl.BlockSpec(shape, lambda l: (0, 0), pipeline_mode=pl.Buffered(1))
    return pl.pallas_call(
        _mix_ln1_kernel,
        grid=(L // tl,),
        in_specs=[
            pl.BlockSpec((B, tl, D), lambda l: (0, l, 0)),
            pl.BlockSpec((B, tl, ATT_WIDTH), lambda l: (0, l, 0)),
            pl.BlockSpec((LANE_TILES, GROUPS_PER_TILE, tl // CHUNK * B, CHUNK_COLS), lambda l: (0, 0, l, 0)),
            resident((SSM_WIDTH, SSM_WIDTH)),
            resident((1, SSM_WIDTH)),
            resident((D, D)),
            resident((1, D)),
            resident((1, D)),
        ],
        out_specs=pl.BlockSpec((B, tl, D), lambda l: (0, l, 0)),
        out_shape=jax.ShapeDtypeStruct((B, L, D), F32),
        scratch_shapes=[pltpu.VMEM((LANE_TILES, tl * B, LANES), F32)],
        compiler_params=_cparams(("parallel",)),
        name="mix_ln1",
    )(x, attn, y_tm, wglu_bf, b_glu, wo_bf, ln_g, ln_b)


HALO = 16
FFN_SUB = 256
FFN_DOWN_PARTS = 2
FFN_ROWS = 128


def _ffn_kernel(x_ref, xh_ref, p_ref, wple_ref, wgate_ref, wup_ref, cw_ref, cb_ref, wd_ref, g_ref, b_ref, o_ref,
                x32_sc, xh32_sc, p32_sc, xb_sc, acc_sc, hg_sc, hv_sc, act_sc, *, tl):
    nb = x_ref.shape[0]
    tm = tl * nb
    d_slabs = x_ref.shape[2] // LANES
    hp = (CONV_WIDTH - 1)
    for bb in range(nb):
        for s in range(d_slabs):
            lanes = slice(s * LANES, (s + 1) * LANES)
            x32_sc[s, pl.ds(bb, tl, stride=nb), :] = x_ref[bb, :, lanes]
            xh32_sc[s, pl.ds(bb, hp, stride=nb), :] = xh_ref[bb, SUBLANES - hp:, lanes]
        for s in range(p_ref.shape[2] // LANES):
            p32_sc[s, pl.ds(bb, tl, stride=nb), :] = p_ref[bb, :, s * LANES:(s + 1) * LANES]
    x_tm = jnp.concatenate([x32_sc[s] for s in range(d_slabs)], axis=1)
    p_tm = jnp.concatenate([p32_sc[s] for s in range(p32_sc.shape[0])], axis=1)
    keep = jnp.where(pl.program_id(0) == 0, 0.0, 1.0)
    xb_sc[0:HALO] = (jnp.concatenate([xh32_sc[s] for s in range(d_slabs)], axis=1) * keep).astype(BF16)
    xb_sc[HALO:] = x_tm.astype(BF16)
    gate = jax.nn.sigmoid(jnp.dot(xb_sc[HALO:], wgate_ref[...], preferred_element_type=F32))
    ple = jnp.dot(p_tm.astype(BF16), wple_ref[...], preferred_element_type=F32) * gate
    acc_sc[...] = ALPHA * x_tm + ple

    def conv(h_sc, slot, cols, r0, rows):
        out = cb_ref[:, cols] + cw_ref[CONV_WIDTH - 1:CONV_WIDTH, cols] * h_sc[slot, HALO + r0:HALO + r0 + rows]
        for j in range(CONV_WIDTH - 1):
            back = (CONV_WIDTH - 1 - j) * nb
            out = out + cw_ref[j:j + 1, cols] * h_sc[slot, HALO + r0 - back:HALO + r0 - back + rows]
        return out

    n_sub = D_FF // FFN_SUB
    g_cols = lambda n: slice(n * FFN_SUB, (n + 1) * FFN_SUB)
    v_cols = lambda n: slice(D_FF + n * FFN_SUB, D_FF + (n + 1) * FFN_SUB)

    def up(n):
        xb = xb_sc[...]
        hg_sc[n % 2] = jnp.dot(xb, wup_ref[:, g_cols(n)], preferred_element_type=F32)
        hv_sc[n % 2] = jnp.dot(xb, wup_ref[:, v_cols(n)], preferred_element_type=F32)

    def gate_block(n):
        for r0 in range(0, tm, FFN_ROWS):
            gc = conv(hg_sc, n % 2, g_cols(n), r0, FFN_ROWS)
            vc = conv(hv_sc, n % 2, v_cols(n), r0, FFN_ROWS)
            act_sc[r0:r0 + FFN_ROWS, g_cols(n)] = (gc * jax.nn.sigmoid(gc) * vc).astype(BF16)

    def down(lo, hi):
        cols = slice(lo * FFN_SUB, hi * FFN_SUB)
        return jnp.dot(act_sc[:, cols], wd_ref[cols, :], preferred_element_type=F32)

    cuts = [n_sub * k // FFN_DOWN_PARTS for k in range(FFN_DOWN_PARTS + 1)]
    up(0)
    for n in range(n_sub):
        if n + 1 < n_sub:
            up(n + 1)
        gate_block(n)
        if n + 1 in cuts[1:-1]:
            lo = cuts[cuts.index(n + 1) - 1]
            acc_sc[...] += down(lo, n + 1)

    res = _layer_norm(acc_sc[...] + down(cuts[-2], n_sub), g_ref[...], b_ref[...])
    for s in range(d_slabs):
        x32_sc[s] = res[:, s * LANES:(s + 1) * LANES]
    for bb in range(nb):
        for s in range(d_slabs):
            o_ref[bb, :, s * LANES:(s + 1) * LANES] = x32_sc[s, pl.ds(bb, tl, stride=nb), :]


def _ffn_ln2(x1, p, wple_bf, wgate_bf, wup_bf, conv_w, conv_b, wdown_bf, ln_g, ln_b, tl=64):
    B, L, D = x1.shape
    tm = tl * B
    assert D_FF % FFN_SUB == 0 and HALO == (CONV_WIDTH - 1) * B and tm % FFN_ROWS == 0
    const = lambda i: (0, 0)
    resident = lambda shape: pl.BlockSpec(shape, const, pipeline_mode=pl.Buffered(1))
    return pl.pallas_call(
        functools.partial(_ffn_kernel, tl=tl),
        grid=(L // tl,),
        in_specs=[
            pl.BlockSpec((B, tl, D), lambda i: (0, i, 0)),
            pl.BlockSpec((B, SUBLANES, D), lambda i: (0, jnp.maximum(i * (tl // SUBLANES) - 1, 0), 0)),
            pl.BlockSpec((B, tl, PLE_DIM), lambda i: (0, i, 0)),
            resident((PLE_DIM, D)),
            resident((D, D)),
            resident((D, 2 * D_FF)),
            resident((CONV_WIDTH, 2 * D_FF)),
            resident((1, 2 * D_FF)),
            resident((D_FF, D)),
            resident((1, D)),
            resident((1, D)),
        ],
        out_specs=pl.BlockSpec((B, tl, D), lambda i: (0, i, 0)),
        out_shape=jax.ShapeDtypeStruct((B, L, D), F32),
        scratch_shapes=[
            pltpu.VMEM((D // LANES, tm, LANES), F32),
            pltpu.VMEM((D // LANES, HALO, LANES), F32),
            pltpu.VMEM((PLE_DIM // LANES, tm, LANES), F32),
            pltpu.VMEM((tm + HALO, D), BF16),
            pltpu.VMEM((tm, D), F32),
            pltpu.VMEM((2, tm + HALO, FFN_SUB), F32),
            pltpu.VMEM((2, tm + HALO, FFN_SUB), F32),
            pltpu.VMEM((tm, D_FF), BF16),
        ],
        compiler_params=_cparams(("parallel",)),
        name="ffn_ln2",
    )(x1, x1, p, wple_bf, wgate_bf, wup_bf, conv_w, conv_b, wdown_bf, ln_g, ln_b)


def kernel(x, p, w_in, diff_lambda_q1, diff_lambda_k1, diff_lambda_q2, diff_lambda_k2, diff_subln_g, ssm_lambda_re, ssm_lambda_im, ssm_log_dt, ssm_b_re, ssm_b_im, ssm_c_re, ssm_c_im, ssm_d, ssm_w_glu, ssm_b_glu, w_o, ln1_g, ln1_b, ffn_w_up, ffn_conv_w, ffn_conv_b, ffn_w_down, w_ple, w_ple_gate, ln2_g, ln2_b):
    B, L, D = x.shape
    assert B == SUBLANES and D == D_MODEL and L % 512 == 0
    for i in range(DEPTH):
        lam_init = 0.8 - 0.6 * math.exp(-0.3 * i)
        row = lambda a: a[i].reshape(1, -1)
        qt, k, vt, u_tm = _in_proj(x, w_in[i])
        attn = _diff_attn_flat(qt, k, vt, row(diff_lambda_q1), row(diff_lambda_k1), row(diff_lambda_q2),
                          row(diff_lambda_k2), diff_subln_g[i], lam_init)
        tables = _ssm_prep(ssm_lambda_re[i], ssm_lambda_im[i], ssm_log_dt[i], ssm_b_re[i], ssm_b_im[i],
                           ssm_c_re[i], ssm_c_im[i])
        y_tm = _ssm(u_tm, tables, ssm_d[i], B, L)
        x1 = _mix_ln1(x, attn, y_tm, ssm_w_glu[i].astype(BF16), row(ssm_b_glu), w_o[i].astype(BF16),
                      row(ln1_g), row(ln1_b))
        x = _ffn_ln2(x1, p[i], w_ple[i].astype(BF16), w_ple_gate[i].astype(BF16), ffn_w_up[i].astype(BF16),
                     ffn_conv_w[i], row(ffn_conv_b), ffn_w_down[i].astype(BF16), row(ln2_g), row(ln2_b))
    return x
```

```python
import functools
import math

import jax
import jax.numpy as jnp
from jax import lax
from jax.experimental import pallas as pl
from jax.experimental.pallas import tpu as pltpu

F32 = jnp.float32
BF16 = jnp.bfloat16

SUBLANES = 8
LANES = 128

DEPTH = 1
D_MODEL = 1024
PLE_DIM = 256
HEADS = 4
QK_DIM = 64
V_DIM = 2 * QK_DIM
QK_WIDTH = HEADS * 2 * QK_DIM
ATT_WIDTH = HEADS * V_DIM
SSM_WIDTH = D_MODEL - ATT_WIDTH
SSM_GROUP = 16
SSM_GROUPS = SSM_WIDTH // SSM_GROUP
SSM_STATE = 64
D_FF = 2816
CONV_WIDTH = 3
LN_EPS = 1e-5
ALPHA = (2 * DEPTH) ** 0.25
QK_SCALE = QK_DIM ** -0.5 * math.log2(math.e)

CHUNK = 16
CHUNK_COLS = CHUNK * SSM_GROUP
GROUPS_PER_TILE = LANES // SSM_GROUP
LANE_TILES = SSM_WIDTH // LANES
PAIRS_PER_TILE = GROUPS_PER_TILE // 2
STATE_COLS = GROUPS_PER_TILE * SSM_STATE

VMEM_LIMIT = 56 * 1024 * 1024


def _cparams(sem):
    return pltpu.CompilerParams(dimension_semantics=sem, vmem_limit_bytes=VMEM_LIMIT)


TB = 256
VT_ROWS = V_DIM + 16


def _in_proj_kernel(x_ref, w_ref, qt_ref, k_ref, vt_ref, u_ref, u_sc):
    wq_ref = w_ref.at[:, :QK_WIDTH]
    wk_ref = w_ref.at[:, QK_WIDTH:2 * QK_WIDTH]
    wv_ref = w_ref.at[:, 2 * QK_WIDTH:2 * QK_WIDTH + ATT_WIDTH]
    wu_ref = w_ref.at[:, 2 * QK_WIDTH + ATT_WIDTH:]
    nb, tl, d = x_ref.shape
    jt = tl // CHUNK
    xb = x_ref[...].reshape(nb * tl, d).astype(BF16)
    hu = jnp.dot(xb, wu_ref[...], preferred_element_type=F32)
    for b in range(nb):
        for q in range(LANE_TILES):
            u_sc[q, pl.ds(b, tl, stride=nb), :] = hu[b * tl:(b + 1) * tl, q * LANES:(q + 1) * LANES]
    for q in range(LANE_TILES):
        tiles = u_sc[q].reshape(jt, CHUNK, nb, LANES)
        slabs = _chunk_slabs([tiles[:, r].reshape(jt * nb, LANES) for r in range(CHUNK)])
        for g in range(GROUPS_PER_TILE):
            u_ref[q, g] = slabs[g]
    tn = (((0,), (1,)), ((), ()))
    qt = lax.dot_general(wq_ref[...], xb, tn, preferred_element_type=F32) * QK_SCALE
    vt = lax.dot_general(wv_ref[...], xb, tn, preferred_element_type=F32)
    pad_row = lax.broadcasted_iota(jnp.int32, (VT_ROWS - V_DIM, TB), 0)
    ones_row = jnp.where(pad_row == 0, 1.0, 0.0).astype(BF16)
    for b in range(nb):
        for h in range(HEADS):
            for c in range(tl // TB):
                cols = slice(b * tl + c * TB, b * tl + (c + 1) * TB)
                qt_ref[b, h, c] = qt[h * LANES:(h + 1) * LANES, cols].astype(BF16)
                vt_ref[b, h, c, :V_DIM] = vt[h * LANES:(h + 1) * LANES, cols].astype(BF16)
                vt_ref[b, h, c, V_DIM:] = ones_row
    k_ref[...] = jnp.dot(xb, wk_ref[...], preferred_element_type=F32).astype(BF16).reshape(k_ref.shape)


def _in_proj(x, w_in, tl=256):
    B, L, D = x.shape
    slab = lambda rows: pl.BlockSpec((B, HEADS, tl // TB, rows, TB), lambda l: (0, 0, l, 0, 0))
    slab_shape = lambda rows: jax.ShapeDtypeStruct((B, HEADS, L // TB, rows, TB), BF16)
    return pl.pallas_call(
        _in_proj_kernel,
        grid=(L // tl,),
        in_specs=[
            pl.BlockSpec((B, tl, D), lambda l: (0, l, 0)),
            pl.BlockSpec(w_in.shape, lambda l: (0, 0), pipeline_mode=pl.Buffered(1)),
        ],
        out_specs=[
            slab(LANES),
            pl.BlockSpec((B, tl, QK_WIDTH), lambda l: (0, l, 0)),
            slab(VT_ROWS),
            pl.BlockSpec((LANE_TILES, GROUPS_PER_TILE, tl // CHUNK * B, CHUNK_COLS), lambda l: (0, 0, l, 0)),
        ],
        out_shape=[
            slab_shape(LANES),
            jax.ShapeDtypeStruct((B, L, QK_WIDTH), BF16),
            slab_shape(VT_ROWS),
            jax.ShapeDtypeStruct((LANE_TILES, GROUPS_PER_TILE, L // CHUNK * B, CHUNK_COLS), F32),
        ],
        scratch_shapes=[pltpu.VMEM((LANE_TILES, tl * B, LANES), F32)],
        compiler_params=_cparams(("parallel",)),
        name="in_proj",
    )(x, w_in.astype(BF16))


def _attn_flat_kernel(lq1_ref, lk1_ref, lq2_ref, lk2_ref, g_ref, qt_ref, k_ref, vt_ref, o_ref,
                      acc_sc, qst_sc, s_sc, p_sc, bm_sc, st_sc, al_sc, *, lam_init):
    n_q = qt_ref.shape[2]
    n_pairs = n_q * (n_q + 1) // 2
    assert TB == 2 * LANES and n_pairs % 2 == 0
    heads = range(HEADS)
    n_strips = 2 * TB // LANES
    lam = (jnp.exp(jnp.sum(lq1_ref[...] * lk1_ref[...], axis=-1, keepdims=True))
           - jnp.exp(jnp.sum(lq2_ref[...] * lk2_ref[...], axis=-1, keepdims=True)) + lam_init)

    def prepare(i, c):
        for hh in heads:
            qt = qt_ref[0, hh, i]
            row = lax.broadcasted_iota(jnp.int32, qt.shape, 0)
            zero = jnp.zeros_like(qt)
            qst_sc[i, hh, :, :TB] = jnp.where(row < QK_DIM, qt, zero)
            qst_sc[i, hh, :, TB:] = jnp.where(row >= QK_DIM, qt, zero)
            acc_sc[i, hh] = jnp.zeros(acc_sc.shape[2:], F32)
            st_sc[i, hh] = jnp.full((1, 2 * TB), -jnp.inf, F32)
        return c

    lax.fori_loop(0, n_q, prepare, 0)
    p_sc[1] = jnp.zeros(p_sc.shape[1:], BF16)
    al_sc[1] = jnp.ones(al_sc.shape[1:], F32)

    def diag_place(i):
        r = i % 4
        return jnp.where(jnp.logical_or(r == 1, r == 2), 1, 0)

    def is_diag(pair):
        return pair[1] == diag_place(pair[0])

    def block_of(pair):
        i, k = pair
        d = diag_place(i)
        return jnp.where(k == d, i, jnp.where(k < d, k, k - 1))

    def scores(pair, slot):
        i, j = pair[0], block_of(pair)
        rows = pl.ds(pl.multiple_of(j * TB, TB), TB)
        for hh in heads:
            kb = k_ref[0, rows, hh * LANES:(hh + 1) * LANES]
            s = jnp.dot(kb, qst_sc[i, hh], preferred_element_type=F32)
            s_sc[slot, hh] = s
            bm_sc[slot, hh] = jnp.max(s, axis=0, keepdims=True)

    def values(pair, slot):
        i, j = pair[0], block_of(pair)
        for hh in heads:
            pv = jnp.dot(vt_ref[0, hh, j], p_sc[slot, hh], preferred_element_type=F32)
            acc_sc[i, hh] = al_sc[slot, hh] * acc_sc[i, hh] + pv

    def softmax(pair, slot, diagonal):
        i, _ = pair
        for hh in heads:
            for c in range(n_strips):
                lanes = slice(c * LANES, (c + 1) * LANES)
                m = st_sc[i, hh, :, lanes]
                if not diagonal:
                    mc = jnp.maximum(m, bm_sc[slot, hh, :, lanes])
                    p = jnp.exp2(s_sc[slot, hh, :, lanes] - mc)
                else:
                    s = s_sc[slot, hh, :, lanes]
                    key = lax.broadcasted_iota(jnp.int32, s.shape, 0)
                    qry = lax.broadcasted_iota(jnp.int32, s.shape, 1) + (c * LANES) % TB
                    s = jnp.where(key <= qry, s, -jnp.inf)
                    mc = jnp.maximum(m, jnp.max(s, axis=0, keepdims=True))
                    p = jnp.exp2(s - mc)
                p_sc[slot, hh, :, lanes] = p.astype(BF16)
                st_sc[i, hh, :, lanes] = mc
                al_sc[slot, hh, :, lanes] = jnp.exp2(m - mc)

    def succ(pair):
        i, k = pair
        last = k == i
        i2 = jnp.minimum(jnp.where(last, i + 1, i), n_q - 1)
        return i2, jnp.where(last, 0, k + 1)

    def two_steps(t, carry):
        prev, cur = carry
        nxt = succ(cur)
        nxt2 = succ(nxt)

        def body(diag0):
            scores(nxt, 1)
            values(prev, 1)
            softmax(cur, 0, diag0)
            scores(nxt2, 0)
            values(cur, 0)
            softmax(nxt, 1, False)

        d0 = is_diag(cur)
        pl.when(d0)(lambda: body(True))
        pl.when(jnp.logical_not(d0))(lambda: body(False))
        return nxt, nxt2

    zero = jnp.int32(0)
    first = (zero, zero)
    scores(first, 0)
    last, _ = lax.fori_loop(0, n_pairs // 2, two_steps, (first, first))
    values(last, 1)

    def finish(t, c):
        for i in (2 * t, 2 * t + 1):
            for hh in heads:
                o = acc_sc[i, hh, :V_DIM] * (1.0 / acc_sc[i, hh, V_DIM:V_DIM + 1])
                o = o[:, :TB] - lam * o[:, TB:]
                ms = jnp.mean(o * o, axis=0, keepdims=True)
                o = o * lax.rsqrt(ms + LN_EPS) * (g_ref[...] * (1.0 - lam_init))
                rows = pl.ds(pl.multiple_of(i * TB, TB), TB)
                o_ref[0, rows, hh * LANES:(hh + 1) * LANES] = o.T.astype(o_ref.dtype)
        return c

    lax.fori_loop(0, n_q // 2, finish, 0)


def _diff_attn_flat(qt, k, vt, lq1, lk1, lq2, lk2, subln_g, lam_init):
    B, L, _ = k.shape
    n_q = L // TB
    vec = pl.BlockSpec((1, QK_DIM), lambda b: (0, 0))
    return pl.pallas_call(
        functools.partial(_attn_flat_kernel, lam_init=lam_init),
        grid=(B,),
        in_specs=[
            vec, vec, vec, vec,
            pl.BlockSpec((V_DIM, 1), lambda b: (0, 0)),
            pl.BlockSpec((1, HEADS, n_q, LANES, TB), lambda b: (b, 0, 0, 0, 0)),
            pl.BlockSpec((1, L, HEADS * LANES), lambda b: (b, 0, 0)),
            pl.BlockSpec((1, HEADS, n_q, VT_ROWS, TB), lambda b: (b, 0, 0, 0, 0)),
        ],
        out_specs=pl.BlockSpec((1, L, HEADS * LANES), lambda b: (b, 0, 0)),
        out_shape=jax.ShapeDtypeStruct((B, L, ATT_WIDTH), BF16),
        scratch_shapes=[
            pltpu.VMEM((n_q, HEADS, VT_ROWS, 2 * TB), F32),
            pltpu.VMEM((n_q, HEADS, LANES, 2 * TB), BF16),
            pltpu.VMEM((2, HEADS, TB, 2 * TB), F32),
            pltpu.VMEM((2, HEADS, TB, 2 * TB), BF16),
            pltpu.VMEM((2, HEADS, 1, 2 * TB), F32),
            pltpu.VMEM((n_q, HEADS, 1, 2 * TB), F32),
            pltpu.VMEM((2, HEADS, 1, 2 * TB), F32),
        ],
        compiler_params=_cparams(("parallel",)),
        name="diff_attn",
    )(lq1, lk1, lq2, lk2, subln_g.reshape(V_DIM, 1), qt, k, vt)


def _block_transpose8(vs):
    lane_blk = lax.broadcasted_iota(jnp.int32, vs[0].shape, 1) // SSM_GROUP
    for d in (4, 2, 1):
        keep = (lane_blk & d) == 0
        new = list(vs)
        for a in range(8):
            if a & d == 0:
                lo, hi = vs[a], vs[a + d]
                new[a] = jnp.where(keep, lo, pltpu.roll(hi, SSM_GROUP * d, 1))
                new[a + d] = jnp.where(keep, pltpu.roll(lo, LANES - SSM_GROUP * d, 1), hi)
        vs = new
    return vs


def _ssm_kernel(u_ref, toep_ref, win_ref, wout_ref, a_ref, d_ref, y_ref, s_sc, x_sc, sp_sc, *, jt):
    rows = jt * SUBLANES

    @pl.when(pl.program_id(1) == 0)
    def _():
        s_sc[...] = jnp.zeros(s_sc.shape, F32)

    ug = [u_ref[0, g].astype(BF16) for g in range(GROUPS_PER_TILE)]

    y = [jnp.dot(ug[g], toep_ref[0, g], preferred_element_type=F32) for g in range(GROUPS_PER_TILE)]

    for m in range(PAIRS_PER_TILE):
        up = jnp.concatenate([ug[2 * m], ug[2 * m + 1]], axis=1)
        xp = jnp.dot(up, win_ref[0, m], preferred_element_type=F32)
        x_sc[0, :, m * LANES:(m + 1) * LANES] = xp[:, :LANES]
        x_sc[1, :, m * LANES:(m + 1) * LANES] = xp[:, LANES:]

    ar = a_ref[0, 0:1, :]
    ai = a_ref[0, 1:2, :]

    def step(j, carry):
        sr, si = carry
        r8 = pl.ds(pl.multiple_of(j * SUBLANES, SUBLANES), SUBLANES)
        sp_sc[0, r8, :] = sr
        sp_sc[1, r8, :] = si
        return (ar * sr - ai * si + x_sc[0, r8, :], ar * si + ai * sr + x_sc[1, r8, :])

    sr, si = lax.fori_loop(0, jt, step, (s_sc[0], s_sc[1]))
    s_sc[0] = sr
    s_sc[1] = si

    for m in range(PAIRS_PER_TILE):
        sp = jnp.concatenate([sp_sc[0, :, m * LANES:(m + 1) * LANES],
                              sp_sc[1, :, m * LANES:(m + 1) * LANES]], axis=1).astype(BF16)
        yc = jnp.dot(sp, wout_ref[0, m], preferred_element_type=F32)
        y[2 * m] = y[2 * m] + yc[:, :CHUNK_COLS]
        y[2 * m + 1] = y[2 * m + 1] + yc[:, CHUNK_COLS:]

    for g in range(GROUPS_PER_TILE):
        y_ref[0, g] = y[g] + d_ref[0, g] * u_ref[0, g]


def _chunk_slabs(tiles):
    z_lo = _block_transpose8(tiles[:8])
    z_hi = _block_transpose8(tiles[8:])
    return [jnp.concatenate([z_lo[g], z_hi[g]], axis=1) for g in range(GROUPS_PER_TILE)]


def _chunk_tiles(slabs):
    out = []
    for half in range(2):
        out += _block_transpose8([s[:, half * LANES:(half + 1) * LANES] for s in slabs])
    return out


def _ssm(u_slab, tables, d_skip, B, L, lt=1024):
    toep, win, wout, a16 = tables
    jt = lt // CHUNK
    n_l = L // lt
    rows = jt * B
    slab = pl.BlockSpec((1, GROUPS_PER_TILE, rows, CHUNK_COLS), lambda q, l: (q, 0, l, 0))
    d_slab = jnp.tile(d_skip.reshape(LANE_TILES, GROUPS_PER_TILE, 1, SSM_GROUP), (1, 1, 1, CHUNK))
    return pl.pallas_call(
        functools.partial(_ssm_kernel, jt=jt),
        grid=(LANE_TILES, n_l),
        in_specs=[
            slab,
            pl.BlockSpec((1, GROUPS_PER_TILE, CHUNK_COLS, CHUNK_COLS), lambda q, l: (q, 0, 0, 0)),
            pl.BlockSpec((1, PAIRS_PER_TILE, 2 * CHUNK_COLS, 2 * LANES), lambda q, l: (q, 0, 0, 0)),
            pl.BlockSpec((1, PAIRS_PER_TILE, 2 * LANES, 2 * CHUNK_COLS), lambda q, l: (q, 0, 0, 0)),
            pl.BlockSpec((1, 2, STATE_COLS), lambda q, l: (q, 0, 0)),
            pl.BlockSpec((1, GROUPS_PER_TILE, 1, CHUNK_COLS), lambda q, l: (q, 0, 0, 0)),
        ],
        out_specs=slab,
        out_shape=jax.ShapeDtypeStruct(u_slab.shape, F32),
        scratch_shapes=[
            pltpu.VMEM((2, B, STATE_COLS), F32),
            pltpu.VMEM((2, rows, STATE_COLS), F32),
            pltpu.VMEM((2, rows, STATE_COLS), F32),
        ],
        compiler_params=_cparams(("parallel", "arbitrary")),
        name="ssm",
    )(u_slab, toep, win, wout, a16, d_slab)


def _cmul(a, b):
    return a[0] * b[0] - a[1] * b[1], a[0] * b[1] + a[1] * b[0]


def _zoh(lr, li, ldt):
    dt = jnp.exp(ldt)
    mag = jnp.exp(lr * dt)
    ab = (mag * jnp.cos(li * dt), mag * jnp.sin(li * dt))
    den = lr * lr + li * li
    zr, zi = ab[0] - 1.0, ab[1]
    return ab, ((zr * lr + zi * li) / den, (zi * lr - zr * li) / den)


def _cpow(base, exponent, bits):
    out = None
    for k in range(bits):
        on = (exponent & (1 << k)) != 0
        sel = (jnp.where(on, base[0], 1.0), jnp.where(on, base[1], 0.0))
        out = sel if out is None else _cmul(out, sel)
        if k + 1 < bits:
            base = _cmul(base, base)
    return out


def _ssm_prep_kernel(row_ref, ct_ref, bt_ref, toep_ref, win_ref, wout_ref, a_ref):
    bits = CHUNK.bit_length() - 1
    lane = lax.broadcasted_iota(jnp.int32, (1, CHUNK_COLS), 1)
    tau = lane // SSM_GROUP
    rowi = lax.broadcasted_iota(jnp.int32, (CHUNK_COLS, 1), 0)
    back = (CHUNK - 1) - rowi // SSM_GROUP
    lane2 = lax.broadcasted_iota(jnp.int32, (1, LANES), 1)
    exact = functools.partial(jnp.dot, precision=lax.Precision.HIGHEST, preferred_element_type=F32)
    h_row = lax.broadcasted_iota(jnp.int32, (SSM_GROUP, CHUNK_COLS), 0)
    repeat = jnp.where(lane % SSM_GROUP == h_row, 1.0, 0.0)
    zoh = [_zoh(row_ref[0, m], row_ref[1, m], row_ref[2, m]) for m in range(PAIRS_PER_TILE)]
    pad = jnp.zeros((SUBLANES - PAIRS_PER_TILE, LANES), F32)
    ab_cols = tuple(jnp.concatenate([z[0][c] for z in zoh] + [pad], axis=0).T for c in range(2))
    a_rows = []
    for m in range(PAIRS_PER_TILE):
        abp, fp = zoh[m]
        pwr = _cpow(abp, back, bits)
        a16 = abp
        for _ in range(bits):
            a16 = _cmul(a16, a16)
        a_rows.append(a16)
        wout_rows = [[None, None], [None, None]]
        for j in range(2):
            g = 2 * m + j
            states = slice(j * SSM_STATE, (j + 1) * SSM_STATE)
            ab = tuple(v[states, m:m + 1] for v in ab_cols)
            pw = _cpow(ab, tau, bits)
            c_rep = tuple(exact(ct_ref[c, g], repeat) for c in range(2))
            g0 = _cmul(pw, c_rep)
            g1 = _cmul(g0, ab)
            mine = (lane2 < SSM_STATE) if j == 0 else (lane2 >= SSM_STATE)
            bb2 = _cmul(fp, (bt_ref[0, g], bt_ref[1, g]))
            bb2 = tuple(jnp.where(mine, v, 0.0) for v in bb2)
            none = jnp.zeros_like(g0[0])
            stack = (lambda v: jnp.concatenate([v, none], axis=0)) if j == 0 else \
                    (lambda v: jnp.concatenate([none, v], axis=0))
            kt = exact(bb2[0], stack(g0[0])) - exact(bb2[1], stack(g0[1]))
            toep_ref[0, g] = jnp.concatenate(
                [jnp.where(lane >= SSM_GROUP * r, pltpu.roll(kt, SSM_GROUP * r, 1), 0.0) if r else kt
                 for r in range(CHUNK)], axis=0).astype(BF16)
            w = _cmul(pwr, tuple(jnp.concatenate([v] * CHUNK, axis=0) for v in bb2))
            win_ref[0, m, j * CHUNK_COLS:(j + 1) * CHUNK_COLS, :] = jnp.concatenate(w, axis=1).astype(BF16)
            wout_rows[0][j] = g1[0]
            wout_rows[1][j] = -g1[1]
        zero = jnp.zeros((SSM_STATE, CHUNK_COLS), F32)
        wout_ref[0, m] = jnp.concatenate([
            jnp.concatenate([wout_rows[0][0], zero], axis=1),
            jnp.concatenate([zero, wout_rows[0][1]], axis=1),
            jnp.concatenate([wout_rows[1][0], zero], axis=1),
            jnp.concatenate([zero, wout_rows[1][1]], axis=1)], axis=0).astype(BF16)
    a_ref[0] = jnp.concatenate([jnp.concatenate([a[0] for a in a_rows], axis=1),
                                jnp.concatenate([a[1] for a in a_rows], axis=1)], axis=0)


def _ssm_prep(lam_re, lam_im, log_dt, b_re, b_im, c_re, c_im):
    G, P, H = b_re.shape
    ldt = jnp.broadcast_to(log_dt[:, None], (G, P))
    rows = jnp.stack([lam_re, lam_im, ldt]).reshape(3, G // 2, 1, 2 * P)
    ct = jnp.stack([c_re, c_im]).transpose(0, 1, 3, 2)
    bt = jnp.tile(jnp.stack([b_re, b_im]).transpose(0, 1, 3, 2), (1, 1, 1, 2))
    gt = GROUPS_PER_TILE
    return pl.pallas_call(
        _ssm_prep_kernel,
        grid=(LANE_TILES,),
        in_specs=[
            pl.BlockSpec((3, PAIRS_PER_TILE, 1, 2 * P), lambda q: (0, q, 0, 0)),
            pl.BlockSpec((2, gt, P, H), lambda q: (0, q, 0, 0)),
            pl.BlockSpec((2, gt, H, 2 * P), lambda q: (0, q, 0, 0)),
        ],
        out_specs=[
            pl.BlockSpec((1, gt, CHUNK_COLS, CHUNK_COLS), lambda q: (q, 0, 0, 0)),
            pl.BlockSpec((1, PAIRS_PER_TILE, 2 * CHUNK_COLS, 2 * LANES), lambda q: (q, 0, 0, 0)),
            pl.BlockSpec((1, PAIRS_PER_TILE, 2 * LANES, 2 * CHUNK_COLS), lambda q: (q, 0, 0, 0)),
            pl.BlockSpec((1, 2, STATE_COLS), lambda q: (q, 0, 0)),
        ],
        out_shape=[
            jax.ShapeDtypeStruct((LANE_TILES, gt, CHUNK_COLS, CHUNK_COLS), BF16),
            jax.ShapeDtypeStruct((LANE_TILES, PAIRS_PER_TILE, 2 * CHUNK_COLS, 2 * LANES), BF16),
            jax.ShapeDtypeStruct((LANE_TILES, PAIRS_PER_TILE, 2 * LANES, 2 * CHUNK_COLS), BF16),
            jax.ShapeDtypeStruct((LANE_TILES, 2, STATE_COLS), F32),
        ],
        compiler_params=_cparams(("parallel",)),
        name="ssm_prep",
    )(rows, ct, bt)


def _layer_norm(r, g, b):
    mu = jnp.mean(r, axis=-1, keepdims=True)
    c = r - mu
    var = jnp.mean(c * c, axis=-1, keepdims=True)
    return c * lax.rsqrt(var + LN_EPS) * g + b


MIX_ROWS = 256


def _gelu_tanh(y):
    return 0.5 * y * (1.0 + jnp.tanh(math.sqrt(2.0 / math.pi) * (y + 0.044715 * (y * y * y))))


def _mix_ln1_kernel(x_ref, attn_ref, y_ref, wglu_ref, bglu_ref, wo_ref, g_ref, b_ref, o_ref, y_sc):
    nb, tl, d = x_ref.shape
    pos = MIX_ROWS // nb
    n_pieces = tl // pos
    jp = pos // CHUNK

    def relayout(n):
        rows = slice(n * jp * nb, (n + 1) * jp * nb)
        for q in range(LANE_TILES):
            tiles = _chunk_tiles([y_ref[q, g, rows] for g in range(GROUPS_PER_TILE)])
            y_sc[q, n * MIX_ROWS:(n + 1) * MIX_ROWS] = jnp.stack(
                [t.reshape(jp, nb, LANES) for t in tiles], axis=1).reshape(MIX_ROWS, LANES)

    def glu(n):
        y = jnp.concatenate(
            [jnp.concatenate([y_sc[q, pl.ds(n * MIX_ROWS + b, pos, stride=nb), :] for q in range(LANE_TILES)], axis=1)
             for b in range(nb)], axis=0)
        gl = _gelu_tanh(y)
        z = gl * jax.nn.sigmoid(jnp.dot(gl.astype(BF16), wglu_ref[...], preferred_element_type=F32) + bglu_ref[...])
        return z.astype(BF16)

    def project(n, z):
        ls = slice(n * pos, (n + 1) * pos)
        mix = (jnp.dot(attn_ref[:, ls].reshape(MIX_ROWS, ATT_WIDTH), wo_ref[:ATT_WIDTH], preferred_element_type=F32)
               + jnp.dot(z, wo_ref[ATT_WIDTH:], preferred_element_type=F32))
        res = _layer_norm(ALPHA * x_ref[:, ls].reshape(MIX_ROWS, d) + mix, g_ref[...], b_ref[...])
        o_ref[:, ls] = res.reshape(nb, pos, d)

    relayout(0)
    z = glu(0)
    for n in range(n_pieces):
        if n + 1 < n_pieces:
            relayout(n + 1)
            z_next = glu(n + 1)
        project(n, z)
        z = z_next


def _mix_ln1(x, attn, y_tm, wglu_bf, b_glu, wo_bf, ln_g, ln_b, tl=256):
    B, L, D = x.shape
    assert MIX_ROWS % (B * CHUNK) == 0 and tl % (MIX_ROWS // B) == 0
    resident = lambda shape: pl.BlockSpec(shape, lambda l: (0, 0), pipeline_mode=pl.Buffered(1))
    return pl.pallas_call(
        _mix_ln1_kernel,
        grid=(L // tl,),
        in_specs=[
            pl.BlockSpec((B, tl, D), lambda l: (0, l, 0)),
            pl.BlockSpec((B, tl, ATT_WIDTH), lambda l: (0, l, 0)),
            pl.BlockSpec((LANE_TILES, GROUPS_PER_TILE, tl // CHUNK * B, CHUNK_COLS), lambda l: (0, 0, l, 0)),
            resident((SSM_WIDTH, SSM_WIDTH)),
            resident((1, SSM_WIDTH)),
            resident((D, D)),
            resident((1, D)),
            resident((1, D)),
        ],
        out_specs=pl.BlockSpec((B, tl, D), lambda l: (0, l, 0)),
        out_shape=jax.ShapeDtypeStruct((B, L, D), F32),
        scratch_shapes=[pltpu.VMEM((LANE_TILES, tl * B, LANES), F32)],
        compiler_params=_cparams(("parallel",)),
        name="mix_ln1",
    )(x, attn, y_tm, wglu_bf, b_glu, wo_bf, ln_g, ln_b)


HALO = 16
FFN_SUB = 256
FFN_DOWN_PARTS = 2
FFN_ROWS = 128


def _ffn_kernel(x_ref, xh_ref, p_ref, wple_ref, wgate_ref, wup_ref, cw_ref, cb_ref, wd_ref, g_ref, b_ref, o_ref,
                x32_sc, xh32_sc, p32_sc, xb_sc, acc_sc, hg_sc, hv_sc, act_sc, *, tl):
    nb = x_ref.shape[0]
    tm = tl * nb
    d_slabs = x_ref.shape[2] // LANES
    hp = (CONV_WIDTH - 1)
    for bb in range(nb):
        for s in range(d_slabs):
            lanes = slice(s * LANES, (s + 1) * LANES)
            x32_sc[s, pl.ds(bb, tl, stride=nb), :] = x_ref[bb, :, lanes]
            xh32_sc[s, pl.ds(bb, hp, stride=nb), :] = xh_ref[bb, SUBLANES - hp:, lanes]
        for s in range(p_ref.shape[2] // LANES):
            p32_sc[s, pl.ds(bb, tl, stride=nb), :] = p_ref[bb, :, s * LANES:(s + 1) * LANES]
    x_tm = jnp.concatenate([x32_sc[s] for s in range(d_slabs)], axis=1)
    p_tm = jnp.concatenate([p32_sc[s] for s in range(p32_sc.shape[0])], axis=1)
    keep = jnp.where(pl.program_id(0) == 0, 0.0, 1.0)
    xb_sc[0:HALO] = (jnp.concatenate([xh32_sc[s] for s in range(d_slabs)], axis=1) * keep).astype(BF16)
    xb_sc[HALO:] = x_tm.astype(BF16)
    gate = jax.nn.sigmoid(jnp.dot(xb_sc[HALO:], wgate_ref[...], preferred_element_type=F32))
    ple = jnp.dot(p_tm.astype(BF16), wple_ref[...], preferred_element_type=F32) * gate
    acc_sc[...] = ALPHA * x_tm + ple

    def conv(h_sc, slot, cols, r0, rows):
        out = cb_ref[:, cols] + cw_ref[CONV_WIDTH - 1:CONV_WIDTH, cols] * h_sc[slot, HALO + r0:HALO + r0 + rows]
        for j in range(CONV_WIDTH - 1):
            back = (CONV_WIDTH - 1 - j) * nb
            out = out + cw_ref[j:j + 1, cols] * h_sc[slot, HALO + r0 - back:HALO + r0 - back + rows]
        return out

    n_sub = D_FF // FFN_SUB
    g_cols = lambda n: slice(n * FFN_SUB, (n + 1) * FFN_SUB)
    v_cols = lambda n: slice(D_FF + n * FFN_SUB, D_FF + (n + 1) * FFN_SUB)

    def up(n):
        xb = xb_sc[...]
        hg_sc[n % 2] = jnp.dot(xb, wup_ref[:, g_cols(n)], preferred_element_type=F32)
        hv_sc[n % 2] = jnp.dot(xb, wup_ref[:, v_cols(n)], preferred_element_type=F32)

    def gate_block(n):
        for r0 in range(0, tm, FFN_ROWS):
            gc = conv(hg_sc, n % 2, g_cols(n), r0, FFN_ROWS)
            vc = conv(hv_sc, n % 2, v_cols(n), r0, FFN_ROWS)
            act_sc[r0:r0 + FFN_ROWS, g_cols(n)] = (gc * jax.nn.sigmoid(gc) * vc).astype(BF16)

    def down(lo, hi):
        cols = slice(lo * FFN_SUB, hi * FFN_SUB)
        return jnp.dot(act_sc[:, cols], wd_ref[cols, :], preferred_element_type=F32)

    cuts = [n_sub * k // FFN_DOWN_PARTS for k in range(FFN_DOWN_PARTS + 1)]
    up(0)
    for n in range(n_sub):
        if n + 1 < n_sub:
            up(n + 1)
        gate_block(n)
        if n + 1 in cuts[1:-1]:
            lo = cuts[cuts.index(n + 1) - 1]
            acc_sc[...] += down(lo, n + 1)

    res = _layer_norm(acc_sc[...] + down(cuts[-2], n_sub), g_ref[...], b_ref[...])
    for s in range(d_slabs):
        x32_sc[s] = res[:, s * LANES:(s + 1) * LANES]
    for bb in range(nb):
        for s in range(d_slabs):
            o_ref[bb, :, s * LANES:(s + 1) * LANES] = x32_sc[s, pl.ds(bb, tl, stride=nb), :]


def _ffn_ln2(x1, p, wple_bf, wgate_bf, wup_bf, conv_w, conv_b, wdown_bf, ln_g, ln_b, tl=64):
    B, L, D = x1.shape
    tm = tl * B
    assert D_FF % FFN_SUB == 0 and HALO == (CONV_WIDTH - 1) * B and tm % FFN_ROWS == 0
    const = lambda i: (0, 0)
    resident = lambda shape: pl.BlockSpec(shape, const, pipeline_mode=pl.Buffered(1))
    return pl.pallas_call(
        functools.partial(_ffn_kernel, tl=tl),
        grid=(L // tl,),
        in_specs=[
            pl.BlockSpec((B, tl, D), lambda i: (0, i, 0)),
            pl.BlockSpec((B, SUBLANES, D), lambda i: (0, jnp.maximum(i * (tl // SUBLANES) - 1, 0), 0)),
            pl.BlockSpec((B, tl, PLE_DIM), lambda i: (0, i, 0)),
            resident((PLE_DIM, D)),
            resident((D, D)),
            resident((D, 2 * D_FF)),
            resident((CONV_WIDTH, 2 * D_FF)),
            resident((1, 2 * D_FF)),
            resident((D_FF, D)),
            resident((1, D)),
            resident((1, D)),
        ],
        out_specs=pl.BlockSpec((B, tl, D), lambda i: (0, i, 0)),
        out_shape=jax.ShapeDtypeStruct((B, L, D), F32),
        scratch_shapes=[
            pltpu.VMEM((D // LANES, tm, LANES), F32),
            pltpu.VMEM((D // LANES, HALO, LANES), F32),
            pltpu.VMEM((PLE_DIM // LANES, tm, LANES), F32),
            pltpu.VMEM((tm + HALO, D), BF16),
            pltpu.VMEM((tm, D), F32),
            pltpu.VMEM((2, tm + HALO, FFN_SUB), F32),
            pltpu.VMEM((2, tm + HALO, FFN_SUB), F32),
            pltpu.VMEM((tm, D_FF), BF16),
        ],
        compiler_params=_cparams(("parallel",)),
        name="ffn_ln2",
    )(x1, x1, p, wple_bf, wgate_bf, wup_bf, conv_w, conv_b, wdown_bf, ln_g, ln_b)


def kernel(x, p, w_in, diff_lambda_q1, diff_lambda_k1, diff_lambda_q2, diff_lambda_k2, diff_subln_g, ssm_lambda_re, ssm_lambda_im, ssm_log_dt, ssm_b_re, ssm_b_im, ssm_c_re, ssm_c_im, ssm_d, ssm_w_glu, ssm_b_glu, w_o, ln1_g, ln1_b, ffn_w_up, ffn_conv_w, ffn_conv_b, ffn_w_down, w_ple, w_ple_gate, ln2_g, ln2_b):
    B, L, D = x.shape
    assert B == SUBLANES and D == D_MODEL and L % 512 == 0
    for i in range(DEPTH):
        lam_init = 0.8 - 0.6 * math.exp(-0.3 * i)
        row = lambda a: a[i].reshape(1, -1)
        qt, k, vt, u_tm = _in_proj(x, w_in[i])
        attn = _diff_attn_flat(qt, k, vt, row(diff_lambda_q1), row(diff_lambda_k1), row(diff_lambda_q2),
                          row(diff_lambda_k2), diff_subln_g[i], lam_init)
        tables = _ssm_prep(ssm_lambda_re[i], ssm_lambda_im[i], ssm_log_dt[i], ssm_b_re[i], ssm_b_im[i],
                           ssm_c_re[i], ssm_c_im[i])
        y_tm = _ssm(u_tm, tables, ssm_d[i], B, L)
        x1 = _mix_ln1(x, attn, y_tm, ssm_w_glu[i].astype(BF16), row(ssm_b_glu), w_o[i].astype(BF16),
                      row(ln1_g), row(ln1_b))
        x = _ffn_ln2(x1, p[i], w_ple[i].astype(BF16), w_ple_gate[i].astype(BF16), ffn_w_up[i].astype(BF16),
                     ffn_conv_w[i], row(ffn_conv_b), ffn_w_down[i].astype(BF16), row(ln2_g), row(ln2_b))
    return x
```

```python
import functools
import math

import jax
import jax.numpy as jnp
from jax import lax
from jax.experimental import pallas as pl
from jax.experimental.pallas import tpu as pltpu

F32 = jnp.float32
BF16 = jnp.bfloat16

SUBLANES = 8
LANES = 128

DEPTH = 1
D_MODEL = 1024
PLE_DIM = 256
HEADS = 4
QK_DIM = 64
V_DIM = 2 * QK_DIM
QK_WIDTH = HEADS * 2 * QK_DIM
ATT_WIDTH = HEADS * V_DIM
SSM_WIDTH = D_MODEL - ATT_WIDTH
SSM_GROUP = 16
SSM_GROUPS = SSM_WIDTH // SSM_GROUP
SSM_STATE = 64
D_FF = 2816
CONV_WIDTH = 3
LN_EPS = 1e-5
ALPHA = (2 * DEPTH) ** 0.25
QK_SCALE = QK_DIM ** -0.5 * math.log2(math.e)

CHUNK = 16
CHUNK_COLS = CHUNK * SSM_GROUP
GROUPS_PER_TILE = LANES // SSM_GROUP
LANE_TILES = SSM_WIDTH // LANES
PAIRS_PER_TILE = GROUPS_PER_TILE // 2
STATE_COLS = GROUPS_PER_TILE * SSM_STATE

VMEM_LIMIT = 56 * 1024 * 1024


def _cparams(sem):
    return pltpu.CompilerParams(dimension_semantics=sem, vmem_limit_bytes=VMEM_LIMIT)


TB = 256
VT_ROWS = V_DIM + 16


def _in_proj_kernel(x_ref, w_ref, qt_ref, k_ref, vt_ref, u_ref, u_sc):
    wq_ref = w_ref.at[:, :QK_WIDTH]
    wk_ref = w_ref.at[:, QK_WIDTH:2 * QK_WIDTH]
    wv_ref = w_ref.at[:, 2 * QK_WIDTH:2 * QK_WIDTH + ATT_WIDTH]
    wu_ref = w_ref.at[:, 2 * QK_WIDTH + ATT_WIDTH:]
    nb, tl, d = x_ref.shape
    jt = tl // CHUNK
    xb = x_ref[...].reshape(nb * tl, d).astype(BF16)
    hu = jnp.dot(xb, wu_ref[...], preferred_element_type=F32)
    for b in range(nb):
        for q in range(LANE_TILES):
            u_sc[q, pl.ds(b, tl, stride=nb), :] = hu[b * tl:(b + 1) * tl, q * LANES:(q + 1) * LANES]
    for q in range(LANE_TILES):
        tiles = u_sc[q].reshape(jt, CHUNK, nb, LANES)
        slabs = _chunk_slabs([tiles[:, r].reshape(jt * nb, LANES) for r in range(CHUNK)])
        for g in range(GROUPS_PER_TILE):
            u_ref[q, g] = slabs[g]
    tn = (((0,), (1,)), ((), ()))
    qt = lax.dot_general(wq_ref[...], xb, tn, preferred_element_type=F32) * QK_SCALE
    vt = lax.dot_general(wv_ref[...], xb, tn, preferred_element_type=F32)
    pad_row = lax.broadcasted_iota(jnp.int32, (VT_ROWS - V_DIM, TB), 0)
    ones_row = jnp.where(pad_row == 0, 1.0, 0.0).astype(BF16)
    for b in range(nb):
        for h in range(HEADS):
            for c in range(tl // TB):
                cols = slice(b * tl + c * TB, b * tl + (c + 1) * TB)
                qt_ref[b, h, c] = qt[h * LANES:(h + 1) * LANES, cols].astype(BF16)
                vt_ref[b, h, c, :V_DIM] = vt[h * LANES:(h + 1) * LANES, cols].astype(BF16)
                vt_ref[b, h, c, V_DIM:] = ones_row
    k_ref[...] = jnp.dot(xb, wk_ref[...], preferred_element_type=F32).astype(BF16).reshape(k_ref.shape)


def _in_proj(x, w_in, tl=256):
    B, L, D = x.shape
    slab = lambda rows: pl.BlockSpec((B, HEADS, tl // TB, rows, TB), lambda l: (0, 0, l, 0, 0))
    slab_shape = lambda rows: jax.ShapeDtypeStruct((B, HEADS, L // TB, rows, TB), BF16)
    return pl.pallas_call(
        _in_proj_kernel,
        grid=(L // tl,),
        in_specs=[
            pl.BlockSpec((B, tl, D), lambda l: (0, l, 0)),
            pl.BlockSpec(w_in.shape, lambda l: (0, 0), pipeline_mode=pl.Buffered(1)),
        ],
        out_specs=[
            slab(LANES),
            pl.BlockSpec((B, tl, QK_WIDTH), lambda l: (0, l, 0)),
            slab(VT_ROWS),
            pl.BlockSpec((LANE_TILES, GROUPS_PER_TILE, tl // CHUNK * B, CHUNK_COLS), lambda l: (0, 0, l, 0)),
        ],
        out_shape=[
            slab_shape(LANES),
            jax.ShapeDtypeStruct((B, L, QK_WIDTH), BF16),
            slab_shape(VT_ROWS),
            jax.ShapeDtypeStruct((LANE_TILES, GROUPS_PER_TILE, L // CHUNK * B, CHUNK_COLS), F32),
        ],
        scratch_shapes=[pltpu.VMEM((LANE_TILES, tl * B, LANES), F32)],
        compiler_params=_cparams(("parallel",)),
        name="in_proj",
    )(x, w_in.astype(BF16))


def _attn_flat_kernel(lq1_ref, lk1_ref, lq2_ref, lk2_ref, g_ref, qt_ref, k_ref, vt_ref, o_ref,
                      acc_sc, qst_sc, s_sc, p_sc, bm_sc, st_sc, al_sc, *, lam_init):
    n_q = qt_ref.shape[2]
    n_pairs = n_q * (n_q + 1) // 2
    assert TB == 2 * LANES and n_pairs % 2 == 0
    heads = range(HEADS)
    n_strips = 2 * TB // LANES
    lam = (jnp.exp(jnp.sum(lq1_ref[...] * lk1_ref[...], axis=-1, keepdims=True))
           - jnp.exp(jnp.sum(lq2_ref[...] * lk2_ref[...], axis=-1, keepdims=True)) + lam_init)

    def prepare(i, c):
        for hh in heads:
            qt = qt_ref[0, hh, i]
            row = lax.broadcasted_iota(jnp.int32, qt.shape, 0)
            zero = jnp.zeros_like(qt)
            qst_sc[i, hh, :, :TB] = jnp.where(row < QK_DIM, qt, zero)
            qst_sc[i, hh, :, TB:] = jnp.where(row >= QK_DIM, qt, zero)
            acc_sc[i, hh] = jnp.zeros(acc_sc.shape[2:], F32)
            st_sc[i, hh] = jnp.full((1, 2 * TB), -jnp.inf, F32)
        return c

    lax.fori_loop(0, n_q, prepare, 0)
    p_sc[1] = jnp.zeros(p_sc.shape[1:], BF16)
    al_sc[1] = jnp.ones(al_sc.shape[1:], F32)

    def diag_place(i):
        r = i % 4
        return jnp.where(jnp.logical_or(r == 1, r == 2), 1, 0)

    def is_diag(pair):
        return pair[1] == diag_place(pair[0])

    def block_of(pair):
        i, k = pair
        d = diag_place(i)
        return jnp.where(k == d, i, jnp.where(k < d, k, k - 1))

    def scores(pair, slot):
        i, j = pair[0], block_of(pair)
        rows = pl.ds(pl.multiple_of(j * TB, TB), TB)
        for hh in heads:
            kb = k_ref[0, rows, hh * LANES:(hh + 1) * LANES]
            s = jnp.dot(kb, qst_sc[i, hh], preferred_element_type=F32)
            s_sc[slot, hh] = s
            bm_sc[slot, hh] = jnp.max(s, axis=0, keepdims=True)

    def values(pair, slot):
        i, j = pair[0], block_of(pair)
        for hh in heads:
            pv = jnp.dot(vt_ref[0, hh, j], p_sc[slot, hh], preferred_element_type=F32)
            acc_sc[i, hh] = al_sc[slot, hh] * acc_sc[i, hh] + pv

    def softmax(pair, slot, diagonal):
        i, _ = pair
        for hh in heads:
            for c in range(n_strips):
                lanes = slice(c * LANES, (c + 1) * LANES)
                m = st_sc[i, hh, :, lanes]
                if not diagonal:
                    mc = jnp.maximum(m, bm_sc[slot, hh, :, lanes])
                    p = jnp.exp2(s_sc[slot, hh, :, lanes] - mc)
                else:
                    s = s_sc[slot, hh, :, lanes]
                    key = lax.broadcasted_iota(jnp.int32, s.shape, 0)
                    qry = lax.broadcasted_iota(jnp.int32, s.shape, 1) + (c * LANES) % TB
                    s = jnp.where(key <= qry, s, -jnp.inf)
                    mc = jnp.maximum(m, jnp.max(s, axis=0, keepdims=True))
                    p = jnp.exp2(s - mc)
                p_sc[slot, hh, :, lanes] = p.astype(BF16)
                st_sc[i, hh, :, lanes] = mc
                al_sc[slot, hh, :, lanes] = jnp.exp2(m - mc)

    def succ(pair):
        i, k = pair
        last = k == i
        i2 = jnp.minimum(jnp.where(last, i + 1, i), n_q - 1)
        return i2, jnp.where(last, 0, k + 1)

    def two_steps(t, carry):
        prev, cur = carry
        nxt = succ(cur)
        nxt2 = succ(nxt)

        def body(diag0):
            scores(nxt, 1)
            values(prev, 1)
            softmax(cur, 0, diag0)
            scores(nxt2, 0)
            values(cur, 0)
            softmax(nxt, 1, False)

        d0 = is_diag(cur)
        pl.when(d0)(lambda: body(True))
        pl.when(jnp.logical_not(d0))(lambda: body(False))
        return nxt, nxt2

    zero = jnp.int32(0)
    first = (zero, zero)
    scores(first, 0)
    last, _ = lax.fori_loop(0, n_pairs // 2, two_steps, (first, first))
    values(last, 1)

    def finish(t, c):
        for i in (2 * t, 2 * t + 1):
            for hh in heads:
                o = acc_sc[i, hh, :V_DIM] * (1.0 / acc_sc[i, hh, V_DIM:V_DIM + 1])
                o = o[:, :TB] - lam * o[:, TB:]
                ms = jnp.mean(o * o, axis=0, keepdims=True)
                o = o * lax.rsqrt(ms + LN_EPS) * (g_ref[...] * (1.0 - lam_init))
                rows = pl.ds(pl.multiple_of(i * TB, TB), TB)
                o_ref[0, rows, hh * LANES:(hh + 1) * LANES] = o.T.astype(o_ref.dtype)
        return c

    lax.fori_loop(0, n_q // 2, finish, 0)


def _diff_attn_flat(qt, k, vt, lq1, lk1, lq2, lk2, subln_g, lam_init):
    B, L, _ = k.shape
    n_q = L // TB
    vec = pl.BlockSpec((1, QK_DIM), lambda b: (0, 0))
    return pl.pallas_call(
        functools.partial(_attn_flat_kernel, lam_init=lam_init),
        grid=(B,),
        in_specs=[
            vec, vec, vec, vec,
            pl.BlockSpec((V_DIM, 1), lambda b: (0, 0)),
            pl.BlockSpec((1, HEADS, n_q, LANES, TB), lambda b: (b, 0, 0, 0, 0)),
            pl.BlockSpec((1, L, HEADS * LANES), lambda b: (b, 0, 0)),
            pl.BlockSpec((1, HEADS, n_q, VT_ROWS, TB), lambda b: (b, 0, 0, 0, 0)),
        ],
        out_specs=pl.BlockSpec((1, L, HEADS * LANES), lambda b: (b, 0, 0)),
        out_shape=jax.ShapeDtypeStruct((B, L, ATT_WIDTH), BF16),
        scratch_shapes=[
            pltpu.VMEM((n_q, HEADS, VT_ROWS, 2 * TB), F32),
            pltpu.VMEM((n_q, HEADS, LANES, 2 * TB), BF16),
            pltpu.VMEM((2, HEADS, TB, 2 * TB), F32),
            pltpu.VMEM((2, HEADS, TB, 2 * TB), BF16),
            pltpu.VMEM((2, HEADS, 1, 2 * TB), F32),
            pltpu.VMEM((n_q, HEADS, 1, 2 * TB), F32),
            pltpu.VMEM((2, HEADS, 1, 2 * TB), F32),
        ],
        compiler_params=_cparams(("parallel",)),
        name="diff_attn",
    )(lq1, lk1, lq2, lk2, subln_g.reshape(V_DIM, 1), qt, k, vt)


def _block_transpose8(vs):
    lane_blk = lax.broadcasted_iota(jnp.int32, vs[0].shape, 1) // SSM_GROUP
    for d in (4, 2, 1):
        keep = (lane_blk & d) == 0
        new = list(vs)
        for a in range(8):
            if a & d == 0:
                lo, hi = vs[a], vs[a + d]
                new[a] = jnp.where(keep, lo, pltpu.roll(hi, SSM_GROUP * d, 1))
                new[a + d] = jnp.where(keep, pltpu.roll(lo, LANES - SSM_GROUP * d, 1), hi)
        vs = new
    return vs


def _ssm_kernel(u_ref, toep_ref, win_ref, wout_ref, a_ref, d_ref, y_ref, s_sc, x_sc, sp_sc, *, jt):
    rows = jt * SUBLANES

    @pl.when(pl.program_id(1) == 0)
    def _():
        s_sc[...] = jnp.zeros(s_sc.shape, F32)

    ug = [u_ref[0, g].astype(BF16) for g in range(GROUPS_PER_TILE)]

    y = [jnp.dot(ug[g], toep_ref[0, g], preferred_element_type=F32) for g in range(GROUPS_PER_TILE)]

    for m in range(PAIRS_PER_TILE):
        up = jnp.concatenate([ug[2 * m], ug[2 * m + 1]], axis=1)
        xp = jnp.dot(up, win_ref[0, m], preferred_element_type=F32)
        x_sc[0, :, m * LANES:(m + 1) * LANES] = xp[:, :LANES]
        x_sc[1, :, m * LANES:(m + 1) * LANES] = xp[:, LANES:]

    ar = a_ref[0, 0:1, :]
    ai = a_ref[0, 1:2, :]

    def step(j, carry):
        sr, si = carry
        r8 = pl.ds(pl.multiple_of(j * SUBLANES, SUBLANES), SUBLANES)
        sp_sc[0, r8, :] = sr
        sp_sc[1, r8, :] = si
        return (ar * sr - ai * si + x_sc[0, r8, :], ar * si + ai * sr + x_sc[1, r8, :])

    sr, si = lax.fori_loop(0, jt, step, (s_sc[0], s_sc[1]))
    s_sc[0] = sr
    s_sc[1] = si

    for m in range(PAIRS_PER_TILE):
        sp = jnp.concatenate([sp_sc[0, :, m * LANES:(m + 1) * LANES],
                              sp_sc[1, :, m * LANES:(m + 1) * LANES]], axis=1).astype(BF16)
        yc = jnp.dot(sp, wout_ref[0, m], preferred_element_type=F32)
        y[2 * m] = y[2 * m] + yc[:, :CHUNK_COLS]
        y[2 * m + 1] = y[2 * m + 1] + yc[:, CHUNK_COLS:]

    for g in range(GROUPS_PER_TILE):
        y_ref[0, g] = y[g] + d_ref[0, g] * u_ref[0, g]


def _chunk_slabs(tiles):
    z_lo = _block_transpose8(tiles[:8])
    z_hi = _block_transpose8(tiles[8:])
    return [jnp.concatenate([z_lo[g], z_hi[g]], axis=1) for g in range(GROUPS_PER_TILE)]


def _chunk_tiles(slabs):
    out = []
    for half in range(2):
        out += _block_transpose8([s[:, half * LANES:(half + 1) * LANES] for s in slabs])
    return out


def _ssm(u_slab, tables, d_skip, B, L, lt=1024):
    toep, win, wout, a16 = tables
    jt = lt // CHUNK
    n_l = L // lt
    rows = jt * B
    slab = pl.BlockSpec((1, GROUPS_PER_TILE, rows, CHUNK_COLS), lambda q, l: (q, 0, l, 0))
    d_slab = jnp.tile(d_skip.reshape(LANE_TILES, GROUPS_PER_TILE, 1, SSM_GROUP), (1, 1, 1, CHUNK))
    return pl.pallas_call(
        functools.partial(_ssm_kernel, jt=jt),
        grid=(LANE_TILES, n_l),
        in_specs=[
            slab,
            pl.BlockSpec((1, GROUPS_PER_TILE, CHUNK_COLS, CHUNK_COLS), lambda q, l: (q, 0, 0, 0)),
            pl.BlockSpec((1, PAIRS_PER_TILE, 2 * CHUNK_COLS, 2 * LANES), lambda q, l: (q, 0, 0, 0)),
            pl.BlockSpec((1, PAIRS_PER_TILE, 2 * LANES, 2 * CHUNK_COLS), lambda q, l: (q, 0, 0, 0)),
            pl.BlockSpec((1, 2, STATE_COLS), lambda q, l: (q, 0, 0)),
            pl.BlockSpec((1, GROUPS_PER_TILE, 1, CHUNK_COLS), lambda q, l: (q, 0, 0, 0)),
        ],
        out_specs=slab,
        out_shape=jax.ShapeDtypeStruct(u_slab.shape, F32),
        scratch_shapes=[
            pltpu.VMEM((2, B, STATE_COLS), F32),
            pltpu.VMEM((2, rows, STATE_COLS), F32),
            pltpu.VMEM((2, rows, STATE_COLS), F32),
        ],
        compiler_params=_cparams(("parallel", "arbitrary")),
        name="ssm",
    )(u_slab, toep, win, wout, a16, d_slab)


def _cmul(a, b):
    return a[0] * b[0] - a[1] * b[1], a[0] * b[1] + a[1] * b[0]


def _zoh(lr, li, ldt):
    dt = jnp.exp(ldt)
    mag = jnp.exp(lr * dt)
    ab = (mag * jnp.cos(li * dt), mag * jnp.sin(li * dt))
    den = lr * lr + li * li
    zr, zi = ab[0] - 1.0, ab[1]
    return ab, ((zr * lr + zi * li) / den, (zi * lr - zr * li) / den)


def _cpow(base, exponent, bits):
    out = None
    for k in range(bits):
        on = (exponent & (1 << k)) != 0
        sel = (jnp.where(on, base[0], 1.0), jnp.where(on, base[1], 0.0))
        out = sel if out is None else _cmul(out, sel)
        if k + 1 < bits:
            base = _cmul(base, base)
    return out


def _ssm_prep_kernel(row_ref, ct_ref, bt_ref, wup_ref, wdn_ref, toep_ref, win_ref, wout_ref, a_ref,
                     wup_bf_ref, wdn_bf_ref):
    wup_bf_ref[...] = wup_ref[...].astype(BF16)
    wdn_bf_ref[...] = wdn_ref[...].astype(BF16)
    bits = CHUNK.bit_length() - 1
    lane = lax.broadcasted_iota(jnp.int32, (1, CHUNK_COLS), 1)
    tau = lane // SSM_GROUP
    rowi = lax.broadcasted_iota(jnp.int32, (CHUNK_COLS, 1), 0)
    back = (CHUNK - 1) - rowi // SSM_GROUP
    lane2 = lax.broadcasted_iota(jnp.int32, (1, LANES), 1)
    exact = functools.partial(jnp.dot, precision=lax.Precision.HIGHEST, preferred_element_type=F32)
    h_row = lax.broadcasted_iota(jnp.int32, (SSM_GROUP, CHUNK_COLS), 0)
    repeat = jnp.where(lane % SSM_GROUP == h_row, 1.0, 0.0)
    zoh = [_zoh(row_ref[0, m], row_ref[1, m], row_ref[2, m]) for m in range(PAIRS_PER_TILE)]
    pad = jnp.zeros((SUBLANES - PAIRS_PER_TILE, LANES), F32)
    ab_cols = tuple(jnp.concatenate([z[0][c] for z in zoh] + [pad], axis=0).T for c in range(2))
    a_rows = []
    for m in range(PAIRS_PER_TILE):
        abp, fp = zoh[m]
        pwr = _cpow(abp, back, bits)
        a16 = abp
        for _ in range(bits):
            a16 = _cmul(a16, a16)
        a_rows.append(a16)
        wout_rows = [[None, None], [None, None]]
        for j in range(2):
            g = 2 * m + j
            states = slice(j * SSM_STATE, (j + 1) * SSM_STATE)
            ab = tuple(v[states, m:m + 1] for v in ab_cols)
            pw = _cpow(ab, tau, bits)
            c_rep = tuple(exact(ct_ref[c, g], repeat) for c in range(2))
            g0 = _cmul(pw, c_rep)
            g1 = _cmul(g0, ab)
            mine = (lane2 < SSM_STATE) if j == 0 else (lane2 >= SSM_STATE)
            bb2 = _cmul(fp, (bt_ref[0, g], bt_ref[1, g]))
            bb2 = tuple(jnp.where(mine, v, 0.0) for v in bb2)
            none = jnp.zeros_like(g0[0])
            stack = (lambda v: jnp.concatenate([v, none], axis=0)) if j == 0 else \
                    (lambda v: jnp.concatenate([none, v], axis=0))
            kt = exact(bb2[0], stack(g0[0])) - exact(bb2[1], stack(g0[1]))
            toep_ref[0, g] = jnp.concatenate(
                [jnp.where(lane >= SSM_GROUP * r, pltpu.roll(kt, SSM_GROUP * r, 1), 0.0) if r else kt
                 for r in range(CHUNK)], axis=0).astype(BF16)
            w = _cmul(pwr, tuple(jnp.concatenate([v] * CHUNK, axis=0) for v in bb2))
            win_ref[0, m, j * CHUNK_COLS:(j + 1) * CHUNK_COLS, :] = jnp.concatenate(w, axis=1).astype(BF16)
            wout_rows[0][j] = g1[0]
            wout_rows[1][j] = -g1[1]
        zero = jnp.zeros((SSM_STATE, CHUNK_COLS), F32)
        wout_ref[0, m] = jnp.concatenate([
            jnp.concatenate([wout_rows[0][0], zero], axis=1),
            jnp.concatenate([zero, wout_rows[0][1]], axis=1),
            jnp.concatenate([wout_rows[1][0], zero], axis=1),
            jnp.concatenate([zero, wout_rows[1][1]], axis=1)], axis=0).astype(BF16)
    a_ref[0] = jnp.concatenate([jnp.concatenate([a[0] for a in a_rows], axis=1),
                                jnp.concatenate([a[1] for a in a_rows], axis=1)], axis=0)


def _ssm_prep(lam_re, lam_im, log_dt, b_re, b_im, c_re, c_im, w_up, w_down):
    G, P, H = b_re.shape
    up_rows, dn_rows = w_up.shape[0] // LANE_TILES, w_down.shape[0] // LANE_TILES
    assert up_rows * LANE_TILES == w_up.shape[0] and dn_rows * LANE_TILES == w_down.shape[0]
    assert up_rows % 16 == 0 and dn_rows % 16 == 0
    up_spec = pl.BlockSpec((up_rows, w_up.shape[1]), lambda q: (q, 0))
    dn_spec = pl.BlockSpec((dn_rows, w_down.shape[1]), lambda q: (q, 0))
    ldt = jnp.broadcast_to(log_dt[:, None], (G, P))
    rows = jnp.stack([lam_re, lam_im, ldt]).reshape(3, G // 2, 1, 2 * P)
    ct = jnp.stack([c_re, c_im]).transpose(0, 1, 3, 2)
    bt = jnp.tile(jnp.stack([b_re, b_im]).transpose(0, 1, 3, 2), (1, 1, 1, 2))
    gt = GROUPS_PER_TILE
    return pl.pallas_call(
        _ssm_prep_kernel,
        grid=(LANE_TILES,),
        in_specs=[
            pl.BlockSpec((3, PAIRS_PER_TILE, 1, 2 * P), lambda q: (0, q, 0, 0)),
            pl.BlockSpec((2, gt, P, H), lambda q: (0, q, 0, 0)),
            pl.BlockSpec((2, gt, H, 2 * P), lambda q: (0, q, 0, 0)),
            up_spec,
            dn_spec,
        ],
        out_specs=[
            pl.BlockSpec((1, gt, CHUNK_COLS, CHUNK_COLS), lambda q: (q, 0, 0, 0)),
            pl.BlockSpec((1, PAIRS_PER_TILE, 2 * CHUNK_COLS, 2 * LANES), lambda q: (q, 0, 0, 0)),
            pl.BlockSpec((1, PAIRS_PER_TILE, 2 * LANES, 2 * CHUNK_COLS), lambda q: (q, 0, 0, 0)),
            pl.BlockSpec((1, 2, STATE_COLS), lambda q: (q, 0, 0)),
            up_spec,
            dn_spec,
        ],
        out_shape=[
            jax.ShapeDtypeStruct((LANE_TILES, gt, CHUNK_COLS, CHUNK_COLS), BF16),
            jax.ShapeDtypeStruct((LANE_TILES, PAIRS_PER_TILE, 2 * CHUNK_COLS, 2 * LANES), BF16),
            jax.ShapeDtypeStruct((LANE_TILES, PAIRS_PER_TILE, 2 * LANES, 2 * CHUNK_COLS), BF16),
            jax.ShapeDtypeStruct((LANE_TILES, 2, STATE_COLS), F32),
            jax.ShapeDtypeStruct(w_up.shape, BF16),
            jax.ShapeDtypeStruct(w_down.shape, BF16),
        ],
        compiler_params=_cparams(("parallel",)),
        name="ssm_prep",
    )(rows, ct, bt, w_up, w_down)


def _layer_norm(r, g, b):
    mu = jnp.mean(r, axis=-1, keepdims=True)
    c = r - mu
    var = jnp.mean(c * c, axis=-1, keepdims=True)
    return c * lax.rsqrt(var + LN_EPS) * g + b


MIX_ROWS = 256


def _gelu_tanh(y):
    return 0.5 * y * (1.0 + jnp.tanh(math.sqrt(2.0 / math.pi) * (y + 0.044715 * (y * y * y))))


def _mix_ln1_kernel(x_ref, attn_ref, y_ref, wglu_ref, bglu_ref, wo_ref, g_ref, b_ref, o_ref, y_sc):
    nb, tl, d = x_ref.shape
    pos = MIX_ROWS // nb
    n_pieces = tl // pos
    jp = pos // CHUNK

    def relayout(n):
        rows = slice(n * jp * nb, (n + 1) * jp * nb)
        for q in range(LANE_TILES):
            tiles = _chunk_tiles([y_ref[q, g, rows] for g in range(GROUPS_PER_TILE)])
            y_sc[q, n * MIX_ROWS:(n + 1) * MIX_ROWS] = jnp.stack(
                [t.reshape(jp, nb, LANES) for t in tiles], axis=1).reshape(MIX_ROWS, LANES)

    def glu(n):
        y = jnp.concatenate(
            [jnp.concatenate([y_sc[q, pl.ds(n * MIX_ROWS + b, pos, stride=nb), :] for q in range(LANE_TILES)], axis=1)
             for b in range(nb)], axis=0)
        gl = _gelu_tanh(y)
        z = gl * jax.nn.sigmoid(jnp.dot(gl.astype(BF16), wglu_ref[...], preferred_element_type=F32) + bglu_ref[...])
        return z.astype(BF16)

    def project(n, z):
        ls = slice(n * pos, (n + 1) * pos)
        mix = (jnp.dot(attn_ref[:, ls].reshape(MIX_ROWS, ATT_WIDTH), wo_ref[:ATT_WIDTH], preferred_element_type=F32)
               + jnp.dot(z, wo_ref[ATT_WIDTH:], preferred_element_type=F32))
        res = _layer_norm(ALPHA * x_ref[:, ls].reshape(MIX_ROWS, d) + mix, g_ref[...], b_ref[...])
        o_ref[:, ls] = res.reshape(nb, pos, d)

    relayout(0)
    z = glu(0)
    for n in range(n_pieces):
        if n + 1 < n_pieces:
            relayout(n + 1)
            z_next = glu(n + 1)
        project(n, z)
        z = z_next


def _mix_ln1(x, attn, y_tm, wglu_bf, b_glu, wo_bf, ln_g, ln_b, tl=256):
    B, L, D = x.shape
    assert MIX_ROWS % (B * CHUNK) == 0 and tl % (MIX_ROWS // B) == 0
    resident = lambda shape: pl.BlockSpec(shape, lambda l: (0, 0), pipeline_mode=pl.Buffered(1))
    return pl.pallas_call(
        _mix_ln1_kernel,
        grid=(L // tl,),
        in_specs=[
            pl.BlockSpec((B, tl, D), lambda l: (0, l, 0)),
            pl.BlockSpec((B, tl, ATT_WIDTH), lambda l: (0, l, 0)),
            pl.BlockSpec((LANE_TILES, GROUPS_PER_TILE, tl // CHUNK * B, CHUNK_COLS), lambda l: (0, 0, l, 0)),
            resident((SSM_WIDTH, SSM_WIDTH)),
            resident((1, SSM_WIDTH)),
            resident((D, D)),
            resident((1, D)),
            resident((1, D)),
        ],
        out_specs=pl.BlockSpec((B, tl, D), lambda l: (0, l, 0)),
        out_shape=jax.ShapeDtypeStruct((B, L, D), F32),
        scratch_shapes=[pltpu.VMEM((LANE_TILES, tl * B, LANES), F32)],
        compiler_params=_cparams(("parallel",)),
        name="mix_ln1",
    )(x, attn, y_tm, wglu_bf, b_glu, wo_bf, ln_g, ln_b)


HALO = 16
FFN_SUB = 256
FFN_DOWN_PARTS = 2
FFN_ROWS = 128


def _ffn_kernel(x_ref, xh_ref, p_ref, wple_ref, wgate_ref, wup_ref, cw_ref, cb_ref, wd_ref, g_ref, b_ref, o_ref,
                x32_sc, xh32_sc, p32_sc, xb_sc, acc_sc, hg_sc, hv_sc, act_sc, *, tl):
    nb = x_ref.shape[0]
    tm = tl * nb
    d_slabs = x_ref.shape[2] // LANES
    hp = (CONV_WIDTH - 1)
    for bb in range(nb):
        for s in range(d_slabs):
            lanes = slice(s * LANES, (s + 1) * LANES)
            x32_sc[s, pl.ds(bb, tl, stride=nb), :] = x_ref[bb, :, lanes]
            xh32_sc[s, pl.ds(bb, hp, stride=nb), :] = xh_ref[bb, SUBLANES - hp:, lanes]
        for s in range(p_ref.shape[2] // LANES):
            p32_sc[s, pl.ds(bb, tl, stride=nb), :] = p_ref[bb, :, s * LANES:(s + 1) * LANES]
    x_tm = jnp.concatenate([x32_sc[s] for s in range(d_slabs)], axis=1)
    p_tm = jnp.concatenate([p32_sc[s] for s in range(p32_sc.shape[0])], axis=1)
    keep = jnp.where(pl.program_id(0) == 0, 0.0, 1.0)
    xb_sc[0:HALO] = (jnp.concatenate([xh32_sc[s] for s in range(d_slabs)], axis=1) * keep).astype(BF16)
    xb_sc[HALO:] = x_tm.astype(BF16)
    gate = jax.nn.sigmoid(jnp.dot(xb_sc[HALO:], wgate_ref[...], preferred_element_type=F32))
    ple = jnp.dot(p_tm.astype(BF16), wple_ref[...], preferred_element_type=F32) * gate
    acc_sc[...] = ALPHA * x_tm + ple

    def conv(h_sc, slot, cols, r0, rows):
        out = cb_ref[:, cols] + cw_ref[CONV_WIDTH - 1:CONV_WIDTH, cols] * h_sc[slot, HALO + r0:HALO + r0 + rows]
        for j in range(CONV_WIDTH - 1):
            back = (CONV_WIDTH - 1 - j) * nb
            out = out + cw_ref[j:j + 1, cols] * h_sc[slot, HALO + r0 - back:HALO + r0 - back + rows]
        return out

    n_sub = D_FF // FFN_SUB
    g_cols = lambda n: slice(n * FFN_SUB, (n + 1) * FFN_SUB)
    v_cols = lambda n: slice(D_FF + n * FFN_SUB, D_FF + (n + 1) * FFN_SUB)

    def up(n):
        xb = xb_sc[...]
        hg_sc[n % 2] = jnp.dot(xb, wup_ref[:, g_cols(n)], preferred_element_type=F32)
        hv_sc[n % 2] = jnp.dot(xb, wup_ref[:, v_cols(n)], preferred_element_type=F32)

    def gate_block(n):
        for r0 in range(0, tm, FFN_ROWS):
            gc = conv(hg_sc, n % 2, g_cols(n), r0, FFN_ROWS)
            vc = conv(hv_sc, n % 2, v_cols(n), r0, FFN_ROWS)
            act_sc[r0:r0 + FFN_ROWS, g_cols(n)] = (gc * jax.nn.sigmoid(gc) * vc).astype(BF16)

    def down(lo, hi):
        cols = slice(lo * FFN_SUB, hi * FFN_SUB)
        return jnp.dot(act_sc[:, cols], wd_ref[cols, :], preferred_element_type=F32)

    cuts = [n_sub * k // FFN_DOWN_PARTS for k in range(FFN_DOWN_PARTS + 1)]
    up(0)
    for n in range(n_sub):
        if n + 1 < n_sub:
            up(n + 1)
        gate_block(n)
        if n + 1 in cuts[1:-1]:
            lo = cuts[cuts.index(n + 1) - 1]
            acc_sc[...] += down(lo, n + 1)

    res = _layer_norm(acc_sc[...] + down(cuts[-2], n_sub), g_ref[...], b_ref[...])
    for s in range(d_slabs):
        x32_sc[s] = res[:, s * LANES:(s + 1) * LANES]
    for bb in range(nb):
        for s in range(d_slabs):
            o_ref[bb, :, s * LANES:(s + 1) * LANES] = x32_sc[s, pl.ds(bb, tl, stride=nb), :]


def _ffn_ln2(x1, p, wple_bf, wgate_bf, wup_bf, conv_w, conv_b, wdown_bf, ln_g, ln_b, tl=64):
    B, L, D = x1.shape
    tm = tl * B
    assert D_FF % FFN_SUB == 0 and HALO == (CONV_WIDTH - 1) * B and tm % FFN_ROWS == 0
    const = lambda i: (0, 0)
    resident = lambda shape: pl.BlockSpec(shape, const, pipeline_mode=pl.Buffered(1))
    return pl.pallas_call(
        functools.partial(_ffn_kernel, tl=tl),
        grid=(L // tl,),
        in_specs=[
            pl.BlockSpec((B, tl, D), lambda i: (0, i, 0)),
            pl.BlockSpec((B, SUBLANES, D), lambda i: (0, jnp.maximum(i * (tl // SUBLANES) - 1, 0), 0)),
            pl.BlockSpec((B, tl, PLE_DIM), lambda i: (0, i, 0)),
            resident((PLE_DIM, D)),
            resident((D, D)),
            resident((D, 2 * D_FF)),
            resident((CONV_WIDTH, 2 * D_FF)),
            resident((1, 2 * D_FF)),
            resident((D_FF, D)),
            resident((1, D)),
            resident((1, D)),
        ],
        out_specs=pl.BlockSpec((B, tl, D), lambda i: (0, i, 0)),
        out_shape=jax.ShapeDtypeStruct((B, L, D), F32),
        scratch_shapes=[
            pltpu.VMEM((D // LANES, tm, LANES), F32),
            pltpu.VMEM((D // LANES, HALO, LANES), F32),
            pltpu.VMEM((PLE_DIM // LANES, tm, LANES), F32),
            pltpu.VMEM((tm + HALO, D), BF16),
            pltpu.VMEM((tm, D), F32),
            pltpu.VMEM((2, tm + HALO, FFN_SUB), F32),
            pltpu.VMEM((2, tm + HALO, FFN_SUB), F32),
            pltpu.VMEM((tm, D_FF), BF16),
        ],
        compiler_params=_cparams(("parallel",)),
        name="ffn_ln2",
    )(x1, x1, p, wple_bf, wgate_bf, wup_bf, conv_w, conv_b, wdown_bf, ln_g, ln_b)


def kernel(x, p, w_in, diff_lambda_q1, diff_lambda_k1, diff_lambda_q2, diff_lambda_k2, diff_subln_g, ssm_lambda_re, ssm_lambda_im, ssm_log_dt, ssm_b_re, ssm_b_im, ssm_c_re, ssm_c_im, ssm_d, ssm_w_glu, ssm_b_glu, w_o, ln1_g, ln1_b, ffn_w_up, ffn_conv_w, ffn_conv_b, ffn_w_down, w_ple, w_ple_gate, ln2_g, ln2_b):
    B, L, D = x.shape
    assert B == SUBLANES and D == D_MODEL and L % 512 == 0
    for i in range(DEPTH):
        lam_init = 0.8 - 0.6 * math.exp(-0.3 * i)
        row = lambda a: a[i].reshape(1, -1)
        qt, k, vt, u_tm = _in_proj(x, w_in[i])
        attn = _diff_attn_flat(qt, k, vt, row(diff_lambda_q1), row(diff_lambda_k1), row(diff_lambda_q2),
                          row(diff_lambda_k2), diff_subln_g[i], lam_init)
        *tables, w_up_bf, w_down_bf = _ssm_prep(ssm_lambda_re[i], ssm_lambda_im[i], ssm_log_dt[i], ssm_b_re[i],
                                                ssm_b_im[i], ssm_c_re[i], ssm_c_im[i], ffn_w_up[i], ffn_w_down[i])
        y_tm = _ssm(u_tm, tables, ssm_d[i], B, L)
        x1 = _mix_ln1(x, attn, y_tm, ssm_w_glu[i].astype(BF16), row(ssm_b_glu), w_o[i].astype(BF16),
                      row(ln1_g), row(ln1_b))
        x = _ffn_ln2(x1, p[i], w_ple[i].astype(BF16), w_ple_gate[i].astype(BF16), w_up_bf,
                     ffn_conv_w[i], row(ffn_conv_b), w_down_bf, row(ln2_g), row(ln2_b))
    return x
```

```python
import functools
import math

import jax
import jax.numpy as jnp
from jax import lax
from jax.experimental import pallas as pl
from jax.experimental.pallas import tpu as pltpu

F32 = jnp.float32
BF16 = jnp.bfloat16

SUBLANES = 8
LANES = 128

DEPTH = 1
D_MODEL = 1024
PLE_DIM = 256
HEADS = 4
QK_DIM = 64
V_DIM = 2 * QK_DIM
QK_WIDTH = HEADS * 2 * QK_DIM
ATT_WIDTH = HEADS * V_DIM
SSM_WIDTH = D_MODEL - ATT_WIDTH
SSM_GROUP = 16
SSM_GROUPS = SSM_WIDTH // SSM_GROUP
SSM_STATE = 64
D_FF = 2816
CONV_WIDTH = 3
LN_EPS = 1e-5
ALPHA = (2 * DEPTH) ** 0.25
QK_SCALE = QK_DIM ** -0.5 * math.log2(math.e)

CHUNK = 16
CHUNK_COLS = CHUNK * SSM_GROUP
GROUPS_PER_TILE = LANES // SSM_GROUP
LANE_TILES = SSM_WIDTH // LANES
PAIRS_PER_TILE = GROUPS_PER_TILE // 2
STATE_COLS = GROUPS_PER_TILE * SSM_STATE

VMEM_LIMIT = 56 * 1024 * 1024


def _cparams(sem):
    return pltpu.CompilerParams(dimension_semantics=sem, vmem_limit_bytes=VMEM_LIMIT)


TB = 256
VT_ROWS = V_DIM + 16


def _in_proj_kernel(x_ref, w_ref, qt_ref, k_ref, vt_ref, u_ref, u_sc):
    wq_ref = w_ref.at[:, :QK_WIDTH]
    wk_ref = w_ref.at[:, QK_WIDTH:2 * QK_WIDTH]
    wv_ref = w_ref.at[:, 2 * QK_WIDTH:2 * QK_WIDTH + ATT_WIDTH]
    wu_ref = w_ref.at[:, 2 * QK_WIDTH + ATT_WIDTH:]
    nb, tl, d = x_ref.shape
    jt = tl // CHUNK
    xb = x_ref[...].reshape(nb * tl, d).astype(BF16)
    hu = jnp.dot(xb, wu_ref[...], preferred_element_type=F32)
    for b in range(nb):
        for q in range(LANE_TILES):
            u_sc[q, pl.ds(b, tl, stride=nb), :] = hu[b * tl:(b + 1) * tl, q * LANES:(q + 1) * LANES]
    for q in range(LANE_TILES):
        tiles = u_sc[q].reshape(jt, CHUNK, nb, LANES)
        slabs = _chunk_slabs([tiles[:, r].reshape(jt * nb, LANES) for r in range(CHUNK)])
        for g in range(GROUPS_PER_TILE):
            u_ref[q, g] = slabs[g]
    tn = (((0,), (1,)), ((), ()))
    qt = lax.dot_general(wq_ref[...], xb, tn, preferred_element_type=F32) * QK_SCALE
    vt = lax.dot_general(wv_ref[...], xb, tn, preferred_element_type=F32)
    pad_row = lax.broadcasted_iota(jnp.int32, (VT_ROWS - V_DIM, TB), 0)
    ones_row = jnp.where(pad_row == 0, 1.0, 0.0).astype(BF16)
    for b in range(nb):
        for h in range(HEADS):
            for c in range(tl // TB):
                cols = slice(b * tl + c * TB, b * tl + (c + 1) * TB)
                qt_ref[b, h, c] = qt[h * LANES:(h + 1) * LANES, cols].astype(BF16)
                vt_ref[b, h, c, :V_DIM] = vt[h * LANES:(h + 1) * LANES, cols].astype(BF16)
                vt_ref[b, h, c, V_DIM:] = ones_row
    k_ref[...] = jnp.dot(xb, wk_ref[...], preferred_element_type=F32).astype(BF16).reshape(k_ref.shape)


def _in_proj(x, w_in, tl=256):
    B, L, D = x.shape
    slab = lambda rows: pl.BlockSpec((B, HEADS, tl // TB, rows, TB), lambda l: (0, 0, l, 0, 0))
    slab_shape = lambda rows: jax.ShapeDtypeStruct((B, HEADS, L // TB, rows, TB), BF16)
    return pl.pallas_call(
        _in_proj_kernel,
        grid=(L // tl,),
        in_specs=[
            pl.BlockSpec((B, tl, D), lambda l: (0, l, 0)),
            pl.BlockSpec(w_in.shape, lambda l: (0, 0), pipeline_mode=pl.Buffered(1)),
        ],
        out_specs=[
            slab(LANES),
            pl.BlockSpec((B, tl, QK_WIDTH), lambda l: (0, l, 0)),
            slab(VT_ROWS),
            pl.BlockSpec((LANE_TILES, GROUPS_PER_TILE, tl // CHUNK * B, CHUNK_COLS), lambda l: (0, 0, l, 0)),
        ],
        out_shape=[
            slab_shape(LANES),
            jax.ShapeDtypeStruct((B, L, QK_WIDTH), BF16),
            slab_shape(VT_ROWS),
            jax.ShapeDtypeStruct((LANE_TILES, GROUPS_PER_TILE, L // CHUNK * B, CHUNK_COLS), F32),
        ],
        scratch_shapes=[pltpu.VMEM((LANE_TILES, tl * B, LANES), F32)],
        compiler_params=_cparams(("parallel",)),
        name="in_proj",
    )(x, w_in.astype(BF16))


def _attn_flat_kernel(lq1_ref, lk1_ref, lq2_ref, lk2_ref, g_ref, qt_ref, k_ref, vt_ref, o_ref,
                      acc_sc, qst_sc, s_sc, p_sc, bm_sc, st_sc, al_sc, *, lam_init):
    n_q = qt_ref.shape[2]
    n_pairs = n_q * (n_q + 1) // 2
    assert TB == 2 * LANES and n_pairs % 2 == 0
    heads = range(HEADS)
    n_strips = 2 * TB // LANES
    lam = (jnp.exp(jnp.sum(lq1_ref[...] * lk1_ref[...], axis=-1, keepdims=True))
           - jnp.exp(jnp.sum(lq2_ref[...] * lk2_ref[...], axis=-1, keepdims=True)) + lam_init)

    def prepare(i, c):
        for hh in heads:
            qt = qt_ref[0, hh, i]
            row = lax.broadcasted_iota(jnp.int32, qt.shape, 0)
            zero = jnp.zeros_like(qt)
            qst_sc[i, hh, :, :TB] = jnp.where(row < QK_DIM, qt, zero)
            qst_sc[i, hh, :, TB:] = jnp.where(row >= QK_DIM, qt, zero)
            acc_sc[i, hh] = jnp.zeros(acc_sc.shape[2:], F32)
            st_sc[i, hh] = jnp.full((1, 2 * TB), -jnp.inf, F32)
        return c

    lax.fori_loop(0, n_q, prepare, 0)
    p_sc[1] = jnp.zeros(p_sc.shape[1:], BF16)
    al_sc[1] = jnp.ones(al_sc.shape[1:], F32)

    def diag_place(i):
        r = i % 4
        return jnp.where(jnp.logical_or(r == 1, r == 2), 1, 0)

    def is_diag(pair):
        return pair[1] == diag_place(pair[0])

    def block_of(pair):
        i, k = pair
        d = diag_place(i)
        return jnp.where(k == d, i, jnp.where(k < d, k, k - 1))

    def scores(pair, slot):
        i, j = pair[0], block_of(pair)
        rows = pl.ds(pl.multiple_of(j * TB, TB), TB)
        for hh in heads:
            kb = k_ref[0, rows, hh * LANES:(hh + 1) * LANES]
            s = jnp.dot(kb, qst_sc[i, hh], preferred_element_type=F32)
            s_sc[slot, hh] = s
            bm_sc[slot, hh] = jnp.max(s, axis=0, keepdims=True)

    def values(pair, slot):
        i, j = pair[0], block_of(pair)
        for hh in heads:
            pv = jnp.dot(vt_ref[0, hh, j], p_sc[slot, hh], preferred_element_type=F32)
            acc_sc[i, hh] = al_sc[slot, hh] * acc_sc[i, hh] + pv

    def softmax(pair, slot, diagonal):
        i, _ = pair
        for hh in heads:
            for c in range(n_strips):
                lanes = slice(c * LANES, (c + 1) * LANES)
                m = st_sc[i, hh, :, lanes]
                if not diagonal:
                    mc = jnp.maximum(m, bm_sc[slot, hh, :, lanes])
                    p = jnp.exp2(s_sc[slot, hh, :, lanes] - mc)
                else:
                    s = s_sc[slot, hh, :, lanes]
                    key = lax.broadcasted_iota(jnp.int32, s.shape, 0)
                    qry = lax.broadcasted_iota(jnp.int32, s.shape, 1) + (c * LANES) % TB
                    s = jnp.where(key <= qry, s, -jnp.inf)
                    mc = jnp.maximum(m, jnp.max(s, axis=0, keepdims=True))
                    p = jnp.exp2(s - mc)
                p_sc[slot, hh, :, lanes] = p.astype(BF16)
                st_sc[i, hh, :, lanes] = mc
                al_sc[slot, hh, :, lanes] = jnp.exp2(m - mc)

    def succ(pair):
        i, k = pair
        last = k == i
        i2 = jnp.minimum(jnp.where(last, i + 1, i), n_q - 1)
        return i2, jnp.where(last, 0, k + 1)

    def two_steps(t, carry):
        prev, cur = carry
        nxt = succ(cur)
        nxt2 = succ(nxt)

        def body(diag0):
            scores(nxt, 1)
            values(prev, 1)
            softmax(cur, 0, diag0)
            scores(nxt2, 0)
            values(cur, 0)
            softmax(nxt, 1, False)

        d0 = is_diag(cur)
        pl.when(d0)(lambda: body(True))
        pl.when(jnp.logical_not(d0))(lambda: body(False))
        return nxt, nxt2

    zero = jnp.int32(0)
    first = (zero, zero)
    scores(first, 0)
    last, _ = lax.fori_loop(0, n_pairs // 2, two_steps, (first, first))
    values(last, 1)

    def finish(t, c):
        for i in (2 * t, 2 * t + 1):
            for hh in heads:
                o = acc_sc[i, hh, :V_DIM] * (1.0 / acc_sc[i, hh, V_DIM:V_DIM + 1])
                o = o[:, :TB] - lam * o[:, TB:]
                ms = jnp.mean(o * o, axis=0, keepdims=True)
                o = o * lax.rsqrt(ms + LN_EPS) * (g_ref[...] * (1.0 - lam_init))
                rows = pl.ds(pl.multiple_of(i * TB, TB), TB)
                o_ref[0, rows, hh * LANES:(hh + 1) * LANES] = o.T.astype(o_ref.dtype)
        return c

    lax.fori_loop(0, n_q // 2, finish, 0)


def _diff_attn_flat(qt, k, vt, lq1, lk1, lq2, lk2, subln_g, lam_init):
    B, L, _ = k.shape
    n_q = L // TB
    vec = pl.BlockSpec((1, QK_DIM), lambda b: (0, 0))
    return pl.pallas_call(
        functools.partial(_attn_flat_kernel, lam_init=lam_init),
        grid=(B,),
        in_specs=[
            vec, vec, vec, vec,
            pl.BlockSpec((V_DIM, 1), lambda b: (0, 0)),
            pl.BlockSpec((1, HEADS, n_q, LANES, TB), lambda b: (b, 0, 0, 0, 0)),
            pl.BlockSpec((1, L, HEADS * LANES), lambda b: (b, 0, 0)),
            pl.BlockSpec((1, HEADS, n_q, VT_ROWS, TB), lambda b: (b, 0, 0, 0, 0)),
        ],
        out_specs=pl.BlockSpec((1, L, HEADS * LANES), lambda b: (b, 0, 0)),
        out_shape=jax.ShapeDtypeStruct((B, L, ATT_WIDTH), BF16),
        scratch_shapes=[
            pltpu.VMEM((n_q, HEADS, VT_ROWS, 2 * TB), F32),
            pltpu.VMEM((n_q, HEADS, LANES, 2 * TB), BF16),
            pltpu.VMEM((2, HEADS, TB, 2 * TB), F32),
            pltpu.VMEM((2, HEADS, TB, 2 * TB), BF16),
            pltpu.VMEM((2, HEADS, 1, 2 * TB), F32),
            pltpu.VMEM((n_q, HEADS, 1, 2 * TB), F32),
            pltpu.VMEM((2, HEADS, 1, 2 * TB), F32),
        ],
        compiler_params=_cparams(("parallel",)),
        name="diff_attn",
    )(lq1, lk1, lq2, lk2, subln_g.reshape(V_DIM, 1), qt, k, vt)


def _block_transpose8(vs):
    lane_blk = lax.broadcasted_iota(jnp.int32, vs[0].shape, 1) // SSM_GROUP
    for d in (4, 2, 1):
        keep = (lane_blk & d) == 0
        new = list(vs)
        for a in range(8):
            if a & d == 0:
                lo, hi = vs[a], vs[a + d]
                new[a] = jnp.where(keep, lo, pltpu.roll(hi, SSM_GROUP * d, 1))
                new[a + d] = jnp.where(keep, pltpu.roll(lo, LANES - SSM_GROUP * d, 1), hi)
        vs = new
    return vs


def _ssm_kernel(u_ref, toep_ref, win_ref, wout_ref, a_ref, d_ref, y_ref, s_sc, x_sc, sp_sc, *, jt):
    rows = jt * SUBLANES

    @pl.when(pl.program_id(1) == 0)
    def _():
        s_sc[...] = jnp.zeros(s_sc.shape, F32)

    ug = [u_ref[0, g].astype(BF16) for g in range(GROUPS_PER_TILE)]

    y = [jnp.dot(ug[g], toep_ref[0, g], preferred_element_type=F32) for g in range(GROUPS_PER_TILE)]

    for m in range(PAIRS_PER_TILE):
        up = jnp.concatenate([ug[2 * m], ug[2 * m + 1]], axis=1)
        xp = jnp.dot(up, win_ref[0, m], preferred_element_type=F32)
        x_sc[0, :, m * LANES:(m + 1) * LANES] = xp[:, :LANES]
        x_sc[1, :, m * LANES:(m + 1) * LANES] = xp[:, LANES:]

    ar = a_ref[0, 0:1, :]
    ai = a_ref[0, 1:2, :]

    def step(j, carry):
        sr, si = carry
        r8 = pl.ds(pl.multiple_of(j * SUBLANES, SUBLANES), SUBLANES)
        sp_sc[0, r8, :] = sr
        sp_sc[1, r8, :] = si
        return (ar * sr - ai * si + x_sc[0, r8, :], ar * si + ai * sr + x_sc[1, r8, :])

    sr, si = lax.fori_loop(0, jt, step, (s_sc[0], s_sc[1]))
    s_sc[0] = sr
    s_sc[1] = si

    for m in range(PAIRS_PER_TILE):
        sp = jnp.concatenate([sp_sc[0, :, m * LANES:(m + 1) * LANES],
                              sp_sc[1, :, m * LANES:(m + 1) * LANES]], axis=1).astype(BF16)
        yc = jnp.dot(sp, wout_ref[0, m], preferred_element_type=F32)
        y[2 * m] = y[2 * m] + yc[:, :CHUNK_COLS]
        y[2 * m + 1] = y[2 * m + 1] + yc[:, CHUNK_COLS:]

    for g in range(GROUPS_PER_TILE):
        y_ref[0, g] = y[g] + d_ref[0, g] * u_ref[0, g]


def _chunk_slabs(tiles):
    z_lo = _block_transpose8(tiles[:8])
    z_hi = _block_transpose8(tiles[8:])
    return [jnp.concatenate([z_lo[g], z_hi[g]], axis=1) for g in range(GROUPS_PER_TILE)]


def _chunk_tiles(slabs):
    out = []
    for half in range(2):
        out += _block_transpose8([s[:, half * LANES:(half + 1) * LANES] for s in slabs])
    return out


def _ssm(u_slab, tables, d_skip, B, L, lt=1024):
    toep, win, wout, a16 = tables
    jt = lt // CHUNK
    n_l = L // lt
    rows = jt * B
    slab = pl.BlockSpec((1, GROUPS_PER_TILE, rows, CHUNK_COLS), lambda q, l: (q, 0, l, 0))
    d_slab = jnp.tile(d_skip.reshape(LANE_TILES, GROUPS_PER_TILE, 1, SSM_GROUP), (1, 1, 1, CHUNK))
    return pl.pallas_call(
        functools.partial(_ssm_kernel, jt=jt),
        grid=(LANE_TILES, n_l),
        in_specs=[
            slab,
            pl.BlockSpec((1, GROUPS_PER_TILE, CHUNK_COLS, CHUNK_COLS), lambda q, l: (q, 0, 0, 0)),
            pl.BlockSpec((1, PAIRS_PER_TILE, 2 * CHUNK_COLS, 2 * LANES), lambda q, l: (q, 0, 0, 0)),
            pl.BlockSpec((1, PAIRS_PER_TILE, 2 * LANES, 2 * CHUNK_COLS), lambda q, l: (q, 0, 0, 0)),
            pl.BlockSpec((1, 2, STATE_COLS), lambda q, l: (q, 0, 0)),
            pl.BlockSpec((1, GROUPS_PER_TILE, 1, CHUNK_COLS), lambda q, l: (q, 0, 0, 0)),
        ],
        out_specs=slab,
        out_shape=jax.ShapeDtypeStruct(u_slab.shape, F32),
        scratch_shapes=[
            pltpu.VMEM((2, B, STATE_COLS), F32),
            pltpu.VMEM((2, rows, STATE_COLS), F32),
            pltpu.VMEM((2, rows, STATE_COLS), F32),
        ],
        compiler_params=_cparams(("parallel", "arbitrary")),
        name="ssm",
    )(u_slab, toep, win, wout, a16, d_slab)


def _cmul(a, b):
    return a[0] * b[0] - a[1] * b[1], a[0] * b[1] + a[1] * b[0]


def _zoh(lr, li, ldt):
    dt = jnp.exp(ldt)
    mag = jnp.exp(lr * dt)
    ab = (mag * jnp.cos(li * dt), mag * jnp.sin(li * dt))
    den = lr * lr + li * li
    zr, zi = ab[0] - 1.0, ab[1]
    return ab, ((zr * lr + zi * li) / den, (zi * lr - zr * li) / den)


def _cpow(base, exponent, bits):
    out = None
    for k in range(bits):
        on = (exponent & (1 << k)) != 0
        sel = (jnp.where(on, base[0], 1.0), jnp.where(on, base[1], 0.0))
        out = sel if out is None else _cmul(out, sel)
        if k + 1 < bits:
            base = _cmul(base, base)
    return out


def _ssm_prep_kernel(row_ref, ct_ref, bt_ref, *refs, n_weights):
    w_refs, (toep_ref, win_ref, wout_ref, a_ref), w_bf_refs = refs[:n_weights], refs[n_weights:n_weights + 4], \
        refs[n_weights + 4:]
    for w_ref, w_bf_ref in zip(w_refs, w_bf_refs):
        w_bf_ref[...] = w_ref[...].astype(BF16)
    bits = CHUNK.bit_length() - 1
    lane = lax.broadcasted_iota(jnp.int32, (1, CHUNK_COLS), 1)
    tau = lane // SSM_GROUP
    rowi = lax.broadcasted_iota(jnp.int32, (CHUNK_COLS, 1), 0)
    back = (CHUNK - 1) - rowi // SSM_GROUP
    lane2 = lax.broadcasted_iota(jnp.int32, (1, LANES), 1)
    exact = functools.partial(jnp.dot, precision=lax.Precision.HIGHEST, preferred_element_type=F32)
    h_row = lax.broadcasted_iota(jnp.int32, (SSM_GROUP, CHUNK_COLS), 0)
    repeat = jnp.where(lane % SSM_GROUP == h_row, 1.0, 0.0)
    zoh = [_zoh(row_ref[0, m], row_ref[1, m], row_ref[2, m]) for m in range(PAIRS_PER_TILE)]
    pad = jnp.zeros((SUBLANES - PAIRS_PER_TILE, LANES), F32)
    ab_cols = tuple(jnp.concatenate([z[0][c] for z in zoh] + [pad], axis=0).T for c in range(2))
    a_rows = []
    for m in range(PAIRS_PER_TILE):
        abp, fp = zoh[m]
        pwr = _cpow(abp, back, bits)
        a16 = abp
        for _ in range(bits):
            a16 = _cmul(a16, a16)
        a_rows.append(a16)
        wout_rows = [[None, None], [None, None]]
        for j in range(2):
            g = 2 * m + j
            states = slice(j * SSM_STATE, (j + 1) * SSM_STATE)
            ab = tuple(v[states, m:m + 1] for v in ab_cols)
            pw = _cpow(ab, tau, bits)
            c_rep = tuple(exact(ct_ref[c, g], repeat) for c in range(2))
            g0 = _cmul(pw, c_rep)
            g1 = _cmul(g0, ab)
            mine = (lane2 < SSM_STATE) if j == 0 else (lane2 >= SSM_STATE)
            bb2 = _cmul(fp, (bt_ref[0, g], bt_ref[1, g]))
            bb2 = tuple(jnp.where(mine, v, 0.0) for v in bb2)
            none = jnp.zeros_like(g0[0])
            stack = (lambda v: jnp.concatenate([v, none], axis=0)) if j == 0 else \
                    (lambda v: jnp.concatenate([none, v], axis=0))
            kt = exact(bb2[0], stack(g0[0])) - exact(bb2[1], stack(g0[1]))
            toep_ref[0, g] = jnp.concatenate(
                [jnp.where(lane >= SSM_GROUP * r, pltpu.roll(kt, SSM_GROUP * r, 1), 0.0) if r else kt
                 for r in range(CHUNK)], axis=0).astype(BF16)
            w = _cmul(pwr, tuple(jnp.concatenate([v] * CHUNK, axis=0) for v in bb2))
            win_ref[0, m, j * CHUNK_COLS:(j + 1) * CHUNK_COLS, :] = jnp.concatenate(w, axis=1).astype(BF16)
            wout_rows[0][j] = g1[0]
            wout_rows[1][j] = -g1[1]
        zero = jnp.zeros((SSM_STATE, CHUNK_COLS), F32)
        wout_ref[0, m] = jnp.concatenate([
            jnp.concatenate([wout_rows[0][0], zero], axis=1),
            jnp.concatenate([zero, wout_rows[0][1]], axis=1),
            jnp.concatenate([wout_rows[1][0], zero], axis=1),
            jnp.concatenate([zero, wout_rows[1][1]], axis=1)], axis=0).astype(BF16)
    a_ref[0] = jnp.concatenate([jnp.concatenate([a[0] for a in a_rows], axis=1),
                                jnp.concatenate([a[1] for a in a_rows], axis=1)], axis=0)


def _ssm_prep(lam_re, lam_im, log_dt, b_re, b_im, c_re, c_im, weights):
    G, P, H = b_re.shape
    w_specs = []
    for w in weights:
        rows = w.shape[0] // LANE_TILES
        assert rows * LANE_TILES == w.shape[0] and rows % 16 == 0
        w_specs.append(pl.BlockSpec((rows, w.shape[1]), lambda q: (q, 0)))
    ldt = jnp.broadcast_to(log_dt[:, None], (G, P))
    rows = jnp.stack([lam_re, lam_im, ldt]).reshape(3, G // 2, 1, 2 * P)
    ct = jnp.stack([c_re, c_im]).transpose(0, 1, 3, 2)
    bt = jnp.tile(jnp.stack([b_re, b_im]).transpose(0, 1, 3, 2), (1, 1, 1, 2))
    gt = GROUPS_PER_TILE
    return pl.pallas_call(
        functools.partial(_ssm_prep_kernel, n_weights=len(weights)),
        grid=(LANE_TILES,),
        in_specs=[
            pl.BlockSpec((3, PAIRS_PER_TILE, 1, 2 * P), lambda q: (0, q, 0, 0)),
            pl.BlockSpec((2, gt, P, H), lambda q: (0, q, 0, 0)),
            pl.BlockSpec((2, gt, H, 2 * P), lambda q: (0, q, 0, 0)),
            *w_specs,
        ],
        out_specs=[
            pl.BlockSpec((1, gt, CHUNK_COLS, CHUNK_COLS), lambda q: (q, 0, 0, 0)),
            pl.BlockSpec((1, PAIRS_PER_TILE, 2 * CHUNK_COLS, 2 * LANES), lambda q: (q, 0, 0, 0)),
            pl.BlockSpec((1, PAIRS_PER_TILE, 2 * LANES, 2 * CHUNK_COLS), lambda q: (q, 0, 0, 0)),
            pl.BlockSpec((1, 2, STATE_COLS), lambda q: (q, 0, 0)),
            *w_specs,
        ],
        out_shape=[
            jax.ShapeDtypeStruct((LANE_TILES, gt, CHUNK_COLS, CHUNK_COLS), BF16),
            jax.ShapeDtypeStruct((LANE_TILES, PAIRS_PER_TILE, 2 * CHUNK_COLS, 2 * LANES), BF16),
            jax.ShapeDtypeStruct((LANE_TILES, PAIRS_PER_TILE, 2 * LANES, 2 * CHUNK_COLS), BF16),
            jax.ShapeDtypeStruct((LANE_TILES, 2, STATE_COLS), F32),
            *[jax.ShapeDtypeStruct(w.shape, BF16) for w in weights],
        ],
        compiler_params=_cparams(("parallel",)),
        name="ssm_prep",
    )(rows, ct, bt, *weights)


def _layer_norm(r, g, b):
    mu = jnp.mean(r, axis=-1, keepdims=True)
    c = r - mu
    var = jnp.mean(c * c, axis=-1, keepdims=True)
    return c * lax.rsqrt(var + LN_EPS) * g + b


MIX_ROWS = 256


def _gelu_tanh(y):
    return 0.5 * y * (1.0 + jnp.tanh(math.sqrt(2.0 / math.pi) * (y + 0.044715 * (y * y * y))))


def _mix_ln1_kernel(x_ref, attn_ref, y_ref, wglu_ref, bglu_ref, wo_ref, g_ref, b_ref, o_ref, y_sc):
    nb, tl, d = x_ref.shape
    pos = MIX_ROWS // nb
    n_pieces = tl // pos
    jp = pos // CHUNK

    def relayout(n):
        rows = slice(n * jp * nb, (n + 1) * jp * nb)
        for q in range(LANE_TILES):
            tiles = _chunk_tiles([y_ref[q, g, rows] for g in range(GROUPS_PER_TILE)])
            y_sc[q, n * MIX_ROWS:(n + 1) * MIX_ROWS] = jnp.stack(
                [t.reshape(jp, nb, LANES) for t in tiles], axis=1).reshape(MIX_ROWS, LANES)

    def glu(n):
        y = jnp.concatenate(
            [jnp.concatenate([y_sc[q, pl.ds(n * MIX_ROWS + b, pos, stride=nb), :] for q in range(LANE_TILES)], axis=1)
             for b in range(nb)], axis=0)
        gl = _gelu_tanh(y)
        z = gl * jax.nn.sigmoid(jnp.dot(gl.astype(BF16), wglu_ref[...], preferred_element_type=F32) + bglu_ref[...])
        return z.astype(BF16)

    def project(n, z):
        ls = slice(n * pos, (n + 1) * pos)
        mix = (jnp.dot(attn_ref[:, ls].reshape(MIX_ROWS, ATT_WIDTH), wo_ref[:ATT_WIDTH], preferred_element_type=F32)
               + jnp.dot(z, wo_ref[ATT_WIDTH:], preferred_element_type=F32))
        res = _layer_norm(ALPHA * x_ref[:, ls].reshape(MIX_ROWS, d) + mix, g_ref[...], b_ref[...])
        o_ref[:, ls] = res.reshape(nb, pos, d)

    relayout(0)
    z = glu(0)
    for n in range(n_pieces):
        if n + 1 < n_pieces:
            relayout(n + 1)
            z_next = glu(n + 1)
        project(n, z)
        z = z_next


def _mix_ln1(x, attn, y_tm, wglu_bf, b_glu, wo_bf, ln_g, ln_b, tl=256):
    B, L, D = x.shape
    assert MIX_ROWS % (B * CHUNK) == 0 and tl % (MIX_ROWS // B) == 0
    resident = lambda shape: pl.BlockSpec(shape, lambda l: (0, 0), pipeline_mode=pl.Buffered(1))
    return pl.pallas_call(
        _mix_ln1_kernel,
        grid=(L // tl,),
        in_specs=[
            pl.BlockSpec((B, tl, D), lambda l: (0, l, 0)),
            pl.BlockSpec((B, tl, ATT_WIDTH), lambda l: (0, l, 0)),
            pl.BlockSpec((LANE_TILES, GROUPS_PER_TILE, tl // CHUNK * B, CHUNK_COLS), lambda l: (0, 0, l, 0)),
            resident((SSM_WIDTH, SSM_WIDTH)),
            resident((1, SSM_WIDTH)),
            resident((D, D)),
            resident((1, D)),
            resident((1, D)),
        ],
        out_specs=pl.BlockSpec((B, tl, D), lambda l: (0, l, 0)),
        out_shape=jax.ShapeDtypeStruct((B, L, D), F32),
        scratch_shapes=[pltpu.VMEM((LANE_TILES, tl * B, LANES), F32)],
        compiler_params=_cparams(("parallel",)),
        name="mix_ln1",
    )(x, attn, y_tm, wglu_bf, b_glu, wo_bf, ln_g, ln_b)


HALO = 16
FFN_SUB = 256
FFN_DOWN_PARTS = 2
FFN_ROWS = 128


def _ffn_kernel(x_ref, xh_ref, p_ref, wple_ref, wgate_ref, wup_ref, cw_ref, cb_ref, wd_ref, g_ref, b_ref, o_ref,
                x32_sc, xh32_sc, p32_sc, xb_sc, acc_sc, hg_sc, hv_sc, act_sc, *, tl):
    nb = x_ref.shape[0]
    tm = tl * nb
    d_slabs = x_ref.shape[2] // LANES
    hp = (CONV_WIDTH - 1)
    for bb in range(nb):
        for s in range(d_slabs):
            lanes = slice(s * LANES, (s + 1) * LANES)
            x32_sc[s, pl.ds(bb, tl, stride=nb), :] = x_ref[bb, :, lanes]
            xh32_sc[s, pl.ds(bb, hp, stride=nb), :] = xh_ref[bb, SUBLANES - hp:, lanes]
        for s in range(p_ref.shape[2] // LANES):
            p32_sc[s, pl.ds(bb, tl, stride=nb), :] = p_ref[bb, :, s * LANES:(s + 1) * LANES]
    x_tm = jnp.concatenate([x32_sc[s] for s in range(d_slabs)], axis=1)
    p_tm = jnp.concatenate([p32_sc[s] for s in range(p32_sc.shape[0])], axis=1)
    keep = jnp.where(pl.program_id(0) == 0, 0.0, 1.0)
    xb_sc[0:HALO] = (jnp.concatenate([xh32_sc[s] for s in range(d_slabs)], axis=1) * keep).astype(BF16)
    xb_sc[HALO:] = x_tm.astype(BF16)
    gate = jax.nn.sigmoid(jnp.dot(xb_sc[HALO:], wgate_ref[...], preferred_element_type=F32))
    ple = jnp.dot(p_tm.astype(BF16), wple_ref[...], preferred_element_type=F32) * gate
    acc_sc[...] = ALPHA * x_tm + ple

    def conv(h_sc, slot, cols, r0, rows):
        out = cb_ref[:, cols] + cw_ref[CONV_WIDTH - 1:CONV_WIDTH, cols] * h_sc[slot, HALO + r0:HALO + r0 + rows]
        for j in range(CONV_WIDTH - 1):
            back = (CONV_WIDTH - 1 - j) * nb
            out = out + cw_ref[j:j + 1, cols] * h_sc[slot, HALO + r0 - back:HALO + r0 - back + rows]
        return out

    n_sub = D_FF // FFN_SUB
    g_cols = lambda n: slice(n * FFN_SUB, (n + 1) * FFN_SUB)
    v_cols = lambda n: slice(D_FF + n * FFN_SUB, D_FF + (n + 1) * FFN_SUB)

    def up(n):
        xb = xb_sc[...]
        hg_sc[n % 2] = jnp.dot(xb, wup_ref[:, g_cols(n)], preferred_element_type=F32)
        hv_sc[n % 2] = jnp.dot(xb, wup_ref[:, v_cols(n)], preferred_element_type=F32)

    def gate_block(n):
        for r0 in range(0, tm, FFN_ROWS):
            gc = conv(hg_sc, n % 2, g_cols(n), r0, FFN_ROWS)
            vc = conv(hv_sc, n % 2, v_cols(n), r0, FFN_ROWS)
            act_sc[r0:r0 + FFN_ROWS, g_cols(n)] = (gc * jax.nn.sigmoid(gc) * vc).astype(BF16)

    def down(lo, hi):
        cols = slice(lo * FFN_SUB, hi * FFN_SUB)
        return jnp.dot(act_sc[:, cols], wd_ref[cols, :], preferred_element_type=F32)

    cuts = [n_sub * k // FFN_DOWN_PARTS for k in range(FFN_DOWN_PARTS + 1)]
    up(0)
    for n in range(n_sub):
        if n + 1 < n_sub:
            up(n + 1)
        gate_block(n)
        if n + 1 in cuts[1:-1]:
            lo = cuts[cuts.index(n + 1) - 1]
            acc_sc[...] += down(lo, n + 1)

    res = _layer_norm(acc_sc[...] + down(cuts[-2], n_sub), g_ref[...], b_ref[...])
    for s in range(d_slabs):
        x32_sc[s] = res[:, s * LANES:(s + 1) * LANES]
    for bb in range(nb):
        for s in range(d_slabs):
            o_ref[bb, :, s * LANES:(s + 1) * LANES] = x32_sc[s, pl.ds(bb, tl, stride=nb), :]


def _ffn_ln2(x1, p, wple_bf, wgate_bf, wup_bf, conv_w, conv_b, wdown_bf, ln_g, ln_b, tl=64):
    B, L, D = x1.shape
    tm = tl * B
    assert D_FF % FFN_SUB == 0 and HALO == (CONV_WIDTH - 1) * B and tm % FFN_ROWS == 0
    const = lambda i: (0, 0)
    resident = lambda shape: pl.BlockSpec(shape, const, pipeline_mode=pl.Buffered(1))
    return pl.pallas_call(
        functools.partial(_ffn_kernel, tl=tl),
        grid=(L // tl,),
        in_specs=[
            pl.BlockSpec((B, tl, D), lambda i: (0, i, 0)),
            pl.BlockSpec((B, SUBLANES, D), lambda i: (0, jnp.maximum(i * (tl // SUBLANES) - 1, 0), 0)),
            pl.BlockSpec((B, tl, PLE_DIM), lambda i: (0, i, 0)),
            resident((PLE_DIM, D)),
            resident((D, D)),
            resident((D, 2 * D_FF)),
            resident((CONV_WIDTH, 2 * D_FF)),
            resident((1, 2 * D_FF)),
            resident((D_FF, D)),
            resident((1, D)),
            resident((1, D)),
        ],
        out_specs=pl.BlockSpec((B, tl, D), lambda i: (0, i, 0)),
        out_shape=jax.ShapeDtypeStruct((B, L, D), F32),
        scratch_shapes=[
            pltpu.VMEM((D // LANES, tm, LANES), F32),
            pltpu.VMEM((D // LANES, HALO, LANES), F32),
            pltpu.VMEM((PLE_DIM // LANES, tm, LANES), F32),
            pltpu.VMEM((tm + HALO, D), BF16),
            pltpu.VMEM((tm, D), F32),
            pltpu.VMEM((2, tm + HALO, FFN_SUB), F32),
            pltpu.VMEM((2, tm + HALO, FFN_SUB), F32),
            pltpu.VMEM((tm, D_FF), BF16),
        ],
        compiler_params=_cparams(("parallel",)),
        name="ffn_ln2",
    )(x1, x1, p, wple_bf, wgate_bf, wup_bf, conv_w, conv_b, wdown_bf, ln_g, ln_b)


def kernel(x, p, w_in, diff_lambda_q1, diff_lambda_k1, diff_lambda_q2, diff_lambda_k2, diff_subln_g, ssm_lambda_re, ssm_lambda_im, ssm_log_dt, ssm_b_re, ssm_b_im, ssm_c_re, ssm_c_im, ssm_d, ssm_w_glu, ssm_b_glu, w_o, ln1_g, ln1_b, ffn_w_up, ffn_conv_w, ffn_conv_b, ffn_w_down, w_ple, w_ple_gate, ln2_g, ln2_b):
    B, L, D = x.shape
    assert B == SUBLANES and D == D_MODEL and L % 512 == 0
    for i in range(DEPTH):
        lam_init = 0.8 - 0.6 * math.exp(-0.3 * i)
        row = lambda a: a[i].reshape(1, -1)
        dense = (w_in[i], ssm_w_glu[i], w_o[i], w_ple[i], w_ple_gate[i], ffn_w_up[i], ffn_w_down[i])
        *tables, w_in_bf, w_glu_bf, w_o_bf, w_ple_bf, w_gate_bf, w_up_bf, w_down_bf = _ssm_prep(
            ssm_lambda_re[i], ssm_lambda_im[i], ssm_log_dt[i], ssm_b_re[i], ssm_b_im[i], ssm_c_re[i], ssm_c_im[i],
            dense)
        qt, k, vt, u_tm = _in_proj(x, w_in_bf)
        attn = _diff_attn_flat(qt, k, vt, row(diff_lambda_q1), row(diff_lambda_k1), row(diff_lambda_q2),
                          row(diff_lambda_k2), diff_subln_g[i], lam_init)
        y_tm = _ssm(u_tm, tables[:4], ssm_d[i], B, L)
        x1 = _mix_ln1(x, attn, y_tm, w_glu_bf, row(ssm_b_glu), w_o_bf, row(ln1_g), row(ln1_b))
        x = _ffn_ln2(x1, p[i], w_ple_bf, w_gate_bf, w_up_bf, ffn_conv_w[i], row(ffn_conv_b), w_down_bf,
                     row(ln2_g), row(ln2_b))
    return x
```

```python
import functools
import math

import jax
import jax.numpy as jnp
from jax import lax
from jax.experimental import pallas as pl
from jax.experimental.pallas import tpu as pltpu

F32 = jnp.float32
BF16 = jnp.bfloat16

SUBLANES = 8
LANES = 128

DEPTH = 1
D_MODEL = 1024
PLE_DIM = 256
HEADS = 4
QK_DIM = 64
V_DIM = 2 * QK_DIM
QK_WIDTH = HEADS * 2 * QK_DIM
ATT_WIDTH = HEADS * V_DIM
SSM_WIDTH = D_MODEL - ATT_WIDTH
SSM_GROUP = 16
SSM_GROUPS = SSM_WIDTH // SSM_GROUP
SSM_STATE = 64
D_FF = 2816
CONV_WIDTH = 3
LN_EPS = 1e-5
ALPHA = (2 * DEPTH) ** 0.25
QK_SCALE = QK_DIM ** -0.5 * math.log2(math.e)

CHUNK = 16
CHUNK_COLS = CHUNK * SSM_GROUP
GROUPS_PER_TILE = LANES // SSM_GROUP
LANE_TILES = SSM_WIDTH // LANES
PAIRS_PER_TILE = GROUPS_PER_TILE // 2
STATE_COLS = GROUPS_PER_TILE * SSM_STATE

VMEM_LIMIT = 56 * 1024 * 1024


def _cparams(sem):
    return pltpu.CompilerParams(dimension_semantics=sem, vmem_limit_bytes=VMEM_LIMIT)


TB = 256
VT_ROWS = V_DIM + 16


def _in_proj_kernel(x_ref, w_ref, qt_ref, k_ref, vt_ref, u_ref, u_sc):
    wq_ref = w_ref.at[:, :QK_WIDTH]
    wk_ref = w_ref.at[:, QK_WIDTH:2 * QK_WIDTH]
    wv_ref = w_ref.at[:, 2 * QK_WIDTH:2 * QK_WIDTH + ATT_WIDTH]
    wu_ref = w_ref.at[:, 2 * QK_WIDTH + ATT_WIDTH:]
    nb, tl, d = x_ref.shape
    jt = tl // CHUNK
    xb = x_ref[...].reshape(nb * tl, d).astype(BF16)
    hu = jnp.dot(xb, wu_ref[...], preferred_element_type=F32)
    for b in range(nb):
        for q in range(LANE_TILES):
            u_sc[q, pl.ds(b, tl, stride=nb), :] = hu[b * tl:(b + 1) * tl, q * LANES:(q + 1) * LANES]
    for q in range(LANE_TILES):
        tiles = u_sc[q].reshape(jt, CHUNK, nb, LANES)
        slabs = _chunk_slabs([tiles[:, r].reshape(jt * nb, LANES) for r in range(CHUNK)])
        for g in range(GROUPS_PER_TILE):
            u_ref[q, g] = slabs[g]
    tn = (((0,), (1,)), ((), ()))
    qt = lax.dot_general(wq_ref[...], xb, tn, preferred_element_type=F32) * QK_SCALE
    vt = lax.dot_general(wv_ref[...], xb, tn, preferred_element_type=F32)
    pad_row = lax.broadcasted_iota(jnp.int32, (VT_ROWS - V_DIM, TB), 0)
    ones_row = jnp.where(pad_row == 0, 1.0, 0.0).astype(BF16)
    for b in range(nb):
        for h in range(HEADS):
            for c in range(tl // TB):
                cols = slice(b * tl + c * TB, b * tl + (c + 1) * TB)
                qt_ref[b, h, c] = qt[h * LANES:(h + 1) * LANES, cols].astype(BF16)
                vt_ref[b, h, c, :V_DIM] = vt[h * LANES:(h + 1) * LANES, cols].astype(BF16)
                vt_ref[b, h, c, V_DIM:] = ones_row
    k_ref[...] = jnp.dot(xb, wk_ref[...], preferred_element_type=F32).astype(BF16).reshape(k_ref.shape)


def _in_proj(x, w_in, tl=256):
    B, L, D = x.shape
    slab = lambda rows: pl.BlockSpec((B, HEADS, tl // TB, rows, TB), lambda l: (0, 0, l, 0, 0))
    slab_shape = lambda rows: jax.ShapeDtypeStruct((B, HEADS, L // TB, rows, TB), BF16)
    return pl.pallas_call(
        _in_proj_kernel,
        grid=(L // tl,),
        in_specs=[
            pl.BlockSpec((B, tl, D), lambda l: (0, l, 0)),
            pl.BlockSpec(w_in.shape, lambda l: (0, 0), pipeline_mode=pl.Buffered(1)),
        ],
        out_specs=[
            slab(LANES),
            pl.BlockSpec((B, tl, QK_WIDTH), lambda l: (0, l, 0)),
            slab(VT_ROWS),
            pl.BlockSpec((LANE_TILES, GROUPS_PER_TILE, tl // CHUNK * B, CHUNK_COLS), lambda l: (0, 0, l, 0)),
        ],
        out_shape=[
            slab_shape(LANES),
            jax.ShapeDtypeStruct((B, L, QK_WIDTH), BF16),
            slab_shape(VT_ROWS),
            jax.ShapeDtypeStruct((LANE_TILES, GROUPS_PER_TILE, L // CHUNK * B, CHUNK_COLS), F32),
        ],
        scratch_shapes=[pltpu.VMEM((LANE_TILES, tl * B, LANES), F32)],
        compiler_params=_cparams(("parallel",)),
        name="in_proj",
    )(x, w_in)


def _attn_flat_kernel(lq1_ref, lk1_ref, lq2_ref, lk2_ref, g_ref, qt_ref, k_ref, vt_ref, o_ref,
                      acc_sc, qst_sc, s_sc, p_sc, bm_sc, st_sc, al_sc, *, lam_init):
    n_q = qt_ref.shape[2]
    n_pairs = n_q * (n_q + 1) // 2
    assert TB == 2 * LANES and n_pairs % 2 == 0
    heads = range(HEADS)
    n_strips = 2 * TB // LANES
    lam = (jnp.exp(jnp.sum(lq1_ref[...] * lk1_ref[...], axis=-1, keepdims=True))
           - jnp.exp(jnp.sum(lq2_ref[...] * lk2_ref[...], axis=-1, keepdims=True)) + lam_init)

    def prepare(i, c):
        for hh in heads:
            qt = qt_ref[0, hh, i]
            row = lax.broadcasted_iota(jnp.int32, qt.shape, 0)
            zero = jnp.zeros_like(qt)
            qst_sc[i, hh, :, :TB] = jnp.where(row < QK_DIM, qt, zero)
            qst_sc[i, hh, :, TB:] = jnp.where(row >= QK_DIM, qt, zero)
            acc_sc[i, hh] = jnp.zeros(acc_sc.shape[2:], F32)
            st_sc[i, hh] = jnp.full((1, 2 * TB), -jnp.inf, F32)
        return c

    lax.fori_loop(0, n_q, prepare, 0)
    p_sc[1] = jnp.zeros(p_sc.shape[1:], BF16)
    al_sc[1] = jnp.ones(al_sc.shape[1:], F32)

    def diag_place(i):
        r = i % 4
        return jnp.where(jnp.logical_or(r == 1, r == 2), 1, 0)

    def is_diag(pair):
        return pair[1] == diag_place(pair[0])

    def block_of(pair):
        i, k = pair
        d = diag_place(i)
        return jnp.where(k == d, i, jnp.where(k < d, k, k - 1))

    def scores(pair, slot):
        i, j = pair[0], block_of(pair)
        rows = pl.ds(pl.multiple_of(j * TB, TB), TB)
        for hh in heads:
            kb = k_ref[0, rows, hh * LANES:(hh + 1) * LANES]
            s = jnp.dot(kb, qst_sc[i, hh], preferred_element_type=F32)
            s_sc[slot, hh] = s
            bm_sc[slot, hh] = jnp.max(s, axis=0, keepdims=True)

    def values(pair, slot):
        i, j = pair[0], block_of(pair)
        for hh in heads:
            pv = jnp.dot(vt_ref[0, hh, j], p_sc[slot, hh], preferred_element_type=F32)
            acc_sc[i, hh] = al_sc[slot, hh] * acc_sc[i, hh] + pv

    def softmax(pair, slot, diagonal):
        i, _ = pair
        for hh in heads:
            for c in range(n_strips):
                lanes = slice(c * LANES, (c + 1) * LANES)
                m = st_sc[i, hh, :, lanes]
                if not diagonal:
                    mc = jnp.maximum(m, bm_sc[slot, hh, :, lanes])
                    p = jnp.exp2(s_sc[slot, hh, :, lanes] - mc)
                else:
                    s = s_sc[slot, hh, :, lanes]
                    key = lax.broadcasted_iota(jnp.int32, s.shape, 0)
                    qry = lax.broadcasted_iota(jnp.int32, s.shape, 1) + (c * LANES) % TB
                    s = jnp.where(key <= qry, s, -jnp.inf)
                    mc = jnp.maximum(m, jnp.max(s, axis=0, keepdims=True))
                    p = jnp.exp2(s - mc)
                p_sc[slot, hh, :, lanes] = p.astype(BF16)
                st_sc[i, hh, :, lanes] = mc
                al_sc[slot, hh, :, lanes] = jnp.exp2(m - mc)

    def succ(pair):
        i, k = pair
        last = k == i
        i2 = jnp.minimum(jnp.where(last, i + 1, i), n_q - 1)
        return i2, jnp.where(last, 0, k + 1)

    def two_steps(t, carry):
        prev, cur = carry
        nxt = succ(cur)
        nxt2 = succ(nxt)

        def body(diag0):
            scores(nxt, 1)
            values(prev, 1)
            softmax(cur, 0, diag0)
            scores(nxt2, 0)
            values(cur, 0)
            softmax(nxt, 1, False)

        d0 = is_diag(cur)
        pl.when(d0)(lambda: body(True))
        pl.when(jnp.logical_not(d0))(lambda: body(False))
        return nxt, nxt2

    zero = jnp.int32(0)
    first = (zero, zero)
    scores(first, 0)
    last, _ = lax.fori_loop(0, n_pairs // 2, two_steps, (first, first))
    values(last, 1)

    def finish(t, c):
        for i in (2 * t, 2 * t + 1):
            for hh in heads:
                o = acc_sc[i, hh, :V_DIM] * (1.0 / acc_sc[i, hh, V_DIM:V_DIM + 1])
                o = o[:, :TB] - lam * o[:, TB:]
                ms = jnp.mean(o * o, axis=0, keepdims=True)
                o = o * lax.rsqrt(ms + LN_EPS) * (g_ref[...] * (1.0 - lam_init))
                rows = pl.ds(pl.multiple_of(i * TB, TB), TB)
                o_ref[0, rows, hh * LANES:(hh + 1) * LANES] = o.T.astype(o_ref.dtype)
        return c

    lax.fori_loop(0, n_q // 2, finish, 0)


def _diff_attn_flat(qt, k, vt, lq1, lk1, lq2, lk2, subln_g, lam_init):
    B, L, _ = k.shape
    n_q = L // TB
    vec = pl.BlockSpec((1, QK_DIM), lambda b: (0, 0))
    return pl.pallas_call(
        functools.partial(_attn_flat_kernel, lam_init=lam_init),
        grid=(B,),
        in_specs=[
            vec, vec, vec, vec,
            pl.BlockSpec((V_DIM, 1), lambda b: (0, 0)),
            pl.BlockSpec((1, HEADS, n_q, LANES, TB), lambda b: (b, 0, 0, 0, 0)),
            pl.BlockSpec((1, L, HEADS * LANES), lambda b: (b, 0, 0)),
            pl.BlockSpec((1, HEADS, n_q, VT_ROWS, TB), lambda b: (b, 0, 0, 0, 0)),
        ],
        out_specs=pl.BlockSpec((1, L, HEADS * LANES), lambda b: (b, 0, 0)),
        out_shape=jax.ShapeDtypeStruct((B, L, ATT_WIDTH), BF16),
        scratch_shapes=[
            pltpu.VMEM((n_q, HEADS, VT_ROWS, 2 * TB), F32),
            pltpu.VMEM((n_q, HEADS, LANES, 2 * TB), BF16),
            pltpu.VMEM((2, HEADS, TB, 2 * TB), F32),
            pltpu.VMEM((2, HEADS, TB, 2 * TB), BF16),
            pltpu.VMEM((2, HEADS, 1, 2 * TB), F32),
            pltpu.VMEM((n_q, HEADS, 1, 2 * TB), F32),
            pltpu.VMEM((2, HEADS, 1, 2 * TB), F32),
        ],
        compiler_params=_cparams(("parallel",)),
        name="diff_attn",
    )(lq1, lk1, lq2, lk2, subln_g.reshape(V_DIM, 1), qt, k, vt)


def _block_transpose8(vs):
    lane_blk = lax.broadcasted_iota(jnp.int32, vs[0].shape, 1) // SSM_GROUP
    for d in (4, 2, 1):
        keep = (lane_blk & d) == 0
        new = list(vs)
        for a in range(8):
            if a & d == 0:
                lo, hi = vs[a], vs[a + d]
                new[a] = jnp.where(keep, lo, pltpu.roll(hi, SSM_GROUP * d, 1))
                new[a + d] = jnp.where(keep, pltpu.roll(lo, LANES - SSM_GROUP * d, 1), hi)
        vs = new
    return vs


def _ssm_kernel(u_ref, toep_ref, win_ref, wout_ref, a_ref, d_ref, y_ref, s_sc, x_sc, sp_sc, *, jt):
    rows = jt * SUBLANES

    @pl.when(pl.program_id(1) == 0)
    def _():
        s_sc[...] = jnp.zeros(s_sc.shape, F32)

    ug = [u_ref[0, g].astype(BF16) for g in range(GROUPS_PER_TILE)]

    y = [jnp.dot(ug[g], toep_ref[0, g], preferred_element_type=F32) for g in range(GROUPS_PER_TILE)]

    for m in range(PAIRS_PER_TILE):
        up = jnp.concatenate([ug[2 * m], ug[2 * m + 1]], axis=1)
        xp = jnp.dot(up, win_ref[0, m], preferred_element_type=F32)
        x_sc[0, :, m * LANES:(m + 1) * LANES] = xp[:, :LANES]
        x_sc[1, :, m * LANES:(m + 1) * LANES] = xp[:, LANES:]

    ar = a_ref[0, 0:1, :]
    ai = a_ref[0, 1:2, :]

    def step(j, carry):
        sr, si = carry
        r8 = pl.ds(pl.multiple_of(j * SUBLANES, SUBLANES), SUBLANES)
        sp_sc[0, r8, :] = sr
        sp_sc[1, r8, :] = si
        return (ar * sr - ai * si + x_sc[0, r8, :], ar * si + ai * sr + x_sc[1, r8, :])

    sr, si = lax.fori_loop(0, jt, step, (s_sc[0], s_sc[1]))
    s_sc[0] = sr
    s_sc[1] = si

    for m in range(PAIRS_PER_TILE):
        sp = jnp.concatenate([sp_sc[0, :, m * LANES:(m + 1) * LANES],
                              sp_sc[1, :, m * LANES:(m + 1) * LANES]], axis=1).astype(BF16)
        yc = jnp.dot(sp, wout_ref[0, m], preferred_element_type=F32)
        y[2 * m] = y[2 * m] + yc[:, :CHUNK_COLS]
        y[2 * m + 1] = y[2 * m + 1] + yc[:, CHUNK_COLS:]

    for g in range(GROUPS_PER_TILE):
        y_ref[0, g] = y[g] + d_ref[0, g] * u_ref[0, g]


def _chunk_slabs(tiles):
    z_lo = _block_transpose8(tiles[:8])
    z_hi = _block_transpose8(tiles[8:])
    return [jnp.concatenate([z_lo[g], z_hi[g]], axis=1) for g in range(GROUPS_PER_TILE)]


def _chunk_tiles(slabs):
    out = []
    for half in range(2):
        out += _block_transpose8([s[:, half * LANES:(half + 1) * LANES] for s in slabs])
    return out


def _ssm(u_slab, tables, d_skip, B, L, lt=1024):
    toep, win, wout, a16 = tables
    jt = lt // CHUNK
    n_l = L // lt
    rows = jt * B
    slab = pl.BlockSpec((1, GROUPS_PER_TILE, rows, CHUNK_COLS), lambda q, l: (q, 0, l, 0))
    d_slab = jnp.tile(d_skip.reshape(LANE_TILES, GROUPS_PER_TILE, 1, SSM_GROUP), (1, 1, 1, CHUNK))
    return pl.pallas_call(
        functools.partial(_ssm_kernel, jt=jt),
        grid=(LANE_TILES, n_l),
        in_specs=[
            slab,
            pl.BlockSpec((1, GROUPS_PER_TILE, CHUNK_COLS, CHUNK_COLS), lambda q, l: (q, 0, 0, 0)),
            pl.BlockSpec((1, PAIRS_PER_TILE, 2 * CHUNK_COLS, 2 * LANES), lambda q, l: (q, 0, 0, 0)),
            pl.BlockSpec((1, PAIRS_PER_TILE, 2 * LANES, 2 * CHUNK_COLS), lambda q, l: (q, 0, 0, 0)),
            pl.BlockSpec((1, 2, STATE_COLS), lambda q, l: (q, 0, 0)),
            pl.BlockSpec((1, GROUPS_PER_TILE, 1, CHUNK_COLS), lambda q, l: (q, 0, 0, 0)),
        ],
        out_specs=slab,
        out_shape=jax.ShapeDtypeStruct(u_slab.shape, F32),
        scratch_shapes=[
            pltpu.VMEM((2, B, STATE_COLS), F32),
            pltpu.VMEM((2, rows, STATE_COLS), F32),
            pltpu.VMEM((2, rows, STATE_COLS), F32),
        ],
        compiler_params=_cparams(("parallel", "arbitrary")),
        name="ssm",
    )(u_slab, toep, win, wout, a16, d_slab)


def _cmul(a, b):
    return a[0] * b[0] - a[1] * b[1], a[0] * b[1] + a[1] * b[0]


def _zoh(lr, li, ldt):
    dt = jnp.exp(ldt)
    mag = jnp.exp(lr * dt)
    ab = (mag * jnp.cos(li * dt), mag * jnp.sin(li * dt))
    den = lr * lr + li * li
    zr, zi = ab[0] - 1.0, ab[1]
    return ab, ((zr * lr + zi * li) / den, (zi * lr - zr * li) / den)


def _cpow(base, exponent, bits):
    out = None
    for k in range(bits):
        on = (exponent & (1 << k)) != 0
        sel = (jnp.where(on, base[0], 1.0), jnp.where(on, base[1], 0.0))
        out = sel if out is None else _cmul(out, sel)
        if k + 1 < bits:
            base = _cmul(base, base)
    return out


def _ssm_prep_kernel(row_ref, ct_ref, bt_ref, *refs, n_weights):
    w_refs, (toep_ref, win_ref, wout_ref, a_ref), w_bf_refs = refs[:n_weights], refs[n_weights:n_weights + 4], \
        refs[n_weights + 4:]
    for w_ref, w_bf_ref in zip(w_refs, w_bf_refs):
        w_bf_ref[...] = w_ref[...].astype(BF16)
    bits = CHUNK.bit_length() - 1
    lane = lax.broadcasted_iota(jnp.int32, (1, CHUNK_COLS), 1)
    tau = lane // SSM_GROUP
    rowi = lax.broadcasted_iota(jnp.int32, (CHUNK_COLS, 1), 0)
    back = (CHUNK - 1) - rowi // SSM_GROUP
    lane2 = lax.broadcasted_iota(jnp.int32, (1, LANES), 1)
    exact = functools.partial(jnp.dot, precision=lax.Precision.HIGHEST, preferred_element_type=F32)
    h_row = lax.broadcasted_iota(jnp.int32, (SSM_GROUP, CHUNK_COLS), 0)
    repeat = jnp.where(lane % SSM_GROUP == h_row, 1.0, 0.0)
    zoh = [_zoh(row_ref[0, m], row_ref[1, m], row_ref[2, m]) for m in range(PAIRS_PER_TILE)]
    pad = jnp.zeros((SUBLANES - PAIRS_PER_TILE, LANES), F32)
    ab_cols = tuple(jnp.concatenate([z[0][c] for z in zoh] + [pad], axis=0).T for c in range(2))
    a_rows = []
    for m in range(PAIRS_PER_TILE):
        abp, fp = zoh[m]
        pwr = _cpow(abp, back, bits)
        a16 = abp
        for _ in range(bits):
            a16 = _cmul(a16, a16)
        a_rows.append(a16)
        wout_rows = [[None, None], [None, None]]
        for j in range(2):
            g = 2 * m + j
            states = slice(j * SSM_STATE, (j + 1) * SSM_STATE)
            ab = tuple(v[states, m:m + 1] for v in ab_cols)
            pw = _cpow(ab, tau, bits)
            c_rep = tuple(exact(ct_ref[c, g], repeat) for c in range(2))
            g0 = _cmul(pw, c_rep)
            g1 = _cmul(g0, ab)
            mine = (lane2 < SSM_STATE) if j == 0 else (lane2 >= SSM_STATE)
            bb2 = _cmul(fp, (bt_ref[0, g], bt_ref[1, g]))
            bb2 = tuple(jnp.where(mine, v, 0.0) for v in bb2)
            none = jnp.zeros_like(g0[0])
            stack = (lambda v: jnp.concatenate([v, none], axis=0)) if j == 0 else \
                    (lambda v: jnp.concatenate([none, v], axis=0))
            kt = exact(bb2[0], stack(g0[0])) - exact(bb2[1], stack(g0[1]))
            toep_ref[0, g] = jnp.concatenate(
                [jnp.where(lane >= SSM_GROUP * r, pltpu.roll(kt, SSM_GROUP * r, 1), 0.0) if r else kt
                 for r in range(CHUNK)], axis=0).astype(BF16)
            w = _cmul(pwr, tuple(jnp.concatenate([v] * CHUNK, axis=0) for v in bb2))
            win_ref[0, m, j * CHUNK_COLS:(j + 1) * CHUNK_COLS, :] = jnp.concatenate(w, axis=1).astype(BF16)
            wout_rows[0][j] = g1[0]
            wout_rows[1][j] = -g1[1]
        zero = jnp.zeros((SSM_STATE, CHUNK_COLS), F32)
        wout_ref[0, m] = jnp.concatenate([
            jnp.concatenate([wout_rows[0][0], zero], axis=1),
            jnp.concatenate([zero, wout_rows[0][1]], axis=1),
            jnp.concatenate([wout_rows[1][0], zero], axis=1),
            jnp.concatenate([zero, wout_rows[1][1]], axis=1)], axis=0).astype(BF16)
    a_ref[0] = jnp.concatenate([jnp.concatenate([a[0] for a in a_rows], axis=1),
                                jnp.concatenate([a[1] for a in a_rows], axis=1)], axis=0)


def _ssm_prep(lam_re, lam_im, log_dt, b_re, b_im, c_re, c_im, weights):
    G, P, H = b_re.shape
    w_specs = []
    for w in weights:
        rows = w.shape[0] // LANE_TILES
        assert rows * LANE_TILES == w.shape[0] and rows % 16 == 0
        w_specs.append(pl.BlockSpec((rows, w.shape[1]), lambda q: (q, 0)))
    ldt = jnp.broadcast_to(log_dt[:, None], (G, P))
    rows = jnp.stack([lam_re, lam_im, ldt]).reshape(3, G // 2, 1, 2 * P)
    ct = jnp.stack([c_re, c_im]).transpose(0, 1, 3, 2)
    bt = jnp.tile(jnp.stack([b_re, b_im]).transpose(0, 1, 3, 2), (1, 1, 1, 2))
    gt = GROUPS_PER_TILE
    return pl.pallas_call(
        functools.partial(_ssm_prep_kernel, n_weights=len(weights)),
        grid=(LANE_TILES,),
        in_specs=[
            pl.BlockSpec((3, PAIRS_PER_TILE, 1, 2 * P), lambda q: (0, q, 0, 0)),
            pl.BlockSpec((2, gt, P, H), lambda q: (0, q, 0, 0)),
            pl.BlockSpec((2, gt, H, 2 * P), lambda q: (0, q, 0, 0)),
            *w_specs,
        ],
        out_specs=[
            pl.BlockSpec((1, gt, CHUNK_COLS, CHUNK_COLS), lambda q: (q, 0, 0, 0)),
            pl.BlockSpec((1, PAIRS_PER_TILE, 2 * CHUNK_COLS, 2 * LANES), lambda q: (q, 0, 0, 0)),
            pl.BlockSpec((1, PAIRS_PER_TILE, 2 * LANES, 2 * CHUNK_COLS), lambda q: (q, 0, 0, 0)),
            pl.BlockSpec((1, 2, STATE_COLS), lambda q: (q, 0, 0)),
            *w_specs,
        ],
        out_shape=[
            jax.ShapeDtypeStruct((LANE_TILES, gt, CHUNK_COLS, CHUNK_COLS), BF16),
            jax.ShapeDtypeStruct((LANE_TILES, PAIRS_PER_TILE, 2 * CHUNK_COLS, 2 * LANES), BF16),
            jax.ShapeDtypeStruct((LANE_TILES, PAIRS_PER_TILE, 2 * LANES, 2 * CHUNK_COLS), BF16),
            jax.ShapeDtypeStruct((LANE_TILES, 2, STATE_COLS), F32),
            *[jax.ShapeDtypeStruct(w.shape, BF16) for w in weights],
        ],
        compiler_params=_cparams(("parallel",)),
        name="ssm_prep",
    )(rows, ct, bt, *weights)


def _layer_norm(r, g, b):
    mu = jnp.mean(r, axis=-1, keepdims=True)
    c = r - mu
    var = jnp.mean(c * c, axis=-1, keepdims=True)
    return c * lax.rsqrt(var + LN_EPS) * g + b


MIX_ROWS = 256


def _gelu_tanh(y):
    return 0.5 * y * (1.0 + jnp.tanh(math.sqrt(2.0 / math.pi) * (y + 0.044715 * (y * y * y))))


def _mix_ln1_kernel(x_ref, attn_ref, y_ref, wglu_ref, bglu_ref, wo_ref, g_ref, b_ref, o_ref, y_sc):
    nb, tl, d = x_ref.shape
    pos = MIX_ROWS // nb
    n_pieces = tl // pos
    jp = pos // CHUNK

    def relayout(n):
        rows = slice(n * jp * nb, (n + 1) * jp * nb)
        for q in range(LANE_TILES):
            tiles = _chunk_tiles([y_ref[q, g, rows] for g in range(GROUPS_PER_TILE)])
            y_sc[q, n * MIX_ROWS:(n + 1) * MIX_ROWS] = jnp.stack(
                [t.reshape(jp, nb, LANES) for t in tiles], axis=1).reshape(MIX_ROWS, LANES)

    def glu(n):
        y = jnp.concatenate(
            [jnp.concatenate([y_sc[q, pl.ds(n * MIX_ROWS + b, pos, stride=nb), :] for q in range(LANE_TILES)], axis=1)
             for b in range(nb)], axis=0)
        gl = _gelu_tanh(y)
        z = gl * jax.nn.sigmoid(jnp.dot(gl.astype(BF16), wglu_ref[...], preferred_element_type=F32) + bglu_ref[...])
        return z.astype(BF16)

    def project(n, z):
        ls = slice(n * pos, (n + 1) * pos)
        mix = (jnp.dot(attn_ref[:, ls].reshape(MIX_ROWS, ATT_WIDTH), wo_ref[:ATT_WIDTH], preferred_element_type=F32)
               + jnp.dot(z, wo_ref[ATT_WIDTH:], preferred_element_type=F32))
        res = _layer_norm(ALPHA * x_ref[:, ls].reshape(MIX_ROWS, d) + mix, g_ref[...], b_ref[...])
        o_ref[:, ls] = res.reshape(nb, pos, d)

    relayout(0)
    z = glu(0)
    for n in range(n_pieces):
        if n + 1 < n_pieces:
            relayout(n + 1)
            z_next = glu(n + 1)
        project(n, z)
        z = z_next


def _mix_ln1(x, attn, y_tm, wglu_bf, b_glu, wo_bf, ln_g, ln_b, tl=256):
    B, L, D = x.shape
    assert MIX_ROWS % (B * CHUNK) == 0 and tl % (MIX_ROWS // B) == 0
    resident = lambda shape: pl.BlockSpec(shape, lambda l: (0, 0), pipeline_mode=pl.Buffered(1))
    return pl.pallas_call(
        _mix_ln1_kernel,
        grid=(L // tl,),
        in_specs=[
            pl.BlockSpec((B, tl, D), lambda l: (0, l, 0)),
            pl.BlockSpec((B, tl, ATT_WIDTH), lambda l: (0, l, 0)),
            pl.BlockSpec((LANE_TILES, GROUPS_PER_TILE, tl // CHUNK * B, CHUNK_COLS), lambda l: (0, 0, l, 0)),
            resident((SSM_WIDTH, SSM_WIDTH)),
            resident((1, SSM_WIDTH)),
            resident((D, D)),
            resident((1, D)),
            resident((1, D)),
        ],
        out_specs=pl.BlockSpec((B, tl, D), lambda l: (0, l, 0)),
        out_shape=jax.ShapeDtypeStruct((B, L, D), F32),
        scratch_shapes=[pltpu.VMEM((LANE_TILES, tl * B, LANES), F32)],
        compiler_params=_cparams(("parallel",)),
        name="mix_ln1",
    )(x, attn, y_tm, wglu_bf, b_glu, wo_bf, ln_g, ln_b)


HALO = 16
FFN_SUB = 256
FFN_DOWN_PARTS = 2
FFN_ROWS = 128


def _ffn_kernel(x_ref, xh_ref, p_ref, wple_ref, wgate_ref, wup_ref, cw_ref, cb_ref, wd_ref, g_ref, b_ref, o_ref,
                x32_sc, xh32_sc, p32_sc, xb_sc, acc_sc, hg_sc, hv_sc, act_sc, *, tl):
    nb = x_ref.shape[0]
    tm = tl * nb
    d_slabs = x_ref.shape[2] // LANES
    hp = (CONV_WIDTH - 1)
    for bb in range(nb):
        for s in range(d_slabs):
            lanes = slice(s * LANES, (s + 1) * LANES)
            x32_sc[s, pl.ds(bb, tl, stride=nb), :] = x_ref[bb, :, lanes]
            xh32_sc[s, pl.ds(bb, hp, stride=nb), :] = xh_ref[bb, SUBLANES - hp:, lanes]
        for s in range(p_ref.shape[2] // LANES):
            p32_sc[s, pl.ds(bb, tl, stride=nb), :] = p_ref[bb, :, s * LANES:(s + 1) * LANES]
    x_tm = jnp.concatenate([x32_sc[s] for s in range(d_slabs)], axis=1)
    p_tm = jnp.concatenate([p32_sc[s] for s in range(p32_sc.shape[0])], axis=1)
    keep = jnp.where(pl.program_id(0) == 0, 0.0, 1.0)
    xb_sc[0:HALO] = (jnp.concatenate([xh32_sc[s] for s in range(d_slabs)], axis=1) * keep).astype(BF16)
    xb_sc[HALO:] = x_tm.astype(BF16)
    gate = jax.nn.sigmoid(jnp.dot(xb_sc[HALO:], wgate_ref[...], preferred_element_type=F32))
    ple = jnp.dot(p_tm.astype(BF16), wple_ref[...], preferred_element_type=F32) * gate
    acc_sc[...] = ALPHA * x_tm + ple

    def conv(h_sc, slot, cols, r0, rows):
        out = cb_ref[:, cols] + cw_ref[CONV_WIDTH - 1:CONV_WIDTH, cols] * h_sc[slot, HALO + r0:HALO + r0 + rows]
        for j in range(CONV_WIDTH - 1):
            back = (CONV_WIDTH - 1 - j) * nb
            out = out + cw_ref[j:j + 1, cols] * h_sc[slot, HALO + r0 - back:HALO + r0 - back + rows]
        return out

    n_sub = D_FF // FFN_SUB
    g_cols = lambda n: slice(n * FFN_SUB, (n + 1) * FFN_SUB)
    v_cols = lambda n: slice(D_FF + n * FFN_SUB, D_FF + (n + 1) * FFN_SUB)

    def up(n):
        xb = xb_sc[...]
        hg_sc[n % 2] = jnp.dot(xb, wup_ref[:, g_cols(n)], preferred_element_type=F32)
        hv_sc[n % 2] = jnp.dot(xb, wup_ref[:, v_cols(n)], preferred_element_type=F32)

    def gate_block(n):
        for r0 in range(0, tm, FFN_ROWS):
            gc = conv(hg_sc, n % 2, g_cols(n), r0, FFN_ROWS)
            vc = conv(hv_sc, n % 2, v_cols(n), r0, FFN_ROWS)
            act_sc[r0:r0 + FFN_ROWS, g_cols(n)] = (gc * jax.nn.sigmoid(gc) * vc).astype(BF16)

    def down(lo, hi):
        cols = slice(lo * FFN_SUB, hi * FFN_SUB)
        return jnp.dot(act_sc[:, cols], wd_ref[cols, :], preferred_element_type=F32)

    cuts = [n_sub * k // FFN_DOWN_PARTS for k in range(FFN_DOWN_PARTS + 1)]
    up(0)
    for n in range(n_sub):
        if n + 1 < n_sub:
            up(n + 1)
        gate_block(n)
        if n + 1 in cuts[1:-1]:
            lo = cuts[cuts.index(n + 1) - 1]
            acc_sc[...] += down(lo, n + 1)

    res = _layer_norm(acc_sc[...] + down(cuts[-2], n_sub), g_ref[...], b_ref[...])
    for s in range(d_slabs):
        x32_sc[s] = res[:, s * LANES:(s + 1) * LANES]
    for bb in range(nb):
        for s in range(d_slabs):
            o_ref[bb, :, s * LANES:(s + 1) * LANES] = x32_sc[s, pl.ds(bb, tl, stride=nb), :]


def _ffn_ln2(x1, p, wple_bf, wgate_bf, wup_bf, conv_w, conv_b, wdown_bf, ln_g, ln_b, tl=64):
    B, L, D = x1.shape
    tm = tl * B
    assert D_FF % FFN_SUB == 0 and HALO == (CONV_WIDTH - 1) * B and tm % FFN_ROWS == 0
    const = lambda i: (0, 0)
    resident = lambda shape: pl.BlockSpec(shape, const, pipeline_mode=pl.Buffered(1))
    return pl.pallas_call(
        functools.partial(_ffn_kernel, tl=tl),
        grid=(L // tl,),
        in_specs=[
            pl.BlockSpec((B, tl, D), lambda i: (0, i, 0)),
            pl.BlockSpec((B, SUBLANES, D), lambda i: (0, jnp.maximum(i * (tl // SUBLANES) - 1, 0), 0)),
            pl.BlockSpec((B, tl, PLE_DIM), lambda i: (0, i, 0)),
            resident((PLE_DIM, D)),
            resident((D, D)),
            resident((D, 2 * D_FF)),
            resident((CONV_WIDTH, 2 * D_FF)),
            resident((1, 2 * D_FF)),
            resident((D_FF, D)),
            resident((1, D)),
            resident((1, D)),
        ],
        out_specs=pl.BlockSpec((B, tl, D), lambda i: (0, i, 0)),
        out_shape=jax.ShapeDtypeStruct((B, L, D), F32),
        scratch_shapes=[
            pltpu.VMEM((D // LANES, tm, LANES), F32),
            pltpu.VMEM((D // LANES, HALO, LANES), F32),
            pltpu.VMEM((PLE_DIM // LANES, tm, LANES), F32),
            pltpu.VMEM((tm + HALO, D), BF16),
            pltpu.VMEM((tm, D), F32),
            pltpu.VMEM((2, tm + HALO, FFN_SUB), F32),
            pltpu.VMEM((2, tm + HALO, FFN_SUB), F32),
            pltpu.VMEM((tm, D_FF), BF16),
        ],
        compiler_params=_cparams(("parallel",)),
        name="ffn_ln2",
    )(x1, x1, p, wple_bf, wgate_bf, wup_bf, conv_w, conv_b, wdown_bf, ln_g, ln_b)


def kernel(x, p, w_in, diff_lambda_q1, diff_lambda_k1, diff_lambda_q2, diff_lambda_k2, diff_subln_g, ssm_lambda_re, ssm_lambda_im, ssm_log_dt, ssm_b_re, ssm_b_im, ssm_c_re, ssm_c_im, ssm_d, ssm_w_glu, ssm_b_glu, w_o, ln1_g, ln1_b, ffn_w_up, ffn_conv_w, ffn_conv_b, ffn_w_down, w_ple, w_ple_gate, ln2_g, ln2_b):
    B, L, D = x.shape
    assert B == SUBLANES and D == D_MODEL and L % 512 == 0
    for i in range(DEPTH):
        lam_init = 0.8 - 0.6 * math.exp(-0.3 * i)
        row = lambda a: a[i].reshape(1, -1)
        dense = (w_in[i], ssm_w_glu[i], w_o[i], w_ple[i], w_ple_gate[i], ffn_w_up[i], ffn_w_down[i])
        *tables, w_in_bf, w_glu_bf, w_o_bf, w_ple_bf, w_gate_bf, w_up_bf, w_down_bf = _ssm_prep(
            ssm_lambda_re[i], ssm_lambda_im[i], ssm_log_dt[i], ssm_b_re[i], ssm_b_im[i], ssm_c_re[i], ssm_c_im[i],
            dense)
        qt, k, vt, u_tm = _in_proj(x, w_in_bf)
        attn = _diff_attn_flat(qt, k, vt, row(diff_lambda_q1), row(diff_lambda_k1), row(diff_lambda_q2),
                          row(diff_lambda_k2), diff_subln_g[i], lam_init)
        y_tm = _ssm(u_tm, tables[:4], ssm_d[i], B, L)
        x1 = _mix_ln1(x, attn, y_tm, w_glu_bf, row(ssm_b_glu), w_o_bf, row(ln1_g), row(ln1_b))
        x = _ffn_ln2(x1, p[i], w_ple_bf, w_gate_bf, w_up_bf, ffn_conv_w[i], row(ffn_conv_b), w_down_bf,
                     row(ln2_g), row(ln2_b))
    return x
```

```python
import functools
import math

import jax
import jax.numpy as jnp
from jax import lax
from jax.experimental import pallas as pl
from jax.experimental.pallas import tpu as pltpu

F32 = jnp.float32
BF16 = jnp.bfloat16

SUBLANES = 8
LANES = 128

DEPTH = 1
D_MODEL = 1024
PLE_DIM = 256
HEADS = 4
QK_DIM = 64
V_DIM = 2 * QK_DIM
QK_WIDTH = HEADS * 2 * QK_DIM
ATT_WIDTH = HEADS * V_DIM
SSM_WIDTH = D_MODEL - ATT_WIDTH
SSM_GROUP = 16
SSM_GROUPS = SSM_WIDTH // SSM_GROUP
SSM_STATE = 64
D_FF = 2816
CONV_WIDTH = 3
LN_EPS = 1e-5
ALPHA = (2 * DEPTH) ** 0.25
QK_SCALE = QK_DIM ** -0.5 * math.log2(math.e)

CHUNK = 16
CHUNK_COLS = CHUNK * SSM_GROUP
GROUPS_PER_TILE = LANES // SSM_GROUP
LANE_TILES = SSM_WIDTH // LANES
PAIRS_PER_TILE = GROUPS_PER_TILE // 2
STATE_COLS = GROUPS_PER_TILE * SSM_STATE

VMEM_LIMIT = 56 * 1024 * 1024


def _cparams(sem):
    return pltpu.CompilerParams(dimension_semantics=sem, vmem_limit_bytes=VMEM_LIMIT)


TB = 256
VT_ROWS = V_DIM + 16


def _in_proj_kernel(x_ref, w_ref, qt_ref, k_ref, vt_ref, u_ref, u_sc):
    wq_ref = w_ref.at[:, :QK_WIDTH]
    wk_ref = w_ref.at[:, QK_WIDTH:2 * QK_WIDTH]
    wv_ref = w_ref.at[:, 2 * QK_WIDTH:2 * QK_WIDTH + ATT_WIDTH]
    wu_ref = w_ref.at[:, 2 * QK_WIDTH + ATT_WIDTH:]
    nb, tl, d = x_ref.shape
    jt = tl // CHUNK
    xb = x_ref[...].reshape(nb * tl, d).astype(BF16)
    hu = jnp.dot(xb, wu_ref[...], preferred_element_type=F32)
    for b in range(nb):
        for q in range(LANE_TILES):
            u_sc[q, pl.ds(b, tl, stride=nb), :] = hu[b * tl:(b + 1) * tl, q * LANES:(q + 1) * LANES]
    for q in range(LANE_TILES):
        tiles = u_sc[q].reshape(jt, CHUNK, nb, LANES)
        slabs = _chunk_slabs([tiles[:, r].reshape(jt * nb, LANES) for r in range(CHUNK)])
        for g in range(GROUPS_PER_TILE):
            u_ref[q, g] = slabs[g]
    tn = (((0,), (1,)), ((), ()))
    qt = lax.dot_general(wq_ref[...], xb, tn, preferred_element_type=F32) * QK_SCALE
    vt = lax.dot_general(wv_ref[...], xb, tn, preferred_element_type=F32)
    pad_row = lax.broadcasted_iota(jnp.int32, (VT_ROWS - V_DIM, TB), 0)
    ones_row = jnp.where(pad_row == 0, 1.0, 0.0).astype(BF16)
    for b in range(nb):
        for h in range(HEADS):
            for c in range(tl // TB):
                cols = slice(b * tl + c * TB, b * tl + (c + 1) * TB)
                qt_ref[b, h, c] = qt[h * LANES:(h + 1) * LANES, cols].astype(BF16)
                vt_ref[b, h, c, :V_DIM] = vt[h * LANES:(h + 1) * LANES, cols].astype(BF16)
                vt_ref[b, h, c, V_DIM:] = ones_row
    k_ref[...] = jnp.dot(xb, wk_ref[...], preferred_element_type=F32).astype(BF16).reshape(k_ref.shape)


def _in_proj(x, w_in, tl=256):
    B, L, D = x.shape
    slab = lambda rows: pl.BlockSpec((B, HEADS, tl // TB, rows, TB), lambda l: (0, 0, l, 0, 0))
    slab_shape = lambda rows: jax.ShapeDtypeStruct((B, HEADS, L // TB, rows, TB), BF16)
    return pl.pallas_call(
        _in_proj_kernel,
        grid=(L // tl,),
        in_specs=[
            pl.BlockSpec((B, tl, D), lambda l: (0, l, 0)),
            pl.BlockSpec(w_in.shape, lambda l: (0, 0), pipeline_mode=pl.Buffered(1)),
        ],
        out_specs=[
            slab(LANES),
            pl.BlockSpec((B, tl, QK_WIDTH), lambda l: (0, l, 0)),
            slab(VT_ROWS),
            pl.BlockSpec((LANE_TILES, GROUPS_PER_TILE, tl // CHUNK * B, CHUNK_COLS), lambda l: (0, 0, l, 0)),
        ],
        out_shape=[
            slab_shape(LANES),
            jax.ShapeDtypeStruct((B, L, QK_WIDTH), BF16),
            slab_shape(VT_ROWS),
            jax.ShapeDtypeStruct((LANE_TILES, GROUPS_PER_TILE, L // CHUNK * B, CHUNK_COLS), F32),
        ],
        scratch_shapes=[pltpu.VMEM((LANE_TILES, tl * B, LANES), F32)],
        compiler_params=_cparams(("parallel",)),
        name="in_proj",
    )(x, w_in)


def _attn_flat_kernel(lq1_ref, lk1_ref, lq2_ref, lk2_ref, g_ref, qt_ref, k_ref, vt_ref, o_ref,
                      acc_sc, qst_sc, s_sc, p_sc, bm_sc, st_sc, al_sc, *, lam_init):
    n_q = qt_ref.shape[2]
    n_pairs = n_q * (n_q + 1) // 2
    assert TB == 2 * LANES and n_pairs % 2 == 0
    heads = range(HEADS)
    n_strips = 2 * TB // LANES
    lam = (jnp.exp(jnp.sum(lq1_ref[...] * lk1_ref[...], axis=-1, keepdims=True))
           - jnp.exp(jnp.sum(lq2_ref[...] * lk2_ref[...], axis=-1, keepdims=True)) + lam_init)

    def prepare(i, c):
        for hh in heads:
            qt = qt_ref[0, hh, i]
            row = lax.broadcasted_iota(jnp.int32, qt.shape, 0)
            zero = jnp.zeros_like(qt)
            qst_sc[i, hh, :, :TB] = jnp.where(row < QK_DIM, qt, zero)
            qst_sc[i, hh, :, TB:] = jnp.where(row >= QK_DIM, qt, zero)
            acc_sc[i, hh] = jnp.zeros(acc_sc.shape[2:], F32)
            st_sc[i, hh] = jnp.full((1, 2 * TB), -jnp.inf, F32)
        return c

    lax.fori_loop(0, n_q, prepare, 0)
    p_sc[1] = jnp.zeros(p_sc.shape[1:], BF16)
    al_sc[1] = jnp.ones(al_sc.shape[1:], F32)

    def diag_place(i):
        r = i % 4
        return jnp.where(jnp.logical_or(r == 1, r == 2), 1, 0)

    def is_diag(pair):
        return pair[1] == diag_place(pair[0])

    def block_of(pair):
        i, k = pair
        d = diag_place(i)
        return jnp.where(k == d, i, jnp.where(k < d, k, k - 1))

    def scores(pair, slot):
        i, j = pair[0], block_of(pair)
        rows = pl.ds(pl.multiple_of(j * TB, TB), TB)
        for hh in heads:
            kb = k_ref[0, rows, hh * LANES:(hh + 1) * LANES]
            s = jnp.dot(kb, qst_sc[i, hh], preferred_element_type=F32)
            s_sc[slot, hh] = s
            bm_sc[slot, hh] = jnp.max(s, axis=0, keepdims=True)

    def values(pair, slot):
        i, j = pair[0], block_of(pair)
        for hh in heads:
            pv = jnp.dot(vt_ref[0, hh, j], p_sc[slot, hh], preferred_element_type=F32)
            acc_sc[i, hh] = al_sc[slot, hh] * acc_sc[i, hh] + pv

    def softmax(pair, slot, diagonal):
        i, _ = pair
        for hh in heads:
            for c in range(n_strips):
                lanes = slice(c * LANES, (c + 1) * LANES)
                m = st_sc[i, hh, :, lanes]
                if not diagonal:
                    mc = jnp.maximum(m, bm_sc[slot, hh, :, lanes])
                    p = jnp.exp2(s_sc[slot, hh, :, lanes] - mc)
                else:
                    s = s_sc[slot, hh, :, lanes]
                    key = lax.broadcasted_iota(jnp.int32, s.shape, 0)
                    qry = lax.broadcasted_iota(jnp.int32, s.shape, 1) + (c * LANES) % TB
                    s = jnp.where(key <= qry, s, -jnp.inf)
                    mc = jnp.maximum(m, jnp.max(s, axis=0, keepdims=True))
                    p = jnp.exp2(s - mc)
                p_sc[slot, hh, :, lanes] = p.astype(BF16)
                st_sc[i, hh, :, lanes] = mc
                al_sc[slot, hh, :, lanes] = jnp.exp2(m - mc)

    def succ(pair):
        i, k = pair
        last = k == i
        i2 = jnp.minimum(jnp.where(last, i + 1, i), n_q - 1)
        return i2, jnp.where(last, 0, k + 1)

    def two_steps(t, carry):
        prev, cur = carry
        nxt = succ(cur)
        nxt2 = succ(nxt)

        def body(diag0):
            scores(nxt, 1)
            values(prev, 1)
            softmax(cur, 0, diag0)
            scores(nxt2, 0)
            values(cur, 0)
            softmax(nxt, 1, False)

        d0 = is_diag(cur)
        pl.when(d0)(lambda: body(True))
        pl.when(jnp.logical_not(d0))(lambda: body(False))
        return nxt, nxt2

    zero = jnp.int32(0)
    first = (zero, zero)
    scores(first, 0)
    last, _ = lax.fori_loop(0, n_pairs // 2, two_steps, (first, first))
    values(last, 1)

    def finish(t, c):
        for i in (2 * t, 2 * t + 1):
            for hh in heads:
                o = acc_sc[i, hh, :V_DIM] * (1.0 / acc_sc[i, hh, V_DIM:V_DIM + 1])
                o = o[:, :TB] - lam * o[:, TB:]
                ms = jnp.mean(o * o, axis=0, keepdims=True)
                o = o * lax.rsqrt(ms + LN_EPS) * (g_ref[...] * (1.0 - lam_init))
                rows = pl.ds(pl.multiple_of(i * TB, TB), TB)
                o_ref[0, rows, hh * LANES:(hh + 1) * LANES] = o.T.astype(o_ref.dtype)
        return c

    lax.fori_loop(0, n_q // 2, finish, 0)


def _diff_attn_flat(qt, k, vt, lq1, lk1, lq2, lk2, subln_g, lam_init):
    B, L, _ = k.shape
    n_q = L // TB
    vec = pl.BlockSpec((1, QK_DIM), lambda b: (0, 0))
    return pl.pallas_call(
        functools.partial(_attn_flat_kernel, lam_init=lam_init),
        grid=(B,),
        in_specs=[
            vec, vec, vec, vec,
            pl.BlockSpec((V_DIM, 1), lambda b: (0, 0)),
            pl.BlockSpec((1, HEADS, n_q, LANES, TB), lambda b: (b, 0, 0, 0, 0)),
            pl.BlockSpec((1, L, HEADS * LANES), lambda b: (b, 0, 0)),
            pl.BlockSpec((1, HEADS, n_q, VT_ROWS, TB), lambda b: (b, 0, 0, 0, 0)),
        ],
        out_specs=pl.BlockSpec((1, L, HEADS * LANES), lambda b: (b, 0, 0)),
        out_shape=jax.ShapeDtypeStruct((B, L, ATT_WIDTH), BF16),
        scratch_shapes=[
            pltpu.VMEM((n_q, HEADS, VT_ROWS, 2 * TB), F32),
            pltpu.VMEM((n_q, HEADS, LANES, 2 * TB), BF16),
            pltpu.VMEM((2, HEADS, TB, 2 * TB), F32),
            pltpu.VMEM((2, HEADS, TB, 2 * TB), BF16),
            pltpu.VMEM((2, HEADS, 1, 2 * TB), F32),
            pltpu.VMEM((n_q, HEADS, 1, 2 * TB), F32),
            pltpu.VMEM((2, HEADS, 1, 2 * TB), F32),
        ],
        compiler_params=_cparams(("parallel",)),
        name="diff_attn",
    )(lq1, lk1, lq2, lk2, subln_g.reshape(V_DIM, 1), qt, k, vt)


def _block_transpose8(vs):
    lane_blk = lax.broadcasted_iota(jnp.int32, vs[0].shape, 1) // SSM_GROUP
    for d in (4, 2, 1):
        keep = (lane_blk & d) == 0
        new = list(vs)
        for a in range(8):
            if a & d == 0:
                lo, hi = vs[a], vs[a + d]
                new[a] = jnp.where(keep, lo, pltpu.roll(hi, SSM_GROUP * d, 1))
                new[a + d] = jnp.where(keep, pltpu.roll(lo, LANES - SSM_GROUP * d, 1), hi)
        vs = new
    return vs


SSM_RING = 3


def _ssm_kernel(u_hbm, toep_ref, win_ref, wout_ref, a_ref, d_ref, y_ref, s_sc, x_sc, sp_sc, u_ring, sems, *, jt):
    rows = jt * SUBLANES
    n_l = pl.num_programs(1)
    n_steps = pl.num_programs(0) * n_l
    step = pl.program_id(0) * n_l + pl.program_id(1)

    def fetch(t):
        src = u_hbm.at[t // n_l, :, pl.ds((t % n_l) * rows, rows), :]
        return pltpu.make_async_copy(src, u_ring.at[t % SSM_RING], sems.at[t % SSM_RING])

    @pl.when(step == 0)
    def _():
        for t in range(SSM_RING - 1):
            fetch(t).start()

    @pl.when(step + SSM_RING - 1 < n_steps)
    def _():
        fetch(step + SSM_RING - 1).start()

    fetch(step).wait()
    u_ref = u_ring.at[step % SSM_RING]

    @pl.when(pl.program_id(1) == 0)
    def _():
        s_sc[...] = jnp.zeros(s_sc.shape, F32)

    ug = [u_ref[g].astype(BF16) for g in range(GROUPS_PER_TILE)]

    y = [jnp.dot(ug[g], toep_ref[0, g], preferred_element_type=F32) for g in range(GROUPS_PER_TILE)]

    for m in range(PAIRS_PER_TILE):
        up = jnp.concatenate([ug[2 * m], ug[2 * m + 1]], axis=1)
        xp = jnp.dot(up, win_ref[0, m], preferred_element_type=F32)
        x_sc[0, :, m * LANES:(m + 1) * LANES] = xp[:, :LANES]
        x_sc[1, :, m * LANES:(m + 1) * LANES] = xp[:, LANES:]

    ar = a_ref[0, 0:1, :]
    ai = a_ref[0, 1:2, :]

    def step(j, carry):
        sr, si = carry
        r8 = pl.ds(pl.multiple_of(j * SUBLANES, SUBLANES), SUBLANES)
        sp_sc[0, r8, :] = sr
        sp_sc[1, r8, :] = si
        return (ar * sr - ai * si + x_sc[0, r8, :], ar * si + ai * sr + x_sc[1, r8, :])

    sr, si = lax.fori_loop(0, jt, step, (s_sc[0], s_sc[1]))
    s_sc[0] = sr
    s_sc[1] = si

    for m in range(PAIRS_PER_TILE):
        sp = jnp.concatenate([sp_sc[0, :, m * LANES:(m + 1) * LANES],
                              sp_sc[1, :, m * LANES:(m + 1) * LANES]], axis=1).astype(BF16)
        yc = jnp.dot(sp, wout_ref[0, m], preferred_element_type=F32)
        y[2 * m] = y[2 * m] + yc[:, :CHUNK_COLS]
        y[2 * m + 1] = y[2 * m + 1] + yc[:, CHUNK_COLS:]

    for g in range(GROUPS_PER_TILE):
        y_ref[0, g] = y[g] + d_ref[0, g] * u_ref[g]


def _chunk_slabs(tiles):
    z_lo = _block_transpose8(tiles[:8])
    z_hi = _block_transpose8(tiles[8:])
    return [jnp.concatenate([z_lo[g], z_hi[g]], axis=1) for g in range(GROUPS_PER_TILE)]


def _chunk_tiles(slabs):
    out = []
    for half in range(2):
        out += _block_transpose8([s[:, half * LANES:(half + 1) * LANES] for s in slabs])
    return out


def _ssm(u_slab, tables, d_skip, B, L, lt=1024):
    toep, win, wout, a16 = tables
    jt = lt // CHUNK
    n_l = L // lt
    rows = jt * B
    slab = pl.BlockSpec((1, GROUPS_PER_TILE, rows, CHUNK_COLS), lambda q, l: (q, 0, l, 0))
    d_slab = jnp.tile(d_skip.reshape(LANE_TILES, GROUPS_PER_TILE, 1, SSM_GROUP), (1, 1, 1, CHUNK))
    return pl.pallas_call(
        functools.partial(_ssm_kernel, jt=jt),
        grid=(LANE_TILES, n_l),
        in_specs=[
            pl.BlockSpec(memory_space=pl.ANY),
            pl.BlockSpec((1, GROUPS_PER_TILE, CHUNK_COLS, CHUNK_COLS), lambda q, l: (q, 0, 0, 0)),
            pl.BlockSpec((1, PAIRS_PER_TILE, 2 * CHUNK_COLS, 2 * LANES), lambda q, l: (q, 0, 0, 0)),
            pl.BlockSpec((1, PAIRS_PER_TILE, 2 * LANES, 2 * CHUNK_COLS), lambda q, l: (q, 0, 0, 0)),
            pl.BlockSpec((1, 2, STATE_COLS), lambda q, l: (q, 0, 0)),
            pl.BlockSpec((1, GROUPS_PER_TILE, 1, CHUNK_COLS), lambda q, l: (q, 0, 0, 0)),
        ],
        out_specs=slab,
        out_shape=jax.ShapeDtypeStruct(u_slab.shape, F32),
        scratch_shapes=[
            pltpu.VMEM((2, B, STATE_COLS), F32),
            pltpu.VMEM((2, rows, STATE_COLS), F32),
            pltpu.VMEM((2, rows, STATE_COLS), F32),
            pltpu.VMEM((SSM_RING, GROUPS_PER_TILE, rows, CHUNK_COLS), F32),
            pltpu.SemaphoreType.DMA((SSM_RING,)),
        ],
        compiler_params=_cparams(("arbitrary", "arbitrary")),
        name="ssm",
    )(u_slab, toep, win, wout, a16, d_slab)


def _cmul(a, b):
    return a[0] * b[0] - a[1] * b[1], a[0] * b[1] + a[1] * b[0]


def _zoh(lr, li, ldt):
    dt = jnp.exp(ldt)
    mag = jnp.exp(lr * dt)
    ab = (mag * jnp.cos(li * dt), mag * jnp.sin(li * dt))
    den = lr * lr + li * li
    zr, zi = ab[0] - 1.0, ab[1]
    return ab, ((zr * lr + zi * li) / den, (zi * lr - zr * li) / den)


def _cpow(base, exponent, bits):
    out = None
    for k in range(bits):
        on = (exponent & (1 << k)) != 0
        sel = (jnp.where(on, base[0], 1.0), jnp.where(on, base[1], 0.0))
        out = sel if out is None else _cmul(out, sel)
        if k + 1 < bits:
            base = _cmul(base, base)
    return out


def _ssm_prep_kernel(row_ref, ct_ref, bt_ref, *refs, n_weights):
    w_refs, (toep_ref, win_ref, wout_ref, a_ref), w_bf_refs = refs[:n_weights], refs[n_weights:n_weights + 4], \
        refs[n_weights + 4:]
    for w_ref, w_bf_ref in zip(w_refs, w_bf_refs):
        w_bf_ref[...] = w_ref[...].astype(BF16)
    bits = CHUNK.bit_length() - 1
    lane = lax.broadcasted_iota(jnp.int32, (1, CHUNK_COLS), 1)
    tau = lane // SSM_GROUP
    rowi = lax.broadcasted_iota(jnp.int32, (CHUNK_COLS, 1), 0)
    back = (CHUNK - 1) - rowi // SSM_GROUP
    lane2 = lax.broadcasted_iota(jnp.int32, (1, LANES), 1)
    exact = functools.partial(jnp.dot, precision=lax.Precision.HIGHEST, preferred_element_type=F32)
    h_row = lax.broadcasted_iota(jnp.int32, (SSM_GROUP, CHUNK_COLS), 0)
    repeat = jnp.where(lane % SSM_GROUP == h_row, 1.0, 0.0)
    zoh = [_zoh(row_ref[0, m], row_ref[1, m], row_ref[2, m]) for m in range(PAIRS_PER_TILE)]
    pad = jnp.zeros((SUBLANES - PAIRS_PER_TILE, LANES), F32)
    ab_cols = tuple(jnp.concatenate([z[0][c] for z in zoh] + [pad], axis=0).T for c in range(2))
    a_rows = []
    for m in range(PAIRS_PER_TILE):
        abp, fp = zoh[m]
        pwr = _cpow(abp, back, bits)
        a16 = abp
        for _ in range(bits):
            a16 = _cmul(a16, a16)
        a_rows.append(a16)
        wout_rows = [[None, None], [None, None]]
        for j in range(2):
            g = 2 * m + j
            states = slice(j * SSM_STATE, (j + 1) * SSM_STATE)
            ab = tuple(v[states, m:m + 1] for v in ab_cols)
            pw = _cpow(ab, tau, bits)
            c_rep = tuple(exact(ct_ref[c, g], repeat) for c in range(2))
            g0 = _cmul(pw, c_rep)
            g1 = _cmul(g0, ab)
            mine = (lane2 < SSM_STATE) if j == 0 else (lane2 >= SSM_STATE)
            bb2 = _cmul(fp, (bt_ref[0, g], bt_ref[1, g]))
            bb2 = tuple(jnp.where(mine, v, 0.0) for v in bb2)
            none = jnp.zeros_like(g0[0])
            stack = (lambda v: jnp.concatenate([v, none], axis=0)) if j == 0 else \
                    (lambda v: jnp.concatenate([none, v], axis=0))
            kt = exact(bb2[0], stack(g0[0])) - exact(bb2[1], stack(g0[1]))
            toep_ref[0, g] = jnp.concatenate(
                [jnp.where(lane >= SSM_GROUP * r, pltpu.roll(kt, SSM_GROUP * r, 1), 0.0) if r else kt
                 for r in range(CHUNK)], axis=0).astype(BF16)
            w = _cmul(pwr, tuple(jnp.concatenate([v] * CHUNK, axis=0) for v in bb2))
            win_ref[0, m, j * CHUNK_COLS:(j + 1) * CHUNK_COLS, :] = jnp.concatenate(w, axis=1).astype(BF16)
            wout_rows[0][j] = g1[0]
            wout_rows[1][j] = -g1[1]
        zero = jnp.zeros((SSM_STATE, CHUNK_COLS), F32)
        wout_ref[0, m] = jnp.concatenate([
            jnp.concatenate([wout_rows[0][0], zero], axis=1),
            jnp.concatenate([zero, wout_rows[0][1]], axis=1),
            jnp.concatenate([wout_rows[1][0], zero], axis=1),
            jnp.concatenate([zero, wout_rows[1][1]], axis=1)], axis=0).astype(BF16)
    a_ref[0] = jnp.concatenate([jnp.concatenate([a[0] for a in a_rows], axis=1),
                                jnp.concatenate([a[1] for a in a_rows], axis=1)], axis=0)


def _ssm_prep(lam_re, lam_im, log_dt, b_re, b_im, c_re, c_im, weights):
    G, P, H = b_re.shape
    w_specs = []
    for w in weights:
        rows = w.shape[0] // LANE_TILES
        assert rows * LANE_TILES == w.shape[0] and rows % 16 == 0
        w_specs.append(pl.BlockSpec((rows, w.shape[1]), lambda q: (q, 0)))
    ldt = jnp.broadcast_to(log_dt[:, None], (G, P))
    rows = jnp.stack([lam_re, lam_im, ldt]).reshape(3, G // 2, 1, 2 * P)
    ct = jnp.stack([c_re, c_im]).transpose(0, 1, 3, 2)
    bt = jnp.tile(jnp.stack([b_re, b_im]).transpose(0, 1, 3, 2), (1, 1, 1, 2))
    gt = GROUPS_PER_TILE
    return pl.pallas_call(
        functools.partial(_ssm_prep_kernel, n_weights=len(weights)),
        grid=(LANE_TILES,),
        in_specs=[
            pl.BlockSpec((3, PAIRS_PER_TILE, 1, 2 * P), lambda q: (0, q, 0, 0)),
            pl.BlockSpec((2, gt, P, H), lambda q: (0, q, 0, 0)),
            pl.BlockSpec((2, gt, H, 2 * P), lambda q: (0, q, 0, 0)),
            *w_specs,
        ],
        out_specs=[
            pl.BlockSpec((1, gt, CHUNK_COLS, CHUNK_COLS), lambda q: (q, 0, 0, 0)),
            pl.BlockSpec((1, PAIRS_PER_TILE, 2 * CHUNK_COLS, 2 * LANES), lambda q: (q, 0, 0, 0)),
            pl.BlockSpec((1, PAIRS_PER_TILE, 2 * LANES, 2 * CHUNK_COLS), lambda q: (q, 0, 0, 0)),
            pl.BlockSpec((1, 2, STATE_COLS), lambda q: (q, 0, 0)),
            *w_specs,
        ],
        out_shape=[
            jax.ShapeDtypeStruct((LANE_TILES, gt, CHUNK_COLS, CHUNK_COLS), BF16),
            jax.ShapeDtypeStruct((LANE_TILES, PAIRS_PER_TILE, 2 * CHUNK_COLS, 2 * LANES), BF16),
            jax.ShapeDtypeStruct((LANE_TILES, PAIRS_PER_TILE, 2 * LANES, 2 * CHUNK_COLS), BF16),
            jax.ShapeDtypeStruct((LANE_TILES, 2, STATE_COLS), F32),
            *[jax.ShapeDtypeStruct(w.shape, BF16) for w in weights],
        ],
        compiler_params=_cparams(("parallel",)),
        name="ssm_prep",
    )(rows, ct, bt, *weights)


def _layer_norm(r, g, b):
    mu = jnp.mean(r, axis=-1, keepdims=True)
    c = r - mu
    var = jnp.mean(c * c, axis=-1, keepdims=True)
    return c * lax.rsqrt(var + LN_EPS) * g + b


MIX_ROWS = 256


def _gelu_tanh(y):
    return 0.5 * y * (1.0 + jnp.tanh(math.sqrt(2.0 / math.pi) * (y + 0.044715 * (y * y * y))))


def _mix_ln1_kernel(x_ref, attn_ref, y_ref, wglu_ref, bglu_ref, wo_ref, g_ref, b_ref, o_ref, y_sc):
    nb, tl, d = x_ref.shape
    pos = MIX_ROWS // nb
    n_pieces = tl // pos
    jp = pos // CHUNK

    def relayout(n):
        rows = slice(n * jp * nb, (n + 1) * jp * nb)
        for q in range(LANE_TILES):
            tiles = _chunk_tiles([y_ref[q, g, rows] for g in range(GROUPS_PER_TILE)])
            y_sc[q, n * MIX_ROWS:(n + 1) * MIX_ROWS] = jnp.stack(
                [t.reshape(jp, nb, LANES) for t in tiles], axis=1).reshape(MIX_ROWS, LANES)

    def glu(n):
        y = jnp.concatenate(
            [jnp.concatenate([y_sc[q, pl.ds(n * MIX_ROWS + b, pos, stride=nb), :] for q in range(LANE_TILES)], axis=1)
             for b in range(nb)], axis=0)
        gl = _gelu_tanh(y)
        z = gl * jax.nn.sigmoid(jnp.dot(gl.astype(BF16), wglu_ref[...], preferred_element_type=F32) + bglu_ref[...])
        return z.astype(BF16)

    def project(n, z):
        ls = slice(n * pos, (n + 1) * pos)
        mix = (jnp.dot(attn_ref[:, ls].reshape(MIX_ROWS, ATT_WIDTH), wo_ref[:ATT_WIDTH], preferred_element_type=F32)
               + jnp.dot(z, wo_ref[ATT_WIDTH:], preferred_element_type=F32))
        res = _layer_norm(ALPHA * x_ref[:, ls].reshape(MIX_ROWS, d) + mix, g_ref[...], b_ref[...])
        o_ref[:, ls] = res.reshape(nb, pos, d)

    relayout(0)
    z = glu(0)
    for n in range(n_pieces):
        if n + 1 < n_pieces:
            relayout(n + 1)
            z_next = glu(n + 1)
        project(n, z)
        z = z_next


def _mix_ln1(x, attn, y_tm, wglu_bf, b_glu, wo_bf, ln_g, ln_b, tl=256):
    B, L, D = x.shape
    assert MIX_ROWS % (B * CHUNK) == 0 and tl % (MIX_ROWS // B) == 0
    resident = lambda shape: pl.BlockSpec(shape, lambda l: (0, 0), pipeline_mode=pl.Buffered(1))
    return pl.pallas_call(
        _mix_ln1_kernel,
        grid=(L // tl,),
        in_specs=[
            pl.BlockSpec((B, tl, D), lambda l: (0, l, 0)),
            pl.BlockSpec((B, tl, ATT_WIDTH), lambda l: (0, l, 0)),
            pl.BlockSpec((LANE_TILES, GROUPS_PER_TILE, tl // CHUNK * B, CHUNK_COLS), lambda l: (0, 0, l, 0)),
            resident((SSM_WIDTH, SSM_WIDTH)),
            resident((1, SSM_WIDTH)),
            resident((D, D)),
            resident((1, D)),
            resident((1, D)),
        ],
        out_specs=pl.BlockSpec((B, tl, D), lambda l: (0, l, 0)),
        out_shape=jax.ShapeDtypeStruct((B, L, D), F32),
        scratch_shapes=[pltpu.VMEM((LANE_TILES, tl * B, LANES), F32)],
        compiler_params=_cparams(("parallel",)),
        name="mix_ln1",
    )(x, attn, y_tm, wglu_bf, b_glu, wo_bf, ln_g, ln_b)


HALO = 16
FFN_SUB = 256
FFN_DOWN_PARTS = 2
FFN_ROWS = 128


def _ffn_kernel(x_ref, xh_ref, p_ref, wple_ref, wgate_ref, wup_ref, cw_ref, cb_ref, wd_ref, g_ref, b_ref, o_ref,
                x32_sc, xh32_sc, p32_sc, xb_sc, acc_sc, hg_sc, hv_sc, act_sc, *, tl):
    nb = x_ref.shape[0]
    tm = tl * nb
    d_slabs = x_ref.shape[2] // LANES
    hp = (CONV_WIDTH - 1)
    for bb in range(nb):
        for s in range(d_slabs):
            lanes = slice(s * LANES, (s + 1) * LANES)
            x32_sc[s, pl.ds(bb, tl, stride=nb), :] = x_ref[bb, :, lanes]
            xh32_sc[s, pl.ds(bb, hp, stride=nb), :] = xh_ref[bb, SUBLANES - hp:, lanes]
        for s in range(p_ref.shape[2] // LANES):
            p32_sc[s, pl.ds(bb, tl, stride=nb), :] = p_ref[bb, :, s * LANES:(s + 1) * LANES]
    x_tm = jnp.concatenate([x32_sc[s] for s in range(d_slabs)], axis=1)
    p_tm = jnp.concatenate([p32_sc[s] for s in range(p32_sc.shape[0])], axis=1)
    keep = jnp.where(pl.program_id(0) == 0, 0.0, 1.0)
    xb_sc[0:HALO] = (jnp.concatenate([xh32_sc[s] for s in range(d_slabs)], axis=1) * keep).astype(BF16)
    xb_sc[HALO:] = x_tm.astype(BF16)
    gate = jax.nn.sigmoid(jnp.dot(xb_sc[HALO:], wgate_ref[...], preferred_element_type=F32))
    ple = jnp.dot(p_tm.astype(BF16), wple_ref[...], preferred_element_type=F32) * gate
    acc_sc[...] = ALPHA * x_tm + ple

    def conv(h_sc, slot, cols, r0, rows):
        out = cb_ref[:, cols] + cw_ref[CONV_WIDTH - 1:CONV_WIDTH, cols] * h_sc[slot, HALO + r0:HALO + r0 + rows]
        for j in range(CONV_WIDTH - 1):
            back = (CONV_WIDTH - 1 - j) * nb
            out = out + cw_ref[j:j + 1, cols] * h_sc[slot, HALO + r0 - back:HALO + r0 - back + rows]
        return out

    n_sub = D_FF // FFN_SUB
    g_cols = lambda n: slice(n * FFN_SUB, (n + 1) * FFN_SUB)
    v_cols = lambda n: slice(D_FF + n * FFN_SUB, D_FF + (n + 1) * FFN_SUB)

    def up(n):
        xb = xb_sc[...]
        hg_sc[n % 2] = jnp.dot(xb, wup_ref[:, g_cols(n)], preferred_element_type=F32)
        hv_sc[n % 2] = jnp.dot(xb, wup_ref[:, v_cols(n)], preferred_element_type=F32)

    def gate_block(n):
        for r0 in range(0, tm, FFN_ROWS):
            gc = conv(hg_sc, n % 2, g_cols(n), r0, FFN_ROWS)
            vc = conv(hv_sc, n % 2, v_cols(n), r0, FFN_ROWS)
            act_sc[r0:r0 + FFN_ROWS, g_cols(n)] = (gc * jax.nn.sigmoid(gc) * vc).astype(BF16)

    def down(lo, hi):
        cols = slice(lo * FFN_SUB, hi * FFN_SUB)
        return jnp.dot(act_sc[:, cols], wd_ref[cols, :], preferred_element_type=F32)

    cuts = [n_sub * k // FFN_DOWN_PARTS for k in range(FFN_DOWN_PARTS + 1)]
    up(0)
    for n in range(n_sub):
        if n + 1 < n_sub:
            up(n + 1)
        gate_block(n)
        if n + 1 in cuts[1:-1]:
            lo = cuts[cuts.index(n + 1) - 1]
            acc_sc[...] += down(lo, n + 1)

    res = _layer_norm(acc_sc[...] + down(cuts[-2], n_sub), g_ref[...], b_ref[...])
    for s in range(d_slabs):
        x32_sc[s] = res[:, s * LANES:(s + 1) * LANES]
    for bb in range(nb):
        for s in range(d_slabs):
            o_ref[bb, :, s * LANES:(s + 1) * LANES] = x32_sc[s, pl.ds(bb, tl, stride=nb), :]


def _ffn_ln2(x1, p, wple_bf, wgate_bf, wup_bf, conv_w, conv_b, wdown_bf, ln_g, ln_b, tl=64):
    B, L, D = x1.shape
    tm = tl * B
    assert D_FF % FFN_SUB == 0 and HALO == (CONV_WIDTH - 1) * B and tm % FFN_ROWS == 0
    const = lambda i: (0, 0)
    resident = lambda shape: pl.BlockSpec(shape, const, pipeline_mode=pl.Buffered(1))
    return pl.pallas_call(
        functools.partial(_ffn_kernel, tl=tl),
        grid=(L // tl,),
        in_specs=[
            pl.BlockSpec((B, tl, D), lambda i: (0, i, 0)),
            pl.BlockSpec((B, SUBLANES, D), lambda i: (0, jnp.maximum(i * (tl // SUBLANES) - 1, 0), 0)),
            pl.BlockSpec((B, tl, PLE_DIM), lambda i: (0, i, 0)),
            resident((PLE_DIM, D)),
            resident((D, D)),
            resident((D, 2 * D_FF)),
            resident((CONV_WIDTH, 2 * D_FF)),
            resident((1, 2 * D_FF)),
            resident((D_FF, D)),
            resident((1, D)),
            resident((1, D)),
        ],
        out_specs=pl.BlockSpec((B, tl, D), lambda i: (0, i, 0)),
        out_shape=jax.ShapeDtypeStruct((B, L, D), F32),
        scratch_shapes=[
            pltpu.VMEM((D // LANES, tm, LANES), F32),
            pltpu.VMEM((D // LANES, HALO, LANES), F32),
            pltpu.VMEM((PLE_DIM // LANES, tm, LANES), F32),
            pltpu.VMEM((tm + HALO, D), BF16),
            pltpu.VMEM((tm, D), F32),
            pltpu.VMEM((2, tm + HALO, FFN_SUB), F32),
            pltpu.VMEM((2, tm + HALO, FFN_SUB), F32),
            pltpu.VMEM((tm, D_FF), BF16),
        ],
        compiler_params=_cparams(("parallel",)),
        name="ffn_ln2",
    )(x1, x1, p, wple_bf, wgate_bf, wup_bf, conv_w, conv_b, wdown_bf, ln_g, ln_b)


def kernel(x, p, w_in, diff_lambda_q1, diff_lambda_k1, diff_lambda_q2, diff_lambda_k2, diff_subln_g, ssm_lambda_re, ssm_lambda_im, ssm_log_dt, ssm_b_re, ssm_b_im, ssm_c_re, ssm_c_im, ssm_d, ssm_w_glu, ssm_b_glu, w_o, ln1_g, ln1_b, ffn_w_up, ffn_conv_w, ffn_conv_b, ffn_w_down, w_ple, w_ple_gate, ln2_g, ln2_b):
    B, L, D = x.shape
    assert B == SUBLANES and D == D_MODEL and L % 512 == 0
    for i in range(DEPTH):
        lam_init = 0.8 - 0.6 * math.exp(-0.3 * i)
        row = lambda a: a[i].reshape(1, -1)
        dense = (w_in[i], ssm_w_glu[i], w_o[i], w_ple[i], w_ple_gate[i], ffn_w_up[i], ffn_w_down[i])
        *tables, w_in_bf, w_glu_bf, w_o_bf, w_ple_bf, w_gate_bf, w_up_bf, w_down_bf = _ssm_prep(
            ssm_lambda_re[i], ssm_lambda_im[i], ssm_log_dt[i], ssm_b_re[i], ssm_b_im[i], ssm_c_re[i], ssm_c_im[i],
            dense)
        qt, k, vt, u_tm = _in_proj(x, w_in_bf)
        attn = _diff_attn_flat(qt, k, vt, row(diff_lambda_q1), row(diff_lambda_k1), row(diff_lambda_q2),
                          row(diff_lambda_k2), diff_subln_g[i], lam_init)
        y_tm = _ssm(u_tm, tables[:4], ssm_d[i], B, L)
        x1 = _mix_ln1(x, attn, y_tm, w_glu_bf, row(ssm_b_glu), w_o_bf, row(ln1_g), row(ln1_b))
        x = _ffn_ln2(x1, p[i], w_ple_bf, w_gate_bf, w_up_bf, ffn_conv_w[i], row(ffn_conv_b), w_down_bf,
                     row(ln2_g), row(ln2_b))
    return x
```
